```python
import math
import jax
import jax.numpy as jnp
from jax import lax
import numpy as np

D_MODEL = 2048
BATCH = 4
SEQ = 4096
DEPTH = 4

GRID_W = 64
CTX_LEN = 256
EPS = 1e-6
NEG_INF = -1e30
F32 = jnp.float32

D_BRANCH = 512
D_MIX = 4 * D_BRANCH
SHORT_CONV = 3

GDN_HEADS = 4
GDN_DK = 128
GDN_DV = 128
GDN_W = GDN_HEADS * GDN_DV
GDN_CHUNK = 64

NA_HEADS = 4
NA_DH = 128
NA_W = NA_HEADS * NA_DH
NA_WIN_R = 8
NA_WIN_C = 16
NA_QBLK_C = 16
NA_KBLK_C = 32

MLA_HEADS = 4
MLA_Q_RANK = 384
MLA_KV_RANK = 256
MLA_NOPE = 128
MLA_ROPE = 64
MLA_V = 128
MLA_W = MLA_HEADS * MLA_V
MLA_QBLOCK = 128
ROPE_THETA = 10000.0

SSM_HEADDIM = 64
SSM_HEADS = D_BRANCH // SSM_HEADDIM
SSM_W = SSM_HEADS * SSM_HEADDIM
SSM_GROUPS = 2
SSM_STATE = 128
SSM_CONV_DIM = SSM_W + 2 * SSM_GROUPS * SSM_STATE
SSM_CHUNK = 64

IN_SIZES = (3 * GDN_W, GDN_W, 2 * GDN_HEADS, 2 * GDN_HEADS,
            3 * NA_W, NA_W,
            MLA_Q_RANK, MLA_KV_RANK, MLA_ROPE, MLA_W,
            SSM_W, SSM_CONV_DIM, 2 * SSM_HEADS)
D_IN = sum(IN_SIZES)

kernel_name = 'hybrid_parallel_heads_diffusion_block'


def rmsnorm(x, w):
    xf = x.astype(F32)
    y = xf * lax.rsqrt(jnp.mean(xf * xf, axis=-1, keepdims=True) + EPS)
    return (y * w.astype(F32)).astype(x.dtype)


def l2norm(x):
    xf = x.astype(F32)
    return xf * lax.rsqrt(jnp.sum(xf * xf, axis=-1, keepdims=True) + EPS)


def dwconv_centred(x, w):
    k = w.shape[0]
    return lax.conv_general_dilated(x, w[:, None, :].astype(x.dtype), window_strides=(1,),
                                    padding=[(k // 2, k // 2)],
                                    dimension_numbers=('NWC', 'WIO', 'NWC'),
                                    feature_group_count=x.shape[-1])


def split_in(t):
    out, start = [], 0
    for size in IN_SIZES:
        out.append(t[..., start:start + size])
        start += size
    return out


def rotate_pairs(x, ang):
    cos = jnp.cos(ang)[:, None, :].astype(x.dtype)
    sin = jnp.sin(ang)[:, None, :].astype(x.dtype)
    x1, x2 = jnp.split(x, 2, axis=-1)
    return jnp.concatenate([x1 * cos - x2 * sin, x2 * cos + x1 * sin], axis=-1)


def axial_rope_2d(x):
    T, n_freq = x.shape[1], x.shape[-1] // 4
    inv_freq = ROPE_THETA ** (-jnp.arange(n_freq, dtype=F32) / n_freq)
    t = jnp.arange(T, dtype=jnp.int32)
    row = (t // GRID_W).astype(F32)
    col = (t % GRID_W).astype(F32)
    x_row, x_col = jnp.split(x, 2, axis=-1)
    return jnp.concatenate([rotate_pairs(x_row, row[:, None] * inv_freq),
                            rotate_pairs(x_col, col[:, None] * inv_freq)], axis=-1)


def softmax_attention(q, k, v, scale):
    s = jnp.einsum('bqhd,bkhd->bhqk', q, k).astype(F32) * scale
    p = jax.nn.softmax(s, axis=-1).astype(v.dtype)
    return jnp.einsum('bhqk,bkhd->bqhd', p, v)


def blocked_attention(q, k, v, scale):
    Bsz, T, H, dq = q.shape
    nb = T // MLA_QBLOCK
    qb = jnp.moveaxis(q.reshape(Bsz, nb, MLA_QBLOCK, H, dq), 1, 0)
    o = lax.map(lambda qblk: softmax_attention(qblk, k, v, scale), qb)
    return jnp.moveaxis(o, 0, 1).reshape(Bsz, T, H, v.shape[-1])


def gdn_scan(q, k, v, g, beta, S0):
    Bsz, H, T, dk = q.shape
    dv = v.shape[-1]
    C = GDN_CHUNK
    n = T // C
    q = q.reshape(Bsz, H, n, C, dk)
    k = k.reshape(Bsz, H, n, C, dk)
    v = v.reshape(Bsz, H, n, C, dv)
    g = g.reshape(Bsz, H, n, C)
    beta = beta.reshape(Bsz, H, n, C)
    gam = jnp.cumsum(g, axis=-1)
    incl = np.tril(np.ones((C, C), bool))
    strict = np.tril(np.ones((C, C), bool), -1)
    diff = gam[..., :, None] - gam[..., None, :]
    dec = jnp.where(incl, jnp.exp(jnp.where(incl, diff, 0.0)), 0.0)
    kk = jnp.einsum('bhnid,bhnjd->bhnij', k, k)
    A = jnp.where(strict, beta[..., :, None] * kk * dec, 0.0)
    eye = jnp.eye(C, dtype=F32)
    T_inv = lax.linalg.triangular_solve(A + eye, jnp.broadcast_to(eye, A.shape), left_side=True,
                                        lower=True, unit_diagonal=True)
    u = jnp.einsum('bhnij,bhnjd->bhnid', T_inv, v * beta[..., None])
    w = jnp.einsum('bhnij,bhnjd->bhnid', T_inv, k * (beta * jnp.exp(gam))[..., None])
    qk = jnp.einsum('bhnid,bhnjd->bhnij', q, k) * dec
    g_last = gam[..., -1]
    q_dec = q * jnp.exp(gam)[..., None]
    k_dec = k * jnp.exp(g_last[..., None] - gam)[..., None]

    def step(S, inp):
        qd, kd, uc, wc, qkc, glc = inp
        v_new = uc - jnp.einsum('bhid,bhde->bhie', wc, S)
        o = jnp.einsum('bhid,bhde->bhie', qd, S) + jnp.einsum('bhij,bhje->bhie', qkc, v_new)
        S = S * jnp.exp(glc)[..., None, None] + jnp.einsum('bhid,bhie->bhde', kd, v_new)
        return S, o

    xs = tuple(jnp.moveaxis(t, 2, 0) for t in (q_dec, k_dec, u, w, qk, g_last))
    S_last, o = lax.scan(step, S0, xs)
    return jnp.moveaxis(o, 0, 2).reshape(Bsz, H, T, dv), S_last


def gdn_branch(qkv, z, beta_in, alpha_in, qkv_c, z_c, beta_in_c, alpha_in_c,
               conv_w, A_log, dt_bias, norm_w):
    decay_rate = jnp.exp(A_log.astype(F32))

    def prep(qkv, beta_in, alpha_in):
        Bsz, T, _ = qkv.shape
        qkv = jax.nn.silu(dwconv_centred(qkv, conv_w))
        t = qkv.astype(F32).reshape(Bsz, T, 3, GDN_HEADS, GDN_DK).transpose(2, 0, 3, 1, 4)
        q = l2norm(t[0]) * GDN_DK ** -0.5
        k = l2norm(t[1])
        v = t[2]
        beta = jax.nn.sigmoid(beta_in.astype(F32)).reshape(Bsz, T, 2, GDN_HEADS).transpose(2, 0, 3, 1)
        g = -decay_rate * jax.nn.softplus(alpha_in.astype(F32).reshape(Bsz, T, 2, GDN_HEADS)
                                          + dt_bias.astype(F32))
        return q, k, v, g.transpose(2, 0, 3, 1), beta

    def flip(t):
        return jnp.flip(t, axis=2)

    ql, kl, vl, gl, bl = prep(qkv, beta_in, alpha_in)
    qc, kc, vc, gc, bc = prep(qkv_c, beta_in_c, alpha_in_c)
    S0 = jnp.zeros((qc.shape[0], GDN_HEADS, GDN_DK, GDN_DV), F32)
    oc_f, Sc_f = gdn_scan(qc, kc, vc, gc[0], bc[0], S0)
    ol_f, _ = gdn_scan(ql, kl, vl, gl[0], bl[0], Sc_f)
    oc_b, Sc_b = gdn_scan(flip(qc), flip(kc), flip(vc), flip(gc[1]), flip(bc[1]), S0)
    ol_b, _ = gdn_scan(flip(ql), flip(kl), flip(vl), flip(gl[1]), flip(bl[1]), Sc_b)

    def out(o, z):
        Bsz, H, T, dv = o.shape
        o = rmsnorm(o.transpose(0, 2, 1, 3), norm_w).reshape(Bsz, T, GDN_W)
        return o.astype(z.dtype) * jax.nn.silu(z)

    return out(ol_f + flip(ol_b), z), out(oc_f + flip(oc_b), z_c)


def na_block_indices(rows):
    wr = min(NA_WIN_R, rows)
    ncb = GRID_W // NA_QBLK_C
    r = np.arange(rows)
    row0 = np.clip(r - wr // 2, 0, rows - wr)
    key_rows = row0[:, None] + np.arange(wr)
    cb = np.arange(ncb)
    col0 = np.clip(cb * NA_QBLK_C - (NA_KBLK_C - NA_QBLK_C) // 2, 0, GRID_W - NA_KBLK_C)
    key_cols = col0[:, None] + np.arange(NA_KBLK_C)
    kn = wr * NA_KBLK_C
    kidx = (key_rows[:, None, :, None] * GRID_W + key_cols[None, :, None, :]).reshape(rows, ncb, kn)
    kcol = np.broadcast_to(key_cols[:, None, :], (ncb, wr, NA_KBLK_C)).reshape(ncb, kn)
    krow = np.broadcast_to(key_rows[:, :, None], (rows, wr, NA_KBLK_C)).reshape(rows, kn)
    qcol = cb[:, None] * NA_QBLK_C + np.arange(NA_QBLK_C)
    win0 = np.clip(qcol - NA_WIN_C // 2, 0, GRID_W - NA_WIN_C)
    valid = (kcol[:, None, :] >= win0[..., None]) & (kcol[:, None, :] < win0[..., None] + NA_WIN_C)
    dri = (krow - r[:, None] + NA_WIN_R - 1)[:, None, None, :]
    dci = np.clip(kcol[:, None, :] - qcol[..., None] + NA_WIN_C - 1, 0, 2 * NA_WIN_C - 2)[None]
    return kidx, valid, dri, dci


def neighbourhood_attention(q, k, v, k_ctx, v_ctx, rpb, scale):
    Bsz, T, H, d = q.shape
    rows = T // GRID_W
    ncb = GRID_W // NA_QBLK_C
    kidx, valid, dri, dci = na_block_indices(rows)
    qb = q.reshape(Bsz, rows, ncb, NA_QBLK_C, H, d)
    kb = jnp.take(k, kidx, axis=1)
    vb = jnp.take(v, kidx, axis=1)
    bias = rpb[:, dri, dci].astype(F32)
    s_win = jnp.einsum('brcqhd,brckhd->bhrcqk', qb, kb).astype(F32) * scale + bias
    s_win = jnp.where(valid, s_win, NEG_INF)
    s_ctx = jnp.einsum('brcqhd,bkhd->bhrcqk', qb, k_ctx).astype(F32) * scale
    p = jax.nn.softmax(jnp.concatenate([s_win, s_ctx], axis=-1), axis=-1).astype(v.dtype)
    kn = kidx.shape[-1]
    o = (jnp.einsum('bhrcqk,brckhd->brcqhd', p[..., :kn], vb)
         + jnp.einsum('bhrcqk,bkhd->brcqhd', p[..., kn:], v_ctx))
    return o.reshape(Bsz, T, H, d)


def na_branch(qkv, z, qkv_c, z_c, q_norm, k_norm, rpb):
    def prep(qkv):
        Bsz, T, _ = qkv.shape
        t = qkv.reshape(Bsz, T, 3, NA_HEADS, NA_DH)
        return rmsnorm(t[:, :, 0], q_norm), rmsnorm(t[:, :, 1], k_norm), t[:, :, 2]

    q, k, v = prep(qkv)
    qc, kc, vc = prep(qkv_c)
    scale = NA_DH ** -0.5
    o_lat = neighbourhood_attention(q, k, v, kc, vc, rpb, scale)
    o_ctx = softmax_attention(qc, kc, vc, scale)
    return (o_lat.reshape(z.shape) * jax.nn.silu(z), o_ctx.reshape(z_c.shape) * jax.nn.silu(z_c))


def mla_branch(cq, ckv, kr, z, cq_c, ckv_c, kr_c, z_c,
               qa_norm, w_uq, kva_norm, w_ukv, q_norm, k_norm):
    def project(cq, ckv, kr, positioned):
        Bsz, T, _ = cq.shape
        q = (rmsnorm(cq, qa_norm) @ w_uq).reshape(Bsz, T, MLA_HEADS, MLA_NOPE + MLA_ROPE)
        kv = (rmsnorm(ckv, kva_norm) @ w_ukv).reshape(Bsz, T, MLA_HEADS, MLA_NOPE + MLA_V)
        k_rope = jnp.broadcast_to(kr[:, :, None, :], (Bsz, T, MLA_HEADS, MLA_ROPE))
        k = jnp.concatenate([kv[..., :MLA_NOPE], k_rope], axis=-1)
        v = kv[..., MLA_NOPE:]
        q = rmsnorm(q, q_norm)
        k = rmsnorm(k, k_norm)
        if positioned:
            q = jnp.concatenate([q[..., :MLA_NOPE], axial_rope_2d(q[..., MLA_NOPE:])], axis=-1)
            k = jnp.concatenate([k[..., :MLA_NOPE], axial_rope_2d(k[..., MLA_NOPE:])], axis=-1)
        return q, k, v

    q, k, v = project(cq, ckv, kr, True)
    qc, kc, vc = project(cq_c, ckv_c, kr_c, False)
    scale = (MLA_NOPE + MLA_ROPE) ** -0.5
    o_lat = blocked_attention(q, jnp.concatenate([kc, k], axis=1), jnp.concatenate([vc, v], axis=1), scale)
    o_ctx = softmax_attention(qc, kc, vc, scale)
    return (o_lat.reshape(z.shape) * jax.nn.silu(z), o_ctx.reshape(z_c.shape) * jax.nn.silu(z_c))


def ssd_scan(xs, dt, A, Bm, Cm, h0):
    Bsz, T, H, P = xs.shape
    G, N = Bm.shape[2], Bm.shape[3]
    C = SSM_CHUNK
    nc = T // C
    Bh = jnp.repeat(Bm.astype(F32), H // G, axis=2).reshape(Bsz, nc, C, H, N)
    Ch = jnp.repeat(Cm.astype(F32), H // G, axis=2).reshape(Bsz, nc, C, H, N)
    xdt = (xs.astype(F32) * dt[..., None]).reshape(Bsz, nc, C, H, P)
    cum = jnp.cumsum((dt * A).reshape(Bsz, nc, C, H), axis=2)
    causal = np.tril(np.ones((C, C), bool))[:, :, None]
    seg = cum[:, :, :, None, :] - cum[:, :, None, :, :]
    L = jnp.where(causal, jnp.exp(jnp.where(causal, seg, 0.0)), 0.0)
    scores = jnp.einsum('bnqhs,bnkhs->bnqkh', Ch, Bh) * L
    y_diag = jnp.einsum('bnqkh,bnkhp->bnqhp', scores, xdt)
    decay_to_end = jnp.exp(cum[:, :, -1:, :] - cum)
    states = jnp.einsum('bnkhs,bnkh,bnkhp->bnhps', Bh, decay_to_end, xdt)
    chunk_decay = jnp.exp(cum[:, :, -1, :])

    def step(h, inp):
        st, dec = inp
        return h * dec[..., None, None] + st, h

    h_last, h_prev = lax.scan(step, h0, (jnp.moveaxis(states, 1, 0), jnp.moveaxis(chunk_decay, 1, 0)))
    y_off = jnp.einsum('bnqhs,nbhps,bnqh->bnqhp', Ch, h_prev, jnp.exp(cum))
    return (y_diag + y_off).reshape(Bsz, T, H, P), h_last


def ssd_branch(z, xbc, dt_in, z_c, xbc_c, dt_in_c, conv_w, conv_b, A_log, dt_bias, D_skip, norm_w):
    A = -jnp.exp(A_log.astype(F32))

    def prep(xbc, dt_in):
        Bsz, T, _ = xbc.shape
        xbc = jax.nn.silu(dwconv_centred(xbc, conv_w) + conv_b)
        xs = xbc[..., :SSM_W].reshape(Bsz, T, SSM_HEADS, SSM_HEADDIM)
        Bm = xbc[..., SSM_W:SSM_W + SSM_GROUPS * SSM_STATE].reshape(Bsz, T, SSM_GROUPS, SSM_STATE)
        Cm = xbc[..., SSM_W + SSM_GROUPS * SSM_STATE:].reshape(Bsz, T, SSM_GROUPS, SSM_STATE)
        dt = jax.nn.softplus(dt_in.astype(F32).reshape(Bsz, T, 2, SSM_HEADS) + dt_bias.astype(F32))
        return xs, Bm, Cm, dt

    def flip(t):
        return jnp.flip(t, axis=1)

    xl, Bl, Cl, dtl = prep(xbc, dt_in)
    xc, Bc, Cc, dtc = prep(xbc_c, dt_in_c)
    h0 = jnp.zeros((xc.shape[0], SSM_HEADS, SSM_HEADDIM, SSM_STATE), F32)
    yc_f, hc_f = ssd_scan(xc, dtc[:, :, 0], A[0], Bc, Cc, h0)
    yl_f, _ = ssd_scan(xl, dtl[:, :, 0], A[0], Bl, Cl, hc_f)
    yc_b, hc_b = ssd_scan(flip(xc), flip(dtc[:, :, 1]), A[1], flip(Bc), flip(Cc), h0)
    yl_b, _ = ssd_scan(flip(xl), flip(dtl[:, :, 1]), A[1], flip(Bl), flip(Cl), hc_b)

    def out(y_f, y_b, xs, z):
        Bsz, T = xs.shape[:2]
        y = y_f + flip(y_b) + D_skip.astype(F32)[:, None] * xs.astype(F32)
        y = y.reshape(Bsz, T, SSM_W).astype(z.dtype) * jax.nn.silu(z)
        return rmsnorm(y, norm_w)

    return out(yl_f, yl_b, xl, z), out(yc_f, yc_b, xc, z_c)


def hybrid_layer(x, ctx, c, c_ctx, norm_w, ada_w, ada_b, w_in,
                 gdn_conv_w, gdn_A_log, gdn_dt_bias, gdn_norm_w,
                 na_q_norm, na_k_norm, na_rpb,
                 mla_qa_norm, mla_w_uq, mla_kva_norm, mla_w_ukv, mla_q_norm, mla_k_norm,
                 ssm_conv_w, ssm_conv_b, ssm_A_log, ssm_dt_bias, ssm_D, ssm_norm_w, w_out):
    shift, scale, gate = jnp.split((jax.nn.silu(c) @ ada_w + ada_b)[:, None, :], 3, axis=-1)
    shift_c, scale_c, gate_c = jnp.split(jax.nn.silu(c_ctx) @ ada_w + ada_b, 3, axis=-1)
    h = rmsnorm(x, norm_w) * (1 + scale) + shift
    hc = rmsnorm(ctx, norm_w) * (1 + scale_c) + shift_c
    (g_qkv, g_z, g_beta, g_alpha, n_qkv, n_z, m_q, m_kv, m_kr, m_z, s_z, s_xbc, s_dt) = split_in(h @ w_in)
    (g_qkv_c, g_z_c, g_beta_c, g_alpha_c, n_qkv_c, n_z_c, m_q_c, m_kv_c, m_kr_c, m_z_c,
     s_z_c, s_xbc_c, s_dt_c) = split_in(hc @ w_in)
    oa, oa_c = gdn_branch(g_qkv, g_z, g_beta, g_alpha, g_qkv_c, g_z_c, g_beta_c, g_alpha_c,
                          gdn_conv_w, gdn_A_log, gdn_dt_bias, gdn_norm_w)
    ob, ob_c = na_branch(n_qkv, n_z, n_qkv_c, n_z_c, na_q_norm, na_k_norm, na_rpb)
    oc, oc_c = mla_branch(m_q, m_kv, m_kr, m_z, m_q_c, m_kv_c, m_kr_c, m_z_c,
                          mla_qa_norm, mla_w_uq, mla_kva_norm, mla_w_ukv, mla_q_norm, mla_k_norm)
    od, od_c = ssd_branch(s_z, s_xbc, s_dt, s_z_c, s_xbc_c, s_dt_c,
                          ssm_conv_w, ssm_conv_b, ssm_A_log, ssm_dt_bias, ssm_D, ssm_norm_w)
    y = jnp.concatenate([oa, ob, oc, od], axis=-1) @ w_out
    y_c = jnp.concatenate([oa_c, ob_c, oc_c, od_c], axis=-1) @ w_out
    return x + gate * y, ctx + gate_c * y_c


def setup_inputs(seed: int = 0) -> dict:
    key = jax.random.key(seed)
    ks = jax.random.split(key, 28)
    L = DEPTH

    def normal(k, shape, s):
        return jax.random.normal(k, shape, F32) * s

    def gain(k, shape):
        return 1.0 + 0.02 * jax.random.normal(k, shape, F32)

    def log_decay_init(k, shape):
        return jnp.log(jax.random.uniform(k, shape, F32, 1.0, 16.0))

    def dt_bias_init(k, shape):
        dt = jnp.exp(jax.random.uniform(k, shape, F32, math.log(1e-3), math.log(1e-1)))
        return dt + jnp.log(-jnp.expm1(-dt))

    return {
        'x': normal(ks[0], (BATCH, SEQ, D_MODEL), 1.0),
        'c': normal(ks[1], (BATCH, D_MODEL), 1.0),
        'ctx': normal(ks[2], (BATCH, CTX_LEN, D_MODEL), 1.0),
        'c_ctx': normal(ks[3], (D_MODEL,), 1.0),
        'norm_w': gain(ks[4], (L, D_MODEL)),
        'ada_w': normal(ks[5], (L, D_MODEL, 3 * D_MODEL), 0.3 * D_MODEL ** -0.5),
        'ada_b': normal(ks[6], (L, 3 * D_MODEL), 0.02),
        'w_in': normal(ks[7], (L, D_MODEL, D_IN), D_MODEL ** -0.5),
        'gdn_conv_w': normal(ks[8], (L, SHORT_CONV, 3 * GDN_W), SHORT_CONV ** -0.5),
        'gdn_A_log': log_decay_init(ks[9], (L, 2, GDN_HEADS)),
        'gdn_dt_bias': dt_bias_init(ks[10], (L, 2, GDN_HEADS)),
        'gdn_norm_w': gain(ks[11], (L, GDN_DV)),
        'na_q_norm': gain(ks[12], (L, NA_DH)),
        'na_k_norm': gain(ks[13], (L, NA_DH)),
        'na_rpb': normal(ks[14], (L, NA_HEADS, 2 * NA_WIN_R - 1, 2 * NA_WIN_C - 1), 0.1),
        'mla_qa_norm': gain(ks[15], (L, MLA_Q_RANK)),
        'mla_w_uq': normal(ks[16], (L, MLA_Q_RANK, MLA_HEADS * (MLA_NOPE + MLA_ROPE)), MLA_Q_RANK ** -0.5),
        'mla_kva_norm': gain(ks[17], (L, MLA_KV_RANK)),
        'mla_w_ukv': normal(ks[18], (L, MLA_KV_RANK, MLA_HEADS * (MLA_NOPE + MLA_V)), MLA_KV_RANK ** -0.5),
        'mla_q_norm': gain(ks[19], (L, MLA_NOPE + MLA_ROPE)),
        'mla_k_norm': gain(ks[20], (L, MLA_NOPE + MLA_ROPE)),
        'ssm_conv_w': normal(ks[21], (L, SHORT_CONV, SSM_CONV_DIM), SHORT_CONV ** -0.5),
        'ssm_conv_b': normal(ks[22], (L, SSM_CONV_DIM), 0.02),
        'ssm_A_log': log_decay_init(ks[23], (L, 2, SSM_HEADS)),
        'ssm_dt_bias': dt_bias_init(ks[24], (L, 2, SSM_HEADS)),
        'ssm_D': gain(ks[25], (L, SSM_HEADS)),
        'ssm_norm_w': gain(ks[26], (L, SSM_W)),
        'w_out': normal(ks[27], (L, D_MIX, D_MODEL), D_MIX ** -0.5),
    }


def reference(x, c, ctx, c_ctx, norm_w, ada_w, ada_b, w_in,
              gdn_conv_w, gdn_A_log, gdn_dt_bias, gdn_norm_w,
              na_q_norm, na_k_norm, na_rpb,
              mla_qa_norm, mla_w_uq, mla_kva_norm, mla_w_ukv, mla_q_norm, mla_k_norm,
              ssm_conv_w, ssm_conv_b, ssm_A_log, ssm_dt_bias, ssm_D, ssm_norm_w, w_out):
    for l in range(DEPTH):
        x, ctx = hybrid_layer(x, ctx, c, c_ctx, norm_w[l], ada_w[l], ada_b[l], w_in[l],
                              gdn_conv_w[l], gdn_A_log[l], gdn_dt_bias[l], gdn_norm_w[l],
                              na_q_norm[l], na_k_norm[l], na_rpb[l],
                              mla_qa_norm[l], mla_w_uq[l], mla_kva_norm[l], mla_w_ukv[l],
                              mla_q_norm[l], mla_k_norm[l],
                              ssm_conv_w[l], ssm_conv_b[l], ssm_A_log[l], ssm_dt_bias[l],
                              ssm_D[l], ssm_norm_w[l], w_out[l])
    return x
```

```python
import functools
import math

import jax
import jax.numpy as jnp
import numpy as np
from jax import lax
from jax.experimental import pallas as pl
from jax.experimental.pallas import tpu as pltpu

F32 = jnp.float32
BF16 = jnp.bfloat16

D_MODEL = 2048
BATCH = 4
SEQ = 4096
DEPTH = 4
GRID_W = 64
CTX_LEN = 256
S_ALL = CTX_LEN + SEQ
EPS = 1e-6
NEG_INF = -1e30

D_BRANCH = 512
D_MIX = 4 * D_BRANCH
SHORT_CONV = 3

GDN_HEADS = 4
GDN_DK = 128
GDN_DV = 128
GDN_W = GDN_HEADS * GDN_DV
GDN_CHUNK = 64

NA_HEADS = 4
NA_DH = 128
NA_W = NA_HEADS * NA_DH
NA_WIN_R = 8
NA_WIN_C = 16
NA_QBLK_C = 16
NA_KBLK_C = 32

MLA_HEADS = 4
MLA_Q_RANK = 384
MLA_KV_RANK = 256
MLA_NOPE = 128
MLA_ROPE = 64
MLA_V = 128
MLA_W = MLA_HEADS * MLA_V
MLA_QBLOCK = 128
ROPE_THETA = 10000.0

SSM_HEADDIM = 64
SSM_HEADS = D_BRANCH // SSM_HEADDIM
SSM_W = SSM_HEADS * SSM_HEADDIM
SSM_GROUPS = 2
SSM_STATE = 128
SSM_CONV_DIM = SSM_W + 2 * SSM_GROUPS * SSM_STATE
SSM_CHUNK = 64

IN_SIZES = (3 * GDN_W, GDN_W, 2 * GDN_HEADS, 2 * GDN_HEADS,
            3 * NA_W, NA_W,
            MLA_Q_RANK, MLA_KV_RANK, MLA_ROPE, MLA_W,
            SSM_W, SSM_CONV_DIM, 2 * SSM_HEADS)
D_IN = sum(IN_SIZES)
_IN_NAMES = ('g_qkv', 'g_z', 'g_beta', 'g_alpha', 'n_qkv', 'n_z',
             'm_q', 'm_kv', 'm_kr', 'm_z', 's_z', 's_xbc', 's_dt')
_IN_START = dict(zip(_IN_NAMES, np.cumsum((0,) + IN_SIZES[:-1]).tolist()))
_IN_SIZE = dict(zip(_IN_NAMES, IN_SIZES))

_P_ORDER = ('g_qkv', 'g_z', 'n_qkv', 'n_z', 'm_z', 's_z', 's_xbc',
            'm_q', 'm_kv', 'm_kr', 'g_beta', 'g_alpha', 's_dt')
_P_START = {}
_off = 0
for _n in _P_ORDER:
    _P_START[_n] = _off
    _off += _IN_SIZE[_n]
LANES = 128
D_INP = -(-_off // LANES) * LANES
_P_PERM = np.concatenate([np.arange(_IN_START[n], _IN_START[n] + _IN_SIZE[n]) for n in _P_ORDER])

VMEM_LIMIT = 48 * 1024 * 1024


def _silu(x):
    return x * jax.nn.sigmoid(x)


def _ada_kernel(c_ref, w_ref, b_ref, o_ref):
    c = c_ref[...]
    a = _silu(c).astype(BF16)
    o_ref[0] = jnp.dot(a, w_ref[0].astype(BF16), preferred_element_type=F32) + b_ref[0]


def _ada_all(c8, ada_w, ada_b):
    tn = 1024
    L = ada_w.shape[0]
    n3 = ada_w.shape[2]
    return pl.pallas_call(
        _ada_kernel,
        out_shape=jax.ShapeDtypeStruct((L, 8, n3), F32),
        grid=(L, n3 // tn),
        in_specs=[pl.BlockSpec((8, D_MODEL), lambda l, j: (0, 0)),
                  pl.BlockSpec((1, D_MODEL, tn), lambda l, j: (l, 0, j)),
                  pl.BlockSpec((1, 1, tn), lambda l, j: (l, 0, j))],
        out_specs=pl.BlockSpec((1, 8, tn), lambda l, j: (l, 0, j)),
        compiler_params=pltpu.CompilerParams(
            dimension_semantics=("arbitrary", "arbitrary"), vmem_limit_bytes=VMEM_LIMIT),
        name="ada_mod",
    )(c8, ada_w, ada_b.reshape(L, 1, n3))


IN_TM = 544
IN_TN = 1152


def _inproj_kernel(x_ref, nw_ref, mod_ref, w_ref, o_ref, h_scr):
    i = pl.program_id(1)
    j = pl.program_id(2)

    @pl.when(j == 0)
    def _():
        x = x_ref[0]
        ms = jnp.mean(x * x, axis=-1, keepdims=True)
        y = x * lax.rsqrt(ms + EPS) * nw_ref[...]
        row = i * IN_TM + lax.broadcasted_iota(jnp.int32, (IN_TM, 1), 0)
        is_ctx = row < CTX_LEN
        m = mod_ref[0]
        shift = jnp.where(is_ctx, m[2:3], m[0:1])
        scale = jnp.where(is_ctx, m[3:4], m[1:2])
        h_scr[...] = (y * (1.0 + scale) + shift).astype(BF16)

    o_ref[0] = jnp.dot(h_scr[...], w_ref[...], preferred_element_type=F32)


def _inproj(xs, norm_w, mod4, w_in_p):
    B = xs.shape[0]
    return pl.pallas_call(
        _inproj_kernel,
        out_shape=jax.ShapeDtypeStruct((B, S_ALL, D_INP), F32),
        grid=(B, S_ALL // IN_TM, D_INP // IN_TN),
        in_specs=[pl.BlockSpec((1, IN_TM, D_MODEL), lambda b, i, j: (b, i, 0)),
                  pl.BlockSpec((1, D_MODEL), lambda b, i, j: (0, 0)),
                  pl.BlockSpec((1, 4, D_MODEL), lambda b, i, j: (b, 0, 0)),
                  pl.BlockSpec((D_MODEL, IN_TN), lambda b, i, j: (0, j))],
        out_specs=pl.BlockSpec((1, IN_TM, IN_TN), lambda b, i, j: (b, i, j)),
        scratch_shapes=[pltpu.VMEM((IN_TM, D_MODEL), BF16)],
        compiler_params=pltpu.CompilerParams(
            dimension_semantics=("arbitrary", "arbitrary", "arbitrary"),
            vmem_limit_bytes=VMEM_LIMIT),
        name="inproj",
    )(xs, norm_w.reshape(1, D_MODEL), mod4, w_in_p)


OUT_TM = 544
OUT_TN = 1024


def _outproj_kernel(a_ref, w_ref, x_ref, g_ref, o_ref):
    i = pl.program_id(1)
    y = jnp.dot(a_ref[0].astype(BF16), w_ref[...], preferred_element_type=F32)
    row = i * OUT_TM + lax.broadcasted_iota(jnp.int32, (OUT_TM, 1), 0)
    g = g_ref[0]
    gate = jnp.where(row < CTX_LEN, g[1:2], g[0:1])
    o_ref[0] = x_ref[0] + gate * y


def _outproj(a, w_out_b, xs, gate2):
    B = xs.shape[0]
    return pl.pallas_call(
        _outproj_kernel,
        out_shape=jax.ShapeDtypeStruct((B, S_ALL, D_MODEL), F32),
        grid=(B, S_ALL // OUT_TM, D_MODEL // OUT_TN),
        in_specs=[pl.BlockSpec((1, OUT_TM, D_MIX), lambda b, i, j: (b, i, 0)),
                  pl.BlockSpec((D_MIX, OUT_TN), lambda b, i, j: (0, j)),
                  pl.BlockSpec((1, OUT_TM, OUT_TN), lambda b, i, j: (b, i, j)),
                  pl.BlockSpec((1, 2, OUT_TN), lambda b, i, j: (b, 0, j))],
        out_specs=pl.BlockSpec((1, OUT_TM, OUT_TN), lambda b, i, j: (b, i, j)),
        compiler_params=pltpu.CompilerParams(
            dimension_semantics=("arbitrary", "arbitrary", "arbitrary"),
            vmem_limit_bytes=VMEM_LIMIT),
        name="outproj",
    )(a, w_out_b, xs, gate2)


def _rmsnorm(x, w):
    xf = x.astype(F32)
    y = xf * lax.rsqrt(jnp.mean(xf * xf, axis=-1, keepdims=True) + EPS)
    return (y * w.astype(F32)).astype(x.dtype)


def _l2norm(x):
    xf = x.astype(F32)
    return xf * lax.rsqrt(jnp.sum(xf * xf, axis=-1, keepdims=True) + EPS)


def _dwconv_centred(x, w):
    k = w.shape[0]
    return lax.conv_general_dilated(x, w[:, None, :].astype(x.dtype), window_strides=(1,),
                                    padding=[(k // 2, k // 2)],
                                    dimension_numbers=('NWC', 'WIO', 'NWC'),
                                    feature_group_count=x.shape[-1])


def _rotate_pairs(x, ang):
    cos = jnp.cos(ang)[:, None, :].astype(x.dtype)
    sin = jnp.sin(ang)[:, None, :].astype(x.dtype)
    x1, x2 = jnp.split(x, 2, axis=-1)
    return jnp.concatenate([x1 * cos - x2 * sin, x2 * cos + x1 * sin], axis=-1)


def _axial_rope_2d(x):
    T, n_freq = x.shape[1], x.shape[-1] // 4
    inv_freq = ROPE_THETA ** (-jnp.arange(n_freq, dtype=F32) / n_freq)
    t = jnp.arange(T, dtype=jnp.int32)
    row = (t // GRID_W).astype(F32)
    col = (t % GRID_W).astype(F32)
    x_row, x_col = jnp.split(x, 2, axis=-1)
    return jnp.concatenate([_rotate_pairs(x_row, row[:, None] * inv_freq),
                            _rotate_pairs(x_col, col[:, None] * inv_freq)], axis=-1)


def _softmax_attention(q, k, v, scale):
    s = jnp.einsum('bqhd,bkhd->bhqk', q, k).astype(F32) * scale
    p = jax.nn.softmax(s, axis=-1).astype(v.dtype)
    return jnp.einsum('bhqk,bkhd->bqhd', p, v)


def _blocked_attention(q, k, v, scale):
    Bsz, T, H, dq = q.shape
    nb = T // MLA_QBLOCK
    qb = jnp.moveaxis(q.reshape(Bsz, nb, MLA_QBLOCK, H, dq), 1, 0)
    o = lax.map(lambda qblk: _softmax_attention(qblk, k, v, scale), qb)
    return jnp.moveaxis(o, 0, 1).reshape(Bsz, T, H, v.shape[-1])


def _gdn_scan(q, k, v, g, beta, S0):
    Bsz, H, T, dk = q.shape
    dv = v.shape[-1]
    C = GDN_CHUNK
    n = T // C
    q = q.reshape(Bsz, H, n, C, dk)
    k = k.reshape(Bsz, H, n, C, dk)
    v = v.reshape(Bsz, H, n, C, dv)
    g = g.reshape(Bsz, H, n, C)
    beta = beta.reshape(Bsz, H, n, C)
    gam = jnp.cumsum(g, axis=-1)
    incl = np.tril(np.ones((C, C), bool))
    strict = np.tril(np.ones((C, C), bool), -1)
    diff = gam[..., :, None] - gam[..., None, :]
    dec = jnp.where(incl, jnp.exp(jnp.where(incl, diff, 0.0)), 0.0)
    kk = jnp.einsum('bhnid,bhnjd->bhnij', k, k)
    A = jnp.where(strict, beta[..., :, None] * kk * dec, 0.0)
    eye = jnp.eye(C, dtype=F32)
    T_inv = lax.linalg.triangular_solve(A + eye, jnp.broadcast_to(eye, A.shape), left_side=True,
                                        lower=True, unit_diagonal=True)
    u = jnp.einsum('bhnij,bhnjd->bhnid', T_inv, v * beta[..., None])
    w = jnp.einsum('bhnij,bhnjd->bhnid', T_inv, k * (beta * jnp.exp(gam))[..., None])
    qk = jnp.einsum('bhnid,bhnjd->bhnij', q, k) * dec
    g_last = gam[..., -1]
    q_dec = q * jnp.exp(gam)[..., None]
    k_dec = k * jnp.exp(g_last[..., None] - gam)[..., None]

    def step(S, inp):
        qd, kd, uc, wc, qkc, glc = inp
        v_new = uc - jnp.einsum('bhid,bhde->bhie', wc, S)
        o = jnp.einsum('bhid,bhde->bhie', qd, S) + jnp.einsum('bhij,bhje->bhie', qkc, v_new)
        S = S * jnp.exp(glc)[..., None, None] + jnp.einsum('bhid,bhie->bhde', kd, v_new)
        return S, o

    xs = tuple(jnp.moveaxis(t, 2, 0) for t in (q_dec, k_dec, u, w, qk, g_last))
    S_last, o = lax.scan(step, S0, xs)
    return jnp.moveaxis(o, 0, 2).reshape(Bsz, H, T, dv), S_last


def _gdn_branch(qkv, z, beta_in, alpha_in, qkv_c, z_c, beta_in_c, alpha_in_c,
                conv_w, A_log, dt_bias, norm_w):
    decay_rate = jnp.exp(A_log.astype(F32))

    def prep(qkv, beta_in, alpha_in):
        Bsz, T, _ = qkv.shape
        qkv = jax.nn.silu(_dwconv_centred(qkv, conv_w))
        t = qkv.astype(F32).reshape(Bsz, T, 3, GDN_HEADS, GDN_DK).transpose(2, 0, 3, 1, 4)
        q = _l2norm(t[0]) * GDN_DK ** -0.5
        k = _l2norm(t[1])
        v = t[2]
        beta = jax.nn.sigmoid(beta_in.astype(F32)).reshape(Bsz, T, 2, GDN_HEADS).transpose(2, 0, 3, 1)
        g = -decay_rate * jax.nn.softplus(alpha_in.astype(F32).reshape(Bsz, T, 2, GDN_HEADS)
                                          + dt_bias.astype(F32))
        return q, k, v, g.transpose(2, 0, 3, 1), beta

    def flip(t):
        return jnp.flip(t, axis=2)

    ql, kl, vl, gl, bl = prep(qkv, beta_in, alpha_in)
    qc, kc, vc, gc, bc = prep(qkv_c, beta_in_c, alpha_in_c)
    S0 = jnp.zeros((qc.shape[0], GDN_HEADS, GDN_DK, GDN_DV), F32)
    oc_f, Sc_f = _gdn_scan(qc, kc, vc, gc[0], bc[0], S0)
    ol_f, _ = _gdn_scan(ql, kl, vl, gl[0], bl[0], Sc_f)
    oc_b, Sc_b = _gdn_scan(flip(qc), flip(kc), flip(vc), flip(gc[1]), flip(bc[1]), S0)
    ol_b, _ = _gdn_scan(flip(ql), flip(kl), flip(vl), flip(gl[1]), flip(bl[1]), Sc_b)

    def out(o, z):
        Bsz, H, T, dv = o.shape
        o = _rmsnorm(o.transpose(0, 2, 1, 3), norm_w).reshape(Bsz, T, GDN_W)
        return o.astype(z.dtype) * jax.nn.silu(z)

    return out(ol_f + flip(ol_b), z), out(oc_f + flip(oc_b), z_c)


def _na_block_indices(rows):
    wr = min(NA_WIN_R, rows)
    ncb = GRID_W // NA_QBLK_C
    r = np.arange(rows)
    row0 = np.clip(r - wr // 2, 0, rows - wr)
    key_rows = row0[:, None] + np.arange(wr)
    cb = np.arange(ncb)
    col0 = np.clip(cb * NA_QBLK_C - (NA_KBLK_C - NA_QBLK_C) // 2, 0, GRID_W - NA_KBLK_C)
    key_cols = col0[:, None] + np.arange(NA_KBLK_C)
    kn = wr * NA_KBLK_C
    kidx = (key_rows[:, None, :, None] * GRID_W + key_cols[None, :, None, :]).reshape(rows, ncb, kn)
    kcol = np.broadcast_to(key_cols[:, None, :], (ncb, wr, NA_KBLK_C)).reshape(ncb, kn)
    krow = np.broadcast_to(key_rows[:, :, None], (rows, wr, NA_KBLK_C)).reshape(rows, kn)
    qcol = cb[:, None] * NA_QBLK_C + np.arange(NA_QBLK_C)
    win0 = np.clip(qcol - NA_WIN_C // 2, 0, GRID_W - NA_WIN_C)
    valid = (kcol[:, None, :] >= win0[..., None]) & (kcol[:, None, :] < win0[..., None] + NA_WIN_C)
    dri = (krow - r[:, None] + NA_WIN_R - 1)[:, None, None, :]
    dci = np.clip(kcol[:, None, :] - qcol[..., None] + NA_WIN_C - 1, 0, 2 * NA_WIN_C - 2)[None]
    return kidx, valid, dri, dci


def _neighbourhood_attention(q, k, v, k_ctx, v_ctx, rpb, scale):
    Bsz, T, H, d = q.shape
    rows = T // GRID_W
    ncb = GRID_W // NA_QBLK_C
    kidx, valid, dri, dci = _na_block_indices(rows)
    qb = q.reshape(Bsz, rows, ncb, NA_QBLK_C, H, d)
    kb = jnp.take(k, kidx, axis=1)
    vb = jnp.take(v, kidx, axis=1)
    bias = rpb[:, dri, dci].astype(F32)
    s_win = jnp.einsum('brcqhd,brckhd->bhrcqk', qb, kb).astype(F32) * scale + bias
    s_win = jnp.where(valid, s_win, NEG_INF)
    s_ctx = jnp.einsum('brcqhd,bkhd->bhrcqk', qb, k_ctx).astype(F32) * scale
    p = jax.nn.softmax(jnp.concatenate([s_win, s_ctx], axis=-1), axis=-1).astype(v.dtype)
    kn = kidx.shape[-1]
    o = (jnp.einsum('bhrcqk,brckhd->brcqhd', p[..., :kn], vb)
         + jnp.einsum('bhrcqk,bkhd->brcqhd', p[..., kn:], v_ctx))
    return o.reshape(Bsz, T, H, d)


def _na_branch(qkv, z, qkv_c, z_c, q_norm, k_norm, rpb):
    def prep(qkv):
        Bsz, T, _ = qkv.shape
        t = qkv.reshape(Bsz, T, 3, NA_HEADS, NA_DH)
        return _rmsnorm(t[:, :, 0], q_norm), _rmsnorm(t[:, :, 1], k_norm), t[:, :, 2]

    q, k, v = prep(qkv)
    qc, kc, vc = prep(qkv_c)
    scale = NA_DH ** -0.5
    o_lat = _neighbourhood_attention(q, k, v, kc, vc, rpb, scale)
    o_ctx = _softmax_attention(qc, kc, vc, scale)
    return (o_lat.reshape(z.shape) * jax.nn.silu(z), o_ctx.reshape(z_c.shape) * jax.nn.silu(z_c))


def _mla_branch(cq, ckv, kr, z, cq_c, ckv_c, kr_c, z_c,
                qa_norm, w_uq, kva_norm, w_ukv, q_norm, k_norm):
    def project(cq, ckv, kr, positioned):
        Bsz, T, _ = cq.shape
        q = (_rmsnorm(cq, qa_norm) @ w_uq).reshape(Bsz, T, MLA_HEADS, MLA_NOPE + MLA_ROPE)
        kv = (_rmsnorm(ckv, kva_norm) @ w_ukv).reshape(Bsz, T, MLA_HEADS, MLA_NOPE + MLA_V)
        k_rope = jnp.broadcast_to(kr[:, :, None, :], (Bsz, T, MLA_HEADS, MLA_ROPE))
        k = jnp.concatenate([kv[..., :MLA_NOPE], k_rope], axis=-1)
        v = kv[..., MLA_NOPE:]
        q = _rmsnorm(q, q_norm)
        k = _rmsnorm(k, k_norm)
        if positioned:
            q = jnp.concatenate([q[..., :MLA_NOPE], _axial_rope_2d(q[..., MLA_NOPE:])], axis=-1)
            k = jnp.concatenate([k[..., :MLA_NOPE], _axial_rope_2d(k[..., MLA_NOPE:])], axis=-1)
        return q, k, v

    q, k, v = project(cq, ckv, kr, True)
    qc, kc, vc = project(cq_c, ckv_c, kr_c, False)
    scale = (MLA_NOPE + MLA_ROPE) ** -0.5
    o_lat = _blocked_attention(q, jnp.concatenate([kc, k], axis=1), jnp.concatenate([vc, v], axis=1), scale)
    o_ctx = _softmax_attention(qc, kc, vc, scale)
    return (o_lat.reshape(z.shape) * jax.nn.silu(z), o_ctx.reshape(z_c.shape) * jax.nn.silu(z_c))


def _ssd_scan(xs, dt, A, Bm, Cm, h0):
    Bsz, T, H, P = xs.shape
    G, N = Bm.shape[2], Bm.shape[3]
    C = SSM_CHUNK
    nc = T // C
    Bh = jnp.repeat(Bm.astype(F32), H // G, axis=2).reshape(Bsz, nc, C, H, N)
    Ch = jnp.repeat(Cm.astype(F32), H // G, axis=2).reshape(Bsz, nc, C, H, N)
    xdt = (xs.astype(F32) * dt[..., None]).reshape(Bsz, nc, C, H, P)
    cum = jnp.cumsum((dt * A).reshape(Bsz, nc, C, H), axis=2)
    causal = np.tril(np.ones((C, C), bool))[:, :, None]
    seg = cum[:, :, :, None, :] - cum[:, :, None, :, :]
    L = jnp.where(causal, jnp.exp(jnp.where(causal, seg, 0.0)), 0.0)
    scores = jnp.einsum('bnqhs,bnkhs->bnqkh', Ch, Bh) * L
    y_diag = jnp.einsum('bnqkh,bnkhp->bnqhp', scores, xdt)
    decay_to_end = jnp.exp(cum[:, :, -1:, :] - cum)
    states = jnp.einsum('bnkhs,bnkh,bnkhp->bnhps', Bh, decay_to_end, xdt)
    chunk_decay = jnp.exp(cum[:, :, -1, :])

    def step(h, inp):
        st, dec = inp
        return h * dec[..., None, None] + st, h

    h_last, h_prev = lax.scan(step, h0, (jnp.moveaxis(states, 1, 0), jnp.moveaxis(chunk_decay, 1, 0)))
    y_off = jnp.einsum('bnqhs,nbhps,bnqh->bnqhp', Ch, h_prev, jnp.exp(cum))
    return (y_diag + y_off).reshape(Bsz, T, H, P), h_last


def _ssd_branch(z, xbc, dt_in, z_c, xbc_c, dt_in_c, conv_w, conv_b, A_log, dt_bias, D_skip, norm_w):
    A = -jnp.exp(A_log.astype(F32))

    def prep(xbc, dt_in):
        Bsz, T, _ = xbc.shape
        xbc = jax.nn.silu(_dwconv_centred(xbc, conv_w) + conv_b)
        xs = xbc[..., :SSM_W].reshape(Bsz, T, SSM_HEADS, SSM_HEADDIM)
        Bm = xbc[..., SSM_W:SSM_W + SSM_GROUPS * SSM_STATE].reshape(Bsz, T, SSM_GROUPS, SSM_STATE)
        Cm = xbc[..., SSM_W + SSM_GROUPS * SSM_STATE:].reshape(Bsz, T, SSM_GROUPS, SSM_STATE)
        dt = jax.nn.softplus(dt_in.astype(F32).reshape(Bsz, T, 2, SSM_HEADS) + dt_bias.astype(F32))
        return xs, Bm, Cm, dt

    def flip(t):
        return jnp.flip(t, axis=1)

    xl, Bl, Cl, dtl = prep(xbc, dt_in)
    xc, Bc, Cc, dtc = prep(xbc_c, dt_in_c)
    h0 = jnp.zeros((xc.shape[0], SSM_HEADS, SSM_HEADDIM, SSM_STATE), F32)
    yc_f, hc_f = _ssd_scan(xc, dtc[:, :, 0], A[0], Bc, Cc, h0)
    yl_f, _ = _ssd_scan(xl, dtl[:, :, 0], A[0], Bl, Cl, hc_f)
    yc_b, hc_b = _ssd_scan(flip(xc), flip(dtc[:, :, 1]), A[1], flip(Bc), flip(Cc), h0)
    yl_b, _ = _ssd_scan(flip(xl), flip(dtl[:, :, 1]), A[1], flip(Bl), flip(Cl), hc_b)

    def out(y_f, y_b, xs, z):
        Bsz, T = xs.shape[:2]
        y = y_f + flip(y_b) + D_skip.astype(F32)[:, None] * xs.astype(F32)
        y = y.reshape(Bsz, T, SSM_W).astype(z.dtype) * jax.nn.silu(z)
        return _rmsnorm(y, norm_w)

    return out(yl_f, yl_b, xl, z), out(yc_f, yc_b, xc, z_c)


def _piece(p, name):
    s = _P_START[name]
    return p[..., s:s + _IN_SIZE[name]]


def _both(a_ctx, a_lat):
    return jnp.concatenate([a_ctx, a_lat], axis=1)


def kernel(x, c, ctx, c_ctx, norm_w, ada_w, ada_b, w_in, gdn_conv_w, gdn_A_log, gdn_dt_bias, gdn_norm_w, na_q_norm, na_k_norm, na_rpb, mla_qa_norm, mla_w_uq, mla_kva_norm, mla_w_ukv, mla_q_norm, mla_k_norm, ssm_conv_w, ssm_conv_b, ssm_A_log, ssm_dt_bias, ssm_D, ssm_norm_w, w_out):
    B = x.shape[0]
    xs = jnp.concatenate([ctx, x], axis=1)
    c8 = jnp.zeros((8, D_MODEL), F32).at[:B].set(c).at[B].set(c_ctx)
    mods = _ada_all(c8, ada_w, ada_b)
    perm = jnp.asarray(_P_PERM)
    for l in range(DEPTH):
        shift, scale, gate = jnp.split(mods[l, :B], 3, axis=-1)
        shift_c, scale_c, gate_c = jnp.split(mods[l, B], 3, axis=-1)
        bc = lambda v: jnp.broadcast_to(v[None], (B, D_MODEL))
        mod4 = jnp.stack([shift, scale, bc(shift_c), bc(scale_c)], axis=1)
        gate2 = jnp.stack([gate, bc(gate_c)], axis=1)
        w_in_p = jnp.pad(w_in[l][:, perm], ((0, 0), (0, D_INP - D_IN))).astype(BF16)
        p = _inproj(xs, norm_w[l], mod4, w_in_p)
        pc, pw = p[:, :CTX_LEN], p[:, CTX_LEN:]
        g = lambda n: (_piece(pw, n), _piece(pc, n))
        (g_qkv, g_qkv_c), (g_z, g_z_c) = g('g_qkv'), g('g_z')
        (g_beta, g_beta_c), (g_alpha, g_alpha_c) = g('g_beta'), g('g_alpha')
        (n_qkv, n_qkv_c), (n_z, n_z_c) = g('n_qkv'), g('n_z')
        (m_q, m_q_c), (m_kv, m_kv_c), (m_kr, m_kr_c), (m_z, m_z_c) = g('m_q'), g('m_kv'), g('m_kr'), g('m_z')
        (s_z, s_z_c), (s_xbc, s_xbc_c), (s_dt, s_dt_c) = g('s_z'), g('s_xbc'), g('s_dt')
        oa, oa_c = _gdn_branch(g_qkv, g_z, g_beta, g_alpha, g_qkv_c, g_z_c, g_beta_c, g_alpha_c,
                               gdn_conv_w[l], gdn_A_log[l], gdn_dt_bias[l], gdn_norm_w[l])
        ob, ob_c = _na_branch(n_qkv, n_z, n_qkv_c, n_z_c, na_q_norm[l], na_k_norm[l], na_rpb[l])
        oc, oc_c = _mla_branch(m_q, m_kv, m_kr, m_z, m_q_c, m_kv_c, m_kr_c, m_z_c,
                               mla_qa_norm[l], mla_w_uq[l], mla_kva_norm[l], mla_w_ukv[l],
                               mla_q_norm[l], mla_k_norm[l])
        od, od_c = _ssd_branch(s_z, s_xbc, s_dt, s_z_c, s_xbc_c, s_dt_c,
                               ssm_conv_w[l], ssm_conv_b[l], ssm_A_log[l], ssm_dt_bias[l],
                               ssm_D[l], ssm_norm_w[l])
        a = jnp.concatenate([_both(oa_c, oa), _both(ob_c, ob), _both(oc_c, oc), _both(od_c, od)], axis=-1)
        xs = _outproj(a, w_out[l].astype(BF16), xs, gate2)
    return xs[:, CTX_LEN:]
```

```python
import functools
import math

import jax
import jax.numpy as jnp
import numpy as np
from jax import lax
from jax.experimental import pallas as pl
from jax.experimental.pallas import tpu as pltpu

F32 = jnp.float32
BF16 = jnp.bfloat16

D_MODEL = 2048
BATCH = 4
SEQ = 4096
DEPTH = 4
GRID_W = 64
GRID_H = SEQ // GRID_W
CTX_LEN = 256
S_ALL = SEQ + CTX_LEN
EPS = 1e-6
NEG_INF = -1e30

D_BRANCH = 512
D_MIX = 4 * D_BRANCH
SHORT_CONV = 3

GDN_HEADS = 4
GDN_DK = 128
GDN_DV = 128
GDN_W = GDN_HEADS * GDN_DV
GDN_CHUNK = 64

NA_HEADS = 4
NA_DH = 128
NA_W = NA_HEADS * NA_DH
NA_WIN_R = 8
NA_WIN_C = 16

MLA_HEADS = 4
MLA_Q_RANK = 384
MLA_KV_RANK = 256
MLA_NOPE = 128
MLA_ROPE = 64
MLA_QK = MLA_NOPE + MLA_ROPE
MLA_V = 128
MLA_W = MLA_HEADS * MLA_V
ROPE_THETA = 10000.0

SSM_HEADDIM = 64
SSM_HEADS = D_BRANCH // SSM_HEADDIM
SSM_W = SSM_HEADS * SSM_HEADDIM
SSM_GROUPS = 2
SSM_STATE = 128
SSM_CONV_DIM = SSM_W + 2 * SSM_GROUPS * SSM_STATE
SSM_CHUNK = 64

IN_SIZES = (3 * GDN_W, GDN_W, 2 * GDN_HEADS, 2 * GDN_HEADS,
            3 * NA_W, NA_W,
            MLA_Q_RANK, MLA_KV_RANK, MLA_ROPE, MLA_W,
            SSM_W, SSM_CONV_DIM, 2 * SSM_HEADS)
D_IN = sum(IN_SIZES)
_IN_NAMES = ('g_qkv', 'g_z', 'g_beta', 'g_alpha', 'n_qkv', 'n_z',
             'm_q', 'm_kv', 'm_kr', 'm_z', 's_z', 's_xbc', 's_dt')
_IN_START = dict(zip(_IN_NAMES, np.cumsum((0,) + IN_SIZES[:-1]).tolist()))
_IN_SIZE = dict(zip(_IN_NAMES, IN_SIZES))

LANES = 128
_P_ORDER = ('g_qkv', 'g_z', 'n_qkv', 'n_z', 'm_z', 's_z', 's_xbc',
            'm_q', 'm_kv', 'm_kr', 'm_kr', 'g_beta', 'g_alpha', 's_dt')
_P_START = {}
_off = 0
for _n in _P_ORDER:
    _P_START.setdefault(_n, _off)
    _off += _IN_SIZE[_n]
D_INP = -(-_off // (11 * LANES)) * (11 * LANES)
_P_PERM = np.concatenate([np.arange(_IN_START[n], _IN_START[n] + _IN_SIZE[n]) for n in _P_ORDER])
P_MLA_BLK = MLA_Q_RANK + MLA_KV_RANK + 2 * MLA_ROPE
assert _P_START['m_q'] % P_MLA_BLK == 0 and _P_START['g_beta'] % LANES == 0

VMEM_LIMIT = 48 * 1024 * 1024


def _silu(x):
    return x * jax.nn.sigmoid(x)


def _dot_nt(a, b):
    return lax.dot_general(a, b, (((1,), (1,)), ((), ())), preferred_element_type=F32)


def _cparams(n_axes):
    return pltpu.CompilerParams(dimension_semantics=("arbitrary",) * n_axes,
                                vmem_limit_bytes=VMEM_LIMIT)


def _ada_kernel(c_ref, w_ref, b_ref, o_ref):
    a = _silu(c_ref[...]).astype(BF16)
    o_ref[0] = jnp.dot(a, w_ref[0].astype(BF16), preferred_element_type=F32) + b_ref[0]


def _ada_all(c8, ada_w, ada_b):
    tn = 1024
    L = ada_w.shape[0]
    n3 = ada_w.shape[2]
    return pl.pallas_call(
        _ada_kernel,
        out_shape=jax.ShapeDtypeStruct((L, 8, n3), F32),
        grid=(L, n3 // tn),
        in_specs=[pl.BlockSpec((8, D_MODEL), lambda l, j: (0, 0)),
                  pl.BlockSpec((1, D_MODEL, tn), lambda l, j: (l, 0, j)),
                  pl.BlockSpec((1, 1, tn), lambda l, j: (l, 0, j))],
        out_specs=pl.BlockSpec((1, 8, tn), lambda l, j: (l, 0, j)),
        compiler_params=_cparams(2),
        name="ada_mod",
    )(c8, ada_w, ada_b.reshape(L, 1, n3))


IN_TM = 544
IN_TN = 1408


def _inproj_kernel(x_ref, nw_ref, mod_ref, w_ref, o_ref, h_scr):
    i = pl.program_id(1)
    j = pl.program_id(2)

    @pl.when(j == 0)
    def _():
        x = x_ref[0]
        ms = jnp.mean(x * x, axis=-1, keepdims=True)
        y = x * lax.rsqrt(ms + EPS) * nw_ref[...]
        row = i * IN_TM + lax.broadcasted_iota(jnp.int32, (IN_TM, 1), 0)
        is_ctx = row >= SEQ
        m = mod_ref[0]
        shift = jnp.where(is_ctx, m[2:3], m[0:1])
        scale = jnp.where(is_ctx, m[3:4], m[1:2])
        h_scr[...] = (y * (1.0 + scale) + shift).astype(BF16)

    o_ref[0] = jnp.dot(h_scr[...], w_ref[...], preferred_element_type=F32)


def _inproj(xs, norm_w, mod4, w_in_p):
    B = xs.shape[0]
    return pl.pallas_call(
        _inproj_kernel,
        out_shape=jax.ShapeDtypeStruct((B, S_ALL, D_INP), F32),
        grid=(B, S_ALL // IN_TM, D_INP // IN_TN),
        in_specs=[pl.BlockSpec((1, IN_TM, D_MODEL), lambda b, i, j: (b, i, 0)),
                  pl.BlockSpec((1, D_MODEL), lambda b, i, j: (0, 0)),
                  pl.BlockSpec((1, 4, D_MODEL), lambda b, i, j: (b, 0, 0)),
                  pl.BlockSpec((D_MODEL, IN_TN), lambda b, i, j: (0, j))],
        out_specs=pl.BlockSpec((1, IN_TM, IN_TN), lambda b, i, j: (b, i, j)),
        scratch_shapes=[pltpu.VMEM((IN_TM, D_MODEL), BF16)],
        compiler_params=_cparams(3),
        name="inproj",
    )(xs, norm_w.reshape(1, D_MODEL), mod4, w_in_p)


OUT_TM = 544
OUT_TN = 1024


def _outproj_kernel(a0_ref, a1_ref, a2_ref, a3_ref, w_ref, x_ref, g_ref, o_ref):
    i = pl.program_id(1)
    y = None
    for n, a_ref in enumerate((a0_ref, a1_ref, a2_ref, a3_ref)):
        t = jnp.dot(a_ref[0], w_ref[n * D_BRANCH:(n + 1) * D_BRANCH, :], preferred_element_type=F32)
        y = t if y is None else y + t
    row = i * OUT_TM + lax.broadcasted_iota(jnp.int32, (OUT_TM, 1), 0)
    g = g_ref[0]
    gate = jnp.where(row >= SEQ, g[1:2], g[0:1])
    o_ref[0] = x_ref[0] + gate * y


def _outproj(branches, w_out_b, xs, gate2):
    B = xs.shape[0]
    a_spec = pl.BlockSpec((1, OUT_TM, D_BRANCH), lambda b, i, j: (b, i, 0))
    return pl.pallas_call(
        _outproj_kernel,
        out_shape=jax.ShapeDtypeStruct((B, S_ALL, D_MODEL), F32),
        grid=(B, S_ALL // OUT_TM, D_MODEL // OUT_TN),
        in_specs=[a_spec, a_spec, a_spec, a_spec,
                  pl.BlockSpec((D_MIX, OUT_TN), lambda b, i, j: (0, j)),
                  pl.BlockSpec((1, OUT_TM, OUT_TN), lambda b, i, j: (b, i, j)),
                  pl.BlockSpec((1, 2, OUT_TN), lambda b, i, j: (b, 0, j))],
        out_specs=pl.BlockSpec((1, OUT_TM, OUT_TN), lambda b, i, j: (b, i, j)),
        compiler_params=_cparams(3),
        name="outproj",
    )(*branches, w_out_b, xs, gate2)


NA_TM = 544
NA_RB = 4
NA_QB = NA_RB * GRID_W
NA_KR = 12
NA_KW = NA_KR * GRID_W
NA_NBLK = GRID_H // NA_RB
assert NA_QB == CTX_LEN


def _na_prep_kernel(q_ref, k_ref, v_ref, qn_ref, kn_ref, qo_ref, ko_ref, vo_ref):
    def headnorm(x, w, extra):
        outs = []
        for h in range(NA_HEADS):
            xh = x[:, h * NA_DH:(h + 1) * NA_DH]
            ms = jnp.mean(xh * xh, axis=-1, keepdims=True)
            outs.append((xh * lax.rsqrt(ms + EPS) * w * extra).astype(BF16))
        return jnp.concatenate(outs, axis=-1)

    qo_ref[0] = headnorm(q_ref[0], qn_ref[...], NA_DH ** -0.5)
    ko_ref[0] = headnorm(k_ref[0], kn_ref[...], 1.0)
    vo_ref[0] = v_ref[0].astype(BF16)


def _na_prep(p, q_norm, k_norm):
    B = p.shape[0]
    c0 = _P_START['n_qkv'] // NA_W
    spec = lambda c: pl.BlockSpec((1, NA_TM, NA_W), lambda b, i, c=c: (b, i, c))
    ospec = pl.BlockSpec((1, NA_TM, NA_W), lambda b, i: (b, i, 0))
    wspec = pl.BlockSpec((1, NA_DH), lambda b, i: (0, 0))
    shp = jax.ShapeDtypeStruct((B, S_ALL, NA_W), BF16)
    return pl.pallas_call(
        _na_prep_kernel,
        out_shape=(shp, shp, shp),
        grid=(B, S_ALL // NA_TM),
        in_specs=[spec(c0), spec(c0 + 1), spec(c0 + 2), wspec, wspec],
        out_specs=(ospec, ospec, ospec),
        compiler_params=_cparams(2),
        name="na_prep",
    )(p, p, p, q_norm.reshape(1, NA_DH), k_norm.reshape(1, NA_DH))


def _na_kernel(q_ref, k_ref, v_ref, z_ref, bias_ref, o_ref):
    rb = pl.program_id(1)
    q = q_ref[0]
    z = z_ref[0]
    kc = k_ref[0, SEQ:S_ALL, :]
    vc = v_ref[0, SEQ:S_ALL, :]

    def finish(h, o, l):
        sl = slice(h * NA_DH, (h + 1) * NA_DH)
        o_ref[0, :, sl] = (o / l * _silu(z[:, sl])).astype(BF16)

    @pl.when(rb < NA_NBLK)
    def _latent():
        base = jnp.clip(rb * NA_RB - NA_RB, 0, GRID_H - NA_KR)
        start = pl.multiple_of(base * GRID_W, GRID_W)
        kw = k_ref[0, pl.ds(start, NA_KW), :]
        vw = v_ref[0, pl.ds(start, NA_KW), :]
        for h in range(NA_HEADS):
            sl = slice(h * NA_DH, (h + 1) * NA_DH)
            s_w = _dot_nt(q[:, sl], kw[:, sl]) + bias_ref[0, h]
            s_c = _dot_nt(q[:, sl], kc[:, sl])
            m = jnp.maximum(jnp.max(s_w, axis=-1, keepdims=True), jnp.max(s_c, axis=-1, keepdims=True))
            p_w = jnp.exp(s_w - m)
            p_c = jnp.exp(s_c - m)
            l = jnp.sum(p_w, axis=-1, keepdims=True) + jnp.sum(p_c, axis=-1, keepdims=True)
            o = (jnp.dot(p_w.astype(BF16), vw[:, sl], preferred_element_type=F32)
                 + jnp.dot(p_c.astype(BF16), vc[:, sl], preferred_element_type=F32))
            finish(h, o, l)

    @pl.when(rb == NA_NBLK)
    def _context():
        for h in range(NA_HEADS):
            sl = slice(h * NA_DH, (h + 1) * NA_DH)
            s_c = _dot_nt(q[:, sl], kc[:, sl])
            m = jnp.max(s_c, axis=-1, keepdims=True)
            p_c = jnp.exp(s_c - m)
            l = jnp.sum(p_c, axis=-1, keepdims=True)
            o = jnp.dot(p_c.astype(BF16), vc[:, sl], preferred_element_type=F32)
            finish(h, o, l)


def _na_bias_index():
    dri, dci, valid = [], [], []
    for rb in (0, 1, NA_NBLK - 1):
        base = int(np.clip(rb * NA_RB - NA_RB, 0, GRID_H - NA_KR))
        qi = np.arange(NA_QB)
        ki = np.arange(NA_KW)
        qr, qc = rb * NA_RB + qi // GRID_W, qi % GRID_W
        kr, kc = base + ki // GRID_W, ki % GRID_W
        row0 = np.clip(qr - NA_WIN_R // 2, 0, GRID_H - NA_WIN_R)
        win0 = np.clip(qc - NA_WIN_C // 2, 0, GRID_W - NA_WIN_C)
        v = ((kr[None] >= row0[:, None]) & (kr[None] < row0[:, None] + NA_WIN_R)
             & (kc[None] >= win0[:, None]) & (kc[None] < win0[:, None] + NA_WIN_C))
        dri.append(np.clip(kr[None] - qr[:, None] + NA_WIN_R - 1, 0, 2 * NA_WIN_R - 2))
        dci.append(np.clip(kc[None] - qc[:, None] + NA_WIN_C - 1, 0, 2 * NA_WIN_C - 2))
        valid.append(v)
    return np.stack(dri), np.stack(dci), np.stack(valid)


def _na_bias(rpb):
    dri, dci, valid = _na_bias_index()
    b = rpb[:, dri, dci]
    b = jnp.where(valid[None], b, NEG_INF)
    return jnp.transpose(b, (1, 0, 2, 3))


def _na_attend(qn, kn, vn, p, bias):
    B = p.shape[0]
    zc = _P_START['n_z'] // NA_W
    last = NA_NBLK - 1
    blk = pl.BlockSpec((1, NA_QB, NA_W), lambda b, r: (b, r, 0))
    full = pl.BlockSpec((1, S_ALL, NA_W), lambda b, r: (b, 0, 0))
    return pl.pallas_call(
        _na_kernel,
        out_shape=jax.ShapeDtypeStruct((B, S_ALL, NA_W), BF16),
        grid=(B, NA_NBLK + 1),
        in_specs=[blk, full, full,
                  pl.BlockSpec((1, NA_QB, NA_W), lambda b, r: (b, r, zc)),
                  pl.BlockSpec((1, NA_HEADS, NA_QB, NA_KW),
                               lambda b, r: (jnp.where(r == 0, 0, jnp.where(r >= last, 2, 1)), 0, 0, 0))],
        out_specs=blk,
        compiler_params=_cparams(2),
        name="na_attend",
    )(qn, kn, vn, p, bias)


MP_TM = 544
MLA_HW = 2 * LANES
MLA_TQ = 512
MLA_TK = 512


def _rope_tables():
    n_freq = MLA_ROPE // 4
    inv_freq = ROPE_THETA ** (-np.arange(n_freq, dtype=np.float64) / n_freq)
    t = np.arange(SEQ)
    ar = (t // GRID_W)[:, None] * inv_freq
    ac = (t % GRID_W)[:, None] * inv_freq
    cos = np.concatenate([np.cos(ar), np.cos(ar), np.cos(ac), np.cos(ac)], axis=1)
    sin = np.concatenate([-np.sin(ar), np.sin(ar), -np.sin(ac), np.sin(ac)], axis=1)
    cos = np.concatenate([cos, np.ones((CTX_LEN, MLA_ROPE))], axis=0)
    sin = np.concatenate([sin, np.zeros((CTX_LEN, MLA_ROPE))], axis=0)
    return (np.tile(cos, (1, MLA_HEADS)).astype(np.float32), np.tile(sin, (1, MLA_HEADS)).astype(np.float32))


def _rope_rotate(t, cos, sin):
    w = t.shape[1]
    lane = lax.broadcasted_iota(jnp.int32, (1, w), 1)
    first = (lane & 31) < 16
    up = pltpu.roll(t, w - 16, axis=1)
    dn = pltpu.roll(t, 16, axis=1)
    return t * cos + jnp.where(first, up, dn) * sin


def _mla_prep_kernel(p_ref, cos_ref, sin_ref, qan_ref, wuq_ref, kvan_ref, wukv_ref, qn_ref, kn_ref,
                     q_out, k_out, v_out):
    x = p_ref[0]
    cq = x[:, :MLA_Q_RANK]
    ckv = x[:, MLA_Q_RANK:MLA_Q_RANK + MLA_KV_RANK]
    kr2 = x[:, MLA_Q_RANK + MLA_KV_RANK:]

    def rms(t, w):
        return t * lax.rsqrt(jnp.mean(t * t, axis=-1, keepdims=True) + EPS) * w

    qf = jnp.dot(rms(cq, qan_ref[...]).astype(BF16), wuq_ref[...], preferred_element_type=F32)
    kvf = jnp.dot(rms(ckv, kvan_ref[...]).astype(BF16), wukv_ref[...], preferred_element_type=F32)
    cos = cos_ref[...]
    sin = sin_ref[...]
    qw = qn_ref[...]
    kw = kn_ref[...]
    n_all = MLA_HEADS * MLA_NOPE
    lane = lax.broadcasted_iota(jnp.int32, (1, LANES), 1)
    halves = (lane < MLA_ROPE, lane >= MLA_ROPE)

    q_rope = qf[:, n_all:]
    q_rope_sq = q_rope * q_rope
    q_rot = _rope_rotate(q_rope * qw[:, n_all:], cos, sin)
    kr_sq = jnp.sum(jnp.where(halves[0], kr2 * kr2, 0.0), axis=-1, keepdims=True)
    k_rot = _rope_rotate(kr2 * kw[:, n_all:], cos[:, :LANES], sin[:, :LANES])
    for h in range(MLA_HEADS):
        half = halves[h % 2]
        vsl = slice((h // 2) * LANES, (h // 2 + 1) * LANES)
        nsl = slice(h * MLA_NOPE, (h + 1) * MLA_NOPE)
        q_nope = qf[:, nsl]
        ss = (jnp.sum(q_nope * q_nope, axis=-1, keepdims=True)
              + jnp.sum(jnp.where(half, q_rope_sq[:, vsl], 0.0), axis=-1, keepdims=True))
        r = lax.rsqrt(ss * (1.0 / MLA_QK) + EPS) * (MLA_QK ** -0.5)
        q_out[0, :, h * MLA_HW:h * MLA_HW + LANES] = (q_nope * qw[:, nsl] * r).astype(BF16)
        q_out[0, :, h * MLA_HW + LANES:(h + 1) * MLA_HW] = (jnp.where(half, q_rot[:, vsl], 0.0) * r).astype(BF16)
        k_nope = kvf[:, nsl]
        ss = jnp.sum(k_nope * k_nope, axis=-1, keepdims=True) + kr_sq
        r = lax.rsqrt(ss * (1.0 / MLA_QK) + EPS)
        k_out[0, :, h * MLA_HW:h * MLA_HW + LANES] = (k_nope * kw[:, nsl] * r).astype(BF16)
        k_out[0, :, h * MLA_HW + LANES:(h + 1) * MLA_HW] = (jnp.where(half, k_rot, 0.0) * r).astype(BF16)
    v_out[0] = kvf[:, n_all:].astype(BF16)


def _mla_prep(p, cos, sin, qa_norm, w_uq, kva_norm, w_ukv, q_norm, k_norm):
    B = p.shape[0]
    H = MLA_HEADS
    uq = w_uq.reshape(MLA_Q_RANK, H, MLA_QK)
    uq = jnp.concatenate([uq[:, :, :MLA_NOPE].reshape(MLA_Q_RANK, -1),
                          uq[:, :, MLA_NOPE:].reshape(MLA_Q_RANK, -1)], axis=1).astype(BF16)
    ukv = w_ukv.reshape(MLA_KV_RANK, H, MLA_NOPE + MLA_V)
    ukv = jnp.concatenate([ukv[:, :, :MLA_NOPE].reshape(MLA_KV_RANK, -1),
                           ukv[:, :, MLA_NOPE:].reshape(MLA_KV_RANK, -1)], axis=1).astype(BF16)
    qn = jnp.concatenate([jnp.tile(q_norm[:MLA_NOPE], H), jnp.tile(q_norm[MLA_NOPE:], H)]).reshape(1, -1)
    kn = jnp.concatenate([jnp.tile(k_norm[:MLA_NOPE], H), jnp.tile(k_norm[MLA_NOPE:], 2)]).reshape(1, -1)
    pc = _P_START['m_q'] // P_MLA_BLK
    const = lambda shape: pl.BlockSpec(shape, lambda b, i: (0, 0))
    rows = lambda w: pl.BlockSpec((MP_TM, w), lambda b, i: (i, 0))
    outs = lambda w: pl.BlockSpec((1, MP_TM, w), lambda b, i: (b, i, 0))
    return pl.pallas_call(
        _mla_prep_kernel,
        out_shape=(jax.ShapeDtypeStruct((B, S_ALL, H * MLA_HW), BF16),
                   jax.ShapeDtypeStruct((B, S_ALL, H * MLA_HW), BF16),
                   jax.ShapeDtypeStruct((B, S_ALL, MLA_W), BF16)),
        grid=(B, S_ALL // MP_TM),
        in_specs=[pl.BlockSpec((1, MP_TM, P_MLA_BLK), lambda b, i: (b, i, pc)),
                  rows(H * MLA_ROPE), rows(H * MLA_ROPE),
                  const((1, MLA_Q_RANK)), const(uq.shape), const((1, MLA_KV_RANK)), const(ukv.shape),
                  const(qn.shape), const(kn.shape)],
        out_specs=(outs(H * MLA_HW), outs(H * MLA_HW), outs(MLA_W)),
        compiler_params=_cparams(2),
        name="mla_prep",
    )(p, cos, sin, qa_norm.reshape(1, -1), uq, kva_norm.reshape(1, -1), ukv, qn, kn)


def _mla_attn_kernel(q_ref, k_ref, v_ref, z_ref, o_ref, *, ctx_start, n_lat_chunks):
    q = q_ref[0]
    tq = q.shape[0]

    def chunk(carry, kc, vc):
        m, l, acc = carry
        s = _dot_nt(q, kc)
        m_new = jnp.maximum(m, jnp.max(s, axis=-1, keepdims=True))
        a = jnp.exp(m - m_new)
        p = jnp.exp(s - m_new)
        l = a * l + jnp.sum(p, axis=-1, keepdims=True)
        acc = a * acc + jnp.dot(p.astype(BF16), vc, preferred_element_type=F32)
        return m_new, l, acc

    carry = (jnp.full((tq, 1), NEG_INF, F32), jnp.zeros((tq, 1), F32), jnp.zeros((tq, MLA_V), F32))
    carry = chunk(carry, k_ref[0, ctx_start:ctx_start + CTX_LEN, :], v_ref[0, ctx_start:ctx_start + CTX_LEN, :])

    def body(i, carry):
        st = pl.multiple_of(i * MLA_TK, MLA_TK)
        return chunk(carry, k_ref[0, pl.ds(st, MLA_TK), :], v_ref[0, pl.ds(st, MLA_TK), :])

    if n_lat_chunks:
        carry = lax.fori_loop(0, n_lat_chunks, body, carry)
    _, l, acc = carry
    o_ref[0] = (acc / l * _silu(z_ref[0])).astype(BF16)


def _mla_attend(q, k, v, p):
    B = p.shape[0]
    H = MLA_HEADS
    zc = _P_START['m_z'] // MLA_V
    ctx_blk = SEQ // CTX_LEN

    def call(tq, q_blk0, n_q, key_rows, key_blk, kern, name):
        return pl.pallas_call(
            kern,
            out_shape=jax.ShapeDtypeStruct((B, n_q * tq, MLA_W), BF16),
            grid=(B, H, n_q),
            in_specs=[pl.BlockSpec((1, tq, MLA_HW), lambda b, h, i: (b, q_blk0 + i, h)),
                      pl.BlockSpec((1, key_rows, MLA_HW), lambda b, h, i: (b, key_blk, h)),
                      pl.BlockSpec((1, key_rows, MLA_V), lambda b, h, i: (b, key_blk, h)),
                      pl.BlockSpec((1, tq, MLA_V), lambda b, h, i: (b, q_blk0 + i, zc + h))],
            out_specs=pl.BlockSpec((1, tq, MLA_V), lambda b, h, i: (b, i, h)),
            compiler_params=_cparams(3),
            name=name,
        )(q, k, v, p)

    lat = call(MLA_TQ, 0, SEQ // MLA_TQ, S_ALL, 0,
               functools.partial(_mla_attn_kernel, ctx_start=SEQ, n_lat_chunks=SEQ // MLA_TK), "mla_attend")
    ctx = call(CTX_LEN, ctx_blk, 1, CTX_LEN, ctx_blk,
               functools.partial(_mla_attn_kernel, ctx_start=0, n_lat_chunks=0), "mla_attend_ctx")
    return lat, ctx


def _rmsnorm(x, w):
    xf = x.astype(F32)
    y = xf * lax.rsqrt(jnp.mean(xf * xf, axis=-1, keepdims=True) + EPS)
    return (y * w.astype(F32)).astype(x.dtype)


def _l2norm(x):
    xf = x.astype(F32)
    return xf * lax.rsqrt(jnp.sum(xf * xf, axis=-1, keepdims=True) + EPS)


def _dwconv_centred(x, w):
    k = w.shape[0]
    return lax.conv_general_dilated(x, w[:, None, :].astype(x.dtype), window_strides=(1,),
                                    padding=[(k // 2, k // 2)],
                                    dimension_numbers=('NWC', 'WIO', 'NWC'),
                                    feature_group_count=x.shape[-1])


def _gdn_scan(q, k, v, g, beta, S0):
    Bsz, H, T, dk = q.shape
    dv = v.shape[-1]
    C = GDN_CHUNK
    n = T // C
    q = q.reshape(Bsz, H, n, C, dk)
    k = k.reshape(Bsz, H, n, C, dk)
    v = v.reshape(Bsz, H, n, C, dv)
    g = g.reshape(Bsz, H, n, C)
    beta = beta.reshape(Bsz, H, n, C)
    gam = jnp.cumsum(g, axis=-1)
    incl = np.tril(np.ones((C, C), bool))
    strict = np.tril(np.ones((C, C), bool), -1)
    diff = gam[..., :, None] - gam[..., None, :]
    dec = jnp.where(incl, jnp.exp(jnp.where(incl, diff, 0.0)), 0.0)
    kk = jnp.einsum('bhnid,bhnjd->bhnij', k, k)
    A = jnp.where(strict, beta[..., :, None] * kk * dec, 0.0)
    eye = jnp.eye(C, dtype=F32)
    T_inv = lax.linalg.triangular_solve(A + eye, jnp.broadcast_to(eye, A.shape), left_side=True,
                                        lower=True, unit_diagonal=True)
    u = jnp.einsum('bhnij,bhnjd->bhnid', T_inv, v * beta[..., None])
    w = jnp.einsum('bhnij,bhnjd->bhnid', T_inv, k * (beta * jnp.exp(gam))[..., None])
    qk = jnp.einsum('bhnid,bhnjd->bhnij', q, k) * dec
    g_last = gam[..., -1]
    q_dec = q * jnp.exp(gam)[..., None]
    k_dec = k * jnp.exp(g_last[..., None] - gam)[..., None]

    def step(S, inp):
        qd, kd, uc, wc, qkc, glc = inp
        v_new = uc - jnp.einsum('bhid,bhde->bhie', wc, S)
        o = jnp.einsum('bhid,bhde->bhie', qd, S) + jnp.einsum('bhij,bhje->bhie', qkc, v_new)
        S = S * jnp.exp(glc)[..., None, None] + jnp.einsum('bhid,bhie->bhde', kd, v_new)
        return S, o

    xs = tuple(jnp.moveaxis(t, 2, 0) for t in (q_dec, k_dec, u, w, qk, g_last))
    S_last, o = lax.scan(step, S0, xs)
    return jnp.moveaxis(o, 0, 2).reshape(Bsz, H, T, dv), S_last


def _gdn_branch(qkv, z, beta_in, alpha_in, qkv_c, z_c, beta_in_c, alpha_in_c,
                conv_w, A_log, dt_bias, norm_w):
    decay_rate = jnp.exp(A_log.astype(F32))

    def prep(qkv, beta_in, alpha_in):
        Bsz, T, _ = qkv.shape
        qkv = jax.nn.silu(_dwconv_centred(qkv, conv_w))
        t = qkv.astype(F32).reshape(Bsz, T, 3, GDN_HEADS, GDN_DK).transpose(2, 0, 3, 1, 4)
        q = _l2norm(t[0]) * GDN_DK ** -0.5
        k = _l2norm(t[1])
        v = t[2]
        beta = jax.nn.sigmoid(beta_in.astype(F32)).reshape(Bsz, T, 2, GDN_HEADS).transpose(2, 0, 3, 1)
        g = -decay_rate * jax.nn.softplus(alpha_in.astype(F32).reshape(Bsz, T, 2, GDN_HEADS)
                                          + dt_bias.astype(F32))
        return q, k, v, g.transpose(2, 0, 3, 1), beta

    def flip(t):
        return jnp.flip(t, axis=2)

    ql, kl, vl, gl, bl = prep(qkv, beta_in, alpha_in)
    qc, kc, vc, gc, bc = prep(qkv_c, beta_in_c, alpha_in_c)
    S0 = jnp.zeros((qc.shape[0], GDN_HEADS, GDN_DK, GDN_DV), F32)
    oc_f, Sc_f = _gdn_scan(qc, kc, vc, gc[0], bc[0], S0)
    ol_f, _ = _gdn_scan(ql, kl, vl, gl[0], bl[0], Sc_f)
    oc_b, Sc_b = _gdn_scan(flip(qc), flip(kc), flip(vc), flip(gc[1]), flip(bc[1]), S0)
    ol_b, _ = _gdn_scan(flip(ql), flip(kl), flip(vl), flip(gl[1]), flip(bl[1]), Sc_b)

    def out(o, z):
        Bsz, H, T, dv = o.shape
        o = _rmsnorm(o.transpose(0, 2, 1, 3), norm_w).reshape(Bsz, T, GDN_W)
        return o.astype(z.dtype) * jax.nn.silu(z)

    return out(ol_f + flip(ol_b), z), out(oc_f + flip(oc_b), z_c)


def _ssd_scan(xs, dt, A, Bm, Cm, h0):
    Bsz, T, H, P = xs.shape
    G, N = Bm.shape[2], Bm.shape[3]
    C = SSM_CHUNK
    nc = T // C
    Bh = jnp.repeat(Bm.astype(F32), H // G, axis=2).reshape(Bsz, nc, C, H, N)
    Ch = jnp.repeat(Cm.astype(F32), H // G, axis=2).reshape(Bsz, nc, C, H, N)
    xdt = (xs.astype(F32) * dt[..., None]).reshape(Bsz, nc, C, H, P)
    cum = jnp.cumsum((dt * A).reshape(Bsz, nc, C, H), axis=2)
    causal = np.tril(np.ones((C, C), bool))[:, :, None]
    seg = cum[:, :, :, None, :] - cum[:, :, None, :, :]
    L = jnp.where(causal, jnp.exp(jnp.where(causal, seg, 0.0)), 0.0)
    scores = jnp.einsum('bnqhs,bnkhs->bnqkh', Ch, Bh) * L
    y_diag = jnp.einsum('bnqkh,bnkhp->bnqhp', scores, xdt)
    decay_to_end = jnp.exp(cum[:, :, -1:, :] - cum)
    states = jnp.einsum('bnkhs,bnkh,bnkhp->bnhps', Bh, decay_to_end, xdt)
    chunk_decay = jnp.exp(cum[:, :, -1, :])

    def step(h, inp):
        st, dec = inp
        return h * dec[..., None, None] + st, h

    h_last, h_prev = lax.scan(step, h0, (jnp.moveaxis(states, 1, 0), jnp.moveaxis(chunk_decay, 1, 0)))
    y_off = jnp.einsum('bnqhs,nbhps,bnqh->bnqhp', Ch, h_prev, jnp.exp(cum))
    return (y_diag + y_off).reshape(Bsz, T, H, P), h_last


def _ssd_branch(z, xbc, dt_in, z_c, xbc_c, dt_in_c, conv_w, conv_b, A_log, dt_bias, D_skip, norm_w):
    A = -jnp.exp(A_log.astype(F32))

    def prep(xbc, dt_in):
        Bsz, T, _ = xbc.shape
        xbc = jax.nn.silu(_dwconv_centred(xbc, conv_w) + conv_b)
        xs = xbc[..., :SSM_W].reshape(Bsz, T, SSM_HEADS, SSM_HEADDIM)
        Bm = xbc[..., SSM_W:SSM_W + SSM_GROUPS * SSM_STATE].reshape(Bsz, T, SSM_GROUPS, SSM_STATE)
        Cm = xbc[..., SSM_W + SSM_GROUPS * SSM_STATE:].reshape(Bsz, T, SSM_GROUPS, SSM_STATE)
        dt = jax.nn.softplus(dt_in.astype(F32).reshape(Bsz, T, 2, SSM_HEADS) + dt_bias.astype(F32))
        return xs, Bm, Cm, dt

    def flip(t):
        return jnp.flip(t, axis=1)

    xl, Bl, Cl, dtl = prep(xbc, dt_in)
    xc, Bc, Cc, dtc = prep(xbc_c, dt_in_c)
    h0 = jnp.zeros((xc.shape[0], SSM_HEADS, SSM_HEADDIM, SSM_STATE), F32)
    yc_f, hc_f = _ssd_scan(xc, dtc[:, :, 0], A[0], Bc, Cc, h0)
    yl_f, _ = _ssd_scan(xl, dtl[:, :, 0], A[0], Bl, Cl, hc_f)
    yc_b, hc_b = _ssd_scan(flip(xc), flip(dtc[:, :, 1]), A[1], flip(Bc), flip(Cc), h0)
    yl_b, _ = _ssd_scan(flip(xl), flip(dtl[:, :, 1]), A[1], flip(Bl), flip(Cl), hc_b)

    def out(y_f, y_b, xs, z):
        Bsz, T = xs.shape[:2]
        y = y_f + flip(y_b) + D_skip.astype(F32)[:, None] * xs.astype(F32)
        y = y.reshape(Bsz, T, SSM_W).astype(z.dtype) * jax.nn.silu(z)
        return _rmsnorm(y, norm_w)

    return out(yl_f, yl_b, xl, z), out(yc_f, yc_b, xc, z_c)


def _piece(p, name):
    s = _P_START[name]
    return p[..., s:s + _IN_SIZE[name]]


def kernel(x, c, ctx, c_ctx, norm_w, ada_w, ada_b, w_in, gdn_conv_w, gdn_A_log, gdn_dt_bias, gdn_norm_w, na_q_norm, na_k_norm, na_rpb, mla_qa_norm, mla_w_uq, mla_kva_norm, mla_w_ukv, mla_q_norm, mla_k_norm, ssm_conv_w, ssm_conv_b, ssm_A_log, ssm_dt_bias, ssm_D, ssm_norm_w, w_out):
    B = x.shape[0]
    xs = jnp.concatenate([x, ctx], axis=1)
    c8 = jnp.zeros((8, D_MODEL), F32).at[:B].set(c).at[B].set(c_ctx)
    mods = _ada_all(c8, ada_w, ada_b)
    perm = jnp.asarray(_P_PERM)
    cos_np, sin_np = _rope_tables()
    cos, sin = jnp.asarray(cos_np), jnp.asarray(sin_np)
    for l in range(DEPTH):
        shift, scale, gate = jnp.split(mods[l, :B], 3, axis=-1)
        shift_c, scale_c, gate_c = jnp.split(mods[l, B], 3, axis=-1)
        bc = lambda v: jnp.broadcast_to(v[None], (B, D_MODEL))
        mod4 = jnp.stack([shift, scale, bc(shift_c), bc(scale_c)], axis=1)
        gate2 = jnp.stack([gate, bc(gate_c)], axis=1)
        w_in_p = jnp.pad(w_in[l][:, perm], ((0, 0), (0, D_INP - perm.shape[0]))).astype(BF16)
        p = _inproj(xs, norm_w[l], mod4, w_in_p)

        pw, pc = p[:, :SEQ], p[:, SEQ:]
        g = lambda n: (_piece(pw, n), _piece(pc, n))
        (g_qkv, g_qkv_c), (g_z, g_z_c) = g('g_qkv'), g('g_z')
        (g_beta, g_beta_c), (g_alpha, g_alpha_c) = g('g_beta'), g('g_alpha')
        (s_z, s_z_c), (s_xbc, s_xbc_c), (s_dt, s_dt_c) = g('s_z'), g('s_xbc'), g('s_dt')
        oa, oa_c = _gdn_branch(g_qkv, g_z, g_beta, g_alpha, g_qkv_c, g_z_c, g_beta_c, g_alpha_c,
                               gdn_conv_w[l], gdn_A_log[l], gdn_dt_bias[l], gdn_norm_w[l])
        oa = jnp.concatenate([oa, oa_c], axis=1).astype(BF16)

        qn, kn, vn = _na_prep(p, na_q_norm[l], na_k_norm[l])
        ob = _na_attend(qn, kn, vn, p, _na_bias(na_rpb[l]))

        mq, mk, mv = _mla_prep(p, cos, sin, mla_qa_norm[l], mla_w_uq[l], mla_kva_norm[l], mla_w_ukv[l],
                               mla_q_norm[l], mla_k_norm[l])
        oc_lat, oc_ctx = _mla_attend(mq, mk, mv, p)
        oc = jnp.concatenate([oc_lat, oc_ctx], axis=1)

        od, od_c = _ssd_branch(s_z, s_xbc, s_dt, s_z_c, s_xbc_c, s_dt_c,
                               ssm_conv_w[l], ssm_conv_b[l], ssm_A_log[l], ssm_dt_bias[l],
                               ssm_D[l], ssm_norm_w[l])
        od = jnp.concatenate([od, od_c], axis=1).astype(BF16)
        xs = _outproj((oa, ob, oc, od), w_out[l].astype(BF16), xs, gate2)
    return xs[:, :SEQ]
```

```python
import functools

import jax
import jax.numpy as jnp
import numpy as np
from jax import lax
from jax.experimental import pallas as pl
from jax.experimental.pallas import tpu as pltpu

F32 = jnp.float32
BF16 = jnp.bfloat16

D_MODEL = 2048
BATCH = 4
SEQ = 4096
DEPTH = 4
GRID_W = 64
GRID_H = SEQ // GRID_W
CTX_LEN = 256
S_ALL = SEQ + CTX_LEN
EPS = 1e-6
NEG_INF = -1e30

D_BRANCH = 512
D_MIX = 4 * D_BRANCH
SHORT_CONV = 3

GDN_HEADS = 4
GDN_DK = 128
GDN_DV = 128
GDN_W = GDN_HEADS * GDN_DV
GDN_CHUNK = 64

NA_HEADS = 4
NA_DH = 128
NA_W = NA_HEADS * NA_DH
NA_WIN_R = 8
NA_WIN_C = 16

MLA_HEADS = 4
MLA_Q_RANK = 384
MLA_KV_RANK = 256
MLA_NOPE = 128
MLA_ROPE = 64
MLA_QK = MLA_NOPE + MLA_ROPE
MLA_V = 128
MLA_W = MLA_HEADS * MLA_V
ROPE_THETA = 10000.0

SSM_HEADDIM = 64
SSM_HEADS = D_BRANCH // SSM_HEADDIM
SSM_W = SSM_HEADS * SSM_HEADDIM
SSM_GROUPS = 2
SSM_STATE = 128
SSM_CONV_DIM = SSM_W + 2 * SSM_GROUPS * SSM_STATE

IN_SIZES = (3 * GDN_W, GDN_W, 2 * GDN_HEADS, 2 * GDN_HEADS,
            3 * NA_W, NA_W,
            MLA_Q_RANK, MLA_KV_RANK, MLA_ROPE, MLA_W,
            SSM_W, SSM_CONV_DIM, 2 * SSM_HEADS)
D_IN = sum(IN_SIZES)
_IN_NAMES = ('g_qkv', 'g_z', 'g_beta', 'g_alpha', 'n_qkv', 'n_z',
             'm_q', 'm_kv', 'm_kr', 'm_z', 's_z', 's_xbc', 's_dt')
_IN_START = dict(zip(_IN_NAMES, np.cumsum((0,) + IN_SIZES[:-1]).tolist()))
_IN_SIZE = dict(zip(_IN_NAMES, IN_SIZES))

LANES = 128
_P_ORDER = ('g_qkv', 'g_z', 'n_qkv', 'n_z', 'm_z', 's_z', 's_xbc',
            'm_q', 'm_kv', 'm_kr', 'm_kr', 'g_beta', 'g_alpha', 's_dt')
_P_START = {}
_off = 0
for _n in _P_ORDER:
    _P_START.setdefault(_n, _off)
    _off += _IN_SIZE[_n]
D_INP = -(-_off // (11 * LANES)) * (11 * LANES)
P_MLA_BLK = MLA_Q_RANK + MLA_KV_RANK + 2 * MLA_ROPE
assert _P_START['m_q'] % P_MLA_BLK == 0 and _P_START['g_beta'] % LANES == 0
P_SMALL_BLK = _P_START['g_beta'] // LANES

VMEM_LIMIT = 48 * 1024 * 1024


def _silu(x):
    return x * jax.nn.sigmoid(x)


def _dot_nt(a, b):
    return lax.dot_general(a, b, (((1,), (1,)), ((), ())), preferred_element_type=F32)


def _dot_tn(a, b):
    return lax.dot_general(a, b, (((0,), (0,)), ((), ())), preferred_element_type=F32)


def _cparams(n_axes):
    return pltpu.CompilerParams(dimension_semantics=("arbitrary",) * n_axes,
                                vmem_limit_bytes=VMEM_LIMIT)


def _ada_kernel(c_ref, w_ref, b_ref, o_ref):
    a = _silu(c_ref[...]).astype(BF16)
    o_ref[0] = jnp.dot(a, w_ref[0].astype(BF16), preferred_element_type=F32) + b_ref[0]


def _ada_all(c8, ada_w, ada_b):
    tn = 1024
    L = ada_w.shape[0]
    n3 = ada_w.shape[2]
    return pl.pallas_call(
        _ada_kernel,
        out_shape=jax.ShapeDtypeStruct((L, 8, n3), F32),
        grid=(L, n3 // tn),
        in_specs=[pl.BlockSpec((8, D_MODEL), lambda l, j: (0, 0)),
                  pl.BlockSpec((1, D_MODEL, tn), lambda l, j: (l, 0, j)),
                  pl.BlockSpec((1, 1, tn), lambda l, j: (l, 0, j))],
        out_specs=pl.BlockSpec((1, 8, tn), lambda l, j: (l, 0, j)),
        compiler_params=_cparams(2),
        name="ada_mod",
    )(c8, ada_w, ada_b.reshape(L, 1, n3))


IN_TM = 544
IN_TN = 1408


def _inproj_kernel(x_ref, nw_ref, mod_ref, w_ref, o_ref, h_scr):
    i = pl.program_id(1)
    j = pl.program_id(2)

    @pl.when(j == 0)
    def _():
        x = x_ref[0]
        ms = jnp.mean(x * x, axis=-1, keepdims=True)
        y = x * lax.rsqrt(ms + EPS) * nw_ref[...]
        row = i * IN_TM + lax.broadcasted_iota(jnp.int32, (IN_TM, 1), 0)
        is_ctx = row >= SEQ
        m = mod_ref[0]
        shift = jnp.where(is_ctx, m[2:3], m[0:1])
        scale = jnp.where(is_ctx, m[3:4], m[1:2])
        h_scr[...] = (y * (1.0 + scale) + shift).astype(BF16)

    o_ref[0] = jnp.dot(h_scr[...], w_ref[...], preferred_element_type=F32)


def _inproj(xs, norm_w, mod4, w_in_p):
    B = xs.shape[0]
    return pl.pallas_call(
        _inproj_kernel,
        out_shape=jax.ShapeDtypeStruct((B, S_ALL, D_INP), F32),
        grid=(B, S_ALL // IN_TM, D_INP // IN_TN),
        in_specs=[pl.BlockSpec((1, IN_TM, D_MODEL), lambda b, i, j: (b, i, 0)),
                  pl.BlockSpec((1, D_MODEL), lambda b, i, j: (0, 0)),
                  pl.BlockSpec((1, 4, D_MODEL), lambda b, i, j: (b, 0, 0)),
                  pl.BlockSpec((D_MODEL, IN_TN), lambda b, i, j: (0, j))],
        out_specs=pl.BlockSpec((1, IN_TM, IN_TN), lambda b, i, j: (b, i, j)),
        scratch_shapes=[pltpu.VMEM((IN_TM, D_MODEL), BF16)],
        compiler_params=_cparams(3),
        name="inproj",
    )(xs, norm_w.reshape(1, D_MODEL), mod4, w_in_p)


OUT_TM = 544
OUT_TN = 1024


def _outproj_kernel(a0_ref, a1_ref, a2_ref, a3_ref, w_ref, x_ref, g_ref, o_ref):
    i = pl.program_id(1)
    y = None
    for n, a_ref in enumerate((a0_ref, a1_ref, a2_ref, a3_ref)):
        t = jnp.dot(a_ref[0], w_ref[n * D_BRANCH:(n + 1) * D_BRANCH, :], preferred_element_type=F32)
        y = t if y is None else y + t
    row = i * OUT_TM + lax.broadcasted_iota(jnp.int32, (OUT_TM, 1), 0)
    g = g_ref[0]
    gate = jnp.where(row >= SEQ, g[1:2], g[0:1])
    o_ref[0] = x_ref[0] + gate * y


def _outproj(branches, w_out_b, xs, gate2):
    B = xs.shape[0]
    a_spec = pl.BlockSpec((1, OUT_TM, D_BRANCH), lambda b, i, j: (b, i, 0))
    return pl.pallas_call(
        _outproj_kernel,
        out_shape=jax.ShapeDtypeStruct((B, S_ALL, D_MODEL), F32),
        grid=(B, S_ALL // OUT_TM, D_MODEL // OUT_TN),
        in_specs=[a_spec, a_spec, a_spec, a_spec,
                  pl.BlockSpec((D_MIX, OUT_TN), lambda b, i, j: (0, j)),
                  pl.BlockSpec((1, OUT_TM, OUT_TN), lambda b, i, j: (b, i, j)),
                  pl.BlockSpec((1, 2, OUT_TN), lambda b, i, j: (b, 0, j))],
        out_specs=pl.BlockSpec((1, OUT_TM, OUT_TN), lambda b, i, j: (b, i, j)),
        compiler_params=_cparams(3),
        name="outproj",
    )(*branches, w_out_b, xs, gate2)


NA_TM = 544
NA_RB = 4
NA_QB = NA_RB * GRID_W
NA_KR = 12
NA_KW = NA_KR * GRID_W
NA_NBLK = GRID_H // NA_RB
assert NA_QB == CTX_LEN


def _na_prep_kernel(q_ref, k_ref, v_ref, qn_ref, kn_ref, qo_ref, ko_ref, vo_ref):
    def headnorm(x, w, extra):
        outs = []
        for h in range(NA_HEADS):
            xh = x[:, h * NA_DH:(h + 1) * NA_DH]
            ms = jnp.mean(xh * xh, axis=-1, keepdims=True)
            outs.append((xh * lax.rsqrt(ms + EPS) * w * extra).astype(BF16))
        return jnp.concatenate(outs, axis=-1)

    qo_ref[0] = headnorm(q_ref[0], qn_ref[...], NA_DH ** -0.5)
    ko_ref[0] = headnorm(k_ref[0], kn_ref[...], 1.0)
    vo_ref[0] = v_ref[0].astype(BF16)


def _na_prep(p, q_norm, k_norm):
    B = p.shape[0]
    c0 = _P_START['n_qkv'] // NA_W
    spec = lambda c: pl.BlockSpec((1, NA_TM, NA_W), lambda b, i, c=c: (b, i, c))
    ospec = pl.BlockSpec((1, NA_TM, NA_W), lambda b, i: (b, i, 0))
    wspec = pl.BlockSpec((1, NA_DH), lambda b, i: (0, 0))
    shp = jax.ShapeDtypeStruct((B, S_ALL, NA_W), BF16)
    return pl.pallas_call(
        _na_prep_kernel,
        out_shape=(shp, shp, shp),
        grid=(B, S_ALL // NA_TM),
        in_specs=[spec(c0), spec(c0 + 1), spec(c0 + 2), wspec, wspec],
        out_specs=(ospec, ospec, ospec),
        compiler_params=_cparams(2),
        name="na_prep",
    )(p, p, p, q_norm.reshape(1, NA_DH), k_norm.reshape(1, NA_DH))


def _na_kernel(q_ref, k_ref, v_ref, z_ref, bias_ref, o_ref):
    rb = pl.program_id(1)
    q = q_ref[0]
    z = z_ref[0]
    kc = k_ref[0, SEQ:S_ALL, :]
    vc = v_ref[0, SEQ:S_ALL, :]

    def finish(h, o, l):
        sl = slice(h * NA_DH, (h + 1) * NA_DH)
        o_ref[0, :, sl] = (o / l * _silu(z[:, sl])).astype(BF16)

    @pl.when(rb < NA_NBLK)
    def _latent():
        base = jnp.clip(rb * NA_RB - NA_RB, 0, GRID_H - NA_KR)
        start = pl.multiple_of(base * GRID_W, GRID_W)
        kw = k_ref[0, pl.ds(start, NA_KW), :]
        vw = v_ref[0, pl.ds(start, NA_KW), :]
        for h in range(NA_HEADS):
            sl = slice(h * NA_DH, (h + 1) * NA_DH)
            s_w = _dot_nt(q[:, sl], kw[:, sl]) + bias_ref[0, h]
            s_c = _dot_nt(q[:, sl], kc[:, sl])
            m = jnp.maximum(jnp.max(s_w, axis=-1, keepdims=True), jnp.max(s_c, axis=-1, keepdims=True))
            p_w = jnp.exp(s_w - m)
            p_c = jnp.exp(s_c - m)
            l = jnp.sum(p_w, axis=-1, keepdims=True) + jnp.sum(p_c, axis=-1, keepdims=True)
            o = (jnp.dot(p_w.astype(BF16), vw[:, sl], preferred_element_type=F32)
                 + jnp.dot(p_c.astype(BF16), vc[:, sl], preferred_element_type=F32))
            finish(h, o, l)

    @pl.when(rb == NA_NBLK)
    def _context():
        for h in range(NA_HEADS):
            sl = slice(h * NA_DH, (h + 1) * NA_DH)
            s_c = _dot_nt(q[:, sl], kc[:, sl])
            m = jnp.max(s_c, axis=-1, keepdims=True)
            p_c = jnp.exp(s_c - m)
            l = jnp.sum(p_c, axis=-1, keepdims=True)
            o = jnp.dot(p_c.astype(BF16), vc[:, sl], preferred_element_type=F32)
            finish(h, o, l)


def _na_bias_index():
    dr = np.zeros((3, NA_RB, NA_KR), np.int64)
    ok = np.zeros((3, NA_RB, NA_KR), bool)
    for ci, rb in enumerate((0, 1, NA_NBLK - 1)):
        base = int(np.clip(rb * NA_RB - NA_RB, 0, GRID_H - NA_KR))
        qr = rb * NA_RB + np.arange(NA_RB)[:, None]
        kr = base + np.arange(NA_KR)[None, :]
        row0 = np.clip(qr - NA_WIN_R // 2, 0, GRID_H - NA_WIN_R)
        ok[ci] = (kr >= row0) & (kr < row0 + NA_WIN_R)
        dr[ci] = np.clip(kr - qr + NA_WIN_R - 1, 0, 2 * NA_WIN_R - 2)
    qc = np.arange(GRID_W)[:, None]
    kc = np.arange(GRID_W)[None, :]
    win0 = np.clip(qc - NA_WIN_C // 2, 0, GRID_W - NA_WIN_C)
    col_ok = (kc >= win0) & (kc < win0 + NA_WIN_C)
    return dr, ok, col_ok


def _na_bias(rpb):
    L, H = rpb.shape[:2]
    nd = 2 * NA_WIN_R - 1
    dr, ok, col_ok = _na_bias_index()
    left = GRID_W - NA_WIN_C
    f = jnp.pad(rpb, ((0, 0), (0, 0), (0, 0), (left, 2 * GRID_W - (2 * NA_WIN_C - 1) - left)))
    skew = jnp.broadcast_to(f[:, :, :, None, :], (L, H, nd, GRID_W, 2 * GRID_W))
    skew = skew.reshape(L, H, nd, -1)[..., :GRID_W * (2 * GRID_W - 1)].reshape(L, H, nd, GRID_W, 2 * GRID_W - 1)
    toe = skew[..., GRID_W - 1:]
    toe = jnp.where(col_ok, toe, NEG_INF)
    dead = jnp.full((L, H, GRID_W, GRID_W), NEG_INF, F32)
    blocks = [toe[:, :, dr[c, a, b]] if ok[c, a, b] else dead
              for c in range(3) for a in range(NA_RB) for b in range(NA_KR)]
    t = jnp.stack(blocks, axis=2).reshape(L, H, 3, NA_RB, NA_KR, GRID_W, GRID_W)
    return jnp.transpose(t, (0, 2, 1, 3, 5, 4, 6)).reshape(L, 3, H, NA_QB, NA_KW)


def _na_attend(qn, kn, vn, p, bias):
    B = p.shape[0]
    zc = _P_START['n_z'] // NA_W
    last = NA_NBLK - 1
    blk = pl.BlockSpec((1, NA_QB, NA_W), lambda b, r: (b, r, 0))
    full = pl.BlockSpec((1, S_ALL, NA_W), lambda b, r: (b, 0, 0))
    return pl.pallas_call(
        _na_kernel,
        out_shape=jax.ShapeDtypeStruct((B, S_ALL, NA_W), BF16),
        grid=(B, NA_NBLK + 1),
        in_specs=[blk, full, full,
                  pl.BlockSpec((1, NA_QB, NA_W), lambda b, r: (b, r, zc)),
                  pl.BlockSpec((1, NA_HEADS, NA_QB, NA_KW),
                               lambda b, r: (jnp.where(r == 0, 0, jnp.where(r >= last, 2, 1)), 0, 0, 0))],
        out_specs=blk,
        compiler_params=_cparams(2),
        name="na_attend",
    )(qn, kn, vn, p, bias)


MP_TM = 544
MLA_HW = 2 * LANES
MLA_TQ = 512
MLA_TK = 512


def _rope_tables():
    n_freq = MLA_ROPE // 4
    inv_freq = ROPE_THETA ** (-np.arange(n_freq, dtype=np.float64) / n_freq)
    t = np.arange(SEQ)
    ar = (t // GRID_W)[:, None] * inv_freq
    ac = (t % GRID_W)[:, None] * inv_freq
    cos = np.concatenate([np.cos(ar), np.cos(ar), np.cos(ac), np.cos(ac)], axis=1)
    sin = np.concatenate([-np.sin(ar), np.sin(ar), -np.sin(ac), np.sin(ac)], axis=1)
    cos = np.concatenate([cos, np.ones((CTX_LEN, MLA_ROPE))], axis=0)
    sin = np.concatenate([sin, np.zeros((CTX_LEN, MLA_ROPE))], axis=0)
    return (np.tile(cos, (1, MLA_HEADS)).astype(np.float32), np.tile(sin, (1, MLA_HEADS)).astype(np.float32))


def _rope_rotate(t, cos, sin):
    w = t.shape[1]
    lane = lax.broadcasted_iota(jnp.int32, (1, w), 1)
    first = (lane & 31) < 16
    up = pltpu.roll(t, w - 16, axis=1)
    dn = pltpu.roll(t, 16, axis=1)
    return t * cos + jnp.where(first, up, dn) * sin


def _mla_prep_kernel(p_ref, cos_ref, sin_ref, qan_ref, wuq_ref, kvan_ref, wukv_ref, qn_ref, kn_ref,
                     q_out, k_out, v_out):
    x = p_ref[0]
    cq = x[:, :MLA_Q_RANK]
    ckv = x[:, MLA_Q_RANK:MLA_Q_RANK + MLA_KV_RANK]
    kr2 = x[:, MLA_Q_RANK + MLA_KV_RANK:]

    def rms(t, w):
        return t * lax.rsqrt(jnp.mean(t * t, axis=-1, keepdims=True) + EPS) * w

    qf = jnp.dot(rms(cq, qan_ref[...]).astype(BF16), wuq_ref[...], preferred_element_type=F32)
    kvf = jnp.dot(rms(ckv, kvan_ref[...]).astype(BF16), wukv_ref[...], preferred_element_type=F32)
    cos = cos_ref[...]
    sin = sin_ref[...]
    qw = qn_ref[...]
    kw = kn_ref[...]
    n_all = MLA_HEADS * MLA_NOPE
    lane = lax.broadcasted_iota(jnp.int32, (1, LANES), 1)
    halves = (lane < MLA_ROPE, lane >= MLA_ROPE)

    q_rope = qf[:, n_all:]
    q_rope_sq = q_rope * q_rope
    q_rot = _rope_rotate(q_rope * qw[:, n_all:], cos, sin)
    kr_sq = jnp.sum(jnp.where(halves[0], kr2 * kr2, 0.0), axis=-1, keepdims=True)
    k_rot = _rope_rotate(kr2 * kw[:, n_all:], cos[:, :LANES], sin[:, :LANES])
    for h in range(MLA_HEADS):
        half = halves[h % 2]
        vsl = slice((h // 2) * LANES, (h // 2 + 1) * LANES)
        nsl = slice(h * MLA_NOPE, (h + 1) * MLA_NOPE)
        q_nope = qf[:, nsl]
        ss = (jnp.sum(q_nope * q_nope, axis=-1, keepdims=True)
              + jnp.sum(jnp.where(half, q_rope_sq[:, vsl], 0.0), axis=-1, keepdims=True))
        r = lax.rsqrt(ss * (1.0 / MLA_QK) + EPS) * (MLA_QK ** -0.5)
        q_out[0, :, h * MLA_HW:h * MLA_HW + LANES] = (q_nope * qw[:, nsl] * r).astype(BF16)
        q_out[0, :, h * MLA_HW + LANES:(h + 1) * MLA_HW] = (jnp.where(half, q_rot[:, vsl], 0.0) * r).astype(BF16)
        k_nope = kvf[:, nsl]
        ss = jnp.sum(k_nope * k_nope, axis=-1, keepdims=True) + kr_sq
        r = lax.rsqrt(ss * (1.0 / MLA_QK) + EPS)
        k_out[0, :, h * MLA_HW:h * MLA_HW + LANES] = (k_nope * kw[:, nsl] * r).astype(BF16)
        k_out[0, :, h * MLA_HW + LANES:(h + 1) * MLA_HW] = (jnp.where(half, k_rot, 0.0) * r).astype(BF16)
    v_out[0] = kvf[:, n_all:].astype(BF16)


def _mla_prep(p, cos, sin, qa_norm, w_uq, kva_norm, w_ukv, q_norm, k_norm):
    B = p.shape[0]
    H = MLA_HEADS
    uq = w_uq.reshape(MLA_Q_RANK, H, MLA_QK)
    uq = jnp.concatenate([uq[:, :, :MLA_NOPE].reshape(MLA_Q_RANK, -1),
                          uq[:, :, MLA_NOPE:].reshape(MLA_Q_RANK, -1)], axis=1).astype(BF16)
    ukv = w_ukv.reshape(MLA_KV_RANK, H, MLA_NOPE + MLA_V)
    ukv = jnp.concatenate([ukv[:, :, :MLA_NOPE].reshape(MLA_KV_RANK, -1),
                           ukv[:, :, MLA_NOPE:].reshape(MLA_KV_RANK, -1)], axis=1).astype(BF16)
    qn = jnp.concatenate([jnp.tile(q_norm[:MLA_NOPE], H), jnp.tile(q_norm[MLA_NOPE:], H)]).reshape(1, -1)
    kn = jnp.concatenate([jnp.tile(k_norm[:MLA_NOPE], H), jnp.tile(k_norm[MLA_NOPE:], 2)]).reshape(1, -1)
    pc = _P_START['m_q'] // P_MLA_BLK
    const = lambda shape: pl.BlockSpec(shape, lambda b, i: (0, 0))
    rows = lambda w: pl.BlockSpec((MP_TM, w), lambda b, i: (i, 0))
    outs = lambda w: pl.BlockSpec((1, MP_TM, w), lambda b, i: (b, i, 0))
    return pl.pallas_call(
        _mla_prep_kernel,
        out_shape=(jax.ShapeDtypeStruct((B, S_ALL, H * MLA_HW), BF16),
                   jax.ShapeDtypeStruct((B, S_ALL, H * MLA_HW), BF16),
                   jax.ShapeDtypeStruct((B, S_ALL, MLA_W), BF16)),
        grid=(B, S_ALL // MP_TM),
        in_specs=[pl.BlockSpec((1, MP_TM, P_MLA_BLK), lambda b, i: (b, i, pc)),
                  rows(H * MLA_ROPE), rows(H * MLA_ROPE),
                  const((1, MLA_Q_RANK)), const(uq.shape), const((1, MLA_KV_RANK)), const(ukv.shape),
                  const(qn.shape), const(kn.shape)],
        out_specs=(outs(H * MLA_HW), outs(H * MLA_HW), outs(MLA_W)),
        compiler_params=_cparams(2),
        name="mla_prep",
    )(p, cos, sin, qa_norm.reshape(1, -1), uq, kva_norm.reshape(1, -1), ukv, qn, kn)


def _mla_attn_kernel(q_ref, k_ref, v_ref, z_ref, o_ref, *, ctx_start, n_lat_chunks):
    q = q_ref[0]
    tq = q.shape[0]

    def chunk(carry, kc, vc):
        m, l, acc = carry
        s = _dot_nt(q, kc)
        m_new = jnp.maximum(m, jnp.max(s, axis=-1, keepdims=True))
        a = jnp.exp(m - m_new)
        p = jnp.exp(s - m_new)
        l = a * l + jnp.sum(p, axis=-1, keepdims=True)
        acc = a * acc + jnp.dot(p.astype(BF16), vc, preferred_element_type=F32)
        return m_new, l, acc

    carry = (jnp.full((tq, 1), NEG_INF, F32), jnp.zeros((tq, 1), F32), jnp.zeros((tq, MLA_V), F32))
    carry = chunk(carry, k_ref[0, ctx_start:ctx_start + CTX_LEN, :], v_ref[0, ctx_start:ctx_start + CTX_LEN, :])

    def body(i, carry):
        st = pl.multiple_of(i * MLA_TK, MLA_TK)
        return chunk(carry, k_ref[0, pl.ds(st, MLA_TK), :], v_ref[0, pl.ds(st, MLA_TK), :])

    if n_lat_chunks:
        carry = lax.fori_loop(0, n_lat_chunks, body, carry)
    _, l, acc = carry
    o_ref[0] = (acc / l * _silu(z_ref[0])).astype(BF16)


def _mla_attend(q, k, v, p):
    B = p.shape[0]
    H = MLA_HEADS
    zc = _P_START['m_z'] // MLA_V
    ctx_blk = SEQ // CTX_LEN

    def call(tq, q_blk0, n_q, key_rows, key_blk, kern, name):
        return pl.pallas_call(
            kern,
            out_shape=jax.ShapeDtypeStruct((B, n_q * tq, MLA_W), BF16),
            grid=(B, H, n_q),
            in_specs=[pl.BlockSpec((1, tq, MLA_HW), lambda b, h, i: (b, q_blk0 + i, h)),
                      pl.BlockSpec((1, key_rows, MLA_HW), lambda b, h, i: (b, key_blk, h)),
                      pl.BlockSpec((1, key_rows, MLA_V), lambda b, h, i: (b, key_blk, h)),
                      pl.BlockSpec((1, tq, MLA_V), lambda b, h, i: (b, q_blk0 + i, zc + h))],
            out_specs=pl.BlockSpec((1, tq, MLA_V), lambda b, h, i: (b, i, h)),
            compiler_params=_cparams(3),
            name=name,
        )(q, k, v, p)

    lat = call(MLA_TQ, 0, SEQ // MLA_TQ, S_ALL, 0,
               functools.partial(_mla_attn_kernel, ctx_start=SEQ, n_lat_chunks=SEQ // MLA_TK), "mla_attend")
    ctx = call(CTX_LEN, ctx_blk, 1, CTX_LEN, ctx_blk,
               functools.partial(_mla_attn_kernel, ctx_start=0, n_lat_chunks=0), "mla_attend_ctx")
    return lat, ctx


SCAN_TILE = 256
SCAN_NT = S_ALL // SCAN_TILE
CTX_TILE = SEQ // SCAN_TILE
HALO = 8


def _split_dot(m_bf16, x):
    hi = x.astype(BF16)
    lo = (x - hi.astype(F32)).astype(BF16)
    return (jnp.dot(m_bf16, hi, preferred_element_type=F32)
            + jnp.dot(m_bf16, lo, preferred_element_type=F32))


def _softplus(t):
    return jnp.maximum(t, 0.0) + jnp.log1p(jnp.exp(-jnp.abs(t)))


def _conv3_silu(x, prev_row, next_row, w, bias=None):
    n = x.shape[0]
    row = lax.broadcasted_iota(jnp.int32, (n, 1), 0)
    xp = jnp.where(row == 0, prev_row, pltpu.roll(x, 1, axis=0))
    xn = jnp.where(row == n - 1, next_row, pltpu.roll(x, n - 1, axis=0))
    y = xp * w[0:1] + x * w[1:2] + xn * w[2:3]
    if bias is not None:
        y = y + bias
    return _silu(y)


def _halo_rows(i, prev_ref, next_ref):
    pv = jnp.where((i == 0) | (i == CTX_TILE), 0.0, 1.0)
    nv = jnp.where((i == CTX_TILE - 1) | (i == SCAN_NT - 1), 0.0, 1.0)
    return prev_ref[0, HALO - 1:HALO, :] * pv, next_ref[0, 0:1, :] * nv


def _halo_specs(width, col_blk):
    rb = SCAN_TILE // HALO
    nblk = S_ALL // HALO
    return [pl.BlockSpec((1, SCAN_TILE, width), lambda b, i: (b, i, col_blk)),
            pl.BlockSpec((1, HALO, width), lambda b, i: (b, jnp.maximum(i * rb - 1, 0), col_blk)),
            pl.BlockSpec((1, HALO, width), lambda b, i: (b, jnp.minimum((i + 1) * rb, nblk - 1), col_blk))]


def _fwd_tile(t):
    return jnp.where(t == 0, CTX_TILE, t - 1)


def _bwd_tile(t):
    return jnp.where(t == 0, CTX_TILE, CTX_TILE - t)


def _chunk_masks(n):
    i = lax.broadcasted_iota(jnp.int32, (n, 1), 0)
    j = lax.broadcasted_iota(jnp.int32, (1, n), 1)
    same = (i // GDN_CHUNK) == (j // GDN_CHUNK)
    return i, j, same


def _gdn_prep_kernel(x_ref, prev_ref, next_ref, s_ref, cw_ref, rate_ref, dtb_ref,
                     q_out, k_out, v_out, g_out):
    i = pl.program_id(1)
    prev_row, next_row = _halo_rows(i, prev_ref, next_ref)
    y = _conv3_silu(x_ref[0], prev_row, next_row, cw_ref[...])
    for h in range(GDN_HEADS):
        sl = slice(h * GDN_DK, (h + 1) * GDN_DK)
        qh = y[:, sl]
        q_out[0, :, sl] = (qh * lax.rsqrt(jnp.sum(qh * qh, axis=-1, keepdims=True) + EPS)
                           * (GDN_DK ** -0.5)).astype(BF16)
        kh = y[:, GDN_W + h * GDN_DK:GDN_W + (h + 1) * GDN_DK]
        k_out[0, :, sl] = (kh * lax.rsqrt(jnp.sum(kh * kh, axis=-1, keepdims=True) + EPS)).astype(BF16)
    v_out[0] = y[:, 2 * GDN_W:].astype(BF16)

    s = s_ref[0]
    lane = lax.broadcasted_iota(jnp.int32, (1, LANES), 1)
    nh2 = 2 * GDN_HEADS
    beta = jax.nn.sigmoid(s)
    g = -rate_ref[...] * _softplus(s + dtb_ref[...])
    g = jnp.where((lane >= nh2) & (lane < 2 * nh2), g, 0.0)
    ii, jj, same = _chunk_masks(SCAN_TILE)
    one = lambda m: jnp.where(m, 1.0, 0.0).astype(BF16)
    fwd_lane = lane < nh2 + GDN_HEADS
    gam = jnp.where(fwd_lane, _split_dot(one(same & (jj <= ii)), g), _split_dot(one(same & (jj >= ii)), g))
    rem = jnp.where(fwd_lane, _split_dot(one(same & (jj > ii)), g), _split_dot(one(same & (jj < ii)), g))
    cf = jnp.where(lane < nh2, beta, jnp.where(lane < 2 * nh2, gam, pltpu.roll(rem, nh2, axis=1)))
    tr = cf.T
    for h in range(GDN_HEADS):
        for r, src in enumerate((h, GDN_HEADS + h, nh2 + h, nh2 + GDN_HEADS + h,
                                 2 * nh2 + h, 2 * nh2 + GDN_HEADS + h)):
            g_out[0, h, r:r + 1, :] = tr[src:src + 1, :]
        g_out[0, h, 6:8, :] = jnp.zeros((2, SCAN_TILE), F32)


def _gdn_prep(p, conv_w, A_log, dt_bias):
    B = p.shape[0]
    W3 = 3 * GDN_W
    nh2 = 2 * GDN_HEADS
    rate = jnp.zeros((1, LANES), F32).at[0, nh2:2 * nh2].set(jnp.exp(A_log).reshape(-1))
    dtb = jnp.zeros((1, LANES), F32).at[0, nh2:2 * nh2].set(dt_bias.reshape(-1))
    shp = jax.ShapeDtypeStruct((B, S_ALL, GDN_W), BF16)
    ospec = pl.BlockSpec((1, SCAN_TILE, GDN_W), lambda b, i: (b, i, 0))
    const = lambda shape: pl.BlockSpec(shape, lambda b, i: (0, 0))
    return pl.pallas_call(
        _gdn_prep_kernel,
        out_shape=(shp, shp, shp, jax.ShapeDtypeStruct((B, GDN_HEADS, 8, S_ALL), F32)),
        grid=(B, SCAN_NT),
        in_specs=_halo_specs(W3, _P_START['g_qkv'] // W3)
        + [pl.BlockSpec((1, SCAN_TILE, LANES), lambda b, i: (b, i, P_SMALL_BLK)),
           const((SHORT_CONV, W3)), const((1, LANES)), const((1, LANES))],
        out_specs=(ospec, ospec, ospec,
                   pl.BlockSpec((1, GDN_HEADS, 8, SCAN_TILE), lambda b, i: (b, 0, 0, i))),
        compiler_params=_cparams(2),
        name="gdn_prep",
    )(p, p, p, p, conv_w, rate, dtb)


def _gdn_dir(q, k, v, gr, s_ref, o_ref, d):
    n = SCAN_TILE
    cf = jnp.concatenate([gr, jnp.zeros((LANES - 8, n), F32)], axis=0).T
    beta, gam_c, rem_c = cf[:, d:d + 1], cf[:, 2 + d:3 + d], cf[:, 4 + d:5 + d]
    gam_r = gr[2 + d:3 + d, :]
    ii, jj, same = _chunk_masks(n)
    incl = same & ((jj <= ii) if d == 0 else (jj >= ii))
    strict = same & ((jj < ii) if d == 0 else (jj > ii))
    kk = _dot_nt(k, k)
    qk = _dot_nt(q, k)
    dec = jnp.exp(jnp.where(incl, gam_c - gam_r, NEG_INF))
    a = jnp.where(strict, beta * kk * dec, 0.0)
    qkd = (qk * dec).astype(BF16)
    kf = k.astype(F32)
    x = jnp.concatenate([v.astype(F32) * beta, kf * (beta * jnp.exp(gam_c))], axis=1)
    pb = a.astype(BF16)
    x = x - jnp.dot(pb, x.astype(BF16), preferred_element_type=F32)
    for _ in range(5):
        pb = jnp.dot(pb, pb, preferred_element_type=F32).astype(BF16)
        x = x + jnp.dot(pb, x.astype(BF16), preferred_element_type=F32)
    u, w = x[:, :GDN_DV], x[:, GDN_DV:].astype(BF16)
    qd = (q.astype(F32) * jnp.exp(gam_c)).astype(BF16)
    kd = (kf * jnp.exp(rem_c)).astype(BF16)
    s = s_ref[...]
    nchunk = n // GDN_CHUNK
    v_new = [None] * nchunk
    qs = [None] * nchunk
    for c in (range(nchunk) if d == 0 else reversed(range(nchunk))):
        rows = slice(c * GDN_CHUNK, (c + 1) * GDN_CHUNK)
        r1 = jnp.dot(jnp.concatenate([w[rows], qd[rows]], axis=0), s.astype(BF16), preferred_element_type=F32)
        vn = u[rows] - r1[:GDN_CHUNK]
        qs[c] = r1[GDN_CHUNK:]
        v_new[c] = vn
        last = (c + 1) * GDN_CHUNK - 1 if d == 0 else c * GDN_CHUNK
        s = s * jnp.exp(cf[last:last + 1, 2 + d:3 + d]) + _dot_tn(kd[rows], vn.astype(BF16))
    s_ref[...] = s
    vn_all = jnp.concatenate(v_new, axis=0).astype(BF16)
    o_ref[0] = jnp.concatenate(qs, axis=0) + jnp.dot(qkd, vn_all, preferred_element_type=F32)


def _gdn_scan_kernel(qf, kf, vf, gf, qb, kb, vb, gb, of_ref, ob_ref, sf_ref, sb_ref):
    @pl.when(pl.program_id(2) == 0)
    def _():
        sf_ref[...] = jnp.zeros_like(sf_ref)
        sb_ref[...] = jnp.zeros_like(sb_ref)

    _gdn_dir(qf[0], kf[0], vf[0], gf[0, 0], sf_ref, of_ref, 0)
    _gdn_dir(qb[0], kb[0], vb[0], gb[0, 0], sb_ref, ob_ref, 1)


def _gdn_scan(q, k, v, g):
    B = q.shape[0]
    tok = lambda order: pl.BlockSpec((1, SCAN_TILE, GDN_DK), lambda b, h, t: (b, order(t), h))
    gsp = lambda order: pl.BlockSpec((1, 1, 8, SCAN_TILE), lambda b, h, t: (b, h, 0, order(t)))
    shp = jax.ShapeDtypeStruct((B, S_ALL, GDN_W), F32)
    f, r = _fwd_tile, _bwd_tile
    return pl.pallas_call(
        _gdn_scan_kernel,
        out_shape=(shp, shp),
        grid=(B, GDN_HEADS, SCAN_NT),
        in_specs=[tok(f), tok(f), tok(f), gsp(f), tok(r), tok(r), tok(r), gsp(r)],
        out_specs=(tok(f), tok(r)),
        scratch_shapes=[pltpu.VMEM((GDN_DK, GDN_DV), F32), pltpu.VMEM((GDN_DK, GDN_DV), F32)],
        compiler_params=_cparams(3),
        name="gdn_scan",
    )(q, k, v, g, q, k, v, g)


GO_TM = 544


def _gdn_out_kernel(of_ref, ob_ref, z_ref, nw_ref, o_ref):
    o = of_ref[0] + ob_ref[0]
    z = z_ref[0]
    for h in range(GDN_HEADS):
        sl = slice(h * GDN_DV, (h + 1) * GDN_DV)
        oh = o[:, sl]
        y = oh * lax.rsqrt(jnp.mean(oh * oh, axis=-1, keepdims=True) + EPS) * nw_ref[...]
        o_ref[0, :, sl] = (y * _silu(z[:, sl])).astype(BF16)


def _gdn_out(o_f, o_b, p, norm_w):
    B = p.shape[0]
    zc = _P_START['g_z'] // GDN_W
    blk = pl.BlockSpec((1, GO_TM, GDN_W), lambda b, i: (b, i, 0))
    return pl.pallas_call(
        _gdn_out_kernel,
        out_shape=jax.ShapeDtypeStruct((B, S_ALL, GDN_W), BF16),
        grid=(B, S_ALL // GO_TM),
        in_specs=[blk, blk, pl.BlockSpec((1, GO_TM, GDN_W), lambda b, i: (b, i, zc)),
                  pl.BlockSpec((1, GDN_DV), lambda b, i: (0, 0))],
        out_specs=blk,
        compiler_params=_cparams(2),
        name="gdn_out",
    )(o_f, o_b, p, norm_w.reshape(1, GDN_DV))


SSD_BC = SSM_GROUPS * SSM_STATE
SSD_HD = 2 * SSM_HEADS
SSD_GW = (SSM_HEADS // SSM_GROUPS) * SSM_HEADDIM


def _ssd_prep_kernel(x_ref, prev_ref, next_ref, s_ref, cw_ref, cb_ref, a_ref, dtb_ref,
                     xs_out, b_out, c_out, bt_out, cf_out, cr_out):
    i = pl.program_id(1)
    prev_row, next_row = _halo_rows(i, prev_ref, next_ref)
    y = _conv3_silu(x_ref[0], prev_row, next_row, cw_ref[...], cb_ref[...])
    xs_out[0] = y[:, :SSM_W]
    b_out[0] = y[:, SSM_W:SSM_W + SSD_BC].astype(BF16)
    c_out[0] = y[:, SSM_W + SSD_BC:].astype(BF16)
    bt_out[0] = y[:, SSM_W:SSM_W + SSD_BC].T.astype(BF16)

    s = s_ref[0]
    lane = lax.broadcasted_iota(jnp.int32, (1, LANES), 1)
    dt = _softplus(s + dtb_ref[...])
    on = (lane >= SSD_HD) & (lane < 2 * SSD_HD)
    a = jnp.where(on, dt * a_ref[...], 0.0)
    n = SCAN_TILE
    ii = lax.broadcasted_iota(jnp.int32, (n, 1), 0)
    jj = lax.broadcasted_iota(jnp.int32, (1, n), 1)
    one = lambda m: jnp.where(m, 1.0, 0.0).astype(BF16)
    fwd_lane = lane < SSD_HD + SSM_HEADS
    cum = jnp.where(fwd_lane, _split_dot(one(jj <= ii), a), _split_dot(one(jj >= ii), a))
    rem = jnp.where(fwd_lane, _split_dot(one(jj > ii), a), _split_dot(one(jj < ii), a))
    cf = jnp.where(lane < SSD_HD, pltpu.roll(dt, LANES - SSD_HD, axis=1),
                   jnp.where(lane < 2 * SSD_HD, cum, pltpu.roll(rem, SSD_HD, axis=1)))
    cf_out[0] = cf
    cr_out[0] = cf.T[SSD_HD:2 * SSD_HD, :]


def _ssd_prep(p, conv_w, conv_b, A_log, dt_bias):
    B = p.shape[0]
    W = SSM_CONV_DIM
    a_vec = jnp.zeros((1, LANES), F32).at[0, SSD_HD:2 * SSD_HD].set(-jnp.exp(A_log).reshape(-1))
    dtb = jnp.zeros((1, LANES), F32).at[0, SSD_HD:2 * SSD_HD].set(dt_bias.reshape(-1))
    const = lambda shape: pl.BlockSpec(shape, lambda b, i: (0, 0))
    tok = lambda w: pl.BlockSpec((1, SCAN_TILE, w), lambda b, i: (b, i, 0))
    return pl.pallas_call(
        _ssd_prep_kernel,
        out_shape=(jax.ShapeDtypeStruct((B, S_ALL, SSM_W), F32),
                   jax.ShapeDtypeStruct((B, S_ALL, SSD_BC), BF16),
                   jax.ShapeDtypeStruct((B, S_ALL, SSD_BC), BF16),
                   jax.ShapeDtypeStruct((B, SSD_BC, S_ALL), BF16),
                   jax.ShapeDtypeStruct((B, S_ALL, LANES), F32),
                   jax.ShapeDtypeStruct((B, SSD_HD, S_ALL), F32)),
        grid=(B, SCAN_NT),
        in_specs=_halo_specs(W, _P_START['s_xbc'] // W)
        + [pl.BlockSpec((1, SCAN_TILE, LANES), lambda b, i: (b, i, P_SMALL_BLK)),
           const((SHORT_CONV, W)), const((1, W)), const((1, LANES)), const((1, LANES))],
        out_specs=(tok(SSM_W), tok(SSD_BC), tok(SSD_BC),
                   pl.BlockSpec((1, SSD_BC, SCAN_TILE), lambda b, i: (b, 0, i)),
                   tok(LANES),
                   pl.BlockSpec((1, SSD_HD, SCAN_TILE), lambda b, i: (b, 0, i))),
        compiler_params=_cparams(2),
        name="ssd_prep",
    )(p, p, p, p, conv_w, conv_b.reshape(1, W), a_vec, dtb)


def _per_head_lanes(cols):
    lane = lax.broadcasted_iota(jnp.int32, (1, LANES), 1)
    lo = lane < SSM_HEADDIM
    return jnp.concatenate([jnp.where(lo, cols[2 * j], cols[2 * j + 1]) for j in range(SSM_HEADS // 2)], axis=1)


def _ssd_dir(xs, bm, cm, bt, cf, cr, h_ref, y_ref, d):
    n = SCAN_TILE
    hd0 = d * SSM_HEADS
    col = lambda base, h: cf[:, base + hd0 + h:base + hd0 + h + 1]
    heads = range(SSM_HEADS)
    dt_x = _per_head_lanes([col(0, h) for h in heads])
    ecum_x = _per_head_lanes([jnp.exp(col(SSD_HD, h)) for h in heads])
    erem_x = _per_head_lanes([jnp.exp(col(2 * SSD_HD, h)) for h in heads])
    last = n - 1 if d == 0 else 0
    etot_x = _per_head_lanes([jnp.exp(cf[last:last + 1, SSD_HD + hd0 + h:SSD_HD + hd0 + h + 1]) for h in heads])
    xdt = xs * dt_x
    xdt_b = xdt.astype(BF16)
    xdec = (xdt * erem_x).astype(BF16)
    ii = lax.broadcasted_iota(jnp.int32, (n, 1), 0)
    jj = lax.broadcasted_iota(jnp.int32, (1, n), 1)
    causal = (jj <= ii) if d == 0 else (jj >= ii)
    lane = lax.broadcasted_iota(jnp.int32, (1, LANES), 1)
    lo = lane < SSM_HEADDIM
    hpg = SSM_HEADS // SSM_GROUPS
    y_parts = []
    for g in range(SSM_GROUPS):
        gs = slice(g * SSM_STATE, (g + 1) * SSM_STATE)
        xl = slice(g * SSD_GW, (g + 1) * SSD_GW)
        cb = _dot_nt(cm[:, gs], bm[:, gs])
        h_prev = h_ref[g]
        y_off = jnp.dot(cm[:, gs], h_prev.astype(BF16), preferred_element_type=F32) * ecum_x[:, xl]
        h_ref[g] = h_prev * etot_x[:, xl] + jnp.dot(bt[gs, :], xdec[:, xl], preferred_element_type=F32)
        pair_out = []
        for j in range(hpg // 2):
            ys = []
            for e in range(2):
                h = g * hpg + 2 * j + e
                seg = cf[:, SSD_HD + hd0 + h:SSD_HD + hd0 + h + 1] - cr[hd0 + h:hd0 + h + 1, :]
                sc = (cb * jnp.exp(jnp.where(causal, seg, NEG_INF))).astype(BF16)
                xp = slice(g * SSD_GW + j * LANES, g * SSD_GW + (j + 1) * LANES)
                ys.append(jnp.dot(sc, xdt_b[:, xp], preferred_element_type=F32))
            pair_out.append(jnp.where(lo, ys[0], ys[1]))
        y_parts.append(jnp.concatenate(pair_out, axis=1) + y_off)
    y_ref[0] = jnp.concatenate(y_parts, axis=1)


def _ssd_scan_kernel(xf, bf, cf_, btf, colf, rowf, xb, bb, cb_, btb, colb, rowb, yf_ref, yb_ref, hf_ref, hb_ref):
    @pl.when(pl.program_id(1) == 0)
    def _():
        hf_ref[...] = jnp.zeros_like(hf_ref)
        hb_ref[...] = jnp.zeros_like(hb_ref)

    _ssd_dir(xf[0], bf[0], cf_[0], btf[0], colf[0], rowf[0], hf_ref, yf_ref, 0)
    _ssd_dir(xb[0], bb[0], cb_[0], btb[0], colb[0], rowb[0], hb_ref, yb_ref, 1)


def _ssd_scan(xs, bm, cm, bt, cf, cr):
    B = xs.shape[0]

    def specs(order):
        tok = lambda w: pl.BlockSpec((1, SCAN_TILE, w), lambda b, t: (b, order(t), 0))
        return [tok(SSM_W), tok(SSD_BC), tok(SSD_BC),
                pl.BlockSpec((1, SSD_BC, SCAN_TILE), lambda b, t: (b, 0, order(t))),
                tok(LANES),
                pl.BlockSpec((1, SSD_HD, SCAN_TILE), lambda b, t: (b, 0, order(t)))]

    shp = jax.ShapeDtypeStruct((B, S_ALL, SSM_W), F32)
    out = lambda order: pl.BlockSpec((1, SCAN_TILE, SSM_W), lambda b, t: (b, order(t), 0))
    hshape = pltpu.VMEM((SSM_GROUPS, SSM_STATE, SSD_GW), F32)
    args = (xs, bm, cm, bt, cf, cr)
    return pl.pallas_call(
        _ssd_scan_kernel,
        out_shape=(shp, shp),
        grid=(B, SCAN_NT),
        in_specs=specs(_fwd_tile) + specs(_bwd_tile),
        out_specs=(out(_fwd_tile), out(_bwd_tile)),
        scratch_shapes=[hshape, hshape],
        compiler_params=_cparams(2),
        name="ssd_scan",
    )(*args, *args)


SO_TM = 544


def _ssd_out_kernel(yf_ref, yb_ref, xs_ref, z_ref, d_ref, nw_ref, o_ref):
    y = yf_ref[0] + yb_ref[0] + d_ref[...] * xs_ref[0]
    y = y * _silu(z_ref[0])
    o_ref[0] = (y * lax.rsqrt(jnp.mean(y * y, axis=-1, keepdims=True) + EPS) * nw_ref[...]).astype(BF16)


def _ssd_out(y_f, y_b, xs, p, d_skip, norm_w):
    B = p.shape[0]
    zc = _P_START['s_z'] // SSM_W
    blk = pl.BlockSpec((1, SO_TM, SSM_W), lambda b, i: (b, i, 0))
    vec = pl.BlockSpec((1, SSM_W), lambda b, i: (0, 0))
    return pl.pallas_call(
        _ssd_out_kernel,
        out_shape=jax.ShapeDtypeStruct((B, S_ALL, SSM_W), BF16),
        grid=(B, S_ALL // SO_TM),
        in_specs=[blk, blk, blk, pl.BlockSpec((1, SO_TM, SSM_W), lambda b, i: (b, i, zc)), vec, vec],
        out_specs=blk,
        compiler_params=_cparams(2),
        name="ssd_out",
    )(y_f, y_b, xs, p, jnp.repeat(d_skip, SSM_HEADDIM).reshape(1, SSM_W), norm_w.reshape(1, SSM_W))


def _repack_w_in(w):
    cols = [w[:, _IN_START[n]:_IN_START[n] + _IN_SIZE[n]] for n in _P_ORDER]
    cols.append(jnp.zeros((w.shape[0], D_INP - _off), w.dtype))
    return jnp.concatenate(cols, axis=1).astype(BF16)


def kernel(x, c, ctx, c_ctx, norm_w, ada_w, ada_b, w_in, gdn_conv_w, gdn_A_log, gdn_dt_bias, gdn_norm_w, na_q_norm, na_k_norm, na_rpb, mla_qa_norm, mla_w_uq, mla_kva_norm, mla_w_ukv, mla_q_norm, mla_k_norm, ssm_conv_w, ssm_conv_b, ssm_A_log, ssm_dt_bias, ssm_D, ssm_norm_w, w_out):
    B = x.shape[0]
    xs = jnp.concatenate([x, ctx], axis=1)
    c8 = jnp.zeros((8, D_MODEL), F32).at[:B].set(c).at[B].set(c_ctx)
    mods = _ada_all(c8, ada_w, ada_b)
    cos_np, sin_np = _rope_tables()
    cos, sin = jnp.asarray(cos_np), jnp.asarray(sin_np)
    na_bias = _na_bias(na_rpb)
    for l in range(DEPTH):
        shift, scale, gate = jnp.split(mods[l, :B], 3, axis=-1)
        shift_c, scale_c, gate_c = jnp.split(mods[l, B], 3, axis=-1)
        bc = lambda v: jnp.broadcast_to(v[None], (B, D_MODEL))
        mod4 = jnp.stack([shift, scale, bc(shift_c), bc(scale_c)], axis=1)
        gate2 = jnp.stack([gate, bc(gate_c)], axis=1)
        p = _inproj(xs, norm_w[l], mod4, _repack_w_in(w_in[l]))

        gq, gk, gv, gg = _gdn_prep(p, gdn_conv_w[l], gdn_A_log[l], gdn_dt_bias[l])
        o_f, o_b = _gdn_scan(gq, gk, gv, gg)
        oa = _gdn_out(o_f, o_b, p, gdn_norm_w[l])

        qn, kn, vn = _na_prep(p, na_q_norm[l], na_k_norm[l])
        ob = _na_attend(qn, kn, vn, p, na_bias[l])

        mq, mk, mv = _mla_prep(p, cos, sin, mla_qa_norm[l], mla_w_uq[l], mla_kva_norm[l], mla_w_ukv[l],
                               mla_q_norm[l], mla_k_norm[l])
        oc_lat, oc_ctx = _mla_attend(mq, mk, mv, p)
        oc = jnp.concatenate([oc_lat, oc_ctx], axis=1)

        sx, sb, sc, sbt, scf, scr = _ssd_prep(p, ssm_conv_w[l], ssm_conv_b[l], ssm_A_log[l], ssm_dt_bias[l])
        y_f, y_b = _ssd_scan(sx, sb, sc, sbt, scf, scr)
        od = _ssd_out(y_f, y_b, sx, p, ssm_D[l], ssm_norm_w[l])
        xs = _outproj((oa, ob, oc, od), w_out[l].astype(BF16), xs, gate2)
    return xs[:, :SEQ]
```

```python
import functools

import jax
import jax.numpy as jnp
import numpy as np
from jax import lax
from jax.experimental import pallas as pl
from jax.experimental.pallas import tpu as pltpu

F32 = jnp.float32
BF16 = jnp.bfloat16

D_MODEL = 2048
BATCH = 4
SEQ = 4096
DEPTH = 4
GRID_W = 64
GRID_H = SEQ // GRID_W
CTX_LEN = 256
S_ALL = SEQ + CTX_LEN
EPS = 1e-6
NEG_INF = -1e30

D_BRANCH = 512
D_MIX = 4 * D_BRANCH
SHORT_CONV = 3

GDN_HEADS = 4
GDN_DK = 128
GDN_DV = 128
GDN_W = GDN_HEADS * GDN_DV
GDN_CHUNK = 64

NA_HEADS = 4
NA_DH = 128
NA_W = NA_HEADS * NA_DH
NA_WIN_R = 8
NA_WIN_C = 16

MLA_HEADS = 4
MLA_Q_RANK = 384
MLA_KV_RANK = 256
MLA_NOPE = 128
MLA_ROPE = 64
MLA_QK = MLA_NOPE + MLA_ROPE
MLA_V = 128
MLA_W = MLA_HEADS * MLA_V
ROPE_THETA = 10000.0

SSM_HEADDIM = 64
SSM_HEADS = D_BRANCH // SSM_HEADDIM
SSM_W = SSM_HEADS * SSM_HEADDIM
SSM_GROUPS = 2
SSM_STATE = 128
SSM_CONV_DIM = SSM_W + 2 * SSM_GROUPS * SSM_STATE

IN_SIZES = (3 * GDN_W, GDN_W, 2 * GDN_HEADS, 2 * GDN_HEADS,
            3 * NA_W, NA_W,
            MLA_Q_RANK, MLA_KV_RANK, MLA_ROPE, MLA_W,
            SSM_W, SSM_CONV_DIM, 2 * SSM_HEADS)
D_IN = sum(IN_SIZES)
_IN_NAMES = ('g_qkv', 'g_z', 'g_beta', 'g_alpha', 'n_qkv', 'n_z',
             'm_q', 'm_kv', 'm_kr', 'm_z', 's_z', 's_xbc', 's_dt')
_IN_START = dict(zip(_IN_NAMES, np.cumsum((0,) + IN_SIZES[:-1]).tolist()))
_IN_SIZE = dict(zip(_IN_NAMES, IN_SIZES))

LANES = 128
_P_ORDER = ('g_qkv', 'g_z', 'n_qkv', 'n_z', 'm_z', 's_z', 's_xbc',
            'm_q', 'm_kv', 'm_kr', 'm_kr', 'g_beta', 'g_alpha', 's_dt')
_P_START = {}
_off = 0
for _n in _P_ORDER:
    _P_START.setdefault(_n, _off)
    _off += _IN_SIZE[_n]
D_INP = -(-_off // (11 * LANES)) * (11 * LANES)
P_MLA_BLK = MLA_Q_RANK + MLA_KV_RANK + 2 * MLA_ROPE
assert _P_START['m_q'] % P_MLA_BLK == 0 and _P_START['g_beta'] % LANES == 0
P_SMALL_BLK = _P_START['g_beta'] // LANES

VMEM_LIMIT = 48 * 1024 * 1024


def _silu(x):
    return x * jax.nn.sigmoid(x)


def _dot_nt(a, b):
    return lax.dot_general(a, b, (((1,), (1,)), ((), ())), preferred_element_type=F32)


def _dot_tn(a, b):
    return lax.dot_general(a, b, (((0,), (0,)), ((), ())), preferred_element_type=F32)


def _cparams(n_axes):
    return pltpu.CompilerParams(dimension_semantics=("arbitrary",) * n_axes,
                                vmem_limit_bytes=VMEM_LIMIT)


def _ada_kernel(c_ref, w_ref, b_ref, o_ref):
    a = _silu(c_ref[...]).astype(BF16)
    o_ref[0] = jnp.dot(a, w_ref[0].astype(BF16), preferred_element_type=F32) + b_ref[0]


def _ada_all(c8, ada_w, ada_b):
    tn = 1024
    L = ada_w.shape[0]
    n3 = ada_w.shape[2]
    return pl.pallas_call(
        _ada_kernel,
        out_shape=jax.ShapeDtypeStruct((L, 8, n3), F32),
        grid=(L, n3 // tn),
        in_specs=[pl.BlockSpec((8, D_MODEL), lambda l, j: (0, 0)),
                  pl.BlockSpec((1, D_MODEL, tn), lambda l, j: (l, 0, j)),
                  pl.BlockSpec((1, 1, tn), lambda l, j: (l, 0, j))],
        out_specs=pl.BlockSpec((1, 8, tn), lambda l, j: (l, 0, j)),
        compiler_params=_cparams(2),
        name="ada_mod",
    )(c8, ada_w, ada_b.reshape(L, 1, n3))


IN_TM = 544
IN_TN = 1408


def _inproj_kernel(x_ref, nw_ref, mod_ref, w_ref, o_ref, h_scr):
    i = pl.program_id(1)
    j = pl.program_id(2)

    @pl.when(j == 0)
    def _():
        x = x_ref[0]
        ms = jnp.mean(x * x, axis=-1, keepdims=True)
        y = x * lax.rsqrt(ms + EPS) * nw_ref[...]
        row = i * IN_TM + lax.broadcasted_iota(jnp.int32, (IN_TM, 1), 0)
        is_ctx = row >= SEQ
        m = mod_ref[0]
        shift = jnp.where(is_ctx, m[2:3], m[0:1])
        scale = jnp.where(is_ctx, m[3:4], m[1:2])
        h_scr[...] = (y * (1.0 + scale) + shift).astype(BF16)

    o_ref[0] = jnp.dot(h_scr[...], w_ref[...], preferred_element_type=F32)


def _inproj(xs, norm_w, mod4, w_in_p):
    B = xs.shape[0]
    return pl.pallas_call(
        _inproj_kernel,
        out_shape=jax.ShapeDtypeStruct((B, S_ALL, D_INP), F32),
        grid=(B, S_ALL // IN_TM, D_INP // IN_TN),
        in_specs=[pl.BlockSpec((1, IN_TM, D_MODEL), lambda b, i, j: (b, i, 0)),
                  pl.BlockSpec((1, D_MODEL), lambda b, i, j: (0, 0)),
                  pl.BlockSpec((1, 4, D_MODEL), lambda b, i, j: (b, 0, 0)),
                  pl.BlockSpec((D_MODEL, IN_TN), lambda b, i, j: (0, j))],
        out_specs=pl.BlockSpec((1, IN_TM, IN_TN), lambda b, i, j: (b, i, j)),
        scratch_shapes=[pltpu.VMEM((IN_TM, D_MODEL), BF16)],
        compiler_params=_cparams(3),
        name="inproj",
    )(xs, norm_w.reshape(1, D_MODEL), mod4, w_in_p)


OUT_TM = 544
OUT_TN = 1024


def _outproj_kernel(a0_ref, a1_ref, a2_ref, a3_ref, w_ref, x_ref, g_ref, o_ref):
    i = pl.program_id(1)
    y = None
    for n, a_ref in enumerate((a0_ref, a1_ref, a2_ref, a3_ref)):
        t = jnp.dot(a_ref[0], w_ref[n * D_BRANCH:(n + 1) * D_BRANCH, :], preferred_element_type=F32)
        y = t if y is None else y + t
    row = i * OUT_TM + lax.broadcasted_iota(jnp.int32, (OUT_TM, 1), 0)
    g = g_ref[0]
    gate = jnp.where(row >= SEQ, g[1:2], g[0:1])
    o_ref[0] = x_ref[0] + gate * y


def _outproj(branches, w_out_b, xs, gate2):
    B = xs.shape[0]
    a_spec = pl.BlockSpec((1, OUT_TM, D_BRANCH), lambda b, i, j: (b, i, 0))
    return pl.pallas_call(
        _outproj_kernel,
        out_shape=jax.ShapeDtypeStruct((B, S_ALL, D_MODEL), F32),
        grid=(B, S_ALL // OUT_TM, D_MODEL // OUT_TN),
        in_specs=[a_spec, a_spec, a_spec, a_spec,
                  pl.BlockSpec((D_MIX, OUT_TN), lambda b, i, j: (0, j)),
                  pl.BlockSpec((1, OUT_TM, OUT_TN), lambda b, i, j: (b, i, j)),
                  pl.BlockSpec((1, 2, OUT_TN), lambda b, i, j: (b, 0, j))],
        out_specs=pl.BlockSpec((1, OUT_TM, OUT_TN), lambda b, i, j: (b, i, j)),
        compiler_params=_cparams(3),
        name="outproj",
    )(*branches, w_out_b, xs, gate2)


NA_TM = 544
NA_RB = 4
NA_QB = NA_RB * GRID_W
NA_KR = 12
NA_KW = NA_KR * GRID_W
NA_NBLK = GRID_H // NA_RB
assert NA_QB == CTX_LEN


def _na_prep_kernel(q_ref, k_ref, v_ref, qn_ref, kn_ref, qo_ref, ko_ref, vo_ref):
    def headnorm(x, w, extra):
        outs = []
        for h in range(NA_HEADS):
            xh = x[:, h * NA_DH:(h + 1) * NA_DH]
            ms = jnp.mean(xh * xh, axis=-1, keepdims=True)
            outs.append((xh * lax.rsqrt(ms + EPS) * w * extra).astype(BF16))
        return jnp.concatenate(outs, axis=-1)

    qo_ref[0] = headnorm(q_ref[0], qn_ref[...], NA_DH ** -0.5)
    ko_ref[0] = headnorm(k_ref[0], kn_ref[...], 1.0)
    vo_ref[0] = v_ref[0].astype(BF16)


def _na_prep(p, q_norm, k_norm):
    B = p.shape[0]
    c0 = _P_START['n_qkv'] // NA_W
    spec = lambda c: pl.BlockSpec((1, NA_TM, NA_W), lambda b, i, c=c: (b, i, c))
    ospec = pl.BlockSpec((1, NA_TM, NA_W), lambda b, i: (b, i, 0))
    wspec = pl.BlockSpec((1, NA_DH), lambda b, i: (0, 0))
    shp = jax.ShapeDtypeStruct((B, S_ALL, NA_W), BF16)
    return pl.pallas_call(
        _na_prep_kernel,
        out_shape=(shp, shp, shp),
        grid=(B, S_ALL // NA_TM),
        in_specs=[spec(c0), spec(c0 + 1), spec(c0 + 2), wspec, wspec],
        out_specs=(ospec, ospec, ospec),
        compiler_params=_cparams(2),
        name="na_prep",
    )(p, p, p, q_norm.reshape(1, NA_DH), k_norm.reshape(1, NA_DH))


def _na_kernel(q_ref, k_ref, v_ref, z_ref, bias_ref, o_ref):
    rb = pl.program_id(1)
    q = q_ref[0]
    z = z_ref[0]
    kc = k_ref[0, SEQ:S_ALL, :]
    vc = v_ref[0, SEQ:S_ALL, :]

    def finish(h, o, l):
        sl = slice(h * NA_DH, (h + 1) * NA_DH)
        o_ref[0, :, sl] = (o / l * _silu(z[:, sl])).astype(BF16)

    @pl.when(rb < NA_NBLK)
    def _latent():
        base = jnp.clip(rb * NA_RB - NA_RB, 0, GRID_H - NA_KR)
        start = pl.multiple_of(base * GRID_W, GRID_W)
        kw = k_ref[0, pl.ds(start, NA_KW), :]
        vw = v_ref[0, pl.ds(start, NA_KW), :]
        for h in range(NA_HEADS):
            sl = slice(h * NA_DH, (h + 1) * NA_DH)
            s_w = _dot_nt(q[:, sl], kw[:, sl]) + bias_ref[0, h]
            s_c = _dot_nt(q[:, sl], kc[:, sl])
            m = jnp.maximum(jnp.max(s_w, axis=-1, keepdims=True), jnp.max(s_c, axis=-1, keepdims=True))
            p_w = jnp.exp(s_w - m)
            p_c = jnp.exp(s_c - m)
            l = jnp.sum(p_w, axis=-1, keepdims=True) + jnp.sum(p_c, axis=-1, keepdims=True)
            o = (jnp.dot(p_w.astype(BF16), vw[:, sl], preferred_element_type=F32)
                 + jnp.dot(p_c.astype(BF16), vc[:, sl], preferred_element_type=F32))
            finish(h, o, l)

    @pl.when(rb == NA_NBLK)
    def _context():
        for h in range(NA_HEADS):
            sl = slice(h * NA_DH, (h + 1) * NA_DH)
            s_c = _dot_nt(q[:, sl], kc[:, sl])
            m = jnp.max(s_c, axis=-1, keepdims=True)
            p_c = jnp.exp(s_c - m)
            l = jnp.sum(p_c, axis=-1, keepdims=True)
            o = jnp.dot(p_c.astype(BF16), vc[:, sl], preferred_element_type=F32)
            finish(h, o, l)


def _na_bias_index():
    dr = np.zeros((3, NA_RB, NA_KR), np.int64)
    ok = np.zeros((3, NA_RB, NA_KR), bool)
    for ci, rb in enumerate((0, 1, NA_NBLK - 1)):
        base = int(np.clip(rb * NA_RB - NA_RB, 0, GRID_H - NA_KR))
        qr = rb * NA_RB + np.arange(NA_RB)[:, None]
        kr = base + np.arange(NA_KR)[None, :]
        row0 = np.clip(qr - NA_WIN_R // 2, 0, GRID_H - NA_WIN_R)
        ok[ci] = (kr >= row0) & (kr < row0 + NA_WIN_R)
        dr[ci] = np.clip(kr - qr + NA_WIN_R - 1, 0, 2 * NA_WIN_R - 2)
    qc = np.arange(GRID_W)[:, None]
    kc = np.arange(GRID_W)[None, :]
    win0 = np.clip(qc - NA_WIN_C // 2, 0, GRID_W - NA_WIN_C)
    col_ok = (kc >= win0) & (kc < win0 + NA_WIN_C)
    return dr, ok, col_ok


def _na_bias(rpb):
    L, H = rpb.shape[:2]
    nd = 2 * NA_WIN_R - 1
    dr, ok, col_ok = _na_bias_index()
    left = GRID_W - NA_WIN_C
    f = jnp.pad(rpb, ((0, 0), (0, 0), (0, 0), (left, 2 * GRID_W - (2 * NA_WIN_C - 1) - left)))
    skew = jnp.broadcast_to(f[:, :, :, None, :], (L, H, nd, GRID_W, 2 * GRID_W))
    skew = skew.reshape(L, H, nd, -1)[..., :GRID_W * (2 * GRID_W - 1)].reshape(L, H, nd, GRID_W, 2 * GRID_W - 1)
    toe = skew[..., GRID_W - 1:]
    toe = jnp.where(col_ok, toe, NEG_INF)
    dead = jnp.full((L, H, GRID_W, GRID_W), NEG_INF, F32)
    blocks = [toe[:, :, dr[c, a, b]] if ok[c, a, b] else dead
              for c in range(3) for a in range(NA_RB) for b in range(NA_KR)]
    t = jnp.stack(blocks, axis=2).reshape(L, H, 3, NA_RB, NA_KR, GRID_W, GRID_W)
    return jnp.transpose(t, (0, 2, 1, 3, 5, 4, 6)).reshape(L, 3, H, NA_QB, NA_KW)


def _na_attend(qn, kn, vn, p, bias):
    B = p.shape[0]
    zc = _P_START['n_z'] // NA_W
    last = NA_NBLK - 1
    blk = pl.BlockSpec((1, NA_QB, NA_W), lambda b, r: (b, r, 0))
    full = pl.BlockSpec((1, S_ALL, NA_W), lambda b, r: (b, 0, 0))
    return pl.pallas_call(
        _na_kernel,
        out_shape=jax.ShapeDtypeStruct((B, S_ALL, NA_W), BF16),
        grid=(B, NA_NBLK + 1),
        in_specs=[blk, full, full,
                  pl.BlockSpec((1, NA_QB, NA_W), lambda b, r: (b, r, zc)),
                  pl.BlockSpec((1, NA_HEADS, NA_QB, NA_KW),
                               lambda b, r: (jnp.where(r == 0, 0, jnp.where(r >= last, 2, 1)), 0, 0, 0))],
        out_specs=blk,
        compiler_params=_cparams(2),
        name="na_attend",
    )(qn, kn, vn, p, bias)


MP_TM = 544
MLA_HW = 2 * LANES
MLA_TQ = 512
MLA_TK = 512


def _rope_tables():
    n_freq = MLA_ROPE // 4
    inv_freq = ROPE_THETA ** (-np.arange(n_freq, dtype=np.float64) / n_freq)
    t = np.arange(SEQ)
    ar = (t // GRID_W)[:, None] * inv_freq
    ac = (t % GRID_W)[:, None] * inv_freq
    cos = np.concatenate([np.cos(ar), np.cos(ar), np.cos(ac), np.cos(ac)], axis=1)
    sin = np.concatenate([-np.sin(ar), np.sin(ar), -np.sin(ac), np.sin(ac)], axis=1)
    cos = np.concatenate([cos, np.ones((CTX_LEN, MLA_ROPE))], axis=0)
    sin = np.concatenate([sin, np.zeros((CTX_LEN, MLA_ROPE))], axis=0)
    return (np.tile(cos, (1, MLA_HEADS)).astype(np.float32), np.tile(sin, (1, MLA_HEADS)).astype(np.float32))


def _rope_rotate(t, cos, sin):
    w = t.shape[1]
    lane = lax.broadcasted_iota(jnp.int32, (1, w), 1)
    first = (lane & 31) < 16
    up = pltpu.roll(t, w - 16, axis=1)
    dn = pltpu.roll(t, 16, axis=1)
    return t * cos + jnp.where(first, up, dn) * sin


def _mla_prep_kernel(p_ref, cos_ref, sin_ref, qan_ref, wuq_ref, kvan_ref, wukv_ref, qn_ref, kn_ref,
                     q_out, k_out, v_out):
    x = p_ref[0]
    cq = x[:, :MLA_Q_RANK]
    ckv = x[:, MLA_Q_RANK:MLA_Q_RANK + MLA_KV_RANK]
    kr2 = x[:, MLA_Q_RANK + MLA_KV_RANK:]

    def rms(t, w):
        return t * lax.rsqrt(jnp.mean(t * t, axis=-1, keepdims=True) + EPS) * w

    qf = jnp.dot(rms(cq, qan_ref[...]).astype(BF16), wuq_ref[...], preferred_element_type=F32)
    kvf = jnp.dot(rms(ckv, kvan_ref[...]).astype(BF16), wukv_ref[...], preferred_element_type=F32)
    cos = cos_ref[...]
    sin = sin_ref[...]
    qw = qn_ref[...]
    kw = kn_ref[...]
    n_all = MLA_HEADS * MLA_NOPE
    lane = lax.broadcasted_iota(jnp.int32, (1, LANES), 1)
    halves = (lane < MLA_ROPE, lane >= MLA_ROPE)

    q_rope = qf[:, n_all:]
    q_rope_sq = q_rope * q_rope
    q_rot = _rope_rotate(q_rope * qw[:, n_all:], cos, sin)
    kr_sq = jnp.sum(jnp.where(halves[0], kr2 * kr2, 0.0), axis=-1, keepdims=True)
    k_rot = _rope_rotate(kr2 * kw[:, n_all:], cos[:, :LANES], sin[:, :LANES])
    for h in range(MLA_HEADS):
        half = halves[h % 2]
        vsl = slice((h // 2) * LANES, (h // 2 + 1) * LANES)
        nsl = slice(h * MLA_NOPE, (h + 1) * MLA_NOPE)
        q_nope = qf[:, nsl]
        ss = (jnp.sum(q_nope * q_nope, axis=-1, keepdims=True)
              + jnp.sum(jnp.where(half, q_rope_sq[:, vsl], 0.0), axis=-1, keepdims=True))
        r = lax.rsqrt(ss * (1.0 / MLA_QK) + EPS) * (MLA_QK ** -0.5)
        q_out[0, :, h * MLA_HW:h * MLA_HW + LANES] = (q_nope * qw[:, nsl] * r).astype(BF16)
        q_out[0, :, h * MLA_HW + LANES:(h + 1) * MLA_HW] = (jnp.where(half, q_rot[:, vsl], 0.0) * r).astype(BF16)
        k_nope = kvf[:, nsl]
        ss = jnp.sum(k_nope * k_nope, axis=-1, keepdims=True) + kr_sq
        r = lax.rsqrt(ss * (1.0 / MLA_QK) + EPS)
        k_out[0, :, h * MLA_HW:h * MLA_HW + LANES] = (k_nope * kw[:, nsl] * r).astype(BF16)
        k_out[0, :, h * MLA_HW + LANES:(h + 1) * MLA_HW] = (jnp.where(half, k_rot, 0.0) * r).astype(BF16)
    v_out[0] = kvf[:, n_all:].astype(BF16)


def _mla_prep(p, cos, sin, qa_norm, w_uq, kva_norm, w_ukv, q_norm, k_norm):
    B = p.shape[0]
    H = MLA_HEADS
    uq = w_uq.reshape(MLA_Q_RANK, H, MLA_QK)
    uq = jnp.concatenate([uq[:, :, :MLA_NOPE].reshape(MLA_Q_RANK, -1),
                          uq[:, :, MLA_NOPE:].reshape(MLA_Q_RANK, -1)], axis=1).astype(BF16)
    ukv = w_ukv.reshape(MLA_KV_RANK, H, MLA_NOPE + MLA_V)
    ukv = jnp.concatenate([ukv[:, :, :MLA_NOPE].reshape(MLA_KV_RANK, -1),
                           ukv[:, :, MLA_NOPE:].reshape(MLA_KV_RANK, -1)], axis=1).astype(BF16)
    qn = jnp.concatenate([jnp.tile(q_norm[:MLA_NOPE], H), jnp.tile(q_norm[MLA_NOPE:], H)]).reshape(1, -1)
    kn = jnp.concatenate([jnp.tile(k_norm[:MLA_NOPE], H), jnp.tile(k_norm[MLA_NOPE:], 2)]).reshape(1, -1)
    pc = _P_START['m_q'] // P_MLA_BLK
    const = lambda shape: pl.BlockSpec(shape, lambda b, i: (0, 0))
    rows = lambda w: pl.BlockSpec((MP_TM, w), lambda b, i: (i, 0))
    outs = lambda w: pl.BlockSpec((1, MP_TM, w), lambda b, i: (b, i, 0))
    return pl.pallas_call(
        _mla_prep_kernel,
        out_shape=(jax.ShapeDtypeStruct((B, S_ALL, H * MLA_HW), BF16),
                   jax.ShapeDtypeStruct((B, S_ALL, H * MLA_HW), BF16),
                   jax.ShapeDtypeStruct((B, S_ALL, MLA_W), BF16)),
        grid=(B, S_ALL // MP_TM),
        in_specs=[pl.BlockSpec((1, MP_TM, P_MLA_BLK), lambda b, i: (b, i, pc)),
                  rows(H * MLA_ROPE), rows(H * MLA_ROPE),
                  const((1, MLA_Q_RANK)), const(uq.shape), const((1, MLA_KV_RANK)), const(ukv.shape),
                  const(qn.shape), const(kn.shape)],
        out_specs=(outs(H * MLA_HW), outs(H * MLA_HW), outs(MLA_W)),
        compiler_params=_cparams(2),
        name="mla_prep",
    )(p, cos, sin, qa_norm.reshape(1, -1), uq, kva_norm.reshape(1, -1), ukv, qn, kn)


def _mla_attn_kernel(q_ref, k_ref, v_ref, z_ref, o_ref, *, ctx_start, n_lat_chunks):
    q = q_ref[0]
    tq = q.shape[0]

    def chunk(carry, kc, vc):
        m, l, acc = carry
        s = _dot_nt(q, kc)
        m_new = jnp.maximum(m, jnp.max(s, axis=-1, keepdims=True))
        a = jnp.exp(m - m_new)
        p = jnp.exp(s - m_new)
        l = a * l + jnp.sum(p, axis=-1, keepdims=True)
        acc = a * acc + jnp.dot(p.astype(BF16), vc, preferred_element_type=F32)
        return m_new, l, acc

    carry = (jnp.full((tq, 1), NEG_INF, F32), jnp.zeros((tq, 1), F32), jnp.zeros((tq, MLA_V), F32))
    carry = chunk(carry, k_ref[0, ctx_start:ctx_start + CTX_LEN, :], v_ref[0, ctx_start:ctx_start + CTX_LEN, :])

    for i in range(n_lat_chunks):
        carry = chunk(carry, k_ref[0, i * MLA_TK:(i + 1) * MLA_TK, :], v_ref[0, i * MLA_TK:(i + 1) * MLA_TK, :])
    _, l, acc = carry
    o_ref[0] = (acc / l * _silu(z_ref[0])).astype(BF16)


def _mla_attend(q, k, v, p):
    B = p.shape[0]
    H = MLA_HEADS
    zc = _P_START['m_z'] // MLA_V
    ctx_blk = SEQ // CTX_LEN

    def call(tq, q_blk0, n_q, key_rows, key_blk, kern, name):
        return pl.pallas_call(
            kern,
            out_shape=jax.ShapeDtypeStruct((B, n_q * tq, MLA_W), BF16),
            grid=(B, H, n_q),
            in_specs=[pl.BlockSpec((1, tq, MLA_HW), lambda b, h, i: (b, q_blk0 + i, h)),
                      pl.BlockSpec((1, key_rows, MLA_HW), lambda b, h, i: (b, key_blk, h)),
                      pl.BlockSpec((1, key_rows, MLA_V), lambda b, h, i: (b, key_blk, h)),
                      pl.BlockSpec((1, tq, MLA_V), lambda b, h, i: (b, q_blk0 + i, zc + h))],
            out_specs=pl.BlockSpec((1, tq, MLA_V), lambda b, h, i: (b, i, h)),
            compiler_params=_cparams(3),
            name=name,
        )(q, k, v, p)

    lat = call(MLA_TQ, 0, SEQ // MLA_TQ, S_ALL, 0,
               functools.partial(_mla_attn_kernel, ctx_start=SEQ, n_lat_chunks=SEQ // MLA_TK), "mla_attend")
    ctx = call(CTX_LEN, ctx_blk, 1, CTX_LEN, ctx_blk,
               functools.partial(_mla_attn_kernel, ctx_start=0, n_lat_chunks=0), "mla_attend_ctx")
    return lat, ctx


SCAN_TILE = 256
SCAN_NT = S_ALL // SCAN_TILE
CTX_TILE = SEQ // SCAN_TILE
HALO = 8


def _split_dot(m_bf16, x):
    hi = x.astype(BF16)
    lo = (x - hi.astype(F32)).astype(BF16)
    return (jnp.dot(m_bf16, hi, preferred_element_type=F32)
            + jnp.dot(m_bf16, lo, preferred_element_type=F32))


def _softplus(t):
    return jnp.maximum(t, 0.0) + jnp.log1p(jnp.exp(-jnp.abs(t)))


def _conv3_silu(x, prev_row, next_row, w, bias=None):
    n = x.shape[0]
    row = lax.broadcasted_iota(jnp.int32, (n, 1), 0)
    xp = jnp.where(row == 0, prev_row, pltpu.roll(x, 1, axis=0))
    xn = jnp.where(row == n - 1, next_row, pltpu.roll(x, n - 1, axis=0))
    y = xp * w[0:1] + x * w[1:2] + xn * w[2:3]
    if bias is not None:
        y = y + bias
    return _silu(y)


def _halo_rows(i, prev_ref, next_ref):
    pv = jnp.where((i == 0) | (i == CTX_TILE), 0.0, 1.0)
    nv = jnp.where((i == CTX_TILE - 1) | (i == SCAN_NT - 1), 0.0, 1.0)
    return prev_ref[0, HALO - 1:HALO, :] * pv, next_ref[0, 0:1, :] * nv


def _halo_specs(width, col_blk):
    rb = SCAN_TILE // HALO
    nblk = S_ALL // HALO
    return [pl.BlockSpec((1, SCAN_TILE, width), lambda b, i: (b, i, col_blk)),
            pl.BlockSpec((1, HALO, width), lambda b, i: (b, jnp.maximum(i * rb - 1, 0), col_blk)),
            pl.BlockSpec((1, HALO, width), lambda b, i: (b, jnp.minimum((i + 1) * rb, nblk - 1), col_blk))]


def _fwd_tile(t):
    return jnp.where(t == 0, CTX_TILE, t - 1)


def _bwd_tile(t):
    return jnp.where(t == 0, CTX_TILE, CTX_TILE - t)


def _chunk_masks(n):
    i = lax.broadcasted_iota(jnp.int32, (n, 1), 0)
    j = lax.broadcasted_iota(jnp.int32, (1, n), 1)
    same = (i // GDN_CHUNK) == (j // GDN_CHUNK)
    return i, j, same


def _gdn_prep_kernel(x_ref, prev_ref, next_ref, s_ref, cw_ref, rate_ref, dtb_ref,
                     q_out, k_out, v_out, g_out):
    i = pl.program_id(1)
    prev_row, next_row = _halo_rows(i, prev_ref, next_ref)
    y = _conv3_silu(x_ref[0], prev_row, next_row, cw_ref[...])
    for h in range(GDN_HEADS):
        sl = slice(h * GDN_DK, (h + 1) * GDN_DK)
        qh = y[:, sl]
        q_out[0, :, sl] = (qh * lax.rsqrt(jnp.sum(qh * qh, axis=-1, keepdims=True) + EPS)
                           * (GDN_DK ** -0.5)).astype(BF16)
        kh = y[:, GDN_W + h * GDN_DK:GDN_W + (h + 1) * GDN_DK]
        k_out[0, :, sl] = (kh * lax.rsqrt(jnp.sum(kh * kh, axis=-1, keepdims=True) + EPS)).astype(BF16)
    v_out[0] = y[:, 2 * GDN_W:].astype(BF16)

    s = s_ref[0]
    lane = lax.broadcasted_iota(jnp.int32, (1, LANES), 1)
    nh2 = 2 * GDN_HEADS
    beta = jax.nn.sigmoid(s)
    g = -rate_ref[...] * _softplus(s + dtb_ref[...])
    g = jnp.where((lane >= nh2) & (lane < 2 * nh2), g, 0.0)
    ii, jj, same = _chunk_masks(SCAN_TILE)
    one = lambda m: jnp.where(m, 1.0, 0.0).astype(BF16)
    fwd_lane = lane < nh2 + GDN_HEADS
    gam = jnp.where(fwd_lane, _split_dot(one(same & (jj <= ii)), g), _split_dot(one(same & (jj >= ii)), g))
    rem = jnp.where(fwd_lane, _split_dot(one(same & (jj > ii)), g), _split_dot(one(same & (jj < ii)), g))
    cf = jnp.where(lane < nh2, beta, jnp.where(lane < 2 * nh2, gam, pltpu.roll(rem, nh2, axis=1)))
    tr = cf.T
    for h in range(GDN_HEADS):
        for r, src in enumerate((h, GDN_HEADS + h, nh2 + h, nh2 + GDN_HEADS + h,
                                 2 * nh2 + h, 2 * nh2 + GDN_HEADS + h)):
            g_out[0, h, r:r + 1, :] = tr[src:src + 1, :]
        g_out[0, h, 6:8, :] = jnp.zeros((2, SCAN_TILE), F32)


def _gdn_prep(p, conv_w, A_log, dt_bias):
    B = p.shape[0]
    W3 = 3 * GDN_W
    nh2 = 2 * GDN_HEADS
    rate = jnp.zeros((1, LANES), F32).at[0, nh2:2 * nh2].set(jnp.exp(A_log).reshape(-1))
    dtb = jnp.zeros((1, LANES), F32).at[0, nh2:2 * nh2].set(dt_bias.reshape(-1))
    shp = jax.ShapeDtypeStruct((B, S_ALL, GDN_W), BF16)
    ospec = pl.BlockSpec((1, SCAN_TILE, GDN_W), lambda b, i: (b, i, 0))
    const = lambda shape: pl.BlockSpec(shape, lambda b, i: (0, 0))
    return pl.pallas_call(
        _gdn_prep_kernel,
        out_shape=(shp, shp, shp, jax.ShapeDtypeStruct((B, GDN_HEADS, 8, S_ALL), F32)),
        grid=(B, SCAN_NT),
        in_specs=_halo_specs(W3, _P_START['g_qkv'] // W3)
        + [pl.BlockSpec((1, SCAN_TILE, LANES), lambda b, i: (b, i, P_SMALL_BLK)),
           const((SHORT_CONV, W3)), const((1, LANES)), const((1, LANES))],
        out_specs=(ospec, ospec, ospec,
                   pl.BlockSpec((1, GDN_HEADS, 8, SCAN_TILE), lambda b, i: (b, 0, 0, i))),
        compiler_params=_cparams(2),
        name="gdn_prep",
    )(p, p, p, p, conv_w, rate, dtb)


def _gdn_dir(q, k, v, gr, s_ref, o_ref, d):
    n = SCAN_TILE
    cf = jnp.concatenate([gr, jnp.zeros((LANES - 8, n), F32)], axis=0).T
    beta, gam_c, rem_c = cf[:, d:d + 1], cf[:, 2 + d:3 + d], cf[:, 4 + d:5 + d]
    gam_r = gr[2 + d:3 + d, :]
    ii, jj, same = _chunk_masks(n)
    incl = same & ((jj <= ii) if d == 0 else (jj >= ii))
    strict = same & ((jj < ii) if d == 0 else (jj > ii))
    kk = _dot_nt(k, k)
    yield
    qk = _dot_nt(q, k)
    yield
    dec = jnp.exp(jnp.where(incl, gam_c - gam_r, NEG_INF))
    a = jnp.where(strict, beta * kk * dec, 0.0)
    qkd = (qk * dec).astype(BF16)
    kf = k.astype(F32)
    x = jnp.concatenate([v.astype(F32) * beta, kf * (beta * jnp.exp(gam_c))], axis=1)
    pb = a.astype(BF16)
    x = x - jnp.dot(pb, x.astype(BF16), preferred_element_type=F32)
    yield
    for _ in range(5):
        pb = jnp.dot(pb, pb, preferred_element_type=F32).astype(BF16)
        yield
        x = x + jnp.dot(pb, x.astype(BF16), preferred_element_type=F32)
        yield
    u, w =x[:, :GDN_DV], x[:, GDN_DV:].astype(BF16)
    qd = (q.astype(F32) * jnp.exp(gam_c)).astype(BF16)
    kd = (kf * jnp.exp(rem_c)).astype(BF16)
    s = s_ref[...]
    nchunk = n // GDN_CHUNK
    v_new = [None] * nchunk
    qs = [None] * nchunk
    for c in (range(nchunk) if d == 0 else reversed(range(nchunk))):
        rows = slice(c * GDN_CHUNK, (c + 1) * GDN_CHUNK)
        r1 = jnp.dot(jnp.concatenate([w[rows], qd[rows]], axis=0), s.astype(BF16), preferred_element_type=F32)
        yield
        vn = u[rows] - r1[:GDN_CHUNK]
        qs[c] = r1[GDN_CHUNK:]
        v_new[c] = vn
        last = (c + 1) * GDN_CHUNK - 1 if d == 0 else c * GDN_CHUNK
        s = s * jnp.exp(cf[last:last + 1, 2 + d:3 + d]) + _dot_tn(kd[rows], vn.astype(BF16))
        yield
    s_ref[...] = s
    vn_all = jnp.concatenate(v_new, axis=0).astype(BF16)
    o_ref[0] = jnp.concatenate(qs, axis=0) + jnp.dot(qkd, vn_all, preferred_element_type=F32)


def _round_robin(gens):
    gens = list(gens)
    while gens:
        alive = []
        for g in gens:
            try:
                next(g)
                alive.append(g)
            except StopIteration:
                pass
        gens = alive


def _gdn_scan_kernel(qf, kf, vf, gf, qb, kb, vb, gb, of_ref, ob_ref, sf_ref, sb_ref):
    @pl.when(pl.program_id(1) == 0)
    def _():
        sf_ref[...] = jnp.zeros_like(sf_ref)
        sb_ref[...] = jnp.zeros_like(sb_ref)

    chains = []
    for h in range(GDN_HEADS):
        sl = slice(h * GDN_DK, (h + 1) * GDN_DK)
        chains.append(_gdn_dir(qf[0, :, sl], kf[0, :, sl], vf[0, :, sl], gf[0, h],
                               sf_ref.at[h], of_ref.at[:, :, sl], 0))
        chains.append(_gdn_dir(qb[0, :, sl], kb[0, :, sl], vb[0, :, sl], gb[0, h],
                               sb_ref.at[h], ob_ref.at[:, :, sl], 1))
    _round_robin(chains)


def _gdn_scan(q, k, v, g):
    B = q.shape[0]
    tok = lambda order: pl.BlockSpec((1, SCAN_TILE, GDN_W), lambda b, t: (b, order(t), 0))
    gsp = lambda order: pl.BlockSpec((1, GDN_HEADS, 8, SCAN_TILE), lambda b, t: (b, 0, 0, order(t)))
    shp = jax.ShapeDtypeStruct((B, S_ALL, GDN_W), F32)
    f, r = _fwd_tile, _bwd_tile
    state = pltpu.VMEM((GDN_HEADS, GDN_DK, GDN_DV), F32)
    return pl.pallas_call(
        _gdn_scan_kernel,
        out_shape=(shp, shp),
        grid=(B, SCAN_NT),
        in_specs=[tok(f), tok(f), tok(f), gsp(f), tok(r), tok(r), tok(r), gsp(r)],
        out_specs=(tok(f), tok(r)),
        scratch_shapes=[state, state],
        compiler_params=_cparams(2),
        name="gdn_scan",
    )(q, k, v, g, q, k, v, g)


GO_TM = 544


def _gdn_out_kernel(of_ref, ob_ref, z_ref, nw_ref, o_ref):
    o = of_ref[0] + ob_ref[0]
    z = z_ref[0]
    for h in range(GDN_HEADS):
        sl = slice(h * GDN_DV, (h + 1) * GDN_DV)
        oh = o[:, sl]
        y = oh * lax.rsqrt(jnp.mean(oh * oh, axis=-1, keepdims=True) + EPS) * nw_ref[...]
        o_ref[0, :, sl] = (y * _silu(z[:, sl])).astype(BF16)


def _gdn_out(o_f, o_b, p, norm_w):
    B = p.shape[0]
    zc = _P_START['g_z'] // GDN_W
    blk = pl.BlockSpec((1, GO_TM, GDN_W), lambda b, i: (b, i, 0))
    return pl.pallas_call(
        _gdn_out_kernel,
        out_shape=jax.ShapeDtypeStruct((B, S_ALL, GDN_W), BF16),
        grid=(B, S_ALL // GO_TM),
        in_specs=[blk, blk, pl.BlockSpec((1, GO_TM, GDN_W), lambda b, i: (b, i, zc)),
                  pl.BlockSpec((1, GDN_DV), lambda b, i: (0, 0))],
        out_specs=blk,
        compiler_params=_cparams(2),
        name="gdn_out",
    )(o_f, o_b, p, norm_w.reshape(1, GDN_DV))


SSD_BC = SSM_GROUPS * SSM_STATE
SSD_HD = 2 * SSM_HEADS
SSD_GW = (SSM_HEADS // SSM_GROUPS) * SSM_HEADDIM


def _ssd_prep_kernel(x_ref, prev_ref, next_ref, s_ref, cw_ref, cb_ref, a_ref, dtb_ref,
                     xs_out, b_out, c_out, bt_out, cf_out, cr_out):
    i = pl.program_id(1)
    prev_row, next_row = _halo_rows(i, prev_ref, next_ref)
    y = _conv3_silu(x_ref[0], prev_row, next_row, cw_ref[...], cb_ref[...])
    xs_out[0] = y[:, :SSM_W]
    b_out[0] = y[:, SSM_W:SSM_W + SSD_BC].astype(BF16)
    c_out[0] = y[:, SSM_W + SSD_BC:].astype(BF16)
    bt_out[0] = y[:, SSM_W:SSM_W + SSD_BC].T.astype(BF16)

    s = s_ref[0]
    lane = lax.broadcasted_iota(jnp.int32, (1, LANES), 1)
    dt = _softplus(s + dtb_ref[...])
    on = (lane >= SSD_HD) & (lane < 2 * SSD_HD)
    a = jnp.where(on, dt * a_ref[...], 0.0)
    n = SCAN_TILE
    ii = lax.broadcasted_iota(jnp.int32, (n, 1), 0)
    jj = lax.broadcasted_iota(jnp.int32, (1, n), 1)
    one = lambda m: jnp.where(m, 1.0, 0.0).astype(BF16)
    fwd_lane = lane < SSD_HD + SSM_HEADS
    cum = jnp.where(fwd_lane, _split_dot(one(jj <= ii), a), _split_dot(one(jj >= ii), a))
    rem = jnp.where(fwd_lane, _split_dot(one(jj > ii), a), _split_dot(one(jj < ii), a))
    cf = jnp.where(lane < SSD_HD, pltpu.roll(dt, LANES - SSD_HD, axis=1),
                   jnp.where(lane < 2 * SSD_HD, cum, pltpu.roll(rem, SSD_HD, axis=1)))
    cf_out[0] = cf
    cr_out[0] = cf.T[SSD_HD:2 * SSD_HD, :]


def _ssd_prep(p, conv_w, conv_b, A_log, dt_bias):
    B = p.shape[0]
    W = SSM_CONV_DIM
    a_vec = jnp.zeros((1, LANES), F32).at[0, SSD_HD:2 * SSD_HD].set(-jnp.exp(A_log).reshape(-1))
    dtb = jnp.zeros((1, LANES), F32).at[0, SSD_HD:2 * SSD_HD].set(dt_bias.reshape(-1))
    const = lambda shape: pl.BlockSpec(shape, lambda b, i: (0, 0))
    tok = lambda w: pl.BlockSpec((1, SCAN_TILE, w), lambda b, i: (b, i, 0))
    return pl.pallas_call(
        _ssd_prep_kernel,
        out_shape=(jax.ShapeDtypeStruct((B, S_ALL, SSM_W), F32),
                   jax.ShapeDtypeStruct((B, S_ALL, SSD_BC), BF16),
                   jax.ShapeDtypeStruct((B, S_ALL, SSD_BC), BF16),
                   jax.ShapeDtypeStruct((B, SSD_BC, S_ALL), BF16),
                   jax.ShapeDtypeStruct((B, S_ALL, LANES), F32),
                   jax.ShapeDtypeStruct((B, SSD_HD, S_ALL), F32)),
        grid=(B, SCAN_NT),
        in_specs=_halo_specs(W, _P_START['s_xbc'] // W)
        + [pl.BlockSpec((1, SCAN_TILE, LANES), lambda b, i: (b, i, P_SMALL_BLK)),
           const((SHORT_CONV, W)), const((1, W)), const((1, LANES)), const((1, LANES))],
        out_specs=(tok(SSM_W), tok(SSD_BC), tok(SSD_BC),
                   pl.BlockSpec((1, SSD_BC, SCAN_TILE), lambda b, i: (b, 0, i)),
                   tok(LANES),
                   pl.BlockSpec((1, SSD_HD, SCAN_TILE), lambda b, i: (b, 0, i))),
        compiler_params=_cparams(2),
        name="ssd_prep",
    )(p, p, p, p, conv_w, conv_b.reshape(1, W), a_vec, dtb)


def _per_head_lanes(cols):
    lane = lax.broadcasted_iota(jnp.int32, (1, LANES), 1)
    lo = lane < SSM_HEADDIM
    return jnp.concatenate([jnp.where(lo, cols[2 * j], cols[2 * j + 1]) for j in range(SSM_HEADS // 2)], axis=1)


def _ssd_dir(xs, bm, cm, bt, cf, cr, h_ref, y_ref, d):
    n = SCAN_TILE
    hd0 = d * SSM_HEADS
    col = lambda base, h: cf[:, base + hd0 + h:base + hd0 + h + 1]
    heads = range(SSM_HEADS)
    dt_x = _per_head_lanes([col(0, h) for h in heads])
    ecum_x = _per_head_lanes([jnp.exp(col(SSD_HD, h)) for h in heads])
    erem_x = _per_head_lanes([jnp.exp(col(2 * SSD_HD, h)) for h in heads])
    last = n - 1 if d == 0 else 0
    etot_x = _per_head_lanes([jnp.exp(cf[last:last + 1, SSD_HD + hd0 + h:SSD_HD + hd0 + h + 1]) for h in heads])
    xdt = xs * dt_x
    xdt_b = xdt.astype(BF16)
    xdec = (xdt * erem_x).astype(BF16)
    ii = lax.broadcasted_iota(jnp.int32, (n, 1), 0)
    jj = lax.broadcasted_iota(jnp.int32, (1, n), 1)
    causal = (jj <= ii) if d == 0 else (jj >= ii)
    lane = lax.broadcasted_iota(jnp.int32, (1, LANES), 1)
    lo = lane < SSM_HEADDIM
    hpg = SSM_HEADS // SSM_GROUPS
    y_parts = []
    for g in range(SSM_GROUPS):
        gs = slice(g * SSM_STATE, (g + 1) * SSM_STATE)
        xl = slice(g * SSD_GW, (g + 1) * SSD_GW)
        cb = _dot_nt(cm[:, gs], bm[:, gs])
        h_prev = h_ref[g]
        y_off = jnp.dot(cm[:, gs], h_prev.astype(BF16), preferred_element_type=F32) * ecum_x[:, xl]
        h_ref[g] = h_prev * etot_x[:, xl] + jnp.dot(bt[gs, :], xdec[:, xl], preferred_element_type=F32)
        pair_out = []
        for j in range(hpg // 2):
            ys = []
            for e in range(2):
                h = g * hpg + 2 * j + e
                seg = cf[:, SSD_HD + hd0 + h:SSD_HD + hd0 + h + 1] - cr[hd0 + h:hd0 + h + 1, :]
                sc = (cb * jnp.exp(jnp.where(causal, seg, NEG_INF))).astype(BF16)
                xp = slice(g * SSD_GW + j * LANES, g * SSD_GW + (j + 1) * LANES)
                ys.append(jnp.dot(sc, xdt_b[:, xp], preferred_element_type=F32))
            pair_out.append(jnp.where(lo, ys[0], ys[1]))
        y_parts.append(jnp.concatenate(pair_out, axis=1) + y_off)
    y_ref[0] = jnp.concatenate(y_parts, axis=1)


def _ssd_scan_kernel(xf, bf, cf_, btf, colf, rowf, xb, bb, cb_, btb, colb, rowb, yf_ref, yb_ref, hf_ref, hb_ref):
    @pl.when(pl.program_id(1) == 0)
    def _():
        hf_ref[...] = jnp.zeros_like(hf_ref)
        hb_ref[...] = jnp.zeros_like(hb_ref)

    _ssd_dir(xf[0], bf[0], cf_[0], btf[0], colf[0], rowf[0], hf_ref, yf_ref, 0)
    _ssd_dir(xb[0], bb[0], cb_[0], btb[0], colb[0], rowb[0], hb_ref, yb_ref, 1)


def _ssd_scan(xs, bm, cm, bt, cf, cr):
    B = xs.shape[0]

    def specs(order):
        tok = lambda w: pl.BlockSpec((1, SCAN_TILE, w), lambda b, t: (b, order(t), 0))
        return [tok(SSM_W), tok(SSD_BC), tok(SSD_BC),
                pl.BlockSpec((1, SSD_BC, SCAN_TILE), lambda b, t: (b, 0, order(t))),
                tok(LANES),
                pl.BlockSpec((1, SSD_HD, SCAN_TILE), lambda b, t: (b, 0, order(t)))]

    shp = jax.ShapeDtypeStruct((B, S_ALL, SSM_W), F32)
    out = lambda order: pl.BlockSpec((1, SCAN_TILE, SSM_W), lambda b, t: (b, order(t), 0))
    hshape = pltpu.VMEM((SSM_GROUPS, SSM_STATE, SSD_GW), F32)
    args = (xs, bm, cm, bt, cf, cr)
    return pl.pallas_call(
        _ssd_scan_kernel,
        out_shape=(shp, shp),
        grid=(B, SCAN_NT),
        in_specs=specs(_fwd_tile) + specs(_bwd_tile),
        out_specs=(out(_fwd_tile), out(_bwd_tile)),
        scratch_shapes=[hshape, hshape],
        compiler_params=_cparams(2),
        name="ssd_scan",
    )(*args, *args)


SO_TM = 544


def _ssd_out_kernel(yf_ref, yb_ref, xs_ref, z_ref, d_ref, nw_ref, o_ref):
    y = yf_ref[0] + yb_ref[0] + d_ref[...] * xs_ref[0]
    y = y * _silu(z_ref[0])
    o_ref[0] = (y * lax.rsqrt(jnp.mean(y * y, axis=-1, keepdims=True) + EPS) * nw_ref[...]).astype(BF16)


def _ssd_out(y_f, y_b, xs, p, d_skip, norm_w):
    B = p.shape[0]
    zc = _P_START['s_z'] // SSM_W
    blk = pl.BlockSpec((1, SO_TM, SSM_W), lambda b, i: (b, i, 0))
    vec = pl.BlockSpec((1, SSM_W), lambda b, i: (0, 0))
    return pl.pallas_call(
        _ssd_out_kernel,
        out_shape=jax.ShapeDtypeStruct((B, S_ALL, SSM_W), BF16),
        grid=(B, S_ALL // SO_TM),
        in_specs=[blk, blk, blk, pl.BlockSpec((1, SO_TM, SSM_W), lambda b, i: (b, i, zc)), vec, vec],
        out_specs=blk,
        compiler_params=_cparams(2),
        name="ssd_out",
    )(y_f, y_b, xs, p, jnp.repeat(d_skip, SSM_HEADDIM).reshape(1, SSM_W), norm_w.reshape(1, SSM_W))


def _repack_w_in(w):
    cols = [w[:, _IN_START[n]:_IN_START[n] + _IN_SIZE[n]] for n in _P_ORDER]
    cols.append(jnp.zeros((w.shape[0], D_INP - _off), w.dtype))
    return jnp.concatenate(cols, axis=1).astype(BF16)


def kernel(x, c, ctx, c_ctx, norm_w, ada_w, ada_b, w_in, gdn_conv_w, gdn_A_log, gdn_dt_bias, gdn_norm_w, na_q_norm, na_k_norm, na_rpb, mla_qa_norm, mla_w_uq, mla_kva_norm, mla_w_ukv, mla_q_norm, mla_k_norm, ssm_conv_w, ssm_conv_b, ssm_A_log, ssm_dt_bias, ssm_D, ssm_norm_w, w_out):
    B = x.shape[0]
    xs = jnp.concatenate([x, ctx], axis=1)
    c8 = jnp.zeros((8, D_MODEL), F32).at[:B].set(c).at[B].set(c_ctx)
    mods = _ada_all(c8, ada_w, ada_b)
    cos_np, sin_np = _rope_tables()
    cos, sin = jnp.asarray(cos_np), jnp.asarray(sin_np)
    na_bias = _na_bias(na_rpb)
    for l in range(DEPTH):
        shift, scale, gate = jnp.split(mods[l, :B], 3, axis=-1)
        shift_c, scale_c, gate_c = jnp.split(mods[l, B], 3, axis=-1)
        bc = lambda v: jnp.broadcast_to(v[None], (B, D_MODEL))
        mod4 = jnp.stack([shift, scale, bc(shift_c), bc(scale_c)], axis=1)
        gate2 = jnp.stack([gate, bc(gate_c)], axis=1)
        p = _inproj(xs, norm_w[l], mod4, _repack_w_in(w_in[l]))

        gq, gk, gv, gg = _gdn_prep(p, gdn_conv_w[l], gdn_A_log[l], gdn_dt_bias[l])
        o_f, o_b = _gdn_scan(gq, gk, gv, gg)
        oa = _gdn_out(o_f, o_b, p, gdn_norm_w[l])

        qn, kn, vn = _na_prep(p, na_q_norm[l], na_k_norm[l])
        ob = _na_attend(qn, kn, vn, p, na_bias[l])

        mq, mk, mv = _mla_prep(p, cos, sin, mla_qa_norm[l], mla_w_uq[l], mla_kva_norm[l], mla_w_ukv[l],
                               mla_q_norm[l], mla_k_norm[l])
        oc_lat, oc_ctx = _mla_attend(mq, mk, mv, p)
        oc = jnp.concatenate([oc_lat, oc_ctx], axis=1)

        sx, sb, sc, sbt, scf, scr = _ssd_prep(p, ssm_conv_w[l], ssm_conv_b[l], ssm_A_log[l], ssm_dt_bias[l])
        y_f, y_b = _ssd_scan(sx, sb, sc, sbt, scf, scr)
        od = _ssd_out(y_f, y_b, sx, p, ssm_D[l], ssm_norm_w[l])
        xs = _outproj((oa, ob, oc, od), w_out[l].astype(BF16), xs, gate2)
    return xs[:, :SEQ]
```

```python
import functools

import jax
import jax.numpy as jnp
import numpy as np
from jax import lax
from jax.experimental import pallas as pl
from jax.experimental.pallas import tpu as pltpu

F32 = jnp.float32
BF16 = jnp.bfloat16

D_MODEL = 2048
BATCH = 4
SEQ = 4096
DEPTH = 4
GRID_W = 64
GRID_H = SEQ // GRID_W
CTX_LEN = 256
S_ALL = SEQ + CTX_LEN
EPS = 1e-6
NEG_INF = -1e30
LOG2E = 1.4426950408889634

D_BRANCH = 512
D_MIX = 4 * D_BRANCH
SHORT_CONV = 3

GDN_HEADS = 4
GDN_DK = 128
GDN_DV = 128
GDN_W = GDN_HEADS * GDN_DV
GDN_CHUNK = 64

NA_HEADS = 4
NA_DH = 128
NA_W = NA_HEADS * NA_DH
NA_WIN_R = 8
NA_WIN_C = 16

MLA_HEADS = 4
MLA_Q_RANK = 384
MLA_KV_RANK = 256
MLA_NOPE = 128
MLA_ROPE = 64
MLA_QK = MLA_NOPE + MLA_ROPE
MLA_V = 128
MLA_W = MLA_HEADS * MLA_V
ROPE_THETA = 10000.0

SSM_HEADDIM = 64
SSM_HEADS = D_BRANCH // SSM_HEADDIM
SSM_W = SSM_HEADS * SSM_HEADDIM
SSM_GROUPS = 2
SSM_STATE = 128
SSM_CONV_DIM = SSM_W + 2 * SSM_GROUPS * SSM_STATE

IN_SIZES = (3 * GDN_W, GDN_W, 2 * GDN_HEADS, 2 * GDN_HEADS,
            3 * NA_W, NA_W,
            MLA_Q_RANK, MLA_KV_RANK, MLA_ROPE, MLA_W,
            SSM_W, SSM_CONV_DIM, 2 * SSM_HEADS)
D_IN = sum(IN_SIZES)
_IN_NAMES = ('g_qkv', 'g_z', 'g_beta', 'g_alpha', 'n_qkv', 'n_z',
             'm_q', 'm_kv', 'm_kr', 'm_z', 's_z', 's_xbc', 's_dt')
_IN_START = dict(zip(_IN_NAMES, np.cumsum((0,) + IN_SIZES[:-1]).tolist()))
_IN_SIZE = dict(zip(_IN_NAMES, IN_SIZES))

LANES = 128
_P_ORDER = ('g_qkv', 'g_z', 'n_qkv', 'n_z', 'm_z', 's_z', 's_xbc',
            'm_q', 'm_kv', 'm_kr', 'm_kr')
_S_ORDER = ('g_beta', 'g_alpha', 's_dt')
_P_START = {}
_off = 0
for _n in _P_ORDER:
    _P_START.setdefault(_n, _off)
    _off += _IN_SIZE[_n]
D_INP = _off
P_MLA_BLK = MLA_Q_RANK + MLA_KV_RANK + 2 * MLA_ROPE
assert _P_START['m_q'] % P_MLA_BLK == 0 and D_INP % LANES == 0

VMEM_LIMIT = 52 * 1024 * 1024


def _silu(x):
    return x * jax.nn.sigmoid(x)


def _dot_nt(a, b):
    return lax.dot_general(a, b, (((1,), (1,)), ((), ())), preferred_element_type=F32)


def _dot_tn(a, b):
    return lax.dot_general(a, b, (((0,), (0,)), ((), ())), preferred_element_type=F32)


def _cparams(n_axes):
    return pltpu.CompilerParams(dimension_semantics=("arbitrary",) * n_axes,
                                vmem_limit_bytes=VMEM_LIMIT)


def _ada_kernel(c_ref, w_ref, b_ref, o_ref):
    a = _silu(c_ref[...]).astype(BF16)
    o_ref[0] = jnp.dot(a, w_ref[0].astype(BF16), preferred_element_type=F32) + b_ref[0]


def _ada_all(c8, ada_w, ada_b):
    tn = 1024
    L = ada_w.shape[0]
    n3 = ada_w.shape[2]
    return pl.pallas_call(
        _ada_kernel,
        out_shape=jax.ShapeDtypeStruct((L, 8, n3), F32),
        grid=(L, n3 // tn),
        in_specs=[pl.BlockSpec((8, D_MODEL), lambda l, j: (0, 0)),
                  pl.BlockSpec((1, D_MODEL, tn), lambda l, j: (l, 0, j)),
                  pl.BlockSpec((1, 1, tn), lambda l, j: (l, 0, j))],
        out_specs=pl.BlockSpec((1, 8, tn), lambda l, j: (l, 0, j)),
        compiler_params=_cparams(2),
        name="ada_mod",
    )(c8, ada_w, ada_b.reshape(L, 1, n3))


IN_TM = 1088
IN_TN = 768
IN_RC = 16


def _inproj_kernel(x_ref, nw_ref, mod_ref, w_ref, ws_ref, o_ref, os_ref, h_scr):
    i = pl.program_id(1)
    j = pl.program_id(2)

    @pl.when(j == 0)
    def _():
        m = mod_ref[0]
        nw = nw_ref[...]
        gain_l = nw * (1.0 + m[1:2])
        gain_c = nw * (1.0 + m[3:4])

        def body(r, carry):
            r0 = pl.multiple_of(r * IN_RC, IN_RC)
            x = x_ref[0, pl.ds(r0, IN_RC), :]
            ms = jnp.mean(x * x, axis=-1, keepdims=True)
            is_ctx = i * IN_TM + r0 >= SEQ
            gain = jnp.where(is_ctx, gain_c, gain_l)
            shift = jnp.where(is_ctx, m[2:3], m[0:1])
            h_scr[pl.ds(r0, IN_RC), :] = (x * lax.rsqrt(ms + EPS) * gain + shift).astype(BF16)
            return carry

        lax.fori_loop(0, IN_TM // IN_RC, body, 0, unroll=2)

    o_ref[0] = jnp.dot(h_scr[...], w_ref[...], preferred_element_type=F32).astype(BF16)

    @pl.when(j == pl.num_programs(2) - 1)
    def _():
        os_ref[0] = jnp.dot(h_scr[...], ws_ref[...], preferred_element_type=F32)


def _inproj(xs, norm_w, mod4, w_main, w_small):
    B = xs.shape[0]
    return pl.pallas_call(
        _inproj_kernel,
        out_shape=(jax.ShapeDtypeStruct((B, S_ALL, D_INP), BF16),
                   jax.ShapeDtypeStruct((B, S_ALL, LANES), F32)),
        grid=(B, S_ALL // IN_TM, D_INP // IN_TN),
        in_specs=[pl.BlockSpec((1, IN_TM, D_MODEL), lambda b, i, j: (b, i, 0)),
                  pl.BlockSpec((1, D_MODEL), lambda b, i, j: (0, 0)),
                  pl.BlockSpec((1, 4, D_MODEL), lambda b, i, j: (b, 0, 0)),
                  pl.BlockSpec((D_MODEL, IN_TN), lambda b, i, j: (0, j)),
                  pl.BlockSpec((D_MODEL, LANES), lambda b, i, j: (0, 0))],
        out_specs=(pl.BlockSpec((1, IN_TM, IN_TN), lambda b, i, j: (b, i, j)),
                   pl.BlockSpec((1, IN_TM, LANES), lambda b, i, j: (b, i, 0))),
        scratch_shapes=[pltpu.VMEM((IN_TM, D_MODEL), BF16)],
        compiler_params=_cparams(3),
        name="inproj",
    )(xs, norm_w.reshape(1, D_MODEL), mod4, w_main, w_small)


OUT_TM = 544
OUT_TM_LAST = 512


def _outproj_kernel(a0_ref, a1_ref, a2_ref, a3_ref, w_ref, x_ref, g_ref, o_ref, *, tm):
    i = pl.program_id(1)
    y = None
    for n, a_ref in enumerate((a0_ref, a1_ref, a2_ref, a3_ref)):
        t = jnp.dot(a_ref[0], w_ref[n * D_BRANCH:(n + 1) * D_BRANCH, :], preferred_element_type=F32)
        y = t if y is None else y + t
    row = i * tm + lax.broadcasted_iota(jnp.int32, (tm, 1), 0)
    g = g_ref[0]
    gate = jnp.where(row >= SEQ, g[1:2], g[0:1])
    o_ref[0] = x_ref[0] + gate * y


def _outproj(branches, w_out_b, xs, gate2, last):
    B = xs.shape[0]
    tm, rows = (OUT_TM_LAST, SEQ) if last else (OUT_TM, S_ALL)
    a_spec = pl.BlockSpec((1, tm, D_BRANCH), lambda b, i: (b, i, 0))
    x_spec = pl.BlockSpec((1, tm, D_MODEL), lambda b, i: (b, i, 0))
    return pl.pallas_call(
        functools.partial(_outproj_kernel, tm=tm),
        out_shape=jax.ShapeDtypeStruct((B, rows, D_MODEL), F32),
        grid=(B, rows // tm),
        in_specs=[a_spec, a_spec, a_spec, a_spec,
                  pl.BlockSpec((D_MIX, D_MODEL), lambda b, i: (0, 0)),
                  x_spec,
                  pl.BlockSpec((1, 2, D_MODEL), lambda b, i: (b, 0, 0))],
        out_specs=x_spec,
        compiler_params=_cparams(2),
        name="outproj",
    )(*branches, w_out_b, xs, gate2)


NA_TM = 544
NA_RB = 4
NA_QB = NA_RB * GRID_W
NA_KR = 12
NA_KW = NA_KR * GRID_W
NA_NBLK = GRID_H // NA_RB
assert NA_QB == CTX_LEN


def _na_prep_kernel(q_ref, k_ref, v_ref, qn_ref, kn_ref, qo_ref, ko_ref, vo_ref):
    def headnorm(x, w, extra):
        outs = []
        for h in range(NA_HEADS):
            xh = x[:, h * NA_DH:(h + 1) * NA_DH]
            ms = jnp.mean(xh * xh, axis=-1, keepdims=True)
            outs.append((xh * lax.rsqrt(ms + EPS) * w * extra).astype(BF16))
        return jnp.concatenate(outs, axis=-1)

    qo_ref[0] = headnorm(q_ref[0].astype(F32), qn_ref[...], NA_DH ** -0.5)
    ko_ref[0] = headnorm(k_ref[0].astype(F32), kn_ref[...], 1.0)
    vo_ref[0] = v_ref[0]


def _na_prep(p, q_norm, k_norm):
    B = p.shape[0]
    c0 = _P_START['n_qkv'] // NA_W
    spec = lambda c: pl.BlockSpec((1, NA_TM, NA_W), lambda b, i, c=c: (b, i, c))
    ospec = pl.BlockSpec((1, NA_TM, NA_W), lambda b, i: (b, i, 0))
    wspec = pl.BlockSpec((1, NA_DH), lambda b, i: (0, 0))
    shp = jax.ShapeDtypeStruct((B, S_ALL, NA_W), BF16)
    return pl.pallas_call(
        _na_prep_kernel,
        out_shape=(shp, shp, shp),
        grid=(B, S_ALL // NA_TM),
        in_specs=[spec(c0), spec(c0 + 1), spec(c0 + 2), wspec, wspec],
        out_specs=(ospec, ospec, ospec),
        compiler_params=_cparams(2),
        name="na_prep",
    )(p, p, p, q_norm.reshape(1, NA_DH), k_norm.reshape(1, NA_DH))


def _na_kernel(q_ref, k_ref, v_ref, z_ref, bias_ref, o_ref):
    rb = pl.program_id(1)
    q = q_ref[0]
    z = z_ref[0].astype(F32)
    kc = k_ref[0, SEQ:S_ALL, :]
    vc = v_ref[0, SEQ:S_ALL, :]

    def finish(h, o, l):
        sl = slice(h * NA_DH, (h + 1) * NA_DH)
        o_ref[0, :, sl] = (o / l * _silu(z[:, sl])).astype(BF16)

    @pl.when(rb < NA_NBLK)
    def _latent():
        base = jnp.clip(rb * NA_RB - NA_RB, 0, GRID_H - NA_KR)
        start = pl.multiple_of(base * GRID_W, GRID_W)
        kw = k_ref[0, pl.ds(start, NA_KW), :]
        vw = v_ref[0, pl.ds(start, NA_KW), :]
        for h in range(NA_HEADS):
            sl = slice(h * NA_DH, (h + 1) * NA_DH)
            s_w = _dot_nt(q[:, sl], kw[:, sl]) + bias_ref[0, h]
            s_c = _dot_nt(q[:, sl], kc[:, sl])
            m = jnp.maximum(jnp.max(s_w, axis=-1, keepdims=True), jnp.max(s_c, axis=-1, keepdims=True))
            p_w = jnp.exp(s_w - m)
            p_c = jnp.exp(s_c - m)
            l = jnp.sum(p_w, axis=-1, keepdims=True) + jnp.sum(p_c, axis=-1, keepdims=True)
            o = (jnp.dot(p_w.astype(BF16), vw[:, sl], preferred_element_type=F32)
                 + jnp.dot(p_c.astype(BF16), vc[:, sl], preferred_element_type=F32))
            finish(h, o, l)

    @pl.when(rb == NA_NBLK)
    def _context():
        for h in range(NA_HEADS):
            sl = slice(h * NA_DH, (h + 1) * NA_DH)
            s_c = _dot_nt(q[:, sl], kc[:, sl])
            m = jnp.max(s_c, axis=-1, keepdims=True)
            p_c = jnp.exp(s_c - m)
            l = jnp.sum(p_c, axis=-1, keepdims=True)
            o = jnp.dot(p_c.astype(BF16), vc[:, sl], preferred_element_type=F32)
            finish(h, o, l)


def _na_bias_index():
    dr = np.zeros((3, NA_RB, NA_KR), np.int64)
    ok = np.zeros((3, NA_RB, NA_KR), bool)
    for ci, rb in enumerate((0, 1, NA_NBLK - 1)):
        base = int(np.clip(rb * NA_RB - NA_RB, 0, GRID_H - NA_KR))
        qr = rb * NA_RB + np.arange(NA_RB)[:, None]
        kr = base + np.arange(NA_KR)[None, :]
        row0 = np.clip(qr - NA_WIN_R // 2, 0, GRID_H - NA_WIN_R)
        ok[ci] = (kr >= row0) & (kr < row0 + NA_WIN_R)
        dr[ci] = np.clip(kr - qr + NA_WIN_R - 1, 0, 2 * NA_WIN_R - 2)
    qc = np.arange(GRID_W)[:, None]
    kc = np.arange(GRID_W)[None, :]
    win0 = np.clip(qc - NA_WIN_C // 2, 0, GRID_W - NA_WIN_C)
    col_ok = (kc >= win0) & (kc < win0 + NA_WIN_C)
    return dr, ok, col_ok


def _na_bias(rpb):
    L, H = rpb.shape[:2]
    nd = 2 * NA_WIN_R - 1
    dr, ok, col_ok = _na_bias_index()
    left = GRID_W - NA_WIN_C
    f = jnp.pad(rpb, ((0, 0), (0, 0), (0, 0), (left, 2 * GRID_W - (2 * NA_WIN_C - 1) - left)))
    skew = jnp.broadcast_to(f[:, :, :, None, :], (L, H, nd, GRID_W, 2 * GRID_W))
    skew = skew.reshape(L, H, nd, -1)[..., :GRID_W * (2 * GRID_W - 1)].reshape(L, H, nd, GRID_W, 2 * GRID_W - 1)
    toe = skew[..., GRID_W - 1:]
    toe = jnp.where(col_ok, toe, NEG_INF)
    dead = jnp.full((L, H, GRID_W, GRID_W), NEG_INF, F32)
    blocks = [toe[:, :, dr[c, a, b]] if ok[c, a, b] else dead
              for c in range(3) for a in range(NA_RB) for b in range(NA_KR)]
    t = jnp.stack(blocks, axis=2).reshape(L, H, 3, NA_RB, NA_KR, GRID_W, GRID_W)
    return jnp.transpose(t, (0, 2, 1, 3, 5, 4, 6)).reshape(L, 3, H, NA_QB, NA_KW)


def _na_attend(qn, kn, vn, p, bias):
    B = p.shape[0]
    zc = _P_START['n_z'] // NA_W
    last = NA_NBLK - 1
    blk = pl.BlockSpec((1, NA_QB, NA_W), lambda b, r: (b, r, 0))
    full = pl.BlockSpec((1, S_ALL, NA_W), lambda b, r: (b, 0, 0))
    return pl.pallas_call(
        _na_kernel,
        out_shape=jax.ShapeDtypeStruct((B, S_ALL, NA_W), BF16),
        grid=(B, NA_NBLK + 1),
        in_specs=[blk, full, full,
                  pl.BlockSpec((1, NA_QB, NA_W), lambda b, r: (b, r, zc)),
                  pl.BlockSpec((1, NA_HEADS, NA_QB, NA_KW),
                               lambda b, r: (jnp.where(r == 0, 0, jnp.where(r >= last, 2, 1)), 0, 0, 0))],
        out_specs=blk,
        compiler_params=_cparams(2),
        name="na_attend",
    )(qn, kn, vn, p, bias)


MP_TM = 544
MLA_HW = 2 * LANES
MLA_TQ = 512
MLA_TK = 512


def _rope_tables():
    n_freq = MLA_ROPE // 4
    inv_freq = ROPE_THETA ** (-np.arange(n_freq, dtype=np.float64) / n_freq)
    t = np.arange(SEQ)
    ar = (t // GRID_W)[:, None] * inv_freq
    ac = (t % GRID_W)[:, None] * inv_freq
    cos = np.concatenate([np.cos(ar), np.cos(ar), np.cos(ac), np.cos(ac)], axis=1)
    sin = np.concatenate([-np.sin(ar), np.sin(ar), -np.sin(ac), np.sin(ac)], axis=1)
    cos = np.concatenate([cos, np.ones((CTX_LEN, MLA_ROPE))], axis=0)
    sin = np.concatenate([sin, np.zeros((CTX_LEN, MLA_ROPE))], axis=0)
    return (np.tile(cos, (1, MLA_HEADS)).astype(np.float32), np.tile(sin, (1, MLA_HEADS)).astype(np.float32))


def _rope_rotate(t, cos, sin):
    w = t.shape[1]
    lane = lax.broadcasted_iota(jnp.int32, (1, w), 1)
    first = (lane & 31) < 16
    up = pltpu.roll(t, w - 16, axis=1)
    dn = pltpu.roll(t, 16, axis=1)
    return t * cos + jnp.where(first, up, dn) * sin


def _mla_prep_kernel(p_ref, cos_ref, sin_ref, qan_ref, wuq_ref, kvan_ref, wukv_ref, qn_ref, kn_ref,
                     q_out, k_out, v_out):
    x = p_ref[0].astype(F32)
    cq = x[:, :MLA_Q_RANK]
    ckv = x[:, MLA_Q_RANK:MLA_Q_RANK + MLA_KV_RANK]
    kr2 = x[:, MLA_Q_RANK + MLA_KV_RANK:]

    def rms(t, w):
        return t * lax.rsqrt(jnp.mean(t * t, axis=-1, keepdims=True) + EPS) * w

    qf = jnp.dot(rms(cq, qan_ref[...]).astype(BF16), wuq_ref[...], preferred_element_type=F32)
    kvf = jnp.dot(rms(ckv, kvan_ref[...]).astype(BF16), wukv_ref[...], preferred_element_type=F32)
    cos = cos_ref[...]
    sin = sin_ref[...]
    qw = qn_ref[...]
    kw = kn_ref[...]
    n_all = MLA_HEADS * MLA_NOPE
    lane = lax.broadcasted_iota(jnp.int32, (1, LANES), 1)
    halves = (lane < MLA_ROPE, lane >= MLA_ROPE)

    q_rope = qf[:, n_all:]
    q_rope_sq = q_rope * q_rope
    q_rot = _rope_rotate(q_rope * qw[:, n_all:], cos, sin)
    kr_sq = jnp.sum(jnp.where(halves[0], kr2 * kr2, 0.0), axis=-1, keepdims=True)
    k_rot = _rope_rotate(kr2 * kw[:, n_all:], cos[:, :LANES], sin[:, :LANES])
    for h in range(MLA_HEADS):
        half = halves[h % 2]
        vsl = slice((h // 2) * LANES, (h // 2 + 1) * LANES)
        nsl = slice(h * MLA_NOPE, (h + 1) * MLA_NOPE)
        q_nope = qf[:, nsl]
        ss = (jnp.sum(q_nope * q_nope, axis=-1, keepdims=True)
              + jnp.sum(jnp.where(half, q_rope_sq[:, vsl], 0.0), axis=-1, keepdims=True))
        r = lax.rsqrt(ss * (1.0 / MLA_QK) + EPS) * (MLA_QK ** -0.5 * LOG2E)
        q_out[0, :, h * MLA_HW:h * MLA_HW + LANES] = (q_nope * qw[:, nsl] * r).astype(BF16)
        q_out[0, :, h * MLA_HW + LANES:(h + 1) * MLA_HW] = (jnp.where(half, q_rot[:, vsl], 0.0) * r).astype(BF16)
        k_nope = kvf[:, nsl]
        ss = jnp.sum(k_nope * k_nope, axis=-1, keepdims=True) + kr_sq
        r = lax.rsqrt(ss * (1.0 / MLA_QK) + EPS)
        k_out[0, :, h * MLA_HW:h * MLA_HW + LANES] = (k_nope * kw[:, nsl] * r).astype(BF16)
        k_out[0, :, h * MLA_HW + LANES:(h + 1) * MLA_HW] = (jnp.where(half, k_rot, 0.0) * r).astype(BF16)
    v_out[0] = kvf[:, n_all:].astype(BF16)


def _mla_prep(p, cos, sin, qa_norm, w_uq, kva_norm, w_ukv, q_norm, k_norm):
    B = p.shape[0]
    H = MLA_HEADS
    uq = w_uq.reshape(MLA_Q_RANK, H, MLA_QK)
    uq = jnp.concatenate([uq[:, :, :MLA_NOPE].reshape(MLA_Q_RANK, -1),
                          uq[:, :, MLA_NOPE:].reshape(MLA_Q_RANK, -1)], axis=1).astype(BF16)
    ukv = w_ukv.reshape(MLA_KV_RANK, H, MLA_NOPE + MLA_V)
    ukv = jnp.concatenate([ukv[:, :, :MLA_NOPE].reshape(MLA_KV_RANK, -1),
                           ukv[:, :, MLA_NOPE:].reshape(MLA_KV_RANK, -1)], axis=1).astype(BF16)
    qn = jnp.concatenate([jnp.tile(q_norm[:MLA_NOPE], H), jnp.tile(q_norm[MLA_NOPE:], H)]).reshape(1, -1)
    kn = jnp.concatenate([jnp.tile(k_norm[:MLA_NOPE], H), jnp.tile(k_norm[MLA_NOPE:], 2)]).reshape(1, -1)
    pc = _P_START['m_q'] // P_MLA_BLK
    const = lambda shape: pl.BlockSpec(shape, lambda b, i: (0, 0))
    rows = lambda w: pl.BlockSpec((MP_TM, w), lambda b, i: (i, 0))
    outs = lambda w: pl.BlockSpec((1, MP_TM, w), lambda b, i: (b, i, 0))
    return pl.pallas_call(
        _mla_prep_kernel,
        out_shape=(jax.ShapeDtypeStruct((B, S_ALL, H * MLA_HW), BF16),
                   jax.ShapeDtypeStruct((B, S_ALL, H * MLA_HW), BF16),
                   jax.ShapeDtypeStruct((B, S_ALL, MLA_W), BF16)),
        grid=(B, S_ALL // MP_TM),
        in_specs=[pl.BlockSpec((1, MP_TM, P_MLA_BLK), lambda b, i: (b, i, pc)),
                  rows(H * MLA_ROPE), rows(H * MLA_ROPE),
                  const((1, MLA_Q_RANK)), const(uq.shape), const((1, MLA_KV_RANK)), const(ukv.shape),
                  const(qn.shape), const(kn.shape)],
        out_specs=(outs(H * MLA_HW), outs(H * MLA_HW), outs(MLA_W)),
        compiler_params=_cparams(2),
        name="mla_prep",
    )(p, cos, sin, qa_norm.reshape(1, -1), uq, kva_norm.reshape(1, -1), ukv, qn, kn)


def _mla_attn_kernel(q_ref, k_ref, v_ref, z_ref, o_ref, *, ctx_start, n_lat_chunks):
    q = q_ref[0]
    tq = q.shape[0]

    def chunk(carry, kc, vc):
        m, l, acc = carry
        s = _dot_nt(q, kc)
        m_new = jnp.maximum(m, jnp.max(s, axis=-1, keepdims=True))
        a = jnp.exp2(m - m_new)
        p = jnp.exp2(s - m_new)
        l = a * l + jnp.sum(p, axis=-1, keepdims=True)
        acc = a * acc + jnp.dot(p.astype(BF16), vc, preferred_element_type=F32)
        return m_new, l, acc

    carry = (jnp.full((tq, 1), NEG_INF, F32), jnp.zeros((tq, 1), F32), jnp.zeros((tq, MLA_V), F32))
    carry = chunk(carry, k_ref[0, ctx_start:ctx_start + CTX_LEN, :], v_ref[0, ctx_start:ctx_start + CTX_LEN, :])

    for i in range(n_lat_chunks):
        carry = chunk(carry, k_ref[0, i * MLA_TK:(i + 1) * MLA_TK, :], v_ref[0, i * MLA_TK:(i + 1) * MLA_TK, :])
    _, l, acc = carry
    o_ref[0] = (acc / l * _silu(z_ref[0].astype(F32))).astype(BF16)


def _mla_attn_ctx_kernel(q_ref, k_ref, v_ref, z_ref, lat_ref, o_ref, **kw):
    del lat_ref
    _mla_attn_kernel(q_ref, k_ref, v_ref, z_ref, o_ref, **kw)


def _mla_attend(q, k, v, p):
    B = p.shape[0]
    H = MLA_HEADS
    zc = _P_START['m_z'] // MLA_V
    ctx_blk = SEQ // CTX_LEN

    def call(tq, q_blk0, n_q, key_rows, key_blk, kern, name, into=None):
        specs = [pl.BlockSpec((1, tq, MLA_HW), lambda b, h, i: (b, q_blk0 + i, h)),
                 pl.BlockSpec((1, key_rows, MLA_HW), lambda b, h, i: (b, key_blk, h)),
                 pl.BlockSpec((1, key_rows, MLA_V), lambda b, h, i: (b, key_blk, h)),
                 pl.BlockSpec((1, tq, MLA_V), lambda b, h, i: (b, q_blk0 + i, zc + h))]
        args = (q, k, v, p)
        if into is not None:
            specs.append(pl.BlockSpec(memory_space=pl.ANY))
            args += (into,)
        return pl.pallas_call(
            kern,
            out_shape=jax.ShapeDtypeStruct((B, S_ALL, MLA_W), BF16),
            grid=(B, H, n_q),
            in_specs=specs,
            out_specs=pl.BlockSpec((1, tq, MLA_V), lambda b, h, i: (b, q_blk0 + i, h)),
            input_output_aliases={} if into is None else {len(args) - 1: 0},
            compiler_params=_cparams(3),
            name=name,
        )(*args)

    lat = call(MLA_TQ, 0, SEQ // MLA_TQ, S_ALL, 0,
               functools.partial(_mla_attn_kernel, ctx_start=SEQ, n_lat_chunks=SEQ // MLA_TK), "mla_attend")
    return call(CTX_LEN, ctx_blk, 1, CTX_LEN, ctx_blk,
                functools.partial(_mla_attn_ctx_kernel, ctx_start=0, n_lat_chunks=0), "mla_attend_ctx", into=lat)


SCAN_TILE = 256
SCAN_NT = S_ALL // SCAN_TILE
CTX_TILE = SEQ // SCAN_TILE
HALO = 16


def _split_dot(m_bf16, x):
    hi = x.astype(BF16)
    lo = (x - hi.astype(F32)).astype(BF16)
    return (jnp.dot(m_bf16, hi, preferred_element_type=F32)
            + jnp.dot(m_bf16, lo, preferred_element_type=F32))


def _softplus(t):
    return jnp.maximum(t, 0.0) + jnp.log1p(jnp.exp(-jnp.abs(t)))


def _conv3_silu(x, prev_row, next_row, w, bias=None):
    n = x.shape[0]
    row = lax.broadcasted_iota(jnp.int32, (n, 1), 0)
    xp = jnp.where(row == 0, prev_row, pltpu.roll(x, 1, axis=0))
    xn = jnp.where(row == n - 1, next_row, pltpu.roll(x, n - 1, axis=0))
    y = xp * w[0:1] + x * w[1:2] + xn * w[2:3]
    if bias is not None:
        y = y + bias
    return _silu(y)


def _halo_rows(i, prev_ref, next_ref):
    pv = jnp.where((i == 0) | (i == CTX_TILE), 0.0, 1.0)
    nv = jnp.where((i == CTX_TILE - 1) | (i == SCAN_NT - 1), 0.0, 1.0)
    return prev_ref[0, HALO - 1:HALO, :].astype(F32) * pv, next_ref[0, 0:1, :].astype(F32) * nv


def _halo_specs(width, col_blk):
    rb = SCAN_TILE // HALO
    nblk = S_ALL // HALO
    return [pl.BlockSpec((1, SCAN_TILE, width), lambda b, i: (b, i, col_blk)),
            pl.BlockSpec((1, HALO, width), lambda b, i: (b, jnp.maximum(i * rb - 1, 0), col_blk)),
            pl.BlockSpec((1, HALO, width), lambda b, i: (b, jnp.minimum((i + 1) * rb, nblk - 1), col_blk))]


def _fwd_tile(t):
    return jnp.where(t == 0, CTX_TILE, t - 1)


def _bwd_tile(t):
    return jnp.where(t == 0, CTX_TILE, CTX_TILE - t)


def _chunk_masks(n):
    i = lax.broadcasted_iota(jnp.int32, (n, 1), 0)
    j = lax.broadcasted_iota(jnp.int32, (1, n), 1)
    same = (i // GDN_CHUNK) == (j // GDN_CHUNK)
    return i, j, same


def _gdn_prep_kernel(x_ref, prev_ref, next_ref, s_ref, cw_ref, rate_ref, dtb_ref,
                     q_out, k_out, v_out, g_out):
    i = pl.program_id(1)
    prev_row, next_row = _halo_rows(i, prev_ref, next_ref)
    y = _conv3_silu(x_ref[0].astype(F32), prev_row, next_row, cw_ref[...])
    for h in range(GDN_HEADS):
        sl = slice(h * GDN_DK, (h + 1) * GDN_DK)
        qh = y[:, sl]
        q_out[0, :, sl] = (qh * lax.rsqrt(jnp.sum(qh * qh, axis=-1, keepdims=True) + EPS)
                           * (GDN_DK ** -0.5)).astype(BF16)
        kh = y[:, GDN_W + h * GDN_DK:GDN_W + (h + 1) * GDN_DK]
        k_out[0, :, sl] = (kh * lax.rsqrt(jnp.sum(kh * kh, axis=-1, keepdims=True) + EPS)).astype(BF16)
    v_out[0] = y[:, 2 * GDN_W:].astype(BF16)

    s = s_ref[0]
    lane = lax.broadcasted_iota(jnp.int32, (1, LANES), 1)
    nh2 = 2 * GDN_HEADS
    beta = jax.nn.sigmoid(s)
    g = -rate_ref[...] * _softplus(s + dtb_ref[...])
    g = jnp.where((lane >= nh2) & (lane < 2 * nh2), g, 0.0)
    ii, jj, same = _chunk_masks(SCAN_TILE)
    one = lambda m: jnp.where(m, 1.0, 0.0).astype(BF16)
    fwd_lane = lane < nh2 + GDN_HEADS
    gam = jnp.where(fwd_lane, _split_dot(one(same & (jj <= ii)), g), _split_dot(one(same & (jj >= ii)), g))
    rem = jnp.where(fwd_lane, _split_dot(one(same & (jj > ii)), g), _split_dot(one(same & (jj < ii)), g))
    cf = jnp.where(lane < nh2, beta, jnp.where(lane < 2 * nh2, gam, pltpu.roll(rem, nh2, axis=1)))
    tr = cf.T
    for h in range(GDN_HEADS):
        for r, src in enumerate((h, GDN_HEADS + h, nh2 + h, nh2 + GDN_HEADS + h,
                                 2 * nh2 + h, 2 * nh2 + GDN_HEADS + h)):
            g_out[0, h, r:r + 1, :] = tr[src:src + 1, :]
        g_out[0, h, 6:8, :] = jnp.zeros((2, SCAN_TILE), F32)


def _gdn_prep(p, ps, conv_w, A_log, dt_bias):
    B = p.shape[0]
    W3 = 3 * GDN_W
    nh2 = 2 * GDN_HEADS
    rate = jnp.zeros((1, LANES), F32).at[0, nh2:2 * nh2].set(jnp.exp(A_log).reshape(-1))
    dtb = jnp.zeros((1, LANES), F32).at[0, nh2:2 * nh2].set(dt_bias.reshape(-1))
    shp = jax.ShapeDtypeStruct((B, S_ALL, GDN_W), BF16)
    ospec = pl.BlockSpec((1, SCAN_TILE, GDN_W), lambda b, i: (b, i, 0))
    const = lambda shape: pl.BlockSpec(shape, lambda b, i: (0, 0))
    return pl.pallas_call(
        _gdn_prep_kernel,
        out_shape=(shp, shp, shp, jax.ShapeDtypeStruct((B, GDN_HEADS, 8, S_ALL), F32)),
        grid=(B, SCAN_NT),
        in_specs=_halo_specs(W3, _P_START['g_qkv'] // W3)
        + [pl.BlockSpec((1, SCAN_TILE, LANES), lambda b, i: (b, i, 0)),
           const((SHORT_CONV, W3)), const((1, LANES)), const((1, LANES))],
        out_specs=(ospec, ospec, ospec,
                   pl.BlockSpec((1, GDN_HEADS, 8, SCAN_TILE), lambda b, i: (b, 0, 0, i))),
        compiler_params=_cparams(2),
        name="gdn_prep",
    )(p, p, p, ps, conv_w, rate, dtb)


def _gdn_dir(q, k, v, gr, s_ref, o_ref, d):
    n = SCAN_TILE
    cf = jnp.concatenate([gr, jnp.zeros((LANES - 8, n), F32)], axis=0).T
    beta, gam_c, rem_c = cf[:, d:d + 1], cf[:, 2 + d:3 + d], cf[:, 4 + d:5 + d]
    gam_r = gr[2 + d:3 + d, :]
    ii, jj, same = _chunk_masks(n)
    incl = same & ((jj <= ii) if d == 0 else (jj >= ii))
    strict = same & ((jj < ii) if d == 0 else (jj > ii))
    kk = _dot_nt(k, k)
    yield
    qk = _dot_nt(q, k)
    yield
    dec = jnp.exp(jnp.where(incl, gam_c - gam_r, NEG_INF))
    a = jnp.where(strict, beta * kk * dec, 0.0)
    qkd = (qk * dec).astype(BF16)
    kf = k.astype(F32)
    x = jnp.concatenate([v.astype(F32) * beta, kf * (beta * jnp.exp(gam_c))], axis=1)
    pb = a.astype(BF16)
    x = x - jnp.dot(pb, x.astype(BF16), preferred_element_type=F32)
    yield
    for _ in range(5):
        pb = jnp.dot(pb, pb, preferred_element_type=F32).astype(BF16)
        yield
        x = x + jnp.dot(pb, x.astype(BF16), preferred_element_type=F32)
        yield
    u, w =x[:, :GDN_DV], x[:, GDN_DV:].astype(BF16)
    qd = (q.astype(F32) * jnp.exp(gam_c)).astype(BF16)
    kd = (kf * jnp.exp(rem_c)).astype(BF16)
    s = s_ref[...]
    nchunk = n // GDN_CHUNK
    v_new = [None] * nchunk
    qs = [None] * nchunk
    for c in (range(nchunk) if d == 0 else reversed(range(nchunk))):
        rows = slice(c * GDN_CHUNK, (c + 1) * GDN_CHUNK)
        r1 = jnp.dot(jnp.concatenate([w[rows], qd[rows]], axis=0), s.astype(BF16), preferred_element_type=F32)
        yield
        vn = u[rows] - r1[:GDN_CHUNK]
        qs[c] = r1[GDN_CHUNK:]
        v_new[c] = vn
        last = (c + 1) * GDN_CHUNK - 1 if d == 0 else c * GDN_CHUNK
        s = s * jnp.exp(cf[last:last + 1, 2 + d:3 + d]) + _dot_tn(kd[rows], vn.astype(BF16))
        yield
    s_ref[...] = s
    vn_all = jnp.concatenate(v_new, axis=0).astype(BF16)
    o_ref[0] = jnp.concatenate(qs, axis=0) + jnp.dot(qkd, vn_all, preferred_element_type=F32)


def _round_robin(gens):
    gens = list(gens)
    while gens:
        alive = []
        for g in gens:
            try:
                next(g)
                alive.append(g)
            except StopIteration:
                pass
        gens = alive


def _gdn_scan_kernel(qf, kf, vf, gf, qb, kb, vb, gb, of_ref, ob_ref, sf_ref, sb_ref):
    @pl.when(pl.program_id(1) == 0)
    def _():
        sf_ref[...] = jnp.zeros_like(sf_ref)
        sb_ref[...] = jnp.zeros_like(sb_ref)

    chains = []
    for h in range(GDN_HEADS):
        sl = slice(h * GDN_DK, (h + 1) * GDN_DK)
        chains.append(_gdn_dir(qf[0, :, sl], kf[0, :, sl], vf[0, :, sl], gf[0, h],
                               sf_ref.at[h], of_ref.at[:, :, sl], 0))
        chains.append(_gdn_dir(qb[0, :, sl], kb[0, :, sl], vb[0, :, sl], gb[0, h],
                               sb_ref.at[h], ob_ref.at[:, :, sl], 1))
    _round_robin(chains)


def _gdn_scan(q, k, v, g):
    B = q.shape[0]
    tok = lambda order: pl.BlockSpec((1, SCAN_TILE, GDN_W), lambda b, t: (b, order(t), 0))
    gsp = lambda order: pl.BlockSpec((1, GDN_HEADS, 8, SCAN_TILE), lambda b, t: (b, 0, 0, order(t)))
    shp = jax.ShapeDtypeStruct((B, S_ALL, GDN_W), F32)
    f, r = _fwd_tile, _bwd_tile
    state = pltpu.VMEM((GDN_HEADS, GDN_DK, GDN_DV), F32)
    return pl.pallas_call(
        _gdn_scan_kernel,
        out_shape=(shp, shp),
        grid=(B, SCAN_NT),
        in_specs=[tok(f), tok(f), tok(f), gsp(f), tok(r), tok(r), tok(r), gsp(r)],
        out_specs=(tok(f), tok(r)),
        scratch_shapes=[state, state],
        compiler_params=_cparams(2),
        name="gdn_scan",
    )(q, k, v, g, q, k, v, g)


GO_TM = 544


def _gdn_out_kernel(of_ref, ob_ref, z_ref, nw_ref, o_ref):
    o = of_ref[0] + ob_ref[0]
    z = z_ref[0].astype(F32)
    for h in range(GDN_HEADS):
        sl = slice(h * GDN_DV, (h + 1) * GDN_DV)
        oh = o[:, sl]
        y = oh * lax.rsqrt(jnp.mean(oh * oh, axis=-1, keepdims=True) + EPS) * nw_ref[...]
        o_ref[0, :, sl] = (y * _silu(z[:, sl])).astype(BF16)


def _gdn_out(o_f, o_b, p, norm_w):
    B = p.shape[0]
    zc = _P_START['g_z'] // GDN_W
    blk = pl.BlockSpec((1, GO_TM, GDN_W), lambda b, i: (b, i, 0))
    return pl.pallas_call(
        _gdn_out_kernel,
        out_shape=jax.ShapeDtypeStruct((B, S_ALL, GDN_W), BF16),
        grid=(B, S_ALL // GO_TM),
        in_specs=[blk, blk, pl.BlockSpec((1, GO_TM, GDN_W), lambda b, i: (b, i, zc)),
                  pl.BlockSpec((1, GDN_DV), lambda b, i: (0, 0))],
        out_specs=blk,
        compiler_params=_cparams(2),
        name="gdn_out",
    )(o_f, o_b, p, norm_w.reshape(1, GDN_DV))


SSD_BC = SSM_GROUPS * SSM_STATE
SSD_HD = 2 * SSM_HEADS
SSD_GW = (SSM_HEADS // SSM_GROUPS) * SSM_HEADDIM


def _ssd_prep_kernel(x_ref, prev_ref, next_ref, s_ref, cw_ref, cb_ref, a_ref, dtb_ref,
                     xs_out, b_out, c_out, bt_out, cf_out, cr_out):
    i = pl.program_id(1)
    prev_row, next_row = _halo_rows(i, prev_ref, next_ref)
    y = _conv3_silu(x_ref[0].astype(F32), prev_row, next_row, cw_ref[...], cb_ref[...])
    xs_out[0] = y[:, :SSM_W]
    b_out[0] = y[:, SSM_W:SSM_W + SSD_BC].astype(BF16)
    c_out[0] = y[:, SSM_W + SSD_BC:].astype(BF16)
    bt_out[0] = y[:, SSM_W:SSM_W + SSD_BC].T.astype(BF16)

    s = s_ref[0]
    lane = lax.broadcasted_iota(jnp.int32, (1, LANES), 1)
    dt = _softplus(s + dtb_ref[...])
    on = (lane >= SSD_HD) & (lane < 2 * SSD_HD)
    a = jnp.where(on, dt * a_ref[...], 0.0)
    n = SCAN_TILE
    ii = lax.broadcasted_iota(jnp.int32, (n, 1), 0)
    jj = lax.broadcasted_iota(jnp.int32, (1, n), 1)
    one = lambda m: jnp.where(m, 1.0, 0.0).astype(BF16)
    fwd_lane = lane < SSD_HD + SSM_HEADS
    cum = jnp.where(fwd_lane, _split_dot(one(jj <= ii), a), _split_dot(one(jj >= ii), a))
    rem = jnp.where(fwd_lane, _split_dot(one(jj > ii), a), _split_dot(one(jj < ii), a))
    cf = jnp.where(lane < SSD_HD, pltpu.roll(dt, LANES - SSD_HD, axis=1),
                   jnp.where(lane < 2 * SSD_HD, cum, pltpu.roll(rem, SSD_HD, axis=1)))
    cf_out[0] = cf
    cr_out[0] = cf.T[SSD_HD:2 * SSD_HD, :]


def _ssd_prep(p, ps, conv_w, conv_b, A_log, dt_bias):
    B = p.shape[0]
    W = SSM_CONV_DIM
    a_vec = jnp.zeros((1, LANES), F32).at[0, SSD_HD:2 * SSD_HD].set(-jnp.exp(A_log).reshape(-1))
    dtb = jnp.zeros((1, LANES), F32).at[0, SSD_HD:2 * SSD_HD].set(dt_bias.reshape(-1))
    const = lambda shape: pl.BlockSpec(shape, lambda b, i: (0, 0))
    tok = lambda w: pl.BlockSpec((1, SCAN_TILE, w), lambda b, i: (b, i, 0))
    return pl.pallas_call(
        _ssd_prep_kernel,
        out_shape=(jax.ShapeDtypeStruct((B, S_ALL, SSM_W), F32),
                   jax.ShapeDtypeStruct((B, S_ALL, SSD_BC), BF16),
                   jax.ShapeDtypeStruct((B, S_ALL, SSD_BC), BF16),
                   jax.ShapeDtypeStruct((B, SSD_BC, S_ALL), BF16),
                   jax.ShapeDtypeStruct((B, S_ALL, LANES), F32),
                   jax.ShapeDtypeStruct((B, SSD_HD, S_ALL), F32)),
        grid=(B, SCAN_NT),
        in_specs=_halo_specs(W, _P_START['s_xbc'] // W)
        + [pl.BlockSpec((1, SCAN_TILE, LANES), lambda b, i: (b, i, 0)),
           const((SHORT_CONV, W)), const((1, W)), const((1, LANES)), const((1, LANES))],
        out_specs=(tok(SSM_W), tok(SSD_BC), tok(SSD_BC),
                   pl.BlockSpec((1, SSD_BC, SCAN_TILE), lambda b, i: (b, 0, i)),
                   tok(LANES),
                   pl.BlockSpec((1, SSD_HD, SCAN_TILE), lambda b, i: (b, 0, i))),
        compiler_params=_cparams(2),
        name="ssd_prep",
    )(p, p, p, ps, conv_w, conv_b.reshape(1, W), a_vec, dtb)


def _per_head_lanes(cols):
    lane = lax.broadcasted_iota(jnp.int32, (1, LANES), 1)
    lo = lane < SSM_HEADDIM
    return jnp.concatenate([jnp.where(lo, cols[2 * j], cols[2 * j + 1]) for j in range(len(cols) // 2)], axis=1)


def _ssd_dir(x_ref, b_ref, c_ref, bt_ref, cf_ref, cr_ref, h_ref, y_ref, d, g):
    n = SCAN_TILE
    hpg = SSM_HEADS // SSM_GROUPS
    hd0 = d * SSM_HEADS + g * hpg
    gs = slice(g * SSM_STATE, (g + 1) * SSM_STATE)
    xl = slice(g * SSD_GW, (g + 1) * SSD_GW)
    cf = cf_ref[0]
    cr = cr_ref[0]
    cm = c_ref[0, :, gs]
    col = lambda base, h: cf[:, base + hd0 + h:base + hd0 + h + 1]
    heads = range(hpg)
    last = n - 1 if d == 0 else 0
    xdt = x_ref[0, :, xl] * _per_head_lanes([col(0, h) for h in heads])
    xdt_b = xdt.astype(BF16)
    xdec = (xdt * _per_head_lanes([jnp.exp(col(2 * SSD_HD, h)) for h in heads])).astype(BF16)
    cb = _dot_nt(cm, b_ref[0, :, gs])
    yield
    h_prev = h_ref[g]
    y_off = jnp.dot(cm, h_prev.astype(BF16), preferred_element_type=F32)
    yield
    y_off = y_off * _per_head_lanes([jnp.exp(col(SSD_HD, h)) for h in heads])
    etot = _per_head_lanes([jnp.exp(cf[last:last + 1, SSD_HD + hd0 + h:SSD_HD + hd0 + h + 1]) for h in heads])
    h_ref[g] = h_prev * etot + jnp.dot(bt_ref[0, gs, :], xdec, preferred_element_type=F32)
    yield
    ii = lax.broadcasted_iota(jnp.int32, (n, 1), 0)
    jj = lax.broadcasted_iota(jnp.int32, (1, n), 1)
    causal = (jj <= ii) if d == 0 else (jj >= ii)
    lo = lax.broadcasted_iota(jnp.int32, (1, LANES), 1) < SSM_HEADDIM
    pair_out = []
    for j in range(hpg // 2):
        ys = []
        for e in range(2):
            h = 2 * j + e
            seg = col(SSD_HD, h) - cr[hd0 + h:hd0 + h + 1, :]
            sc = (cb * jnp.exp(jnp.where(causal, seg, NEG_INF))).astype(BF16)
            ys.append(jnp.dot(sc, xdt_b[:, j * LANES:(j + 1) * LANES], preferred_element_type=F32))
            yield
        pair_out.append(jnp.where(lo, ys[0], ys[1]))
    y_ref[0, :, xl] = jnp.concatenate(pair_out, axis=1) + y_off


def _ssd_scan_kernel(xf, bf, cf_, btf, colf, rowf, xb, bb, cb_, btb, colb, rowb, yf_ref, yb_ref, hf_ref, hb_ref):
    @pl.when(pl.program_id(1) == 0)
    def _():
        hf_ref[...] = jnp.zeros_like(hf_ref)
        hb_ref[...] = jnp.zeros_like(hb_ref)

    chains = []
    for g in range(SSM_GROUPS):
        chains.append(_ssd_dir(xf, bf, cf_, btf, colf, rowf, hf_ref, yf_ref, 0, g))
        chains.append(_ssd_dir(xb, bb, cb_, btb, colb, rowb, hb_ref, yb_ref, 1, g))
    _round_robin(chains)


def _ssd_scan(xs, bm, cm, bt, cf, cr):
    B = xs.shape[0]

    def specs(order):
        tok = lambda w: pl.BlockSpec((1, SCAN_TILE, w), lambda b, t: (b, order(t), 0))
        return [tok(SSM_W), tok(SSD_BC), tok(SSD_BC),
                pl.BlockSpec((1, SSD_BC, SCAN_TILE), lambda b, t: (b, 0, order(t))),
                tok(LANES),
                pl.BlockSpec((1, SSD_HD, SCAN_TILE), lambda b, t: (b, 0, order(t)))]

    shp = jax.ShapeDtypeStruct((B, S_ALL, SSM_W), F32)
    out = lambda order: pl.BlockSpec((1, SCAN_TILE, SSM_W), lambda b, t: (b, order(t), 0))
    hshape = pltpu.VMEM((SSM_GROUPS, SSM_STATE, SSD_GW), F32)
    args = (xs, bm, cm, bt, cf, cr)
    return pl.pallas_call(
        _ssd_scan_kernel,
        out_shape=(shp, shp),
        grid=(B, SCAN_NT),
        in_specs=specs(_fwd_tile) + specs(_bwd_tile),
        out_specs=(out(_fwd_tile), out(_bwd_tile)),
        scratch_shapes=[hshape, hshape],
        compiler_params=_cparams(2),
        name="ssd_scan",
    )(*args, *args)


SO_TM = 544


def _ssd_out_kernel(yf_ref, yb_ref, xs_ref, z_ref, d_ref, nw_ref, o_ref):
    y = yf_ref[0] + yb_ref[0] + d_ref[...] * xs_ref[0]
    y = y * _silu(z_ref[0].astype(F32))
    o_ref[0] = (y * lax.rsqrt(jnp.mean(y * y, axis=-1, keepdims=True) + EPS) * nw_ref[...]).astype(BF16)


def _ssd_out(y_f, y_b, xs, p, d_skip, norm_w):
    B = p.shape[0]
    zc = _P_START['s_z'] // SSM_W
    blk = pl.BlockSpec((1, SO_TM, SSM_W), lambda b, i: (b, i, 0))
    vec = pl.BlockSpec((1, SSM_W), lambda b, i: (0, 0))
    return pl.pallas_call(
        _ssd_out_kernel,
        out_shape=jax.ShapeDtypeStruct((B, S_ALL, SSM_W), BF16),
        grid=(B, S_ALL // SO_TM),
        in_specs=[blk, blk, blk, pl.BlockSpec((1, SO_TM, SSM_W), lambda b, i: (b, i, zc)), vec, vec],
        out_specs=blk,
        compiler_params=_cparams(2),
        name="ssd_out",
    )(y_f, y_b, xs, p, jnp.repeat(d_skip, SSM_HEADDIM).reshape(1, SSM_W), norm_w.reshape(1, SSM_W))


def _repack_w_in(w):
    cut = lambda names: [w[:, _IN_START[n]:_IN_START[n] + _IN_SIZE[n]] for n in names]
    small = cut(_S_ORDER)
    small.append(jnp.zeros((w.shape[0], LANES - sum(_IN_SIZE[n] for n in _S_ORDER)), w.dtype))
    return (jnp.concatenate(cut(_P_ORDER), axis=1).astype(BF16),
            jnp.concatenate(small, axis=1).astype(BF16))


def kernel(x, c, ctx, c_ctx, norm_w, ada_w, ada_b, w_in, gdn_conv_w, gdn_A_log, gdn_dt_bias, gdn_norm_w, na_q_norm, na_k_norm, na_rpb, mla_qa_norm, mla_w_uq, mla_kva_norm, mla_w_ukv, mla_q_norm, mla_k_norm, ssm_conv_w, ssm_conv_b, ssm_A_log, ssm_dt_bias, ssm_D, ssm_norm_w, w_out):
    B = x.shape[0]
    xs = jnp.concatenate([x, ctx], axis=1)
    c8 = jnp.zeros((8, D_MODEL), F32).at[:B].set(c).at[B].set(c_ctx)
    mods = _ada_all(c8, ada_w, ada_b)
    cos_np, sin_np = _rope_tables()
    cos, sin = jnp.asarray(cos_np), jnp.asarray(sin_np)
    na_bias = _na_bias(na_rpb)
    for l in range(DEPTH):
        shift, scale, gate = jnp.split(mods[l, :B], 3, axis=-1)
        shift_c, scale_c, gate_c = jnp.split(mods[l, B], 3, axis=-1)
        bc = lambda v: jnp.broadcast_to(v[None], (B, D_MODEL))
        mod4 = jnp.stack([shift, scale, bc(shift_c), bc(scale_c)], axis=1)
        gate2 = jnp.stack([gate, bc(gate_c)], axis=1)
        p, ps = _inproj(xs, norm_w[l], mod4, *_repack_w_in(w_in[l]))

        gq, gk, gv, gg = _gdn_prep(p, ps, gdn_conv_w[l], gdn_A_log[l], gdn_dt_bias[l])
        o_f, o_b = _gdn_scan(gq, gk, gv, gg)
        oa = _gdn_out(o_f, o_b, p, gdn_norm_w[l])

        qn, kn, vn = _na_prep(p, na_q_norm[l], na_k_norm[l])
        ob = _na_attend(qn, kn, vn, p, na_bias[l])

        mq, mk, mv = _mla_prep(p, cos, sin, mla_qa_norm[l], mla_w_uq[l], mla_kva_norm[l], mla_w_ukv[l],
                               mla_q_norm[l], mla_k_norm[l])
        oc = _mla_attend(mq, mk, mv, p)

        sx, sb, sc, sbt, scf, scr = _ssd_prep(p, ps, ssm_conv_w[l], ssm_conv_b[l], ssm_A_log[l], ssm_dt_bias[l])
        y_f, y_b = _ssd_scan(sx, sb, sc, sbt, scf, scr)
        od = _ssd_out(y_f, y_b, sx, p, ssm_D[l], ssm_norm_w[l])
        xs = _outproj((oa, ob, oc, od), w_out[l].astype(BF16), xs, gate2, last=(l == DEPTH - 1))
    return xs
```

```python
import functools

import jax
import jax.numpy as jnp
import numpy as np
from jax import lax
from jax.experimental import pallas as pl
from jax.experimental.pallas import tpu as pltpu

F32 = jnp.float32
BF16 = jnp.bfloat16

D_MODEL = 2048
BATCH = 4
SEQ = 4096
DEPTH = 4
GRID_W = 64
GRID_H = SEQ // GRID_W
CTX_LEN = 256
S_ALL = SEQ + CTX_LEN
EPS = 1e-6
NEG_INF = -1e30
LOG2E = 1.4426950408889634

D_BRANCH = 512
D_MIX = 4 * D_BRANCH
SHORT_CONV = 3

GDN_HEADS = 4
GDN_DK = 128
GDN_DV = 128
GDN_W = GDN_HEADS * GDN_DV
GDN_CHUNK = 64

NA_HEADS = 4
NA_DH = 128
NA_W = NA_HEADS * NA_DH
NA_WIN_R = 8
NA_WIN_C = 16

MLA_HEADS = 4
MLA_Q_RANK = 384
MLA_KV_RANK = 256
MLA_NOPE = 128
MLA_ROPE = 64
MLA_QK = MLA_NOPE + MLA_ROPE
MLA_V = 128
MLA_W = MLA_HEADS * MLA_V
ROPE_THETA = 10000.0

SSM_HEADDIM = 64
SSM_HEADS = D_BRANCH // SSM_HEADDIM
SSM_W = SSM_HEADS * SSM_HEADDIM
SSM_GROUPS = 2
SSM_STATE = 128
SSM_CONV_DIM = SSM_W + 2 * SSM_GROUPS * SSM_STATE

IN_SIZES = (3 * GDN_W, GDN_W, 2 * GDN_HEADS, 2 * GDN_HEADS,
            3 * NA_W, NA_W,
            MLA_Q_RANK, MLA_KV_RANK, MLA_ROPE, MLA_W,
            SSM_W, SSM_CONV_DIM, 2 * SSM_HEADS)
D_IN = sum(IN_SIZES)
_IN_NAMES = ('g_qkv', 'g_z', 'g_beta', 'g_alpha', 'n_qkv', 'n_z',
             'm_q', 'm_kv', 'm_kr', 'm_z', 's_z', 's_xbc', 's_dt')
_IN_START = dict(zip(_IN_NAMES, np.cumsum((0,) + IN_SIZES[:-1]).tolist()))
_IN_SIZE = dict(zip(_IN_NAMES, IN_SIZES))

LANES = 128
_P_ORDER = ('g_qkv', 'g_z', 'n_qkv', 'n_z', 'm_z', 's_z', 's_xbc',
            'm_q', 'm_kv', 'm_kr', 'm_kr')
_S_ORDER = ('g_beta', 'g_alpha', 's_dt')
_P_START = {}
_off = 0
for _n in _P_ORDER:
    _P_START.setdefault(_n, _off)
    _off += _IN_SIZE[_n]
MXU_N = 256
D_INP = -(-_off // (2 * MXU_N)) * (2 * MXU_N)
P_MLA_BLK = MLA_Q_RANK + MLA_KV_RANK + 2 * MLA_ROPE
assert _P_START['m_q'] % P_MLA_BLK == 0 and D_INP % LANES == 0

VMEM_LIMIT = 52 * 1024 * 1024


def _silu(x):
    return x * jax.nn.sigmoid(x)


def _dot_nt(a, b):
    return lax.dot_general(a, b, (((1,), (1,)), ((), ())), preferred_element_type=F32)


def _dot_tn(a, b):
    return lax.dot_general(a, b, (((0,), (0,)), ((), ())), preferred_element_type=F32)


def _cparams(n_axes):
    return pltpu.CompilerParams(dimension_semantics=("arbitrary",) * n_axes,
                                vmem_limit_bytes=VMEM_LIMIT)


def _ada_kernel(c_ref, w_ref, b_ref, o_ref):
    a = _silu(c_ref[...]).astype(BF16)
    o_ref[0] = jnp.dot(a, w_ref[0].astype(BF16), preferred_element_type=F32) + b_ref[0]


def _ada_all(c8, ada_w, ada_b):
    tn = 1536
    L = ada_w.shape[0]
    n3 = ada_w.shape[2]
    return pl.pallas_call(
        _ada_kernel,
        out_shape=jax.ShapeDtypeStruct((L, 8, n3), F32),
        grid=(L, n3 // tn),
        in_specs=[pl.BlockSpec((8, D_MODEL), lambda l, j: (0, 0)),
                  pl.BlockSpec((1, D_MODEL, tn), lambda l, j: (l, 0, j)),
                  pl.BlockSpec((1, 1, tn), lambda l, j: (l, 0, j))],
        out_specs=pl.BlockSpec((1, 8, tn), lambda l, j: (l, 0, j)),
        compiler_params=_cparams(2),
        name="ada_mod",
    )(c8, ada_w, ada_b.reshape(L, 1, n3))


IN_TM = 1088
IN_TN = 1024
IN_RC = 16


def _inproj_kernel(x_ref, nw_ref, mod_ref, w_ref, o_ref, os_ref, h_scr):
    i = pl.program_id(1)
    j = pl.program_id(2)

    @pl.when(j == 0)
    def _():
        m = mod_ref[0]
        nw = nw_ref[...]
        gain_l = nw * (1.0 + m[1:2])
        gain_c = nw * (1.0 + m[3:4])

        def body(r, carry):
            r0 = pl.multiple_of(r * IN_RC, IN_RC)
            x = x_ref[0, pl.ds(r0, IN_RC), :]
            ms = jnp.mean(x * x, axis=-1, keepdims=True)
            is_ctx = i * IN_TM + r0 >= SEQ
            gain = jnp.where(is_ctx, gain_c, gain_l)
            shift = jnp.where(is_ctx, m[2:3], m[0:1])
            h_scr[pl.ds(r0, IN_RC), :] = (x * lax.rsqrt(ms + EPS) * gain + shift).astype(BF16)
            return carry

        lax.fori_loop(0, IN_TM // IN_RC, body, 0, unroll=2)

    y = jnp.dot(h_scr[...], w_ref[...], preferred_element_type=F32)
    o_ref[0] = y.astype(BF16)

    @pl.when(j == pl.num_programs(2) - 1)
    def _():
        os_ref[0] = y[:, IN_TN - LANES:]


def _inproj(xs, norm_w, mod4, w_main):
    B = xs.shape[0]
    return pl.pallas_call(
        _inproj_kernel,
        out_shape=(jax.ShapeDtypeStruct((B, S_ALL, D_INP), BF16),
                   jax.ShapeDtypeStruct((B, S_ALL, LANES), F32)),
        grid=(B, S_ALL // IN_TM, D_INP // IN_TN),
        in_specs=[pl.BlockSpec((1, IN_TM, D_MODEL), lambda b, i, j: (b, i, 0)),
                  pl.BlockSpec((1, D_MODEL), lambda b, i, j: (0, 0)),
                  pl.BlockSpec((1, 4, D_MODEL), lambda b, i, j: (b, 0, 0)),
                  pl.BlockSpec((D_MODEL, IN_TN), lambda b, i, j: (0, j))],
        out_specs=(pl.BlockSpec((1, IN_TM, IN_TN), lambda b, i, j: (b, i, j)),
                   pl.BlockSpec((1, IN_TM, LANES), lambda b, i, j: (b, i, 0))),
        scratch_shapes=[pltpu.VMEM((IN_TM, D_MODEL), BF16)],
        compiler_params=_cparams(3),
        name="inproj",
    )(xs, norm_w.reshape(1, D_MODEL), mod4, w_main)


OUT_TM = 544
OUT_TM_LAST = 512


def _outproj_kernel(a0_ref, a1_ref, a2_ref, a3_ref, w_ref, x_ref, g_ref, o_ref, *, tm):
    i = pl.program_id(1)
    y = None
    for n, a_ref in enumerate((a0_ref, a1_ref, a2_ref, a3_ref)):
        t = jnp.dot(a_ref[0], w_ref[n * D_BRANCH:(n + 1) * D_BRANCH, :], preferred_element_type=F32)
        y = t if y is None else y + t
    row = i * tm + lax.broadcasted_iota(jnp.int32, (tm, 1), 0)
    g = g_ref[0]
    gate = jnp.where(row >= SEQ, g[1:2], g[0:1])
    o_ref[0] = x_ref[0] + gate * y


def _outproj(branches, w_out_b, xs, gate2, last):
    B = xs.shape[0]
    tm, rows = (OUT_TM_LAST, SEQ) if last else (OUT_TM, S_ALL)
    a_spec = pl.BlockSpec((1, tm, D_BRANCH), lambda b, i: (b, i, 0))
    x_spec = pl.BlockSpec((1, tm, D_MODEL), lambda b, i: (b, i, 0))
    return pl.pallas_call(
        functools.partial(_outproj_kernel, tm=tm),
        out_shape=jax.ShapeDtypeStruct((B, rows, D_MODEL), F32),
        grid=(B, rows // tm),
        in_specs=[a_spec, a_spec, a_spec, a_spec,
                  pl.BlockSpec((D_MIX, D_MODEL), lambda b, i: (0, 0)),
                  x_spec,
                  pl.BlockSpec((1, 2, D_MODEL), lambda b, i: (b, 0, 0))],
        out_specs=x_spec,
        compiler_params=_cparams(2),
        name="outproj",
    )(*branches, w_out_b, xs, gate2)


NA_TM = 544
NA_RB = 4
NA_QB = NA_RB * GRID_W
NA_KR = 12
NA_KW = NA_KR * GRID_W
NA_NBLK = GRID_H // NA_RB
assert NA_QB == CTX_LEN


def _na_prep_kernel(q_ref, k_ref, v_ref, qn_ref, kn_ref, qo_ref, ko_ref, vo_ref):
    def headnorm(x, w, extra):
        outs = []
        for h in range(NA_HEADS):
            xh = x[:, h * NA_DH:(h + 1) * NA_DH]
            ms = jnp.mean(xh * xh, axis=-1, keepdims=True)
            outs.append((xh * lax.rsqrt(ms + EPS) * w * extra).astype(BF16))
        return jnp.concatenate(outs, axis=-1)

    qo_ref[0] = headnorm(q_ref[0].astype(F32), qn_ref[...], NA_DH ** -0.5)
    ko_ref[0] = headnorm(k_ref[0].astype(F32), kn_ref[...], 1.0)
    vo_ref[0] = v_ref[0]


def _na_prep(p, q_norm, k_norm):
    B = p.shape[0]
    c0 = _P_START['n_qkv'] // NA_W
    spec = lambda c: pl.BlockSpec((1, NA_TM, NA_W), lambda b, i, c=c: (b, i, c))
    ospec = pl.BlockSpec((1, NA_TM, NA_W), lambda b, i: (b, i, 0))
    wspec = pl.BlockSpec((1, NA_DH), lambda b, i: (0, 0))
    shp = jax.ShapeDtypeStruct((B, S_ALL, NA_W), BF16)
    return pl.pallas_call(
        _na_prep_kernel,
        out_shape=(shp, shp, shp),
        grid=(B, S_ALL // NA_TM),
        in_specs=[spec(c0), spec(c0 + 1), spec(c0 + 2), wspec, wspec],
        out_specs=(ospec, ospec, ospec),
        compiler_params=_cparams(2),
        name="na_prep",
    )(p, p, p, q_norm.reshape(1, NA_DH), k_norm.reshape(1, NA_DH))


def _na_kernel(q_ref, k_ref, v_ref, z_ref, bias_ref, o_ref):
    rb = pl.program_id(1)
    q = q_ref[0]
    z = z_ref[0].astype(F32)
    kc = k_ref[0, SEQ:S_ALL, :]
    vc = v_ref[0, SEQ:S_ALL, :]

    def finish(h, o, l):
        sl = slice(h * NA_DH, (h + 1) * NA_DH)
        o_ref[0, :, sl] = (o / l * _silu(z[:, sl])).astype(BF16)

    @pl.when(rb < NA_NBLK)
    def _latent():
        base = jnp.clip(rb * NA_RB - NA_RB, 0, GRID_H - NA_KR)
        start = pl.multiple_of(base * GRID_W, GRID_W)
        kw = k_ref[0, pl.ds(start, NA_KW), :]
        vw = v_ref[0, pl.ds(start, NA_KW), :]
        for h in range(NA_HEADS):
            sl = slice(h * NA_DH, (h + 1) * NA_DH)
            s_w = _dot_nt(q[:, sl], kw[:, sl]) + bias_ref[0, h].astype(F32)
            s_c = _dot_nt(q[:, sl], kc[:, sl])
            m = jnp.maximum(jnp.max(s_w, axis=-1, keepdims=True), jnp.max(s_c, axis=-1, keepdims=True))
            p_w = jnp.exp(s_w - m)
            p_c = jnp.exp(s_c - m)
            l = jnp.sum(p_w, axis=-1, keepdims=True) + jnp.sum(p_c, axis=-1, keepdims=True)
            o = (jnp.dot(p_w.astype(BF16), vw[:, sl], preferred_element_type=F32)
                 + jnp.dot(p_c.astype(BF16), vc[:, sl], preferred_element_type=F32))
            finish(h, o, l)

    @pl.when(rb == NA_NBLK)
    def _context():
        for h in range(NA_HEADS):
            sl = slice(h * NA_DH, (h + 1) * NA_DH)
            s_c = _dot_nt(q[:, sl], kc[:, sl])
            m = jnp.max(s_c, axis=-1, keepdims=True)
            p_c = jnp.exp(s_c - m)
            l = jnp.sum(p_c, axis=-1, keepdims=True)
            o = jnp.dot(p_c.astype(BF16), vc[:, sl], preferred_element_type=F32)
            finish(h, o, l)


def _na_bias_index():
    dr = np.zeros((3, NA_RB, NA_KR), np.int64)
    ok = np.zeros((3, NA_RB, NA_KR), bool)
    for ci, rb in enumerate((0, 1, NA_NBLK - 1)):
        base = int(np.clip(rb * NA_RB - NA_RB, 0, GRID_H - NA_KR))
        qr = rb * NA_RB + np.arange(NA_RB)[:, None]
        kr = base + np.arange(NA_KR)[None, :]
        row0 = np.clip(qr - NA_WIN_R // 2, 0, GRID_H - NA_WIN_R)
        ok[ci] = (kr >= row0) & (kr < row0 + NA_WIN_R)
        dr[ci] = np.clip(kr - qr + NA_WIN_R - 1, 0, 2 * NA_WIN_R - 2)
    qc = np.arange(GRID_W)[:, None]
    kc = np.arange(GRID_W)[None, :]
    win0 = np.clip(qc - NA_WIN_C // 2, 0, GRID_W - NA_WIN_C)
    col_ok = (kc >= win0) & (kc < win0 + NA_WIN_C)
    return dr, ok, col_ok


def _na_bias(rpb):
    L, H = rpb.shape[:2]
    nd = 2 * NA_WIN_R - 1
    dr, ok, col_ok = _na_bias_index()
    left = GRID_W - NA_WIN_C
    f = jnp.pad(rpb, ((0, 0), (0, 0), (0, 0), (left, 2 * GRID_W - (2 * NA_WIN_C - 1) - left)))
    skew = jnp.broadcast_to(f[:, :, :, None, :], (L, H, nd, GRID_W, 2 * GRID_W))
    skew = skew.reshape(L, H, nd, -1)[..., :GRID_W * (2 * GRID_W - 1)].reshape(L, H, nd, GRID_W, 2 * GRID_W - 1)
    toe = skew[..., GRID_W - 1:]
    toe = jnp.where(col_ok, toe, NEG_INF)
    dead = jnp.full((L, H, GRID_W, GRID_W), NEG_INF, F32)
    blocks = [toe[:, :, dr[c, a, b]] if ok[c, a, b] else dead
              for c in range(3) for a in range(NA_RB) for b in range(NA_KR)]
    t = jnp.stack(blocks, axis=2).reshape(L, H, 3, NA_RB, NA_KR, GRID_W, GRID_W)
    return jnp.transpose(t.astype(BF16), (0, 2, 1, 3, 5, 4, 6)).reshape(L, 3, H, NA_QB, NA_KW)


def _na_attend(qn, kn, vn, p, bias):
    B = p.shape[0]
    zc = _P_START['n_z'] // NA_W
    last = NA_NBLK - 1
    blk = pl.BlockSpec((1, NA_QB, NA_W), lambda b, r: (b, r, 0))
    full = pl.BlockSpec((1, S_ALL, NA_W), lambda b, r: (b, 0, 0))
    return pl.pallas_call(
        _na_kernel,
        out_shape=jax.ShapeDtypeStruct((B, S_ALL, NA_W), BF16),
        grid=(B, NA_NBLK + 1),
        in_specs=[blk, full, full,
                  pl.BlockSpec((1, NA_QB, NA_W), lambda b, r: (b, r, zc)),
                  pl.BlockSpec((1, NA_HEADS, NA_QB, NA_KW),
                               lambda b, r: (jnp.where(r == 0, 0, jnp.where(r >= last, 2, 1)), 0, 0, 0))],
        out_specs=blk,
        compiler_params=_cparams(2),
        name="na_attend",
    )(qn, kn, vn, p, bias)


MP_TM = 544
MLA_HW = 2 * LANES
MLA_TQ = 1024
MLA_SUBQ = 512
MLA_TK = 512


def _rope_tables():
    n_freq = MLA_ROPE // 4
    inv_freq = ROPE_THETA ** (-np.arange(n_freq, dtype=np.float64) / n_freq)
    t = np.arange(SEQ)
    ar = (t // GRID_W)[:, None] * inv_freq
    ac = (t % GRID_W)[:, None] * inv_freq
    cos = np.concatenate([np.cos(ar), np.cos(ar), np.cos(ac), np.cos(ac)], axis=1)
    sin = np.concatenate([-np.sin(ar), np.sin(ar), -np.sin(ac), np.sin(ac)], axis=1)
    cos = np.concatenate([cos, np.ones((CTX_LEN, MLA_ROPE))], axis=0)
    sin = np.concatenate([sin, np.zeros((CTX_LEN, MLA_ROPE))], axis=0)
    return (np.tile(cos, (1, MLA_HEADS)).astype(np.float32), np.tile(sin, (1, MLA_HEADS)).astype(np.float32))


def _rope_rotate(t, cos, sin):
    w = t.shape[1]
    lane = lax.broadcasted_iota(jnp.int32, (1, w), 1)
    first = (lane & 31) < 16
    up = pltpu.roll(t, w - 16, axis=1)
    dn = pltpu.roll(t, 16, axis=1)
    return t * cos + jnp.where(first, up, dn) * sin


def _mla_prep_kernel(p_ref, cos_ref, sin_ref, qan_ref, wuq_ref, kvan_ref, wukv_ref, qn_ref, kn_ref,
                     q_out, k_out, v_out):
    x = p_ref[0].astype(F32)
    cq = x[:, :MLA_Q_RANK]
    ckv = x[:, MLA_Q_RANK:MLA_Q_RANK + MLA_KV_RANK]
    kr2 = x[:, MLA_Q_RANK + MLA_KV_RANK:]

    def rms(t, w):
        return t * lax.rsqrt(jnp.mean(t * t, axis=-1, keepdims=True) + EPS) * w

    qf = jnp.dot(rms(cq, qan_ref[...]).astype(BF16), wuq_ref[...], preferred_element_type=F32)
    kvf = jnp.dot(rms(ckv, kvan_ref[...]).astype(BF16), wukv_ref[...], preferred_element_type=F32)
    cos = cos_ref[...]
    sin = sin_ref[...]
    qw = qn_ref[...]
    kw = kn_ref[...]
    n_all = MLA_HEADS * MLA_NOPE
    lane = lax.broadcasted_iota(jnp.int32, (1, LANES), 1)
    halves = (lane < MLA_ROPE, lane >= MLA_ROPE)

    q_rope = qf[:, n_all:]
    q_rope_sq = q_rope * q_rope
    q_rot = _rope_rotate(q_rope * qw[:, n_all:], cos, sin)
    kr_sq = jnp.sum(jnp.where(halves[0], kr2 * kr2, 0.0), axis=-1, keepdims=True)
    k_rot = _rope_rotate(kr2 * kw[:, n_all:], cos[:, :LANES], sin[:, :LANES])
    for h in range(MLA_HEADS):
        half = halves[h % 2]
        vsl = slice((h // 2) * LANES, (h // 2 + 1) * LANES)
        nsl = slice(h * MLA_NOPE, (h + 1) * MLA_NOPE)
        q_nope = qf[:, nsl]
        ss = (jnp.sum(q_nope * q_nope, axis=-1, keepdims=True)
              + jnp.sum(jnp.where(half, q_rope_sq[:, vsl], 0.0), axis=-1, keepdims=True))
        r = lax.rsqrt(ss * (1.0 / MLA_QK) + EPS) * (MLA_QK ** -0.5 * LOG2E)
        q_out[0, :, h * MLA_HW:h * MLA_HW + LANES] = (q_nope * qw[:, nsl] * r).astype(BF16)
        q_out[0, :, h * MLA_HW + LANES:(h + 1) * MLA_HW] = (jnp.where(half, q_rot[:, vsl], 0.0) * r).astype(BF16)
        k_nope = kvf[:, nsl]
        ss = jnp.sum(k_nope * k_nope, axis=-1, keepdims=True) + kr_sq
        r = lax.rsqrt(ss * (1.0 / MLA_QK) + EPS)
        k_out[0, :, h * MLA_HW:h * MLA_HW + LANES] = (k_nope * kw[:, nsl] * r).astype(BF16)
        k_out[0, :, h * MLA_HW + LANES:(h + 1) * MLA_HW] = (jnp.where(half, k_rot, 0.0) * r).astype(BF16)
    v_out[0] = kvf[:, n_all:].astype(BF16)


def _mla_prep(p, cos, sin, qa_norm, w_uq, kva_norm, w_ukv, q_norm, k_norm):
    B = p.shape[0]
    H = MLA_HEADS
    uq = w_uq.reshape(MLA_Q_RANK, H, MLA_QK)
    uq = jnp.concatenate([uq[:, :, :MLA_NOPE].reshape(MLA_Q_RANK, -1),
                          uq[:, :, MLA_NOPE:].reshape(MLA_Q_RANK, -1)], axis=1).astype(BF16)
    ukv = w_ukv.reshape(MLA_KV_RANK, H, MLA_NOPE + MLA_V)
    ukv = jnp.concatenate([ukv[:, :, :MLA_NOPE].reshape(MLA_KV_RANK, -1),
                           ukv[:, :, MLA_NOPE:].reshape(MLA_KV_RANK, -1)], axis=1).astype(BF16)
    qn = jnp.concatenate([jnp.tile(q_norm[:MLA_NOPE], H), jnp.tile(q_norm[MLA_NOPE:], H)]).reshape(1, -1)
    kn = jnp.concatenate([jnp.tile(k_norm[:MLA_NOPE], H), jnp.tile(k_norm[MLA_NOPE:], 2)]).reshape(1, -1)
    pc = _P_START['m_q'] // P_MLA_BLK
    const = lambda shape: pl.BlockSpec(shape, lambda b, i: (0, 0))
    rows = lambda w: pl.BlockSpec((MP_TM, w), lambda b, i: (i, 0))
    outs = lambda w: pl.BlockSpec((1, MP_TM, w), lambda b, i: (b, i, 0))
    return pl.pallas_call(
        _mla_prep_kernel,
        out_shape=(jax.ShapeDtypeStruct((B, S_ALL, H * MLA_HW), BF16),
                   jax.ShapeDtypeStruct((B, S_ALL, H * MLA_HW), BF16),
                   jax.ShapeDtypeStruct((B, S_ALL, MLA_W), BF16)),
        grid=(B, S_ALL // MP_TM),
        in_specs=[pl.BlockSpec((1, MP_TM, P_MLA_BLK), lambda b, i: (b, i, pc)),
                  rows(H * MLA_ROPE), rows(H * MLA_ROPE),
                  const((1, MLA_Q_RANK)), const(uq.shape), const((1, MLA_KV_RANK)), const(ukv.shape),
                  const(qn.shape), const(kn.shape)],
        out_specs=(outs(H * MLA_HW), outs(H * MLA_HW), outs(MLA_W)),
        compiler_params=_cparams(2),
        name="mla_prep",
    )(p, cos, sin, qa_norm.reshape(1, -1), uq, kva_norm.reshape(1, -1), ukv, qn, kn)


def _mla_attn_kernel(q_ref, k_ref, v_ref, z_ref, o_ref, *, ctx_start, n_lat_chunks):
    tq = q_ref.shape[1]
    sub = min(tq, MLA_SUBQ)
    bounds = [(ctx_start, ctx_start + CTX_LEN)] + [(i * MLA_TK, (i + 1) * MLA_TK) for i in range(n_lat_chunks)]

    def rows_chain(r0):
        q = q_ref[0, r0:r0 + sub, :]
        m = jnp.full((sub, 1), NEG_INF, F32)
        l = jnp.zeros((sub, 1), F32)
        acc = jnp.zeros((sub, MLA_V), F32)
        for lo, hi in bounds:
            s = _dot_nt(q, k_ref[0, lo:hi, :])
            yield
            m_new = jnp.maximum(m, jnp.max(s, axis=-1, keepdims=True))
            a = jnp.exp2(m - m_new)
            p = jnp.exp2(s - m_new)
            l = a * l + jnp.sum(p, axis=-1, keepdims=True)
            acc = a * acc + jnp.dot(p.astype(BF16), v_ref[0, lo:hi, :], preferred_element_type=F32)
            m = m_new
            yield
        z = z_ref[0, r0:r0 + sub, :].astype(F32)
        o_ref[0, r0:r0 + sub, :] = (acc / l * _silu(z)).astype(BF16)

    _round_robin(rows_chain(r0) for r0 in range(0, tq, sub))


def _mla_attn_ctx_kernel(q_ref, k_ref, v_ref, z_ref, lat_ref, o_ref, **kw):
    del lat_ref
    _mla_attn_kernel(q_ref, k_ref, v_ref, z_ref, o_ref, **kw)


def _mla_attend(q, k, v, p):
    B = p.shape[0]
    H = MLA_HEADS
    zc = _P_START['m_z'] // MLA_V
    ctx_blk = SEQ // CTX_LEN

    def call(tq, q_blk0, n_q, key_rows, key_blk, kern, name, into=None):
        specs = [pl.BlockSpec((1, tq, MLA_HW), lambda b, h, i: (b, q_blk0 + i, h)),
                 pl.BlockSpec((1, key_rows, MLA_HW), lambda b, h, i: (b, key_blk, h)),
                 pl.BlockSpec((1, key_rows, MLA_V), lambda b, h, i: (b, key_blk, h)),
                 pl.BlockSpec((1, tq, MLA_V), lambda b, h, i: (b, q_blk0 + i, zc + h))]
        args = (q, k, v, p)
        if into is not None:
            specs.append(pl.BlockSpec(memory_space=pl.ANY))
            args += (into,)
        return pl.pallas_call(
            kern,
            out_shape=jax.ShapeDtypeStruct((B, S_ALL, MLA_W), BF16),
            grid=(B, H, n_q),
            in_specs=specs,
            out_specs=pl.BlockSpec((1, tq, MLA_V), lambda b, h, i: (b, q_blk0 + i, h)),
            input_output_aliases={} if into is None else {len(args) - 1: 0},
            compiler_params=_cparams(3),
            name=name,
        )(*args)

    lat = call(MLA_TQ, 0, SEQ // MLA_TQ, S_ALL, 0,
               functools.partial(_mla_attn_kernel, ctx_start=SEQ, n_lat_chunks=SEQ // MLA_TK), "mla_attend")
    return call(CTX_LEN, ctx_blk, 1, CTX_LEN, ctx_blk,
                functools.partial(_mla_attn_ctx_kernel, ctx_start=0, n_lat_chunks=0), "mla_attend_ctx", into=lat)


SCAN_TILE = 256
SCAN_NT = S_ALL // SCAN_TILE
CTX_TILE = SEQ // SCAN_TILE
HALO = 16


def _split_dot(m_bf16, x):
    hi = x.astype(BF16)
    lo = (x - hi.astype(F32)).astype(BF16)
    return (jnp.dot(m_bf16, hi, preferred_element_type=F32)
            + jnp.dot(m_bf16, lo, preferred_element_type=F32))


def _softplus(t):
    return jnp.maximum(t, 0.0) + jnp.log1p(jnp.exp(-jnp.abs(t)))


def _conv3_silu(x, prev_row, next_row, w, bias=None):
    n = x.shape[0]
    row = lax.broadcasted_iota(jnp.int32, (n, 1), 0)
    xp = jnp.where(row == 0, prev_row, pltpu.roll(x, 1, axis=0))
    xn = jnp.where(row == n - 1, next_row, pltpu.roll(x, n - 1, axis=0))
    y = xp * w[0:1] + x * w[1:2] + xn * w[2:3]
    if bias is not None:
        y = y + bias
    return _silu(y)


def _halo_rows(i, prev_ref, next_ref):
    pv = jnp.where((i == 0) | (i == CTX_TILE), 0.0, 1.0)
    nv = jnp.where((i == CTX_TILE - 1) | (i == SCAN_NT - 1), 0.0, 1.0)
    return prev_ref[0, HALO - 1:HALO, :].astype(F32) * pv, next_ref[0, 0:1, :].astype(F32) * nv


def _halo_specs(width, col_blk):
    rb = SCAN_TILE // HALO
    nblk = S_ALL // HALO
    return [pl.BlockSpec((1, SCAN_TILE, width), lambda b, i: (b, i, col_blk)),
            pl.BlockSpec((1, HALO, width), lambda b, i: (b, jnp.maximum(i * rb - 1, 0), col_blk)),
            pl.BlockSpec((1, HALO, width), lambda b, i: (b, jnp.minimum((i + 1) * rb, nblk - 1), col_blk))]


def _fwd_tile(t):
    return jnp.where(t == 0, CTX_TILE, t - 1)


def _bwd_tile(t):
    return jnp.where(t == 0, CTX_TILE, CTX_TILE - t)


def _chunk_masks(n):
    i = lax.broadcasted_iota(jnp.int32, (n, 1), 0)
    j = lax.broadcasted_iota(jnp.int32, (1, n), 1)
    same = (i // GDN_CHUNK) == (j // GDN_CHUNK)
    return i, j, same


def _gdn_prep_kernel(x_ref, prev_ref, next_ref, s_ref, cw_ref, rate_ref, dtb_ref,
                     q_out, k_out, v_out, g_out):
    i = pl.program_id(1)
    prev_row, next_row = _halo_rows(i, prev_ref, next_ref)
    y = _conv3_silu(x_ref[0].astype(F32), prev_row, next_row, cw_ref[...])
    for h in range(GDN_HEADS):
        sl = slice(h * GDN_DK, (h + 1) * GDN_DK)
        qh = y[:, sl]
        q_out[0, :, sl] = (qh * lax.rsqrt(jnp.sum(qh * qh, axis=-1, keepdims=True) + EPS)
                           * (GDN_DK ** -0.5)).astype(BF16)
        kh = y[:, GDN_W + h * GDN_DK:GDN_W + (h + 1) * GDN_DK]
        k_out[0, :, sl] = (kh * lax.rsqrt(jnp.sum(kh * kh, axis=-1, keepdims=True) + EPS)).astype(BF16)
    v_out[0] = y[:, 2 * GDN_W:].astype(BF16)

    s = s_ref[0]
    lane = lax.broadcasted_iota(jnp.int32, (1, LANES), 1)
    nh2 = 2 * GDN_HEADS
    beta = jax.nn.sigmoid(s)
    g = -rate_ref[...] * _softplus(s + dtb_ref[...])
    g = jnp.where((lane >= nh2) & (lane < 2 * nh2), g, 0.0)
    ii, jj, same = _chunk_masks(SCAN_TILE)
    one = lambda m: jnp.where(m, 1.0, 0.0).astype(BF16)
    fwd_lane = lane < nh2 + GDN_HEADS
    gam = jnp.where(fwd_lane, _split_dot(one(same & (jj <= ii)), g), _split_dot(one(same & (jj >= ii)), g))
    rem = jnp.where(fwd_lane, _split_dot(one(same & (jj > ii)), g), _split_dot(one(same & (jj < ii)), g))
    cf = jnp.where(lane < nh2, beta, jnp.where(lane < 2 * nh2, gam, pltpu.roll(rem, nh2, axis=1)))
    tr = cf.T
    for h in range(GDN_HEADS):
        for r, src in enumerate((h, GDN_HEADS + h, nh2 + h, nh2 + GDN_HEADS + h,
                                 2 * nh2 + h, 2 * nh2 + GDN_HEADS + h)):
            g_out[0, h, r:r + 1, :] = tr[src:src + 1, :]
        g_out[0, h, 6:8, :] = jnp.zeros((2, SCAN_TILE), F32)


def _gdn_prep(p, ps, conv_w, A_log, dt_bias):
    B = p.shape[0]
    W3 = 3 * GDN_W
    nh2 = 2 * GDN_HEADS
    rate = jnp.zeros((1, LANES), F32).at[0, nh2:2 * nh2].set(jnp.exp(A_log).reshape(-1))
    dtb = jnp.zeros((1, LANES), F32).at[0, nh2:2 * nh2].set(dt_bias.reshape(-1))
    shp = jax.ShapeDtypeStruct((B, S_ALL, GDN_W), BF16)
    ospec = pl.BlockSpec((1, SCAN_TILE, GDN_W), lambda b, i: (b, i, 0))
    const = lambda shape: pl.BlockSpec(shape, lambda b, i: (0, 0))
    return pl.pallas_call(
        _gdn_prep_kernel,
        out_shape=(shp, shp, shp, jax.ShapeDtypeStruct((B, GDN_HEADS, 8, S_ALL), F32)),
        grid=(B, SCAN_NT),
        in_specs=_halo_specs(W3, _P_START['g_qkv'] // W3)
        + [pl.BlockSpec((1, SCAN_TILE, LANES), lambda b, i: (b, i, 0)),
           const((SHORT_CONV, W3)), const((1, LANES)), const((1, LANES))],
        out_specs=(ospec, ospec, ospec,
                   pl.BlockSpec((1, GDN_HEADS, 8, SCAN_TILE), lambda b, i: (b, 0, 0, i))),
        compiler_params=_cparams(2),
        name="gdn_prep",
    )(p, p, p, ps, conv_w, rate, dtb)


def _gdn_dir(q, k, v, gr, s_ref, o_ref, d):
    n = SCAN_TILE
    cf = jnp.concatenate([gr, jnp.zeros((LANES - 8, n), F32)], axis=0).T
    beta, gam_c, rem_c = cf[:, d:d + 1], cf[:, 2 + d:3 + d], cf[:, 4 + d:5 + d]
    gam_r = gr[2 + d:3 + d, :]
    ii, jj, same = _chunk_masks(n)
    incl = same & ((jj <= ii) if d == 0 else (jj >= ii))
    strict = same & ((jj < ii) if d == 0 else (jj > ii))
    kk = _dot_nt(k, k)
    yield
    qk = _dot_nt(q, k)
    yield
    dec = jnp.exp(jnp.where(incl, gam_c - gam_r, NEG_INF))
    a = jnp.where(strict, beta * kk * dec, 0.0)
    qkd = (qk * dec).astype(BF16)
    kf = k.astype(F32)
    x = jnp.concatenate([v.astype(F32) * beta, kf * (beta * jnp.exp(gam_c))], axis=1)
    pb = a.astype(BF16)
    x = x - jnp.dot(pb, x.astype(BF16), preferred_element_type=F32)
    yield
    for _ in range(5):
        pb = jnp.dot(pb, pb, preferred_element_type=F32).astype(BF16)
        yield
        x = x + jnp.dot(pb, x.astype(BF16), preferred_element_type=F32)
        yield
    u, w =x[:, :GDN_DV], x[:, GDN_DV:].astype(BF16)
    qd = (q.astype(F32) * jnp.exp(gam_c)).astype(BF16)
    kd = (kf * jnp.exp(rem_c)).astype(BF16)
    s = s_ref[...]
    nchunk = n // GDN_CHUNK
    v_new = [None] * nchunk
    qs = [None] * nchunk
    for c in (range(nchunk) if d == 0 else reversed(range(nchunk))):
        rows = slice(c * GDN_CHUNK, (c + 1) * GDN_CHUNK)
        r1 = jnp.dot(jnp.concatenate([w[rows], qd[rows]], axis=0), s.astype(BF16), preferred_element_type=F32)
        yield
        vn = u[rows] - r1[:GDN_CHUNK]
        qs[c] = r1[GDN_CHUNK:]
        v_new[c] = vn
        last = (c + 1) * GDN_CHUNK - 1 if d == 0 else c * GDN_CHUNK
        s = s * jnp.exp(cf[last:last + 1, 2 + d:3 + d]) + _dot_tn(kd[rows], vn.astype(BF16))
        yield
    s_ref[...] = s
    vn_all = jnp.concatenate(v_new, axis=0).astype(BF16)
    o_ref[0] = (jnp.concatenate(qs, axis=0) + jnp.dot(qkd, vn_all, preferred_element_type=F32)).astype(BF16)


def _round_robin(gens):
    gens = list(gens)
    while gens:
        alive = []
        for g in gens:
            try:
                next(g)
                alive.append(g)
            except StopIteration:
                pass
        gens = alive


def _gdn_scan_kernel(qf, kf, vf, gf, qb, kb, vb, gb, of_ref, ob_ref, sf_ref, sb_ref):
    @pl.when(pl.program_id(1) == 0)
    def _():
        sf_ref[...] = jnp.zeros_like(sf_ref)
        sb_ref[...] = jnp.zeros_like(sb_ref)

    chains = []
    for h in range(GDN_HEADS):
        sl = slice(h * GDN_DK, (h + 1) * GDN_DK)
        chains.append(_gdn_dir(qf[0, :, sl], kf[0, :, sl], vf[0, :, sl], gf[0, h],
                               sf_ref.at[h], of_ref.at[:, :, sl], 0))
        chains.append(_gdn_dir(qb[0, :, sl], kb[0, :, sl], vb[0, :, sl], gb[0, h],
                               sb_ref.at[h], ob_ref.at[:, :, sl], 1))
    _round_robin(chains)


def _gdn_scan(q, k, v, g):
    B = q.shape[0]
    tok = lambda order: pl.BlockSpec((1, SCAN_TILE, GDN_W), lambda b, t: (b, order(t), 0))
    gsp = lambda order: pl.BlockSpec((1, GDN_HEADS, 8, SCAN_TILE), lambda b, t: (b, 0, 0, order(t)))
    shp = jax.ShapeDtypeStruct((B, S_ALL, GDN_W), BF16)
    f, r = _fwd_tile, _bwd_tile
    state = pltpu.VMEM((GDN_HEADS, GDN_DK, GDN_DV), F32)
    return pl.pallas_call(
        _gdn_scan_kernel,
        out_shape=(shp, shp),
        grid=(B, SCAN_NT),
        in_specs=[tok(f), tok(f), tok(f), gsp(f), tok(r), tok(r), tok(r), gsp(r)],
        out_specs=(tok(f), tok(r)),
        scratch_shapes=[state, state],
        compiler_params=_cparams(2),
        name="gdn_scan",
    )(q, k, v, g, q, k, v, g)


GO_TM = 544


def _gdn_out_kernel(of_ref, ob_ref, z_ref, nw_ref, o_ref):
    o = of_ref[0].astype(F32) + ob_ref[0].astype(F32)
    z = z_ref[0].astype(F32)
    for h in range(GDN_HEADS):
        sl = slice(h * GDN_DV, (h + 1) * GDN_DV)
        oh = o[:, sl]
        y = oh * lax.rsqrt(jnp.mean(oh * oh, axis=-1, keepdims=True) + EPS) * nw_ref[...]
        o_ref[0, :, sl] = (y * _silu(z[:, sl])).astype(BF16)


def _gdn_out(o_f, o_b, p, norm_w):
    B = p.shape[0]
    zc = _P_START['g_z'] // GDN_W
    blk = pl.BlockSpec((1, GO_TM, GDN_W), lambda b, i: (b, i, 0))
    return pl.pallas_call(
        _gdn_out_kernel,
        out_shape=jax.ShapeDtypeStruct((B, S_ALL, GDN_W), BF16),
        grid=(B, S_ALL // GO_TM),
        in_specs=[blk, blk, pl.BlockSpec((1, GO_TM, GDN_W), lambda b, i: (b, i, zc)),
                  pl.BlockSpec((1, GDN_DV), lambda b, i: (0, 0))],
        out_specs=blk,
        compiler_params=_cparams(2),
        name="gdn_out",
    )(o_f, o_b, p, norm_w.reshape(1, GDN_DV))


SSD_BC = SSM_GROUPS * SSM_STATE
SSD_HD = 2 * SSM_HEADS
SSD_GW = (SSM_HEADS // SSM_GROUPS) * SSM_HEADDIM


def _ssd_prep_kernel(x_ref, prev_ref, next_ref, s_ref, cw_ref, cb_ref, a_ref, dtb_ref,
                     xs_out, b_out, c_out, bt_out, cf_out, cr_out):
    i = pl.program_id(1)
    prev_row, next_row = _halo_rows(i, prev_ref, next_ref)
    y = _conv3_silu(x_ref[0].astype(F32), prev_row, next_row, cw_ref[...], cb_ref[...])
    xs_out[0] = y[:, :SSM_W]
    b_out[0] = y[:, SSM_W:SSM_W + SSD_BC].astype(BF16)
    c_out[0] = y[:, SSM_W + SSD_BC:].astype(BF16)
    bt_out[0] = y[:, SSM_W:SSM_W + SSD_BC].T.astype(BF16)

    s = s_ref[0]
    lane = lax.broadcasted_iota(jnp.int32, (1, LANES), 1)
    dt = _softplus(s + dtb_ref[...])
    on = (lane >= SSD_HD) & (lane < 2 * SSD_HD)
    a = jnp.where(on, dt * a_ref[...], 0.0)
    n = SCAN_TILE
    ii = lax.broadcasted_iota(jnp.int32, (n, 1), 0)
    jj = lax.broadcasted_iota(jnp.int32, (1, n), 1)
    one = lambda m: jnp.where(m, 1.0, 0.0).astype(BF16)
    fwd_lane = lane < SSD_HD + SSM_HEADS
    cum = jnp.where(fwd_lane, _split_dot(one(jj <= ii), a), _split_dot(one(jj >= ii), a))
    rem = jnp.where(fwd_lane, _split_dot(one(jj > ii), a), _split_dot(one(jj < ii), a))
    cf = jnp.where(lane < SSD_HD, pltpu.roll(dt, LANES - SSD_HD, axis=1),
                   jnp.where(lane < 2 * SSD_HD, cum, pltpu.roll(rem, SSD_HD, axis=1)))
    cf_out[0] = cf
    cr_out[0] = cf.T[SSD_HD:2 * SSD_HD, :]


def _ssd_prep(p, ps, conv_w, conv_b, A_log, dt_bias):
    B = p.shape[0]
    W = SSM_CONV_DIM
    a_vec = jnp.zeros((1, LANES), F32).at[0, SSD_HD:2 * SSD_HD].set(-jnp.exp(A_log).reshape(-1))
    dtb = jnp.zeros((1, LANES), F32).at[0, SSD_HD:2 * SSD_HD].set(dt_bias.reshape(-1))
    const = lambda shape: pl.BlockSpec(shape, lambda b, i: (0, 0))
    tok = lambda w: pl.BlockSpec((1, SCAN_TILE, w), lambda b, i: (b, i, 0))
    return pl.pallas_call(
        _ssd_prep_kernel,
        out_shape=(jax.ShapeDtypeStruct((B, S_ALL, SSM_W), F32),
                   jax.ShapeDtypeStruct((B, S_ALL, SSD_BC), BF16),
                   jax.ShapeDtypeStruct((B, S_ALL, SSD_BC), BF16),
                   jax.ShapeDtypeStruct((B, SSD_BC, S_ALL), BF16),
                   jax.ShapeDtypeStruct((B, S_ALL, LANES), F32),
                   jax.ShapeDtypeStruct((B, SSD_HD, S_ALL), F32)),
        grid=(B, SCAN_NT),
        in_specs=_halo_specs(W, _P_START['s_xbc'] // W)
        + [pl.BlockSpec((1, SCAN_TILE, LANES), lambda b, i: (b, i, 0)),
           const((SHORT_CONV, W)), const((1, W)), const((1, LANES)), const((1, LANES))],
        out_specs=(tok(SSM_W), tok(SSD_BC), tok(SSD_BC),
                   pl.BlockSpec((1, SSD_BC, SCAN_TILE), lambda b, i: (b, 0, i)),
                   tok(LANES),
                   pl.BlockSpec((1, SSD_HD, SCAN_TILE), lambda b, i: (b, 0, i))),
        compiler_params=_cparams(2),
        name="ssd_prep",
    )(p, p, p, ps, conv_w, conv_b.reshape(1, W), a_vec, dtb)


def _per_head_lanes(cols):
    lane = lax.broadcasted_iota(jnp.int32, (1, LANES), 1)
    lo = lane < SSM_HEADDIM
    return jnp.concatenate([jnp.where(lo, cols[2 * j], cols[2 * j + 1]) for j in range(len(cols) // 2)], axis=1)


def _ssd_dir(x_ref, b_ref, c_ref, bt_ref, cf_ref, cr_ref, h_ref, y_ref, d, g):
    n = SCAN_TILE
    hpg = SSM_HEADS // SSM_GROUPS
    hd0 = d * SSM_HEADS + g * hpg
    gs = slice(g * SSM_STATE, (g + 1) * SSM_STATE)
    xl = slice(g * SSD_GW, (g + 1) * SSD_GW)
    cf = cf_ref[0]
    cr = cr_ref[0]
    cm = c_ref[0, :, gs]
    col = lambda base, h: cf[:, base + hd0 + h:base + hd0 + h + 1]
    heads = range(hpg)
    last = n - 1 if d == 0 else 0
    xdt = x_ref[0, :, xl] * _per_head_lanes([col(0, h) for h in heads])
    xdt_b = xdt.astype(BF16)
    xdec = (xdt * _per_head_lanes([jnp.exp(col(2 * SSD_HD, h)) for h in heads])).astype(BF16)
    cb = _dot_nt(cm, b_ref[0, :, gs])
    yield
    h_prev = h_ref[g]
    y_off = jnp.dot(cm, h_prev.astype(BF16), preferred_element_type=F32)
    yield
    y_off = y_off * _per_head_lanes([jnp.exp(col(SSD_HD, h)) for h in heads])
    etot = _per_head_lanes([jnp.exp(cf[last:last + 1, SSD_HD + hd0 + h:SSD_HD + hd0 + h + 1]) for h in heads])
    h_ref[g] = h_prev * etot + jnp.dot(bt_ref[0, gs, :], xdec, preferred_element_type=F32)
    yield
    ii = lax.broadcasted_iota(jnp.int32, (n, 1), 0)
    jj = lax.broadcasted_iota(jnp.int32, (1, n), 1)
    causal = (jj <= ii) if d == 0 else (jj >= ii)
    lo = lax.broadcasted_iota(jnp.int32, (1, LANES), 1) < SSM_HEADDIM
    pair_out = []
    for j in range(hpg // 2):
        ys = []
        for e in range(2):
            h = 2 * j + e
            seg = col(SSD_HD, h) - cr[hd0 + h:hd0 + h + 1, :]
            sc = (cb * jnp.exp(jnp.where(causal, seg, NEG_INF))).astype(BF16)
            ys.append(jnp.dot(sc, xdt_b[:, j * LANES:(j + 1) * LANES], preferred_element_type=F32))
            yield
        pair_out.append(jnp.where(lo, ys[0], ys[1]))
    y_ref[0, :, xl] = (jnp.concatenate(pair_out, axis=1) + y_off).astype(BF16)


def _ssd_scan_kernel(xf, bf, cf_, btf, colf, rowf, xb, bb, cb_, btb, colb, rowb, yf_ref, yb_ref, hf_ref, hb_ref):
    @pl.when(pl.program_id(1) == 0)
    def _():
        hf_ref[...] = jnp.zeros_like(hf_ref)
        hb_ref[...] = jnp.zeros_like(hb_ref)

    chains = []
    for g in range(SSM_GROUPS):
        chains.append(_ssd_dir(xf, bf, cf_, btf, colf, rowf, hf_ref, yf_ref, 0, g))
        chains.append(_ssd_dir(xb, bb, cb_, btb, colb, rowb, hb_ref, yb_ref, 1, g))
    _round_robin(chains)


def _ssd_scan(xs, bm, cm, bt, cf, cr):
    B = xs.shape[0]

    def specs(order):
        tok = lambda w: pl.BlockSpec((1, SCAN_TILE, w), lambda b, t: (b, order(t), 0))
        return [tok(SSM_W), tok(SSD_BC), tok(SSD_BC),
                pl.BlockSpec((1, SSD_BC, SCAN_TILE), lambda b, t: (b, 0, order(t))),
                tok(LANES),
                pl.BlockSpec((1, SSD_HD, SCAN_TILE), lambda b, t: (b, 0, order(t)))]

    shp = jax.ShapeDtypeStruct((B, S_ALL, SSM_W), BF16)
    out = lambda order: pl.BlockSpec((1, SCAN_TILE, SSM_W), lambda b, t: (b, order(t), 0))
    hshape = pltpu.VMEM((SSM_GROUPS, SSM_STATE, SSD_GW), F32)
    args = (xs, bm, cm, bt, cf, cr)
    return pl.pallas_call(
        _ssd_scan_kernel,
        out_shape=(shp, shp),
        grid=(B, SCAN_NT),
        in_specs=specs(_fwd_tile) + specs(_bwd_tile),
        out_specs=(out(_fwd_tile), out(_bwd_tile)),
        scratch_shapes=[hshape, hshape],
        compiler_params=_cparams(2),
        name="ssd_scan",
    )(*args, *args)


SO_TM = 544


def _ssd_out_kernel(yf_ref, yb_ref, xs_ref, z_ref, d_ref, nw_ref, o_ref):
    y = yf_ref[0].astype(F32) + yb_ref[0].astype(F32) + d_ref[...] * xs_ref[0]
    y = y * _silu(z_ref[0].astype(F32))
    o_ref[0] = (y * lax.rsqrt(jnp.mean(y * y, axis=-1, keepdims=True) + EPS) * nw_ref[...]).astype(BF16)


def _ssd_out(y_f, y_b, xs, p, d_skip, norm_w):
    B = p.shape[0]
    zc = _P_START['s_z'] // SSM_W
    blk = pl.BlockSpec((1, SO_TM, SSM_W), lambda b, i: (b, i, 0))
    vec = pl.BlockSpec((1, SSM_W), lambda b, i: (0, 0))
    return pl.pallas_call(
        _ssd_out_kernel,
        out_shape=jax.ShapeDtypeStruct((B, S_ALL, SSM_W), BF16),
        grid=(B, S_ALL // SO_TM),
        in_specs=[blk, blk, blk, pl.BlockSpec((1, SO_TM, SSM_W), lambda b, i: (b, i, zc)), vec, vec],
        out_specs=blk,
        compiler_params=_cparams(2),
        name="ssd_out",
    )(y_f, y_b, xs, p, jnp.repeat(d_skip, SSM_HEADDIM).reshape(1, SSM_W), norm_w.reshape(1, SSM_W))


def _repack_w_in(w):
    cut = lambda names: [w[:, _IN_START[n]:_IN_START[n] + _IN_SIZE[n]] for n in names]
    zeros = lambda n: jnp.zeros((w.shape[0], n), w.dtype)
    n_small = sum(_IN_SIZE[n] for n in _S_ORDER)
    cols = cut(_P_ORDER) + [zeros(D_INP - LANES - _off)] + cut(_S_ORDER) + [zeros(LANES - n_small)]
    return jnp.concatenate(cols, axis=1).astype(BF16)


def kernel(x, c, ctx, c_ctx, norm_w, ada_w, ada_b, w_in, gdn_conv_w, gdn_A_log, gdn_dt_bias, gdn_norm_w, na_q_norm, na_k_norm, na_rpb, mla_qa_norm, mla_w_uq, mla_kva_norm, mla_w_ukv, mla_q_norm, mla_k_norm, ssm_conv_w, ssm_conv_b, ssm_A_log, ssm_dt_bias, ssm_D, ssm_norm_w, w_out):
    B = x.shape[0]
    xs = jnp.concatenate([x, ctx], axis=1)
    c8 = jnp.zeros((8, D_MODEL), F32).at[:B].set(c).at[B].set(c_ctx)
    mods = _ada_all(c8, ada_w, ada_b)
    cos_np, sin_np = _rope_tables()
    cos, sin = jnp.asarray(cos_np), jnp.asarray(sin_np)
    na_bias = _na_bias(na_rpb)
    for l in range(DEPTH):
        shift, scale, gate = jnp.split(mods[l, :B], 3, axis=-1)
        shift_c, scale_c, gate_c = jnp.split(mods[l, B], 3, axis=-1)
        bc = lambda v: jnp.broadcast_to(v[None], (B, D_MODEL))
        mod4 = jnp.stack([shift, scale, bc(shift_c), bc(scale_c)], axis=1)
        gate2 = jnp.stack([gate, bc(gate_c)], axis=1)
        p, ps = _inproj(xs, norm_w[l], mod4, _repack_w_in(w_in[l]))

        gq, gk, gv, gg = _gdn_prep(p, ps, gdn_conv_w[l], gdn_A_log[l], gdn_dt_bias[l])
        o_f, o_b = _gdn_scan(gq, gk, gv, gg)
        oa = _gdn_out(o_f, o_b, p, gdn_norm_w[l])

        qn, kn, vn = _na_prep(p, na_q_norm[l], na_k_norm[l])
        ob = _na_attend(qn, kn, vn, p, na_bias[l])

        mq, mk, mv = _mla_prep(p, cos, sin, mla_qa_norm[l], mla_w_uq[l], mla_kva_norm[l], mla_w_ukv[l],
                               mla_q_norm[l], mla_k_norm[l])
        oc = _mla_attend(mq, mk, mv, p)

        sx, sb, sc, sbt, scf, scr = _ssd_prep(p, ps, ssm_conv_w[l], ssm_conv_b[l], ssm_A_log[l], ssm_dt_bias[l])
        y_f, y_b = _ssd_scan(sx, sb, sc, sbt, scf, scr)
        od = _ssd_out(y_f, y_b, sx, p, ssm_D[l], ssm_norm_w[l])
        xs = _outproj((oa, ob, oc, od), w_out[l].astype(BF16), xs, gate2, last=(l == DEPTH - 1))
    return xs
```

```python
import functools

import jax
import jax.numpy as jnp
import numpy as np
from jax import lax
from jax.experimental import pallas as pl
from jax.experimental.pallas import tpu as pltpu

F32 = jnp.float32
BF16 = jnp.bfloat16

D_MODEL = 2048
BATCH = 4
SEQ = 4096
DEPTH = 4
GRID_W = 64
GRID_H = SEQ // GRID_W
CTX_LEN = 256
S_ALL = SEQ + CTX_LEN
EPS = 1e-6
NEG_INF = -1e30
LOG2E = 1.4426950408889634

D_BRANCH = 512
D_MIX = 4 * D_BRANCH
SHORT_CONV = 3

GDN_HEADS = 4
GDN_DK = 128
GDN_DV = 128
GDN_W = GDN_HEADS * GDN_DV
GDN_CHUNK = 64

NA_HEADS = 4
NA_DH = 128
NA_W = NA_HEADS * NA_DH
NA_WIN_R = 8
NA_WIN_C = 16

MLA_HEADS = 4
MLA_Q_RANK = 384
MLA_KV_RANK = 256
MLA_NOPE = 128
MLA_ROPE = 64
MLA_QK = MLA_NOPE + MLA_ROPE
MLA_V = 128
MLA_W = MLA_HEADS * MLA_V
ROPE_THETA = 10000.0

SSM_HEADDIM = 64
SSM_HEADS = D_BRANCH // SSM_HEADDIM
SSM_W = SSM_HEADS * SSM_HEADDIM
SSM_GROUPS = 2
SSM_STATE = 128
SSM_CONV_DIM = SSM_W + 2 * SSM_GROUPS * SSM_STATE

IN_SIZES = (3 * GDN_W, GDN_W, 2 * GDN_HEADS, 2 * GDN_HEADS,
            3 * NA_W, NA_W,
            MLA_Q_RANK, MLA_KV_RANK, MLA_ROPE, MLA_W,
            SSM_W, SSM_CONV_DIM, 2 * SSM_HEADS)
D_IN = sum(IN_SIZES)
_IN_NAMES = ('g_qkv', 'g_z', 'g_beta', 'g_alpha', 'n_qkv', 'n_z',
             'm_q', 'm_kv', 'm_kr', 'm_z', 's_z', 's_xbc', 's_dt')
_IN_START = dict(zip(_IN_NAMES, np.cumsum((0,) + IN_SIZES[:-1]).tolist()))
_IN_SIZE = dict(zip(_IN_NAMES, IN_SIZES))

LANES = 128
_P_ORDER = ('g_qkv', 'g_z', 'n_qkv', 'n_z', 'm_z', 's_z', 's_xbc',
            'm_q', 'm_kv', 'm_kr', 'm_kr')
_S_ORDER = ('g_beta', 'g_alpha', 's_dt')
_P_START = {}
_off = 0
for _n in _P_ORDER:
    _P_START.setdefault(_n, _off)
    _off += _IN_SIZE[_n]
MXU_N = 256
D_INP = -(-_off // (2 * MXU_N)) * (2 * MXU_N)
P_MLA_BLK = MLA_Q_RANK + MLA_KV_RANK + 2 * MLA_ROPE
assert _P_START['m_q'] % P_MLA_BLK == 0 and D_INP % LANES == 0

VMEM_LIMIT = 52 * 1024 * 1024


def _silu(x):
    return x * jax.nn.sigmoid(x)


def _dot_nt(a, b):
    return lax.dot_general(a, b, (((1,), (1,)), ((), ())), preferred_element_type=F32)


def _dot_tn(a, b):
    return lax.dot_general(a, b, (((0,), (0,)), ((), ())), preferred_element_type=F32)


def _cparams(n_axes):
    return pltpu.CompilerParams(dimension_semantics=("arbitrary",) * n_axes,
                                vmem_limit_bytes=VMEM_LIMIT)


def _ada_kernel(c_ref, w_ref, b_ref, o_ref):
    a = _silu(c_ref[...]).astype(BF16)
    o_ref[0] = jnp.dot(a, w_ref[0].astype(BF16), preferred_element_type=F32) + b_ref[0]


def _ada_all(c8, ada_w, ada_b):
    tn = 1536
    L = ada_w.shape[0]
    n3 = ada_w.shape[2]
    return pl.pallas_call(
        _ada_kernel,
        out_shape=jax.ShapeDtypeStruct((L, 8, n3), F32),
        grid=(L, n3 // tn),
        in_specs=[pl.BlockSpec((8, D_MODEL), lambda l, j: (0, 0)),
                  pl.BlockSpec((1, D_MODEL, tn), lambda l, j: (l, 0, j)),
                  pl.BlockSpec((1, 1, tn), lambda l, j: (l, 0, j))],
        out_specs=pl.BlockSpec((1, 8, tn), lambda l, j: (l, 0, j)),
        compiler_params=_cparams(2),
        name="ada_mod",
    )(c8, ada_w, ada_b.reshape(L, 1, n3))


IN_TM = 1088
IN_TN = 1024
IN_RC = 16
IN_SECTIONS = 4


def _inproj_kernel(x_ref, nw_ref, mod_ref, w_ref, o_ref, os_ref, h_scr):
    i = pl.program_id(1)
    j = pl.program_id(2)

    @pl.when(j == 0)
    def _():
        m = mod_ref[0]
        nw = nw_ref[...]
        gain_l = nw * (1.0 + m[1:2])
        gain_c = nw * (1.0 + m[3:4])
        sec = IN_TM // IN_SECTIONS
        for c in range(IN_SECTIONS):
            for r0 in range(c * sec, (c + 1) * sec, IN_RC):
                x = x_ref[0, r0:r0 + IN_RC, :]
                ms = jnp.mean(x * x, axis=-1, keepdims=True)
                is_ctx = i * IN_TM + r0 >= SEQ
                gain = jnp.where(is_ctx, gain_c, gain_l)
                shift = jnp.where(is_ctx, m[2:3], m[0:1])
                h_scr[r0:r0 + IN_RC, :] = (x * lax.rsqrt(ms + EPS) * gain + shift).astype(BF16)
            rows = slice(c * sec, (c + 1) * sec)
            o_ref[0, rows, :] = jnp.dot(h_scr[rows, :], w_ref[...], preferred_element_type=F32).astype(BF16)

    @pl.when(j > 0)
    def _():
        y = jnp.dot(h_scr[...], w_ref[...], preferred_element_type=F32)
        o_ref[0] = y.astype(BF16)

        @pl.when(j == pl.num_programs(2) - 1)
        def _():
            os_ref[0] = y[:, IN_TN - LANES:]


def _inproj(xs, norm_w, mod4, w_main):
    B = xs.shape[0]
    return pl.pallas_call(
        _inproj_kernel,
        out_shape=(jax.ShapeDtypeStruct((B, S_ALL, D_INP), BF16),
                   jax.ShapeDtypeStruct((B, S_ALL, LANES), F32)),
        grid=(B, S_ALL // IN_TM, D_INP // IN_TN),
        in_specs=[pl.BlockSpec((1, IN_TM, D_MODEL), lambda b, i, j: (b, i, 0)),
                  pl.BlockSpec((1, D_MODEL), lambda b, i, j: (0, 0)),
                  pl.BlockSpec((1, 4, D_MODEL), lambda b, i, j: (b, 0, 0)),
                  pl.BlockSpec((D_MODEL, IN_TN), lambda b, i, j: (0, j))],
        out_specs=(pl.BlockSpec((1, IN_TM, IN_TN), lambda b, i, j: (b, i, j)),
                   pl.BlockSpec((1, IN_TM, LANES), lambda b, i, j: (b, i, 0))),
        scratch_shapes=[pltpu.VMEM((IN_TM, D_MODEL), BF16)],
        compiler_params=_cparams(3),
        name="inproj",
    )(xs, norm_w.reshape(1, D_MODEL), mod4, w_main)


OUT_TM = 544
OUT_TM_LAST = 512


def _gdn_gate(o_f, o_b, z, nw):
    o = o_f.astype(F32) + o_b.astype(F32)
    z = z.astype(F32)
    outs = []
    for h in range(GDN_HEADS):
        sl = slice(h * GDN_DV, (h + 1) * GDN_DV)
        oh = o[:, sl]
        y = oh * lax.rsqrt(jnp.mean(oh * oh, axis=-1, keepdims=True) + EPS) * nw
        outs.append((y * _silu(z[:, sl])).astype(BF16))
    return jnp.concatenate(outs, axis=-1)


def _ssd_gate(y_f, y_b, xs, z, d_skip, nw):
    y = y_f.astype(F32) + y_b.astype(F32) + d_skip * xs
    y = y * _silu(z.astype(F32))
    return (y * lax.rsqrt(jnp.mean(y * y, axis=-1, keepdims=True) + EPS) * nw).astype(BF16)


def _outproj_kernel(of_ref, ob_ref, gz_ref, gnw_ref, na_ref, mla_ref, yf_ref, yb_ref, sx_ref, sz_ref,
                    dsk_ref, snw_ref, w_ref, x_ref, g_ref, o_ref, *, tm):
    i = pl.program_id(1)
    branches = (_gdn_gate(of_ref[0], ob_ref[0], gz_ref[0], gnw_ref[...]),
                na_ref[0], mla_ref[0],
                _ssd_gate(yf_ref[0], yb_ref[0], sx_ref[0], sz_ref[0], dsk_ref[...], snw_ref[...]))
    y = None
    for n, a in enumerate(branches):
        t = jnp.dot(a, w_ref[n * D_BRANCH:(n + 1) * D_BRANCH, :], preferred_element_type=F32)
        y = t if y is None else y + t
    row = i * tm + lax.broadcasted_iota(jnp.int32, (tm, 1), 0)
    g = g_ref[0]
    gate = jnp.where(row >= SEQ, g[1:2], g[0:1])
    o_ref[0] = x_ref[0] + gate * y


def _outproj(gdn, na, mla, ssd, p, w_out_b, xs, gate2, last):
    B = xs.shape[0]
    o_f, o_b, g_nw = gdn
    y_f, y_b, s_x, d_skip, s_nw = ssd
    tm, rows = (OUT_TM_LAST, SEQ) if last else (OUT_TM, S_ALL)
    a_spec = pl.BlockSpec((1, tm, D_BRANCH), lambda b, i: (b, i, 0))
    z_spec = lambda name: pl.BlockSpec((1, tm, D_BRANCH), lambda b, i: (b, i, _P_START[name] // D_BRANCH))
    x_spec = pl.BlockSpec((1, tm, D_MODEL), lambda b, i: (b, i, 0))
    vec = lambda n: pl.BlockSpec((1, n), lambda b, i: (0, 0))
    return pl.pallas_call(
        functools.partial(_outproj_kernel, tm=tm),
        out_shape=jax.ShapeDtypeStruct((B, rows, D_MODEL), F32),
        grid=(B, rows // tm),
        in_specs=[a_spec, a_spec, z_spec('g_z'), vec(GDN_DV), a_spec, a_spec,
                  a_spec, a_spec, a_spec, z_spec('s_z'), vec(SSM_W), vec(SSM_W),
                  pl.BlockSpec((D_MIX, D_MODEL), lambda b, i: (0, 0)),
                  x_spec,
                  pl.BlockSpec((1, 2, D_MODEL), lambda b, i: (b, 0, 0))],
        out_specs=x_spec,
        compiler_params=_cparams(2),
        name="outproj",
    )(o_f, o_b, p, g_nw.reshape(1, GDN_DV), na, mla,
      y_f, y_b, s_x, p, jnp.repeat(d_skip, SSM_HEADDIM).reshape(1, SSM_W), s_nw.reshape(1, SSM_W),
      w_out_b, xs, gate2)


NA_TM = 544
NA_RB = 4
NA_QB = NA_RB * GRID_W
NA_KR = 12
NA_KW = NA_KR * GRID_W
NA_NBLK = GRID_H // NA_RB
assert NA_QB == CTX_LEN


def _na_prep_kernel(q_ref, k_ref, v_ref, qn_ref, kn_ref, qo_ref, ko_ref, vo_ref):
    def headnorm(x, w, extra):
        outs = []
        for h in range(NA_HEADS):
            xh = x[:, h * NA_DH:(h + 1) * NA_DH]
            ms = jnp.mean(xh * xh, axis=-1, keepdims=True)
            outs.append((xh * lax.rsqrt(ms + EPS) * w * extra).astype(BF16))
        return jnp.concatenate(outs, axis=-1)

    qo_ref[0] = headnorm(q_ref[0].astype(F32), qn_ref[...], NA_DH ** -0.5)
    ko_ref[0] = headnorm(k_ref[0].astype(F32), kn_ref[...], 1.0)
    vo_ref[0] = v_ref[0]


def _na_prep(p, q_norm, k_norm):
    B = p.shape[0]
    c0 = _P_START['n_qkv'] // NA_W
    spec = lambda c: pl.BlockSpec((1, NA_TM, NA_W), lambda b, i, c=c: (b, i, c))
    ospec = pl.BlockSpec((1, NA_TM, NA_W), lambda b, i: (b, i, 0))
    wspec = pl.BlockSpec((1, NA_DH), lambda b, i: (0, 0))
    shp = jax.ShapeDtypeStruct((B, S_ALL, NA_W), BF16)
    return pl.pallas_call(
        _na_prep_kernel,
        out_shape=(shp, shp, shp),
        grid=(B, S_ALL // NA_TM),
        in_specs=[spec(c0), spec(c0 + 1), spec(c0 + 2), wspec, wspec],
        out_specs=(ospec, ospec, ospec),
        compiler_params=_cparams(2),
        name="na_prep",
    )(p, p, p, q_norm.reshape(1, NA_DH), k_norm.reshape(1, NA_DH))


def _na_kernel(q_ref, k_ref, v_ref, z_ref, bias_ref, o_ref):
    rb = pl.program_id(1)
    q = q_ref[0]
    z = z_ref[0].astype(F32)
    kc = k_ref[0, SEQ:S_ALL, :]
    vc = v_ref[0, SEQ:S_ALL, :]

    def finish(h, o, l):
        sl = slice(h * NA_DH, (h + 1) * NA_DH)
        o_ref[0, :, sl] = (o / l * _silu(z[:, sl])).astype(BF16)

    @pl.when(rb < NA_NBLK)
    def _latent():
        base = jnp.clip(rb * NA_RB - NA_RB, 0, GRID_H - NA_KR)
        start = pl.multiple_of(base * GRID_W, GRID_W)
        kw = k_ref[0, pl.ds(start, NA_KW), :]
        vw = v_ref[0, pl.ds(start, NA_KW), :]
        for h in range(NA_HEADS):
            sl = slice(h * NA_DH, (h + 1) * NA_DH)
            s_w = _dot_nt(q[:, sl], kw[:, sl]) + bias_ref[0, h].astype(F32)
            s_c = _dot_nt(q[:, sl], kc[:, sl])
            m = jnp.maximum(jnp.max(s_w, axis=-1, keepdims=True), jnp.max(s_c, axis=-1, keepdims=True))
            p_w = jnp.exp(s_w - m)
            p_c = jnp.exp(s_c - m)
            l = jnp.sum(p_w, axis=-1, keepdims=True) + jnp.sum(p_c, axis=-1, keepdims=True)
            o = (jnp.dot(p_w.astype(BF16), vw[:, sl], preferred_element_type=F32)
                 + jnp.dot(p_c.astype(BF16), vc[:, sl], preferred_element_type=F32))
            finish(h, o, l)

    @pl.when(rb == NA_NBLK)
    def _context():
        for h in range(NA_HEADS):
            sl = slice(h * NA_DH, (h + 1) * NA_DH)
            s_c = _dot_nt(q[:, sl], kc[:, sl])
            m = jnp.max(s_c, axis=-1, keepdims=True)
            p_c = jnp.exp(s_c - m)
            l = jnp.sum(p_c, axis=-1, keepdims=True)
            o = jnp.dot(p_c.astype(BF16), vc[:, sl], preferred_element_type=F32)
            finish(h, o, l)


def _na_bias_index():
    dr = np.zeros((3, NA_RB, NA_KR), np.int64)
    ok = np.zeros((3, NA_RB, NA_KR), bool)
    for ci, rb in enumerate((0, 1, NA_NBLK - 1)):
        base = int(np.clip(rb * NA_RB - NA_RB, 0, GRID_H - NA_KR))
        qr = rb * NA_RB + np.arange(NA_RB)[:, None]
        kr = base + np.arange(NA_KR)[None, :]
        row0 = np.clip(qr - NA_WIN_R // 2, 0, GRID_H - NA_WIN_R)
        ok[ci] = (kr >= row0) & (kr < row0 + NA_WIN_R)
        dr[ci] = np.clip(kr - qr + NA_WIN_R - 1, 0, 2 * NA_WIN_R - 2)
    qc = np.arange(GRID_W)[:, None]
    kc = np.arange(GRID_W)[None, :]
    win0 = np.clip(qc - NA_WIN_C // 2, 0, GRID_W - NA_WIN_C)
    col_ok = (kc >= win0) & (kc < win0 + NA_WIN_C)
    return dr, ok, col_ok


def _na_bias(rpb):
    L, H = rpb.shape[:2]
    nd = 2 * NA_WIN_R - 1
    dr, ok, col_ok = _na_bias_index()
    left = GRID_W - NA_WIN_C
    f = jnp.pad(rpb, ((0, 0), (0, 0), (0, 0), (left, 2 * GRID_W - (2 * NA_WIN_C - 1) - left)))
    skew = jnp.broadcast_to(f[:, :, :, None, :], (L, H, nd, GRID_W, 2 * GRID_W))
    skew = skew.reshape(L, H, nd, -1)[..., :GRID_W * (2 * GRID_W - 1)].reshape(L, H, nd, GRID_W, 2 * GRID_W - 1)
    toe = skew[..., GRID_W - 1:]
    toe = jnp.where(col_ok, toe, NEG_INF)
    dead = jnp.full((L, H, GRID_W, GRID_W), NEG_INF, F32)
    blocks = [toe[:, :, dr[c, a, b]] if ok[c, a, b] else dead
              for c in range(3) for a in range(NA_RB) for b in range(NA_KR)]
    t = jnp.stack(blocks, axis=2).reshape(L, H, 3, NA_RB, NA_KR, GRID_W, GRID_W)
    return jnp.transpose(t.astype(BF16), (0, 2, 1, 3, 5, 4, 6)).reshape(L, 3, H, NA_QB, NA_KW)


def _na_attend(qn, kn, vn, p, bias):
    B = p.shape[0]
    zc = _P_START['n_z'] // NA_W
    last = NA_NBLK - 1
    blk = pl.BlockSpec((1, NA_QB, NA_W), lambda b, r: (b, r, 0))
    full = pl.BlockSpec((1, S_ALL, NA_W), lambda b, r: (b, 0, 0))
    return pl.pallas_call(
        _na_kernel,
        out_shape=jax.ShapeDtypeStruct((B, S_ALL, NA_W), BF16),
        grid=(B, NA_NBLK + 1),
        in_specs=[blk, full, full,
                  pl.BlockSpec((1, NA_QB, NA_W), lambda b, r: (b, r, zc)),
                  pl.BlockSpec((1, NA_HEADS, NA_QB, NA_KW),
                               lambda b, r: (jnp.where(r == 0, 0, jnp.where(r >= last, 2, 1)), 0, 0, 0))],
        out_specs=blk,
        compiler_params=_cparams(2),
        name="na_attend",
    )(qn, kn, vn, p, bias)


MP_TM = 544
MLA_HW = 2 * LANES
MLA_TQ = 1024
MLA_SUBQ = 512
MLA_TK = 512


def _rope_tables():
    n_freq = MLA_ROPE // 4
    inv_freq = ROPE_THETA ** (-np.arange(n_freq, dtype=np.float64) / n_freq)
    t = np.arange(SEQ)
    ar = (t // GRID_W)[:, None] * inv_freq
    ac = (t % GRID_W)[:, None] * inv_freq
    cos = np.concatenate([np.cos(ar), np.cos(ar), np.cos(ac), np.cos(ac)], axis=1)
    sin = np.concatenate([-np.sin(ar), np.sin(ar), -np.sin(ac), np.sin(ac)], axis=1)
    cos = np.concatenate([cos, np.ones((CTX_LEN, MLA_ROPE))], axis=0)
    sin = np.concatenate([sin, np.zeros((CTX_LEN, MLA_ROPE))], axis=0)
    return (np.tile(cos, (1, MLA_HEADS)).astype(np.float32), np.tile(sin, (1, MLA_HEADS)).astype(np.float32))


def _rope_rotate(t, cos, sin):
    w = t.shape[1]
    lane = lax.broadcasted_iota(jnp.int32, (1, w), 1)
    first = (lane & 31) < 16
    up = pltpu.roll(t, w - 16, axis=1)
    dn = pltpu.roll(t, 16, axis=1)
    return t * cos + jnp.where(first, up, dn) * sin


def _mla_prep_kernel(p_ref, cos_ref, sin_ref, qan_ref, wuq_ref, kvan_ref, wukv_ref, qn_ref, kn_ref,
                     q_out, k_out, v_out):
    x = p_ref[0].astype(F32)
    cq = x[:, :MLA_Q_RANK]
    ckv = x[:, MLA_Q_RANK:MLA_Q_RANK + MLA_KV_RANK]
    kr2 = x[:, MLA_Q_RANK + MLA_KV_RANK:]

    def rms(t, w):
        return t * lax.rsqrt(jnp.mean(t * t, axis=-1, keepdims=True) + EPS) * w

    qf = jnp.dot(rms(cq, qan_ref[...]).astype(BF16), wuq_ref[...], preferred_element_type=F32)
    kvf = jnp.dot(rms(ckv, kvan_ref[...]).astype(BF16), wukv_ref[...], preferred_element_type=F32)
    cos = cos_ref[...]
    sin = sin_ref[...]
    qw = qn_ref[...]
    kw = kn_ref[...]
    n_all = MLA_HEADS * MLA_NOPE
    lane = lax.broadcasted_iota(jnp.int32, (1, LANES), 1)
    halves = (lane < MLA_ROPE, lane >= MLA_ROPE)

    q_rope = qf[:, n_all:]
    q_rope_sq = q_rope * q_rope
    q_rot = _rope_rotate(q_rope * qw[:, n_all:], cos, sin)
    kr_sq = jnp.sum(jnp.where(halves[0], kr2 * kr2, 0.0), axis=-1, keepdims=True)
    k_rot = _rope_rotate(kr2 * kw[:, n_all:], cos[:, :LANES], sin[:, :LANES])
    for h in range(MLA_HEADS):
        half = halves[h % 2]
        vsl = slice((h // 2) * LANES, (h // 2 + 1) * LANES)
        nsl = slice(h * MLA_NOPE, (h + 1) * MLA_NOPE)
        q_nope = qf[:, nsl]
        ss = (jnp.sum(q_nope * q_nope, axis=-1, keepdims=True)
              + jnp.sum(jnp.where(half, q_rope_sq[:, vsl], 0.0), axis=-1, keepdims=True))
        r = lax.rsqrt(ss * (1.0 / MLA_QK) + EPS) * (MLA_QK ** -0.5 * LOG2E)
        q_out[0, :, h * MLA_HW:h * MLA_HW + LANES] = (q_nope * qw[:, nsl] * r).astype(BF16)
        q_out[0, :, h * MLA_HW + LANES:(h + 1) * MLA_HW] = (jnp.where(half, q_rot[:, vsl], 0.0) * r).astype(BF16)
        k_nope = kvf[:, nsl]
        ss = jnp.sum(k_nope * k_nope, axis=-1, keepdims=True) + kr_sq
        r = lax.rsqrt(ss * (1.0 / MLA_QK) + EPS)
        k_out[0, :, h * MLA_HW:h * MLA_HW + LANES] = (k_nope * kw[:, nsl] * r).astype(BF16)
        k_out[0, :, h * MLA_HW + LANES:(h + 1) * MLA_HW] = (jnp.where(half, k_rot, 0.0) * r).astype(BF16)
    v_out[0] = kvf[:, n_all:].astype(BF16)


def _mla_prep(p, cos, sin, qa_norm, w_uq, kva_norm, w_ukv, q_norm, k_norm):
    B = p.shape[0]
    H = MLA_HEADS
    uq = w_uq.reshape(MLA_Q_RANK, H, MLA_QK)
    uq = jnp.concatenate([uq[:, :, :MLA_NOPE].reshape(MLA_Q_RANK, -1),
                          uq[:, :, MLA_NOPE:].reshape(MLA_Q_RANK, -1)], axis=1).astype(BF16)
    ukv = w_ukv.reshape(MLA_KV_RANK, H, MLA_NOPE + MLA_V)
    ukv = jnp.concatenate([ukv[:, :, :MLA_NOPE].reshape(MLA_KV_RANK, -1),
                           ukv[:, :, MLA_NOPE:].reshape(MLA_KV_RANK, -1)], axis=1).astype(BF16)
    qn = jnp.concatenate([jnp.tile(q_norm[:MLA_NOPE], H), jnp.tile(q_norm[MLA_NOPE:], H)]).reshape(1, -1)
    kn = jnp.concatenate([jnp.tile(k_norm[:MLA_NOPE], H), jnp.tile(k_norm[MLA_NOPE:], 2)]).reshape(1, -1)
    pc = _P_START['m_q'] // P_MLA_BLK
    const = lambda shape: pl.BlockSpec(shape, lambda b, i: (0, 0))
    rows = lambda w: pl.BlockSpec((MP_TM, w), lambda b, i: (i, 0))
    outs = lambda w: pl.BlockSpec((1, MP_TM, w), lambda b, i: (b, i, 0))
    return pl.pallas_call(
        _mla_prep_kernel,
        out_shape=(jax.ShapeDtypeStruct((B, S_ALL, H * MLA_HW), BF16),
                   jax.ShapeDtypeStruct((B, S_ALL, H * MLA_HW), BF16),
                   jax.ShapeDtypeStruct((B, S_ALL, MLA_W), BF16)),
        grid=(B, S_ALL // MP_TM),
        in_specs=[pl.BlockSpec((1, MP_TM, P_MLA_BLK), lambda b, i: (b, i, pc)),
                  rows(H * MLA_ROPE), rows(H * MLA_ROPE),
                  const((1, MLA_Q_RANK)), const(uq.shape), const((1, MLA_KV_RANK)), const(ukv.shape),
                  const(qn.shape), const(kn.shape)],
        out_specs=(outs(H * MLA_HW), outs(H * MLA_HW), outs(MLA_W)),
        compiler_params=_cparams(2),
        name="mla_prep",
    )(p, cos, sin, qa_norm.reshape(1, -1), uq, kva_norm.reshape(1, -1), ukv, qn, kn)


def _mla_attn_kernel(q_ref, k_ref, v_ref, z_ref, o_ref, *, ctx_start, n_lat_chunks):
    tq = q_ref.shape[1]
    sub = min(tq, MLA_SUBQ)
    bounds = [(ctx_start, ctx_start + CTX_LEN)] + [(i * MLA_TK, (i + 1) * MLA_TK) for i in range(n_lat_chunks)]

    def rows_chain(r0):
        q = q_ref[0, r0:r0 + sub, :]
        m = jnp.full((sub, 1), NEG_INF, F32)
        l = jnp.zeros((sub, 1), F32)
        acc = jnp.zeros((sub, MLA_V), F32)
        for lo, hi in bounds:
            s = _dot_nt(q, k_ref[0, lo:hi, :])
            yield
            m_new = jnp.maximum(m, jnp.max(s, axis=-1, keepdims=True))
            a = jnp.exp2(m - m_new)
            p = jnp.exp2(s - m_new)
            l = a * l + jnp.sum(p, axis=-1, keepdims=True)
            acc = a * acc + jnp.dot(p.astype(BF16), v_ref[0, lo:hi, :], preferred_element_type=F32)
            m = m_new
            yield
        z = z_ref[0, r0:r0 + sub, :].astype(F32)
        o_ref[0, r0:r0 + sub, :] = (acc / l * _silu(z)).astype(BF16)

    _round_robin(rows_chain(r0) for r0 in range(0, tq, sub))


def _mla_attn_ctx_kernel(q_ref, k_ref, v_ref, z_ref, lat_ref, o_ref, **kw):
    del lat_ref
    _mla_attn_kernel(q_ref, k_ref, v_ref, z_ref, o_ref, **kw)


def _mla_attend(q, k, v, p):
    B = p.shape[0]
    H = MLA_HEADS
    zc = _P_START['m_z'] // MLA_V
    ctx_blk = SEQ // CTX_LEN

    def call(tq, q_blk0, n_q, key_rows, key_blk, kern, name, into=None):
        specs = [pl.BlockSpec((1, tq, MLA_HW), lambda b, h, i: (b, q_blk0 + i, h)),
                 pl.BlockSpec((1, key_rows, MLA_HW), lambda b, h, i: (b, key_blk, h)),
                 pl.BlockSpec((1, key_rows, MLA_V), lambda b, h, i: (b, key_blk, h)),
                 pl.BlockSpec((1, tq, MLA_V), lambda b, h, i: (b, q_blk0 + i, zc + h))]
        args = (q, k, v, p)
        if into is not None:
            specs.append(pl.BlockSpec(memory_space=pl.ANY))
            args += (into,)
        return pl.pallas_call(
            kern,
            out_shape=jax.ShapeDtypeStruct((B, S_ALL, MLA_W), BF16),
            grid=(B, H, n_q),
            in_specs=specs,
            out_specs=pl.BlockSpec((1, tq, MLA_V), lambda b, h, i: (b, q_blk0 + i, h)),
            input_output_aliases={} if into is None else {len(args) - 1: 0},
            compiler_params=_cparams(3),
            name=name,
        )(*args)

    lat = call(MLA_TQ, 0, SEQ // MLA_TQ, S_ALL, 0,
               functools.partial(_mla_attn_kernel, ctx_start=SEQ, n_lat_chunks=SEQ // MLA_TK), "mla_attend")
    return call(CTX_LEN, ctx_blk, 1, CTX_LEN, ctx_blk,
                functools.partial(_mla_attn_ctx_kernel, ctx_start=0, n_lat_chunks=0), "mla_attend_ctx", into=lat)


SCAN_TILE = 256
SCAN_NT = S_ALL // SCAN_TILE
CTX_TILE = SEQ // SCAN_TILE
HALO = 16


def _split_dot(m_bf16, x):
    hi = x.astype(BF16)
    lo = (x - hi.astype(F32)).astype(BF16)
    return (jnp.dot(m_bf16, hi, preferred_element_type=F32)
            + jnp.dot(m_bf16, lo, preferred_element_type=F32))


def _softplus(t):
    return jnp.maximum(t, 0.0) + jnp.log1p(jnp.exp(-jnp.abs(t)))


def _conv3_silu(x, prev_row, next_row, w, bias=None):
    n = x.shape[0]
    row = lax.broadcasted_iota(jnp.int32, (n, 1), 0)
    xp = jnp.where(row == 0, prev_row, pltpu.roll(x, 1, axis=0))
    xn = jnp.where(row == n - 1, next_row, pltpu.roll(x, n - 1, axis=0))
    y = xp * w[0:1] + x * w[1:2] + xn * w[2:3]
    if bias is not None:
        y = y + bias
    return _silu(y)


def _halo_rows(i, prev_ref, next_ref):
    pv = jnp.where((i == 0) | (i == CTX_TILE), 0.0, 1.0)
    nv = jnp.where((i == CTX_TILE - 1) | (i == SCAN_NT - 1), 0.0, 1.0)
    return prev_ref[0, HALO - 1:HALO, :].astype(F32) * pv, next_ref[0, 0:1, :].astype(F32) * nv


def _halo_specs(width, col_blk):
    rb = SCAN_TILE // HALO
    nblk = S_ALL // HALO
    return [pl.BlockSpec((1, SCAN_TILE, width), lambda b, i: (b, i, col_blk)),
            pl.BlockSpec((1, HALO, width), lambda b, i: (b, jnp.maximum(i * rb - 1, 0), col_blk)),
            pl.BlockSpec((1, HALO, width), lambda b, i: (b, jnp.minimum((i + 1) * rb, nblk - 1), col_blk))]


def _fwd_tile(t):
    return jnp.where(t == 0, CTX_TILE, t - 1)


def _bwd_tile(t):
    return jnp.where(t == 0, CTX_TILE, CTX_TILE - t)


def _chunk_masks(n):
    i = lax.broadcasted_iota(jnp.int32, (n, 1), 0)
    j = lax.broadcasted_iota(jnp.int32, (1, n), 1)
    same = (i // GDN_CHUNK) == (j // GDN_CHUNK)
    return i, j, same


def _gdn_prep_kernel(x_ref, prev_ref, next_ref, s_ref, cw_ref, rate_ref, dtb_ref,
                     q_out, k_out, v_out, g_out):
    i = pl.program_id(1)
    prev_row, next_row = _halo_rows(i, prev_ref, next_ref)
    y = _conv3_silu(x_ref[0].astype(F32), prev_row, next_row, cw_ref[...])
    for h in range(GDN_HEADS):
        sl = slice(h * GDN_DK, (h + 1) * GDN_DK)
        qh = y[:, sl]
        q_out[0, :, sl] = (qh * lax.rsqrt(jnp.sum(qh * qh, axis=-1, keepdims=True) + EPS)
                           * (GDN_DK ** -0.5)).astype(BF16)
        kh = y[:, GDN_W + h * GDN_DK:GDN_W + (h + 1) * GDN_DK]
        k_out[0, :, sl] = (kh * lax.rsqrt(jnp.sum(kh * kh, axis=-1, keepdims=True) + EPS)).astype(BF16)
    v_out[0] = y[:, 2 * GDN_W:].astype(BF16)

    s = s_ref[0]
    lane = lax.broadcasted_iota(jnp.int32, (1, LANES), 1)
    nh2 = 2 * GDN_HEADS
    beta = jax.nn.sigmoid(s)
    g = -rate_ref[...] * _softplus(s + dtb_ref[...])
    g = jnp.where((lane >= nh2) & (lane < 2 * nh2), g, 0.0)
    ii, jj, same = _chunk_masks(SCAN_TILE)
    one = lambda m: jnp.where(m, 1.0, 0.0).astype(BF16)
    fwd_lane = lane < nh2 + GDN_HEADS
    gam = jnp.where(fwd_lane, _split_dot(one(same & (jj <= ii)), g), _split_dot(one(same & (jj >= ii)), g))
    rem = jnp.where(fwd_lane, _split_dot(one(same & (jj > ii)), g), _split_dot(one(same & (jj < ii)), g))
    cf = jnp.where(lane < nh2, beta, jnp.where(lane < 2 * nh2, gam, pltpu.roll(rem, nh2, axis=1)))
    tr = cf.T
    for h in range(GDN_HEADS):
        for r, src in enumerate((h, GDN_HEADS + h, nh2 + h, nh2 + GDN_HEADS + h,
                                 2 * nh2 + h, 2 * nh2 + GDN_HEADS + h)):
            g_out[0, h, r:r + 1, :] = tr[src:src + 1, :]
        g_out[0, h, 6:8, :] = jnp.zeros((2, SCAN_TILE), F32)


def _gdn_prep(p, ps, conv_w, A_log, dt_bias):
    B = p.shape[0]
    W3 = 3 * GDN_W
    nh2 = 2 * GDN_HEADS
    rate = jnp.zeros((1, LANES), F32).at[0, nh2:2 * nh2].set(jnp.exp(A_log).reshape(-1))
    dtb = jnp.zeros((1, LANES), F32).at[0, nh2:2 * nh2].set(dt_bias.reshape(-1))
    shp = jax.ShapeDtypeStruct((B, S_ALL, GDN_W), BF16)
    ospec = pl.BlockSpec((1, SCAN_TILE, GDN_W), lambda b, i: (b, i, 0))
    const = lambda shape: pl.BlockSpec(shape, lambda b, i: (0, 0))
    return pl.pallas_call(
        _gdn_prep_kernel,
        out_shape=(shp, shp, shp, jax.ShapeDtypeStruct((B, GDN_HEADS, 8, S_ALL), F32)),
        grid=(B, SCAN_NT),
        in_specs=_halo_specs(W3, _P_START['g_qkv'] // W3)
        + [pl.BlockSpec((1, SCAN_TILE, LANES), lambda b, i: (b, i, 0)),
           const((SHORT_CONV, W3)), const((1, LANES)), const((1, LANES))],
        out_specs=(ospec, ospec, ospec,
                   pl.BlockSpec((1, GDN_HEADS, 8, SCAN_TILE), lambda b, i: (b, 0, 0, i))),
        compiler_params=_cparams(2),
        name="gdn_prep",
    )(p, p, p, ps, conv_w, rate, dtb)


def _gdn_dir(q, k, v, gr, s_ref, o_ref, d):
    n = SCAN_TILE
    cf = jnp.concatenate([gr, jnp.zeros((LANES - 8, n), F32)], axis=0).T
    beta, gam_c, rem_c = cf[:, d:d + 1], cf[:, 2 + d:3 + d], cf[:, 4 + d:5 + d]
    gam_r = gr[2 + d:3 + d, :]
    ii, jj, same = _chunk_masks(n)
    incl = same & ((jj <= ii) if d == 0 else (jj >= ii))
    strict = same & ((jj < ii) if d == 0 else (jj > ii))
    kk = _dot_nt(k, k)
    yield
    qk = _dot_nt(q, k)
    yield
    dec = jnp.exp(jnp.where(incl, gam_c - gam_r, NEG_INF))
    a = jnp.where(strict, beta * kk * dec, 0.0)
    qkd = (qk * dec).astype(BF16)
    kf = k.astype(F32)
    x = jnp.concatenate([v.astype(F32) * beta, kf * (beta * jnp.exp(gam_c))], axis=1)
    pb = a.astype(BF16)
    x = x - jnp.dot(pb, x.astype(BF16), preferred_element_type=F32)
    yield
    for _ in range(5):
        pb = jnp.dot(pb, pb, preferred_element_type=F32).astype(BF16)
        yield
        x = x + jnp.dot(pb, x.astype(BF16), preferred_element_type=F32)
        yield
    u, w =x[:, :GDN_DV], x[:, GDN_DV:].astype(BF16)
    qd = (q.astype(F32) * jnp.exp(gam_c)).astype(BF16)
    kd = (kf * jnp.exp(rem_c)).astype(BF16)
    s = s_ref[...]
    nchunk = n // GDN_CHUNK
    v_new = [None] * nchunk
    qs = [None] * nchunk
    for c in (range(nchunk) if d == 0 else reversed(range(nchunk))):
        rows = slice(c * GDN_CHUNK, (c + 1) * GDN_CHUNK)
        r1 = jnp.dot(jnp.concatenate([w[rows], qd[rows]], axis=0), s.astype(BF16), preferred_element_type=F32)
        yield
        vn = u[rows] - r1[:GDN_CHUNK]
        qs[c] = r1[GDN_CHUNK:]
        v_new[c] = vn
        last = (c + 1) * GDN_CHUNK - 1 if d == 0 else c * GDN_CHUNK
        s = s * jnp.exp(cf[last:last + 1, 2 + d:3 + d]) + _dot_tn(kd[rows], vn.astype(BF16))
        yield
    s_ref[...] = s
    vn_all = jnp.concatenate(v_new, axis=0).astype(BF16)
    o_ref[0] = (jnp.concatenate(qs, axis=0) + jnp.dot(qkd, vn_all, preferred_element_type=F32)).astype(BF16)


def _round_robin(gens):
    gens = list(gens)
    while gens:
        alive = []
        for g in gens:
            try:
                next(g)
                alive.append(g)
            except StopIteration:
                pass
        gens = alive


def _gdn_scan_kernel(qf, kf, vf, gf, qb, kb, vb, gb, of_ref, ob_ref, sf_ref, sb_ref):
    @pl.when(pl.program_id(1) == 0)
    def _():
        sf_ref[...] = jnp.zeros_like(sf_ref)
        sb_ref[...] = jnp.zeros_like(sb_ref)

    chains = []
    for h in range(GDN_HEADS):
        sl = slice(h * GDN_DK, (h + 1) * GDN_DK)
        chains.append(_gdn_dir(qf[0, :, sl], kf[0, :, sl], vf[0, :, sl], gf[0, h],
                               sf_ref.at[h], of_ref.at[:, :, sl], 0))
        chains.append(_gdn_dir(qb[0, :, sl], kb[0, :, sl], vb[0, :, sl], gb[0, h],
                               sb_ref.at[h], ob_ref.at[:, :, sl], 1))
    _round_robin(chains)


def _gdn_scan(q, k, v, g):
    B = q.shape[0]
    tok = lambda order: pl.BlockSpec((1, SCAN_TILE, GDN_W), lambda b, t: (b, order(t), 0))
    gsp = lambda order: pl.BlockSpec((1, GDN_HEADS, 8, SCAN_TILE), lambda b, t: (b, 0, 0, order(t)))
    shp = jax.ShapeDtypeStruct((B, S_ALL, GDN_W), BF16)
    f, r = _fwd_tile, _bwd_tile
    state = pltpu.VMEM((GDN_HEADS, GDN_DK, GDN_DV), F32)
    return pl.pallas_call(
        _gdn_scan_kernel,
        out_shape=(shp, shp),
        grid=(B, SCAN_NT),
        in_specs=[tok(f), tok(f), tok(f), gsp(f), tok(r), tok(r), tok(r), gsp(r)],
        out_specs=(tok(f), tok(r)),
        scratch_shapes=[state, state],
        compiler_params=_cparams(2),
        name="gdn_scan",
    )(q, k, v, g, q, k, v, g)


SSD_BC = SSM_GROUPS * SSM_STATE
SSD_HD = 2 * SSM_HEADS
SSD_GW = (SSM_HEADS // SSM_GROUPS) * SSM_HEADDIM


def _ssd_prep_kernel(x_ref, prev_ref, next_ref, s_ref, cw_ref, cb_ref, a_ref, dtb_ref,
                     xs_out, b_out, c_out, bt_out, cf_out, cr_out):
    i = pl.program_id(1)
    prev_row, next_row = _halo_rows(i, prev_ref, next_ref)
    y = _conv3_silu(x_ref[0].astype(F32), prev_row, next_row, cw_ref[...], cb_ref[...])
    xs_out[0] = y[:, :SSM_W]
    b_out[0] = y[:, SSM_W:SSM_W + SSD_BC].astype(BF16)
    c_out[0] = y[:, SSM_W + SSD_BC:].astype(BF16)
    bt_out[0] = y[:, SSM_W:SSM_W + SSD_BC].T.astype(BF16)

    s = s_ref[0]
    lane = lax.broadcasted_iota(jnp.int32, (1, LANES), 1)
    dt = _softplus(s + dtb_ref[...])
    on = (lane >= SSD_HD) & (lane < 2 * SSD_HD)
    a = jnp.where(on, dt * a_ref[...], 0.0)
    n = SCAN_TILE
    ii = lax.broadcasted_iota(jnp.int32, (n, 1), 0)
    jj = lax.broadcasted_iota(jnp.int32, (1, n), 1)
    one = lambda m: jnp.where(m, 1.0, 0.0).astype(BF16)
    fwd_lane = lane < SSD_HD + SSM_HEADS
    cum = jnp.where(fwd_lane, _split_dot(one(jj <= ii), a), _split_dot(one(jj >= ii), a))
    rem = jnp.where(fwd_lane, _split_dot(one(jj > ii), a), _split_dot(one(jj < ii), a))
    cf = jnp.where(lane < SSD_HD, pltpu.roll(dt, LANES - SSD_HD, axis=1),
                   jnp.where(lane < 2 * SSD_HD, cum, pltpu.roll(rem, SSD_HD, axis=1)))
    cf_out[0] = cf
    cr_out[0] = cf.T[SSD_HD:2 * SSD_HD, :]


def _ssd_prep(p, ps, conv_w, conv_b, A_log, dt_bias):
    B = p.shape[0]
    W = SSM_CONV_DIM
    a_vec = jnp.zeros((1, LANES), F32).at[0, SSD_HD:2 * SSD_HD].set(-jnp.exp(A_log).reshape(-1))
    dtb = jnp.zeros((1, LANES), F32).at[0, SSD_HD:2 * SSD_HD].set(dt_bias.reshape(-1))
    const = lambda shape: pl.BlockSpec(shape, lambda b, i: (0, 0))
    tok = lambda w: pl.BlockSpec((1, SCAN_TILE, w), lambda b, i: (b, i, 0))
    return pl.pallas_call(
        _ssd_prep_kernel,
        out_shape=(jax.ShapeDtypeStruct((B, S_ALL, SSM_W), F32),
                   jax.ShapeDtypeStruct((B, S_ALL, SSD_BC), BF16),
                   jax.ShapeDtypeStruct((B, S_ALL, SSD_BC), BF16),
                   jax.ShapeDtypeStruct((B, SSD_BC, S_ALL), BF16),
                   jax.ShapeDtypeStruct((B, S_ALL, LANES), F32),
                   jax.ShapeDtypeStruct((B, SSD_HD, S_ALL), F32)),
        grid=(B, SCAN_NT),
        in_specs=_halo_specs(W, _P_START['s_xbc'] // W)
        + [pl.BlockSpec((1, SCAN_TILE, LANES), lambda b, i: (b, i, 0)),
           const((SHORT_CONV, W)), const((1, W)), const((1, LANES)), const((1, LANES))],
        out_specs=(tok(SSM_W), tok(SSD_BC), tok(SSD_BC),
                   pl.BlockSpec((1, SSD_BC, SCAN_TILE), lambda b, i: (b, 0, i)),
                   tok(LANES),
                   pl.BlockSpec((1, SSD_HD, SCAN_TILE), lambda b, i: (b, 0, i))),
        compiler_params=_cparams(2),
        name="ssd_prep",
    )(p, p, p, ps, conv_w, conv_b.reshape(1, W), a_vec, dtb)


def _per_head_lanes(cols):
    lane = lax.broadcasted_iota(jnp.int32, (1, LANES), 1)
    lo = lane < SSM_HEADDIM
    return jnp.concatenate([jnp.where(lo, cols[2 * j], cols[2 * j + 1]) for j in range(len(cols) // 2)], axis=1)


def _ssd_dir(x_ref, b_ref, c_ref, bt_ref, cf_ref, cr_ref, h_ref, y_ref, d, g):
    n = SCAN_TILE
    hpg = SSM_HEADS // SSM_GROUPS
    hd0 = d * SSM_HEADS + g * hpg
    gs = slice(g * SSM_STATE, (g + 1) * SSM_STATE)
    xl = slice(g * SSD_GW, (g + 1) * SSD_GW)
    cf = cf_ref[0]
    cr = cr_ref[0]
    cm = c_ref[0, :, gs]
    col = lambda base, h: cf[:, base + hd0 + h:base + hd0 + h + 1]
    heads = range(hpg)
    last = n - 1 if d == 0 else 0
    xdt = x_ref[0, :, xl] * _per_head_lanes([col(0, h) for h in heads])
    xdt_b = xdt.astype(BF16)
    xdec = (xdt * _per_head_lanes([jnp.exp(col(2 * SSD_HD, h)) for h in heads])).astype(BF16)
    cb = _dot_nt(cm, b_ref[0, :, gs])
    yield
    h_prev = h_ref[g]
    y_off = jnp.dot(cm, h_prev.astype(BF16), preferred_element_type=F32)
    yield
    y_off = y_off * _per_head_lanes([jnp.exp(col(SSD_HD, h)) for h in heads])
    etot = _per_head_lanes([jnp.exp(cf[last:last + 1, SSD_HD + hd0 + h:SSD_HD + hd0 + h + 1]) for h in heads])
    h_ref[g] = h_prev * etot + jnp.dot(bt_ref[0, gs, :], xdec, preferred_element_type=F32)
    yield
    ii = lax.broadcasted_iota(jnp.int32, (n, 1), 0)
    jj = lax.broadcasted_iota(jnp.int32, (1, n), 1)
    causal = (jj <= ii) if d == 0 else (jj >= ii)
    lo = lax.broadcasted_iota(jnp.int32, (1, LANES), 1) < SSM_HEADDIM
    pair_out = []
    for j in range(hpg // 2):
        ys = []
        for e in range(2):
            h = 2 * j + e
            seg = col(SSD_HD, h) - cr[hd0 + h:hd0 + h + 1, :]
            sc = (cb * jnp.exp(jnp.where(causal, seg, NEG_INF))).astype(BF16)
            ys.append(jnp.dot(sc, xdt_b[:, j * LANES:(j + 1) * LANES], preferred_element_type=F32))
            yield
        pair_out.append(jnp.where(lo, ys[0], ys[1]))
    y_ref[0, :, xl] = (jnp.concatenate(pair_out, axis=1) + y_off).astype(BF16)


def _ssd_scan_kernel(xf, bf, cf_, btf, colf, rowf, xb, bb, cb_, btb, colb, rowb, yf_ref, yb_ref, hf_ref, hb_ref):
    @pl.when(pl.program_id(1) == 0)
    def _():
        hf_ref[...] = jnp.zeros_like(hf_ref)
        hb_ref[...] = jnp.zeros_like(hb_ref)

    chains = []
    for g in range(SSM_GROUPS):
        chains.append(_ssd_dir(xf, bf, cf_, btf, colf, rowf, hf_ref, yf_ref, 0, g))
        chains.append(_ssd_dir(xb, bb, cb_, btb, colb, rowb, hb_ref, yb_ref, 1, g))
    _round_robin(chains)


def _ssd_scan(xs, bm, cm, bt, cf, cr):
    B = xs.shape[0]

    def specs(order):
        tok = lambda w: pl.BlockSpec((1, SCAN_TILE, w), lambda b, t: (b, order(t), 0))
        return [tok(SSM_W), tok(SSD_BC), tok(SSD_BC),
                pl.BlockSpec((1, SSD_BC, SCAN_TILE), lambda b, t: (b, 0, order(t))),
                tok(LANES),
                pl.BlockSpec((1, SSD_HD, SCAN_TILE), lambda b, t: (b, 0, order(t)))]

    shp = jax.ShapeDtypeStruct((B, S_ALL, SSM_W), BF16)
    out = lambda order: pl.BlockSpec((1, SCAN_TILE, SSM_W), lambda b, t: (b, order(t), 0))
    hshape = pltpu.VMEM((SSM_GROUPS, SSM_STATE, SSD_GW), F32)
    args = (xs, bm, cm, bt, cf, cr)
    return pl.pallas_call(
        _ssd_scan_kernel,
        out_shape=(shp, shp),
        grid=(B, SCAN_NT),
        in_specs=specs(_fwd_tile) + specs(_bwd_tile),
        out_specs=(out(_fwd_tile), out(_bwd_tile)),
        scratch_shapes=[hshape, hshape],
        compiler_params=_cparams(2),
        name="ssd_scan",
    )(*args, *args)


def _repack_w_in(w):
    cut = lambda names: [w[:, _IN_START[n]:_IN_START[n] + _IN_SIZE[n]] for n in names]
    zeros = lambda n: jnp.zeros((w.shape[0], n), w.dtype)
    n_small = sum(_IN_SIZE[n] for n in _S_ORDER)
    cols = cut(_P_ORDER) + [zeros(D_INP - LANES - _off)] + cut(_S_ORDER) + [zeros(LANES - n_small)]
    return jnp.concatenate(cols, axis=1).astype(BF16)


def kernel(x, c, ctx, c_ctx, norm_w, ada_w, ada_b, w_in, gdn_conv_w, gdn_A_log, gdn_dt_bias, gdn_norm_w, na_q_norm, na_k_norm, na_rpb, mla_qa_norm, mla_w_uq, mla_kva_norm, mla_w_ukv, mla_q_norm, mla_k_norm, ssm_conv_w, ssm_conv_b, ssm_A_log, ssm_dt_bias, ssm_D, ssm_norm_w, w_out):
    B = x.shape[0]
    xs = jnp.concatenate([x, ctx], axis=1)
    c8 = jnp.zeros((8, D_MODEL), F32).at[:B].set(c).at[B].set(c_ctx)
    mods = _ada_all(c8, ada_w, ada_b)
    cos_np, sin_np = _rope_tables()
    cos, sin = jnp.asarray(cos_np), jnp.asarray(sin_np)
    na_bias = _na_bias(na_rpb)
    for l in range(DEPTH):
        shift, scale, gate = jnp.split(mods[l, :B], 3, axis=-1)
        shift_c, scale_c, gate_c = jnp.split(mods[l, B], 3, axis=-1)
        bc = lambda v: jnp.broadcast_to(v[None], (B, D_MODEL))
        mod4 = jnp.stack([shift, scale, bc(shift_c), bc(scale_c)], axis=1)
        gate2 = jnp.stack([gate, bc(gate_c)], axis=1)
        p, ps = _inproj(xs, norm_w[l], mod4, _repack_w_in(w_in[l]))

        gq, gk, gv, gg = _gdn_prep(p, ps, gdn_conv_w[l], gdn_A_log[l], gdn_dt_bias[l])
        o_f, o_b = _gdn_scan(gq, gk, gv, gg)

        qn, kn, vn = _na_prep(p, na_q_norm[l], na_k_norm[l])
        ob = _na_attend(qn, kn, vn, p, na_bias[l])

        mq, mk, mv = _mla_prep(p, cos, sin, mla_qa_norm[l], mla_w_uq[l], mla_kva_norm[l], mla_w_ukv[l],
                               mla_q_norm[l], mla_k_norm[l])
        oc = _mla_attend(mq, mk, mv, p)

        sx, sb, sc, sbt, scf, scr = _ssd_prep(p, ps, ssm_conv_w[l], ssm_conv_b[l], ssm_A_log[l], ssm_dt_bias[l])
        y_f, y_b = _ssd_scan(sx, sb, sc, sbt, scf, scr)
        xs = _outproj((o_f, o_b, gdn_norm_w[l]), ob, oc, (y_f, y_b, sx, ssm_D[l], ssm_norm_w[l]), p,
                      w_out[l].astype(BF16), xs, gate2, last=(l == DEPTH - 1))
    return xs
```

```python
import functools

import jax
import jax.numpy as jnp
import numpy as np
from jax import lax
from jax.experimental import pallas as pl
from jax.experimental.pallas import tpu as pltpu

F32 = jnp.float32
BF16 = jnp.bfloat16

D_MODEL = 2048
BATCH = 4
SEQ = 4096
DEPTH = 4
GRID_W = 64
GRID_H = SEQ // GRID_W
CTX_LEN = 256
S_ALL = SEQ + CTX_LEN
EPS = 1e-6
NEG_INF = -1e30
LOG2E = 1.4426950408889634

D_BRANCH = 512
D_MIX = 4 * D_BRANCH
SHORT_CONV = 3

GDN_HEADS = 4
GDN_DK = 128
GDN_DV = 128
GDN_W = GDN_HEADS * GDN_DV
GDN_CHUNK = 64

NA_HEADS = 4
NA_DH = 128
NA_W = NA_HEADS * NA_DH
NA_WIN_R = 8
NA_WIN_C = 16

MLA_HEADS = 4
MLA_Q_RANK = 384
MLA_KV_RANK = 256
MLA_NOPE = 128
MLA_ROPE = 64
MLA_QK = MLA_NOPE + MLA_ROPE
MLA_V = 128
MLA_W = MLA_HEADS * MLA_V
ROPE_THETA = 10000.0

SSM_HEADDIM = 64
SSM_HEADS = D_BRANCH // SSM_HEADDIM
SSM_W = SSM_HEADS * SSM_HEADDIM
SSM_GROUPS = 2
SSM_STATE = 128
SSM_CONV_DIM = SSM_W + 2 * SSM_GROUPS * SSM_STATE

IN_SIZES = (3 * GDN_W, GDN_W, 2 * GDN_HEADS, 2 * GDN_HEADS,
            3 * NA_W, NA_W,
            MLA_Q_RANK, MLA_KV_RANK, MLA_ROPE, MLA_W,
            SSM_W, SSM_CONV_DIM, 2 * SSM_HEADS)
D_IN = sum(IN_SIZES)
_IN_NAMES = ('g_qkv', 'g_z', 'g_beta', 'g_alpha', 'n_qkv', 'n_z',
             'm_q', 'm_kv', 'm_kr', 'm_z', 's_z', 's_xbc', 's_dt')
_IN_START = dict(zip(_IN_NAMES, np.cumsum((0,) + IN_SIZES[:-1]).tolist()))
_IN_SIZE = dict(zip(_IN_NAMES, IN_SIZES))

LANES = 128
_P_ORDER = ('g_qkv', 'g_z', 'n_qkv', 'n_z', 'm_z', 's_z', 's_xbc',
            'm_q', 'm_kv', 'm_kr', 'm_kr')
_S_ORDER = ('g_beta', 'g_alpha', 's_dt')
_P_START = {}
_off = 0
for _n in _P_ORDER:
    _P_START.setdefault(_n, _off)
    _off += _IN_SIZE[_n]
MXU_N = 256
D_INP = -(-_off // (2 * MXU_N)) * (2 * MXU_N)
P_MLA_BLK = MLA_Q_RANK + MLA_KV_RANK + 2 * MLA_ROPE
assert _P_START['m_q'] % P_MLA_BLK == 0 and D_INP % LANES == 0

VMEM_LIMIT = 52 * 1024 * 1024


def _silu(x):
    return x * jax.nn.sigmoid(x)


def _dot_nt(a, b):
    return lax.dot_general(a, b, (((1,), (1,)), ((), ())), preferred_element_type=F32)


def _dot_tn(a, b):
    return lax.dot_general(a, b, (((0,), (0,)), ((), ())), preferred_element_type=F32)


def _cparams(n_axes):
    return pltpu.CompilerParams(dimension_semantics=("arbitrary",) * n_axes,
                                vmem_limit_bytes=VMEM_LIMIT)


def _ada_kernel(c_ref, w_ref, b_ref, o_ref):
    a = _silu(c_ref[...]).astype(BF16)
    o_ref[0] = jnp.dot(a, w_ref[0].astype(BF16), preferred_element_type=F32) + b_ref[0]


def _ada_all(c8, ada_w, ada_b):
    tn = 1536
    L = ada_w.shape[0]
    n3 = ada_w.shape[2]
    return pl.pallas_call(
        _ada_kernel,
        out_shape=jax.ShapeDtypeStruct((L, 8, n3), F32),
        grid=(L, n3 // tn),
        in_specs=[pl.BlockSpec((8, D_MODEL), lambda l, j: (0, 0)),
                  pl.BlockSpec((1, D_MODEL, tn), lambda l, j: (l, 0, j)),
                  pl.BlockSpec((1, 1, tn), lambda l, j: (l, 0, j))],
        out_specs=pl.BlockSpec((1, 8, tn), lambda l, j: (l, 0, j)),
        compiler_params=_cparams(2),
        name="ada_mod",
    )(c8, ada_w, ada_b.reshape(L, 1, n3))


IN_TM = 1088
IN_TN = 1024
IN_RC = 16
IN_SECTIONS = 4


def _inproj_kernel(x_ref, nw_ref, mod_ref, w_ref, o_ref, os_ref, h_scr):
    i = pl.program_id(1)
    j = pl.program_id(2)

    @pl.when(j == 0)
    def _():
        m = mod_ref[0]
        nw = nw_ref[...]
        gain_l = nw * (1.0 + m[1:2])
        gain_c = nw * (1.0 + m[3:4])
        sec = IN_TM // IN_SECTIONS
        for c in range(IN_SECTIONS):
            for r0 in range(c * sec, (c + 1) * sec, IN_RC):
                x = x_ref[0, r0:r0 + IN_RC, :]
                ms = jnp.mean(x * x, axis=-1, keepdims=True)
                is_ctx = i * IN_TM + r0 >= SEQ
                gain = jnp.where(is_ctx, gain_c, gain_l)
                shift = jnp.where(is_ctx, m[2:3], m[0:1])
                h_scr[r0:r0 + IN_RC, :] = (x * lax.rsqrt(ms + EPS) * gain + shift).astype(BF16)
            rows = slice(c * sec, (c + 1) * sec)
            o_ref[0, rows, :] = jnp.dot(h_scr[rows, :], w_ref[...], preferred_element_type=F32).astype(BF16)

    @pl.when(j > 0)
    def _():
        y = jnp.dot(h_scr[...], w_ref[...], preferred_element_type=F32)
        o_ref[0] = y.astype(BF16)

        @pl.when(j == pl.num_programs(2) - 1)
        def _():
            os_ref[0] = y[:, IN_TN - LANES:]


def _inproj(xs, norm_w, mod4, w_main):
    B = xs.shape[0]
    return pl.pallas_call(
        _inproj_kernel,
        out_shape=(jax.ShapeDtypeStruct((B, S_ALL, D_INP), BF16),
                   jax.ShapeDtypeStruct((B, S_ALL, LANES), F32)),
        grid=(B, S_ALL // IN_TM, D_INP // IN_TN),
        in_specs=[pl.BlockSpec((1, IN_TM, D_MODEL), lambda b, i, j: (b, i, 0)),
                  pl.BlockSpec((1, D_MODEL), lambda b, i, j: (0, 0)),
                  pl.BlockSpec((1, 4, D_MODEL), lambda b, i, j: (b, 0, 0)),
                  pl.BlockSpec((D_MODEL, IN_TN), lambda b, i, j: (0, j))],
        out_specs=(pl.BlockSpec((1, IN_TM, IN_TN), lambda b, i, j: (b, i, j)),
                   pl.BlockSpec((1, IN_TM, LANES), lambda b, i, j: (b, i, 0))),
        scratch_shapes=[pltpu.VMEM((IN_TM, D_MODEL), BF16)],
        compiler_params=_cparams(3),
        name="inproj",
    )(xs, norm_w.reshape(1, D_MODEL), mod4, w_main)


OUT_TM = 544
OUT_TM_LAST = 512
OUT_SECTIONS = 2


def _gdn_gate(o_f, o_b, z, nw):
    o = o_f.astype(F32) + o_b.astype(F32)
    z = z.astype(F32)
    outs = []
    for h in range(GDN_HEADS):
        sl = slice(h * GDN_DV, (h + 1) * GDN_DV)
        oh = o[:, sl]
        y = oh * lax.rsqrt(jnp.mean(oh * oh, axis=-1, keepdims=True) + EPS) * nw
        outs.append((y * _silu(z[:, sl])).astype(BF16))
    return jnp.concatenate(outs, axis=-1)


def _ssd_gate(y_f, y_b, xs, z, d_skip, nw):
    y = y_f.astype(F32) + y_b.astype(F32) + d_skip * xs
    y = y * _silu(z.astype(F32))
    return (y * lax.rsqrt(jnp.mean(y * y, axis=-1, keepdims=True) + EPS) * nw).astype(BF16)


def _outproj_kernel(of_ref, ob_ref, gz_ref, gnw_ref, na_ref, mla_ref, yf_ref, yb_ref, sx_ref, sz_ref,
                    dsk_ref, snw_ref, w_ref, x_ref, g_ref, o_ref, *, tm):
    i = pl.program_id(1)
    g = g_ref[0]
    sec = tm // OUT_SECTIONS
    for c in range(OUT_SECTIONS):
        rows = slice(c * sec, (c + 1) * sec)
        branches = (_gdn_gate(of_ref[0, rows, :], ob_ref[0, rows, :], gz_ref[0, rows, :], gnw_ref[...]),
                    na_ref[0, rows, :], mla_ref[0, rows, :],
                    _ssd_gate(yf_ref[0, rows, :], yb_ref[0, rows, :], sx_ref[0, rows, :], sz_ref[0, rows, :],
                              dsk_ref[...], snw_ref[...]))
        y = None
        for n, a in enumerate(branches):
            t = jnp.dot(a, w_ref[n * D_BRANCH:(n + 1) * D_BRANCH, :], preferred_element_type=F32)
            y = t if y is None else y + t
        row = i * tm + c * sec + lax.broadcasted_iota(jnp.int32, (sec, 1), 0)
        gate = jnp.where(row >= SEQ, g[1:2], g[0:1])
        o_ref[0, rows, :] = x_ref[0, rows, :] + gate * y


def _outproj(gdn, na, mla, ssd, p, w_out_b, xs, gate2, last):
    B = xs.shape[0]
    o_f, o_b, g_nw = gdn
    y_f, y_b, s_x, d_skip, s_nw = ssd
    tm, rows = (OUT_TM_LAST, SEQ) if last else (OUT_TM, S_ALL)
    a_spec = pl.BlockSpec((1, tm, D_BRANCH), lambda b, i: (b, i, 0))
    z_spec = lambda name: pl.BlockSpec((1, tm, D_BRANCH), lambda b, i: (b, i, _P_START[name] // D_BRANCH))
    x_spec = pl.BlockSpec((1, tm, D_MODEL), lambda b, i: (b, i, 0))
    vec = lambda n: pl.BlockSpec((1, n), lambda b, i: (0, 0))
    return pl.pallas_call(
        functools.partial(_outproj_kernel, tm=tm),
        out_shape=jax.ShapeDtypeStruct((B, rows, D_MODEL), F32),
        grid=(B, rows // tm),
        in_specs=[a_spec, a_spec, z_spec('g_z'), vec(GDN_DV), a_spec, a_spec,
                  a_spec, a_spec, a_spec, z_spec('s_z'), vec(SSM_W), vec(SSM_W),
                  pl.BlockSpec((D_MIX, D_MODEL), lambda b, i: (0, 0)),
                  x_spec,
                  pl.BlockSpec((1, 2, D_MODEL), lambda b, i: (b, 0, 0))],
        out_specs=x_spec,
        compiler_params=_cparams(2),
        name="outproj",
    )(o_f, o_b, p, g_nw.reshape(1, GDN_DV), na, mla,
      y_f, y_b, s_x, p, jnp.repeat(d_skip, SSM_HEADDIM).reshape(1, SSM_W), s_nw.reshape(1, SSM_W),
      w_out_b, xs, gate2)


NA_TM = 544
NA_RB = 4
NA_QB = NA_RB * GRID_W
NA_KR = 12
NA_KW = NA_KR * GRID_W
NA_NBLK = GRID_H // NA_RB
assert NA_QB == CTX_LEN


def _na_prep_kernel(q_ref, k_ref, v_ref, qn_ref, kn_ref, qo_ref, ko_ref, vo_ref):
    def headnorm(x, w, extra):
        outs = []
        for h in range(NA_HEADS):
            xh = x[:, h * NA_DH:(h + 1) * NA_DH]
            ms = jnp.mean(xh * xh, axis=-1, keepdims=True)
            outs.append((xh * lax.rsqrt(ms + EPS) * w * extra).astype(BF16))
        return jnp.concatenate(outs, axis=-1)

    qo_ref[0] = headnorm(q_ref[0].astype(F32), qn_ref[...], NA_DH ** -0.5 * LOG2E)
    ko_ref[0] = headnorm(k_ref[0].astype(F32), kn_ref[...], 1.0)
    vo_ref[0] = v_ref[0]


def _na_prep(p, q_norm, k_norm):
    B = p.shape[0]
    c0 = _P_START['n_qkv'] // NA_W
    spec = lambda c: pl.BlockSpec((1, NA_TM, NA_W), lambda b, i, c=c: (b, i, c))
    ospec = pl.BlockSpec((1, NA_TM, NA_W), lambda b, i: (b, i, 0))
    wspec = pl.BlockSpec((1, NA_DH), lambda b, i: (0, 0))
    shp = jax.ShapeDtypeStruct((B, S_ALL, NA_W), BF16)
    return pl.pallas_call(
        _na_prep_kernel,
        out_shape=(shp, shp, shp),
        grid=(B, S_ALL // NA_TM),
        in_specs=[spec(c0), spec(c0 + 1), spec(c0 + 2), wspec, wspec],
        out_specs=(ospec, ospec, ospec),
        compiler_params=_cparams(2),
        name="na_prep",
    )(p, p, p, q_norm.reshape(1, NA_DH), k_norm.reshape(1, NA_DH))


def _na_fill_bias(toe_ref, bias_scr, cls):
    dr, ok, _ = _na_bias_index()
    dead = jnp.full((GRID_W, GRID_W), NEG_INF, F32)
    for h in range(NA_HEADS):
        for a in range(NA_RB):
            for b in range(0, NA_KR, 2):
                pair = [toe_ref[h, int(dr[cls, a, b + e])] if ok[cls, a, b + e] else dead for e in range(2)]
                bias_scr[h, a * GRID_W:(a + 1) * GRID_W, b * GRID_W:(b + 2) * GRID_W] = jnp.concatenate(pair, axis=1)


def _na_kernel(q_ref, k_ref, v_ref, z_ref, toe_ref, o_ref, bias_scr):
    rb = pl.program_id(1)
    q = q_ref[0]
    z = z_ref[0].astype(F32)
    kc = k_ref[0, SEQ:S_ALL, :]
    vc = v_ref[0, SEQ:S_ALL, :]

    for cls, first_rb in enumerate((0, 1, NA_NBLK - 1)):
        @pl.when(rb == first_rb)
        def _(cls=cls):
            _na_fill_bias(toe_ref, bias_scr, cls)

    def finish(h, o, l):
        sl = slice(h * NA_DH, (h + 1) * NA_DH)
        o_ref[0, :, sl] = (o / l * _silu(z[:, sl])).astype(BF16)

    @pl.when(rb < NA_NBLK)
    def _latent():
        base = jnp.clip(rb * NA_RB - NA_RB, 0, GRID_H - NA_KR)
        start = pl.multiple_of(base * GRID_W, GRID_W)
        kw = k_ref[0, pl.ds(start, NA_KW), :]
        vw = v_ref[0, pl.ds(start, NA_KW), :]
        def head(h):
            sl = slice(h * NA_DH, (h + 1) * NA_DH)
            s_w = _dot_nt(q[:, sl], kw[:, sl])
            yield
            s_c = _dot_nt(q[:, sl], kc[:, sl])
            yield
            s_w = s_w + bias_scr[h]
            m = jnp.maximum(jnp.max(s_w, axis=-1, keepdims=True), jnp.max(s_c, axis=-1, keepdims=True))
            p_w = jnp.exp2(s_w - m)
            p_c = jnp.exp2(s_c - m)
            l = jnp.sum(p_w, axis=-1, keepdims=True) + jnp.sum(p_c, axis=-1, keepdims=True)
            o = jnp.dot(p_w.astype(BF16), vw[:, sl], preferred_element_type=F32)
            yield
            o = o + jnp.dot(p_c.astype(BF16), vc[:, sl], preferred_element_type=F32)
            yield
            finish(h, o, l)

        _round_robin(head(h) for h in range(NA_HEADS))

    @pl.when(rb == NA_NBLK)
    def _context():
        for h in range(NA_HEADS):
            sl = slice(h * NA_DH, (h + 1) * NA_DH)
            s_c = _dot_nt(q[:, sl], kc[:, sl])
            m = jnp.max(s_c, axis=-1, keepdims=True)
            p_c = jnp.exp2(s_c - m)
            l = jnp.sum(p_c, axis=-1, keepdims=True)
            o = jnp.dot(p_c.astype(BF16), vc[:, sl], preferred_element_type=F32)
            finish(h, o, l)


def _na_bias_index():
    dr = np.zeros((3, NA_RB, NA_KR), np.int64)
    ok = np.zeros((3, NA_RB, NA_KR), bool)
    for ci, rb in enumerate((0, 1, NA_NBLK - 1)):
        base = int(np.clip(rb * NA_RB - NA_RB, 0, GRID_H - NA_KR))
        qr = rb * NA_RB + np.arange(NA_RB)[:, None]
        kr = base + np.arange(NA_KR)[None, :]
        row0 = np.clip(qr - NA_WIN_R // 2, 0, GRID_H - NA_WIN_R)
        ok[ci] = (kr >= row0) & (kr < row0 + NA_WIN_R)
        dr[ci] = np.clip(kr - qr + NA_WIN_R - 1, 0, 2 * NA_WIN_R - 2)
    qc = np.arange(GRID_W)[:, None]
    kc = np.arange(GRID_W)[None, :]
    win0 = np.clip(qc - NA_WIN_C // 2, 0, GRID_W - NA_WIN_C)
    col_ok = (kc >= win0) & (kc < win0 + NA_WIN_C)
    return dr, ok, col_ok


def _na_bias(rpb):
    L, H = rpb.shape[:2]
    nd = 2 * NA_WIN_R - 1
    _, _, col_ok = _na_bias_index()
    left = GRID_W - NA_WIN_C
    f = jnp.pad(rpb * LOG2E, ((0, 0), (0, 0), (0, 0), (left, 2 * GRID_W - (2 * NA_WIN_C - 1) - left)))
    skew = jnp.broadcast_to(f[:, :, :, None, :], (L, H, nd, GRID_W, 2 * GRID_W))
    skew = skew.reshape(L, H, nd, -1)[..., :GRID_W * (2 * GRID_W - 1)].reshape(L, H, nd, GRID_W, 2 * GRID_W - 1)
    toe = skew[..., GRID_W - 1:]
    return jnp.where(col_ok, toe, NEG_INF)


def _na_attend(qn, kn, vn, p, toe):
    B = p.shape[0]
    zc = _P_START['n_z'] // NA_W
    blk = pl.BlockSpec((1, NA_QB, NA_W), lambda b, r: (b, r, 0))
    full = pl.BlockSpec((1, S_ALL, NA_W), lambda b, r: (b, 0, 0))
    return pl.pallas_call(
        _na_kernel,
        out_shape=jax.ShapeDtypeStruct((B, S_ALL, NA_W), BF16),
        grid=(B, NA_NBLK + 1),
        in_specs=[blk, full, full,
                  pl.BlockSpec((1, NA_QB, NA_W), lambda b, r: (b, r, zc)),
                  pl.BlockSpec(toe.shape, lambda b, r: (0, 0, 0, 0))],
        out_specs=blk,
        scratch_shapes=[pltpu.VMEM((NA_HEADS, NA_QB, NA_KW), F32)],
        compiler_params=_cparams(2),
        name="na_attend",
    )(qn, kn, vn, p, toe)


MP_TM = 544
MLA_HW = 2 * LANES
MLA_TQ = 1024
MLA_SUBQ = 512
MLA_TK = 512


def _rope_tables():
    n_freq = MLA_ROPE // 4
    inv_freq = ROPE_THETA ** (-np.arange(n_freq, dtype=np.float64) / n_freq)
    t = np.arange(SEQ)
    ar = (t // GRID_W)[:, None] * inv_freq
    ac = (t % GRID_W)[:, None] * inv_freq
    cos = np.concatenate([np.cos(ar), np.cos(ar), np.cos(ac), np.cos(ac)], axis=1)
    sin = np.concatenate([-np.sin(ar), np.sin(ar), -np.sin(ac), np.sin(ac)], axis=1)
    cos = np.concatenate([cos, np.ones((CTX_LEN, MLA_ROPE))], axis=0)
    sin = np.concatenate([sin, np.zeros((CTX_LEN, MLA_ROPE))], axis=0)
    return (np.tile(cos, (1, MLA_HEADS)).astype(np.float32), np.tile(sin, (1, MLA_HEADS)).astype(np.float32))


def _rope_rotate(t, cos, sin):
    w = t.shape[1]
    lane = lax.broadcasted_iota(jnp.int32, (1, w), 1)
    first = (lane & 31) < 16
    up = pltpu.roll(t, w - 16, axis=1)
    dn = pltpu.roll(t, 16, axis=1)
    return t * cos + jnp.where(first, up, dn) * sin


def _mla_prep_kernel(p_ref, cos_ref, sin_ref, qan_ref, wuq_ref, kvan_ref, wukv_ref, qn_ref, kn_ref,
                     q_out, k_out, v_out):
    x = p_ref[0].astype(F32)
    cq = x[:, :MLA_Q_RANK]
    ckv = x[:, MLA_Q_RANK:MLA_Q_RANK + MLA_KV_RANK]
    kr2 = x[:, MLA_Q_RANK + MLA_KV_RANK:]

    def rms(t, w):
        return t * lax.rsqrt(jnp.mean(t * t, axis=-1, keepdims=True) + EPS) * w

    qf = jnp.dot(rms(cq, qan_ref[...]).astype(BF16), wuq_ref[...], preferred_element_type=F32)
    kvf = jnp.dot(rms(ckv, kvan_ref[...]).astype(BF16), wukv_ref[...], preferred_element_type=F32)
    cos = cos_ref[...]
    sin = sin_ref[...]
    qw = qn_ref[...]
    kw = kn_ref[...]
    n_all = MLA_HEADS * MLA_NOPE
    lane = lax.broadcasted_iota(jnp.int32, (1, LANES), 1)
    halves = (lane < MLA_ROPE, lane >= MLA_ROPE)

    q_rope = qf[:, n_all:]
    q_rope_sq = q_rope * q_rope
    q_rot = _rope_rotate(q_rope * qw[:, n_all:], cos, sin)
    kr_sq = jnp.sum(jnp.where(halves[0], kr2 * kr2, 0.0), axis=-1, keepdims=True)
    k_rot = _rope_rotate(kr2 * kw[:, n_all:], cos[:, :LANES], sin[:, :LANES])
    for h in range(MLA_HEADS):
        half = halves[h % 2]
        vsl = slice((h // 2) * LANES, (h // 2 + 1) * LANES)
        nsl = slice(h * MLA_NOPE, (h + 1) * MLA_NOPE)
        q_nope = qf[:, nsl]
        ss = (jnp.sum(q_nope * q_nope, axis=-1, keepdims=True)
              + jnp.sum(jnp.where(half, q_rope_sq[:, vsl], 0.0), axis=-1, keepdims=True))
        r = lax.rsqrt(ss * (1.0 / MLA_QK) + EPS) * (MLA_QK ** -0.5 * LOG2E)
        q_out[0, :, h * MLA_HW:h * MLA_HW + LANES] = (q_nope * qw[:, nsl] * r).astype(BF16)
        q_out[0, :, h * MLA_HW + LANES:(h + 1) * MLA_HW] = (jnp.where(half, q_rot[:, vsl], 0.0) * r).astype(BF16)
        k_nope = kvf[:, nsl]
        ss = jnp.sum(k_nope * k_nope, axis=-1, keepdims=True) + kr_sq
        r = lax.rsqrt(ss * (1.0 / MLA_QK) + EPS)
        k_out[0, :, h * MLA_HW:h * MLA_HW + LANES] = (k_nope * kw[:, nsl] * r).astype(BF16)
        k_out[0, :, h * MLA_HW + LANES:(h + 1) * MLA_HW] = (jnp.where(half, k_rot, 0.0) * r).astype(BF16)
    v_out[0] = kvf[:, n_all:].astype(BF16)


def _mla_prep(p, cos, sin, qa_norm, w_uq, kva_norm, w_ukv, q_norm, k_norm):
    B = p.shape[0]
    H = MLA_HEADS
    uq = w_uq.reshape(MLA_Q_RANK, H, MLA_QK)
    uq = jnp.concatenate([uq[:, :, :MLA_NOPE].reshape(MLA_Q_RANK, -1),
                          uq[:, :, MLA_NOPE:].reshape(MLA_Q_RANK, -1)], axis=1).astype(BF16)
    ukv = w_ukv.reshape(MLA_KV_RANK, H, MLA_NOPE + MLA_V)
    ukv = jnp.concatenate([ukv[:, :, :MLA_NOPE].reshape(MLA_KV_RANK, -1),
                           ukv[:, :, MLA_NOPE:].reshape(MLA_KV_RANK, -1)], axis=1).astype(BF16)
    qn = jnp.concatenate([jnp.tile(q_norm[:MLA_NOPE], H), jnp.tile(q_norm[MLA_NOPE:], H)]).reshape(1, -1)
    kn = jnp.concatenate([jnp.tile(k_norm[:MLA_NOPE], H), jnp.tile(k_norm[MLA_NOPE:], 2)]).reshape(1, -1)
    pc = _P_START['m_q'] // P_MLA_BLK
    const = lambda shape: pl.BlockSpec(shape, lambda b, i: (0, 0))
    rows = lambda w: pl.BlockSpec((MP_TM, w), lambda b, i: (i, 0))
    outs = lambda w: pl.BlockSpec((1, MP_TM, w), lambda b, i: (b, i, 0))
    return pl.pallas_call(
        _mla_prep_kernel,
        out_shape=(jax.ShapeDtypeStruct((B, S_ALL, H * MLA_HW), BF16),
                   jax.ShapeDtypeStruct((B, S_ALL, H * MLA_HW), BF16),
                   jax.ShapeDtypeStruct((B, S_ALL, MLA_W), BF16)),
        grid=(B, S_ALL // MP_TM),
        in_specs=[pl.BlockSpec((1, MP_TM, P_MLA_BLK), lambda b, i: (b, i, pc)),
                  rows(H * MLA_ROPE), rows(H * MLA_ROPE),
                  const((1, MLA_Q_RANK)), const(uq.shape), const((1, MLA_KV_RANK)), const(ukv.shape),
                  const(qn.shape), const(kn.shape)],
        out_specs=(outs(H * MLA_HW), outs(H * MLA_HW), outs(MLA_W)),
        compiler_params=_cparams(2),
        name="mla_prep",
    )(p, cos, sin, qa_norm.reshape(1, -1), uq, kva_norm.reshape(1, -1), ukv, qn, kn)


def _mla_attn_kernel(q_ref, k_ref, v_ref, z_ref, o_ref, *, ctx_start, n_lat_chunks):
    tq = q_ref.shape[1]
    sub = min(tq, MLA_SUBQ)
    bounds = [(ctx_start, ctx_start + CTX_LEN)] + [(i * MLA_TK, (i + 1) * MLA_TK) for i in range(n_lat_chunks)]

    def rows_chain(r0):
        q = q_ref[0, r0:r0 + sub, :]
        m = jnp.full((sub, 1), NEG_INF, F32)
        l = jnp.zeros((sub, 1), F32)
        acc = jnp.zeros((sub, MLA_V), F32)
        for lo, hi in bounds:
            s = _dot_nt(q, k_ref[0, lo:hi, :])
            yield
            m_new = jnp.maximum(m, jnp.max(s, axis=-1, keepdims=True))
            a = jnp.exp2(m - m_new)
            p = jnp.exp2(s - m_new)
            l = a * l + jnp.sum(p, axis=-1, keepdims=True)
            acc = a * acc + jnp.dot(p.astype(BF16), v_ref[0, lo:hi, :], preferred_element_type=F32)
            m = m_new
            yield
        z = z_ref[0, r0:r0 + sub, :].astype(F32)
        o_ref[0, r0:r0 + sub, :] = (acc / l * _silu(z)).astype(BF16)

    _round_robin(rows_chain(r0) for r0 in range(0, tq, sub))


def _mla_attn_ctx_kernel(q_ref, k_ref, v_ref, z_ref, lat_ref, o_ref, **kw):
    del lat_ref
    _mla_attn_kernel(q_ref, k_ref, v_ref, z_ref, o_ref, **kw)


def _mla_attend(q, k, v, p):
    B = p.shape[0]
    H = MLA_HEADS
    zc = _P_START['m_z'] // MLA_V
    ctx_blk = SEQ // CTX_LEN

    def call(tq, q_blk0, n_q, key_rows, key_blk, kern, name, into=None):
        specs = [pl.BlockSpec((1, tq, MLA_HW), lambda b, h, i: (b, q_blk0 + i, h)),
                 pl.BlockSpec((1, key_rows, MLA_HW), lambda b, h, i: (b, key_blk, h)),
                 pl.BlockSpec((1, key_rows, MLA_V), lambda b, h, i: (b, key_blk, h)),
                 pl.BlockSpec((1, tq, MLA_V), lambda b, h, i: (b, q_blk0 + i, zc + h))]
        args = (q, k, v, p)
        if into is not None:
            specs.append(pl.BlockSpec(memory_space=pl.ANY))
            args += (into,)
        return pl.pallas_call(
            kern,
            out_shape=jax.ShapeDtypeStruct((B, S_ALL, MLA_W), BF16),
            grid=(B, H, n_q),
            in_specs=specs,
            out_specs=pl.BlockSpec((1, tq, MLA_V), lambda b, h, i: (b, q_blk0 + i, h)),
            input_output_aliases={} if into is None else {len(args) - 1: 0},
            compiler_params=_cparams(3),
            name=name,
        )(*args)

    lat = call(MLA_TQ, 0, SEQ // MLA_TQ, S_ALL, 0,
               functools.partial(_mla_attn_kernel, ctx_start=SEQ, n_lat_chunks=SEQ // MLA_TK), "mla_attend")
    return call(CTX_LEN, ctx_blk, 1, CTX_LEN, ctx_blk,
                functools.partial(_mla_attn_ctx_kernel, ctx_start=0, n_lat_chunks=0), "mla_attend_ctx", into=lat)


SCAN_TILE = 256
SCAN_NT = S_ALL // SCAN_TILE
CTX_TILE = SEQ // SCAN_TILE
HALO = 16


def _split_dot(m_bf16, x):
    hi = x.astype(BF16)
    lo = (x - hi.astype(F32)).astype(BF16)
    return (jnp.dot(m_bf16, hi, preferred_element_type=F32)
            + jnp.dot(m_bf16, lo, preferred_element_type=F32))


def _softplus(t):
    return jnp.maximum(t, 0.0) + jnp.log1p(jnp.exp(-jnp.abs(t)))


def _conv3_silu(xb, prev_row, next_row, w, bias=None):
    n = xb.shape[0]
    i = lax.broadcasted_iota(jnp.int32, (n, 1), 0)
    j = lax.broadcasted_iota(jnp.int32, (1, n), 1)
    one = lambda m: jnp.where(m, 1.0, 0.0).astype(BF16)
    xp = jnp.dot(one(i == j + 1), xb, preferred_element_type=F32)
    xn = jnp.dot(one(i + 1 == j), xb, preferred_element_type=F32)
    y = xp * w[0:1] + xb.astype(F32) * w[1:2] + xn * w[2:3]
    r8 = lax.broadcasted_iota(jnp.int32, (8, 1), 0)
    top = y[0:8] + jnp.where(r8 == 0, prev_row * w[0:1], 0.0)
    bot = y[n - 8:] + jnp.where(r8 == 7, next_row * w[2:3], 0.0)
    y = jnp.concatenate([top, y[8:n - 8], bot], axis=0)
    if bias is not None:
        y = y + bias
    return _silu(y)


def _halo_rows(i, prev_ref, next_ref):
    pv = jnp.where((i == 0) | (i == CTX_TILE), 0.0, 1.0)
    nv = jnp.where((i == CTX_TILE - 1) | (i == SCAN_NT - 1), 0.0, 1.0)
    return prev_ref[0, HALO - 1:HALO, :].astype(F32) * pv, next_ref[0, 0:1, :].astype(F32) * nv


def _halo_specs(width, col_blk):
    rb = SCAN_TILE // HALO
    nblk = S_ALL // HALO
    return [pl.BlockSpec((1, SCAN_TILE, width), lambda b, i: (b, i, col_blk)),
            pl.BlockSpec((1, HALO, width), lambda b, i: (b, jnp.maximum(i * rb - 1, 0), col_blk)),
            pl.BlockSpec((1, HALO, width), lambda b, i: (b, jnp.minimum((i + 1) * rb, nblk - 1), col_blk))]


def _fwd_tile(t):
    return jnp.where(t == 0, CTX_TILE, t - 1)


def _bwd_tile(t):
    return jnp.where(t == 0, CTX_TILE, CTX_TILE - t)


def _chunk_masks(n):
    i = lax.broadcasted_iota(jnp.int32, (n, 1), 0)
    j = lax.broadcasted_iota(jnp.int32, (1, n), 1)
    same = (i // GDN_CHUNK) == (j // GDN_CHUNK)
    return i, j, same


def _gdn_prep_kernel(x_ref, prev_ref, next_ref, s_ref, cw_ref, rate_ref, dtb_ref,
                     q_out, k_out, v_out, g_out):
    i = pl.program_id(1)
    prev_row, next_row = _halo_rows(i, prev_ref, next_ref)
    y = _conv3_silu(x_ref[0], prev_row, next_row, cw_ref[...])
    for h in range(GDN_HEADS):
        sl = slice(h * GDN_DK, (h + 1) * GDN_DK)
        qh = y[:, sl]
        q_out[0, :, sl] = (qh * lax.rsqrt(jnp.sum(qh * qh, axis=-1, keepdims=True) + EPS)
                           * (GDN_DK ** -0.5)).astype(BF16)
        kh = y[:, GDN_W + h * GDN_DK:GDN_W + (h + 1) * GDN_DK]
        k_out[0, :, sl] = (kh * lax.rsqrt(jnp.sum(kh * kh, axis=-1, keepdims=True) + EPS)).astype(BF16)
    v_out[0] = y[:, 2 * GDN_W:].astype(BF16)

    s = s_ref[0]
    lane = lax.broadcasted_iota(jnp.int32, (1, LANES), 1)
    nh2 = 2 * GDN_HEADS
    beta = jax.nn.sigmoid(s)
    g = -rate_ref[...] * _softplus(s + dtb_ref[...])
    g = jnp.where((lane >= nh2) & (lane < 2 * nh2), g, 0.0)
    ii, jj, same = _chunk_masks(SCAN_TILE)
    one = lambda m: jnp.where(m, 1.0, 0.0).astype(BF16)
    fwd_lane = lane < nh2 + GDN_HEADS
    gam = jnp.where(fwd_lane, _split_dot(one(same & (jj <= ii)), g), _split_dot(one(same & (jj >= ii)), g))
    rem = jnp.where(fwd_lane, _split_dot(one(same & (jj > ii)), g), _split_dot(one(same & (jj < ii)), g))
    cf = jnp.where(lane < nh2, beta, jnp.where(lane < 2 * nh2, gam, pltpu.roll(rem, nh2, axis=1)))
    tr = cf.T
    for h in range(GDN_HEADS):
        for r, src in enumerate((h, GDN_HEADS + h, nh2 + h, nh2 + GDN_HEADS + h,
                                 2 * nh2 + h, 2 * nh2 + GDN_HEADS + h)):
            g_out[0, h, r:r + 1, :] = tr[src:src + 1, :]
        g_out[0, h, 6:8, :] = jnp.zeros((2, SCAN_TILE), F32)


def _gdn_prep(p, ps, conv_w, A_log, dt_bias):
    B = p.shape[0]
    W3 = 3 * GDN_W
    nh2 = 2 * GDN_HEADS
    rate = jnp.zeros((1, LANES), F32).at[0, nh2:2 * nh2].set(jnp.exp(A_log).reshape(-1))
    dtb = jnp.zeros((1, LANES), F32).at[0, nh2:2 * nh2].set(dt_bias.reshape(-1))
    shp = jax.ShapeDtypeStruct((B, S_ALL, GDN_W), BF16)
    ospec = pl.BlockSpec((1, SCAN_TILE, GDN_W), lambda b, i: (b, i, 0))
    const = lambda shape: pl.BlockSpec(shape, lambda b, i: (0, 0))
    return pl.pallas_call(
        _gdn_prep_kernel,
        out_shape=(shp, shp, shp, jax.ShapeDtypeStruct((B, GDN_HEADS, 8, S_ALL), F32)),
        grid=(B, SCAN_NT),
        in_specs=_halo_specs(W3, _P_START['g_qkv'] // W3)
        + [pl.BlockSpec((1, SCAN_TILE, LANES), lambda b, i: (b, i, 0)),
           const((SHORT_CONV, W3)), const((1, LANES)), const((1, LANES))],
        out_specs=(ospec, ospec, ospec,
                   pl.BlockSpec((1, GDN_HEADS, 8, SCAN_TILE), lambda b, i: (b, 0, 0, i))),
        compiler_params=_cparams(2),
        name="gdn_prep",
    )(p, p, p, ps, conv_w, rate, dtb)


def _gdn_dir(q, k, v, gr, s_ref, o_ref, d):
    n = SCAN_TILE
    cf = jnp.concatenate([gr, jnp.zeros((LANES - 8, n), F32)], axis=0).T
    beta, gam_c = cf[:, d:d + 1], cf[:, 2 + d:3 + d]
    ecf = jnp.exp(cf)
    e_gam, e_rem = ecf[:, 2 + d:3 + d], ecf[:, 4 + d:5 + d]
    gam_r = gr[2 + d:3 + d, :]
    ii, jj, same = _chunk_masks(n)
    incl = same & ((jj <= ii) if d == 0 else (jj >= ii))
    strict = same & ((jj < ii) if d == 0 else (jj > ii))
    kk = _dot_nt(k, k)
    yield
    qk = _dot_nt(q, k)
    yield
    dec = jnp.exp(jnp.where(incl, gam_c - gam_r, NEG_INF))
    a = jnp.where(strict, beta * kk * dec, 0.0)
    qkd = (qk * dec).astype(BF16)
    kf = k.astype(F32)
    x = jnp.concatenate([v.astype(F32) * beta, kf * (beta * e_gam)], axis=1)
    pb = a.astype(BF16)
    x = x - jnp.dot(pb, x.astype(BF16), preferred_element_type=F32)
    yield
    for _ in range(5):
        pb = jnp.dot(pb, pb, preferred_element_type=F32).astype(BF16)
        yield
        x = x + jnp.dot(pb, x.astype(BF16), preferred_element_type=F32)
        yield
    u, w =x[:, :GDN_DV], x[:, GDN_DV:].astype(BF16)
    qd = (q.astype(F32) * e_gam).astype(BF16)
    kd = (kf * e_rem).astype(BF16)
    s = s_ref[...]
    nchunk = n // GDN_CHUNK
    v_new = [None] * nchunk
    qs = [None] * nchunk
    for c in (range(nchunk) if d == 0 else reversed(range(nchunk))):
        rows = slice(c * GDN_CHUNK, (c + 1) * GDN_CHUNK)
        r1 = jnp.dot(jnp.concatenate([w[rows], qd[rows]], axis=0), s.astype(BF16), preferred_element_type=F32)
        yield
        vn = u[rows] - r1[:GDN_CHUNK]
        qs[c] = r1[GDN_CHUNK:]
        v_new[c] = vn
        last = (c + 1) * GDN_CHUNK - 1 if d == 0 else c * GDN_CHUNK
        s = s * ecf[last:last + 1, 2 + d:3 + d] + _dot_tn(kd[rows], vn.astype(BF16))
        yield
    s_ref[...] = s
    vn_all = jnp.concatenate(v_new, axis=0).astype(BF16)
    o_ref[0] = (jnp.concatenate(qs, axis=0) + jnp.dot(qkd, vn_all, preferred_element_type=F32)).astype(BF16)


def _round_robin(gens):
    gens = list(gens)
    while gens:
        alive = []
        for g in gens:
            try:
                next(g)
                alive.append(g)
            except StopIteration:
                pass
        gens = alive


def _gdn_scan_kernel(qf, kf, vf, gf, qb, kb, vb, gb, of_ref, ob_ref, sf_ref, sb_ref):
    @pl.when(pl.program_id(1) == 0)
    def _():
        sf_ref[...] = jnp.zeros_like(sf_ref)
        sb_ref[...] = jnp.zeros_like(sb_ref)

    chains = []
    for h in range(GDN_HEADS):
        sl = slice(h * GDN_DK, (h + 1) * GDN_DK)
        chains.append(_gdn_dir(qf[0, :, sl], kf[0, :, sl], vf[0, :, sl], gf[0, h],
                               sf_ref.at[h], of_ref.at[:, :, sl], 0))
        chains.append(_gdn_dir(qb[0, :, sl], kb[0, :, sl], vb[0, :, sl], gb[0, h],
                               sb_ref.at[h], ob_ref.at[:, :, sl], 1))
    _round_robin(chains)


def _gdn_scan(q, k, v, g):
    B = q.shape[0]
    tok = lambda order: pl.BlockSpec((1, SCAN_TILE, GDN_W), lambda b, t: (b, order(t), 0))
    gsp = lambda order: pl.BlockSpec((1, GDN_HEADS, 8, SCAN_TILE), lambda b, t: (b, 0, 0, order(t)))
    shp = jax.ShapeDtypeStruct((B, S_ALL, GDN_W), BF16)
    f, r = _fwd_tile, _bwd_tile
    state = pltpu.VMEM((GDN_HEADS, GDN_DK, GDN_DV), F32)
    return pl.pallas_call(
        _gdn_scan_kernel,
        out_shape=(shp, shp),
        grid=(B, SCAN_NT),
        in_specs=[tok(f), tok(f), tok(f), gsp(f), tok(r), tok(r), tok(r), gsp(r)],
        out_specs=(tok(f), tok(r)),
        scratch_shapes=[state, state],
        compiler_params=_cparams(2),
        name="gdn_scan",
    )(q, k, v, g, q, k, v, g)


SSD_BC = SSM_GROUPS * SSM_STATE
SSD_HD = 2 * SSM_HEADS
SSD_GW = (SSM_HEADS // SSM_GROUPS) * SSM_HEADDIM


def _ssd_prep_kernel(x_ref, prev_ref, next_ref, s_ref, cw_ref, cb_ref, a_ref, dtb_ref,
                     xs_out, b_out, c_out, bt_out, cf_out, cr_out):
    i = pl.program_id(1)
    prev_row, next_row = _halo_rows(i, prev_ref, next_ref)
    y = _conv3_silu(x_ref[0], prev_row, next_row, cw_ref[...], cb_ref[...])
    xs_out[0] = y[:, :SSM_W]
    b_out[0] = y[:, SSM_W:SSM_W + SSD_BC].astype(BF16)
    c_out[0] = y[:, SSM_W + SSD_BC:].astype(BF16)
    bt_out[0] = y[:, SSM_W:SSM_W + SSD_BC].T.astype(BF16)

    s = s_ref[0]
    lane = lax.broadcasted_iota(jnp.int32, (1, LANES), 1)
    dt = _softplus(s + dtb_ref[...])
    on = (lane >= SSD_HD) & (lane < 2 * SSD_HD)
    a = jnp.where(on, dt * a_ref[...], 0.0)
    n = SCAN_TILE
    ii = lax.broadcasted_iota(jnp.int32, (n, 1), 0)
    jj = lax.broadcasted_iota(jnp.int32, (1, n), 1)
    one = lambda m: jnp.where(m, 1.0, 0.0).astype(BF16)
    fwd_lane = lane < SSD_HD + SSM_HEADS
    cum = jnp.where(fwd_lane, _split_dot(one(jj <= ii), a), _split_dot(one(jj >= ii), a))
    rem = jnp.where(fwd_lane, _split_dot(one(jj > ii), a), _split_dot(one(jj < ii), a))
    cf = jnp.where(lane < SSD_HD, pltpu.roll(dt, LANES - SSD_HD, axis=1),
                   jnp.where(lane < 2 * SSD_HD, cum, pltpu.roll(rem, SSD_HD, axis=1)))
    cf_out[0] = cf
    cr_out[0] = cf.T[SSD_HD:2 * SSD_HD, :]


def _ssd_prep(p, ps, conv_w, conv_b, A_log, dt_bias):
    B = p.shape[0]
    W = SSM_CONV_DIM
    a_vec = jnp.zeros((1, LANES), F32).at[0, SSD_HD:2 * SSD_HD].set(-jnp.exp(A_log).reshape(-1))
    dtb = jnp.zeros((1, LANES), F32).at[0, SSD_HD:2 * SSD_HD].set(dt_bias.reshape(-1))
    const = lambda shape: pl.BlockSpec(shape, lambda b, i: (0, 0))
    tok = lambda w: pl.BlockSpec((1, SCAN_TILE, w), lambda b, i: (b, i, 0))
    return pl.pallas_call(
        _ssd_prep_kernel,
        out_shape=(jax.ShapeDtypeStruct((B, S_ALL, SSM_W), F32),
                   jax.ShapeDtypeStruct((B, S_ALL, SSD_BC), BF16),
                   jax.ShapeDtypeStruct((B, S_ALL, SSD_BC), BF16),
                   jax.ShapeDtypeStruct((B, SSD_BC, S_ALL), BF16),
                   jax.ShapeDtypeStruct((B, S_ALL, LANES), F32),
                   jax.ShapeDtypeStruct((B, SSD_HD, S_ALL), F32)),
        grid=(B, SCAN_NT),
        in_specs=_halo_specs(W, _P_START['s_xbc'] // W)
        + [pl.BlockSpec((1, SCAN_TILE, LANES), lambda b, i: (b, i, 0)),
           const((SHORT_CONV, W)), const((1, W)), const((1, LANES)), const((1, LANES))],
        out_specs=(tok(SSM_W), tok(SSD_BC), tok(SSD_BC),
                   pl.BlockSpec((1, SSD_BC, SCAN_TILE), lambda b, i: (b, 0, i)),
                   tok(LANES),
                   pl.BlockSpec((1, SSD_HD, SCAN_TILE), lambda b, i: (b, 0, i))),
        compiler_params=_cparams(2),
        name="ssd_prep",
    )(p, p, p, ps, conv_w, conv_b.reshape(1, W), a_vec, dtb)


def _per_head_lanes(cols):
    lane = lax.broadcasted_iota(jnp.int32, (1, LANES), 1)
    lo = lane < SSM_HEADDIM
    return jnp.concatenate([jnp.where(lo, cols[2 * j], cols[2 * j + 1]) for j in range(len(cols) // 2)], axis=1)


def _ssd_dir(x_ref, b_ref, c_ref, bt_ref, cf_ref, cr_ref, h_ref, y_ref, d, g):
    n = SCAN_TILE
    hpg = SSM_HEADS // SSM_GROUPS
    hd0 = d * SSM_HEADS + g * hpg
    gs = slice(g * SSM_STATE, (g + 1) * SSM_STATE)
    xl = slice(g * SSD_GW, (g + 1) * SSD_GW)
    cf = cf_ref[0]
    cr = cr_ref[0]
    cm = c_ref[0, :, gs]
    log_lanes = lax.broadcasted_iota(jnp.int32, (1, LANES), 1) >= SSD_HD
    ecf = jnp.exp(jnp.where(log_lanes, cf, 0.0))
    col = lambda base, h: cf[:, base + hd0 + h:base + hd0 + h + 1]
    ecol = lambda base, h: ecf[:, base + hd0 + h:base + hd0 + h + 1]
    heads = range(hpg)
    last = n - 1 if d == 0 else 0
    xdt = x_ref[0, :, xl] * _per_head_lanes([col(0, h) for h in heads])
    xdt_b = xdt.astype(BF16)
    xdec = (xdt * _per_head_lanes([ecol(2 * SSD_HD, h) for h in heads])).astype(BF16)
    cb = _dot_nt(cm, b_ref[0, :, gs])
    yield
    h_prev = h_ref[g]
    y_off = jnp.dot(cm, h_prev.astype(BF16), preferred_element_type=F32)
    yield
    y_off = y_off * _per_head_lanes([ecol(SSD_HD, h) for h in heads])
    etot = _per_head_lanes([ecf[last:last + 1, SSD_HD + hd0 + h:SSD_HD + hd0 + h + 1] for h in heads])
    h_ref[g] = h_prev * etot + jnp.dot(bt_ref[0, gs, :], xdec, preferred_element_type=F32)
    yield
    ii = lax.broadcasted_iota(jnp.int32, (n, 1), 0)
    jj = lax.broadcasted_iota(jnp.int32, (1, n), 1)
    causal = (jj <= ii) if d == 0 else (jj >= ii)
    lo = lax.broadcasted_iota(jnp.int32, (1, LANES), 1) < SSM_HEADDIM
    pair_out = []
    for j in range(hpg // 2):
        ys = []
        for e in range(2):
            h = 2 * j + e
            seg = col(SSD_HD, h) - cr[hd0 + h:hd0 + h + 1, :]
            sc = (cb * jnp.exp(jnp.where(causal, seg, NEG_INF))).astype(BF16)
            ys.append(jnp.dot(sc, xdt_b[:, j * LANES:(j + 1) * LANES], preferred_element_type=F32))
            yield
        pair_out.append(jnp.where(lo, ys[0], ys[1]))
    y_ref[0, :, xl] = (jnp.concatenate(pair_out, axis=1) + y_off).astype(BF16)


def _ssd_scan_kernel(xf, bf, cf_, btf, colf, rowf, xb, bb, cb_, btb, colb, rowb, yf_ref, yb_ref, hf_ref, hb_ref):
    @pl.when(pl.program_id(1) == 0)
    def _():
        hf_ref[...] = jnp.zeros_like(hf_ref)
        hb_ref[...] = jnp.zeros_like(hb_ref)

    chains = []
    for g in range(SSM_GROUPS):
        chains.append(_ssd_dir(xf, bf, cf_, btf, colf, rowf, hf_ref, yf_ref, 0, g))
        chains.append(_ssd_dir(xb, bb, cb_, btb, colb, rowb, hb_ref, yb_ref, 1, g))
    _round_robin(chains)


def _ssd_scan(xs, bm, cm, bt, cf, cr):
    B = xs.shape[0]

    def specs(order):
        tok = lambda w: pl.BlockSpec((1, SCAN_TILE, w), lambda b, t: (b, order(t), 0))
        return [tok(SSM_W), tok(SSD_BC), tok(SSD_BC),
                pl.BlockSpec((1, SSD_BC, SCAN_TILE), lambda b, t: (b, 0, order(t))),
                tok(LANES),
                pl.BlockSpec((1, SSD_HD, SCAN_TILE), lambda b, t: (b, 0, order(t)))]

    shp = jax.ShapeDtypeStruct((B, S_ALL, SSM_W), BF16)
    out = lambda order: pl.BlockSpec((1, SCAN_TILE, SSM_W), lambda b, t: (b, order(t), 0))
    hshape = pltpu.VMEM((SSM_GROUPS, SSM_STATE, SSD_GW), F32)
    args = (xs, bm, cm, bt, cf, cr)
    return pl.pallas_call(
        _ssd_scan_kernel,
        out_shape=(shp, shp),
        grid=(B, SCAN_NT),
        in_specs=specs(_fwd_tile) + specs(_bwd_tile),
        out_specs=(out(_fwd_tile), out(_bwd_tile)),
        scratch_shapes=[hshape, hshape],
        compiler_params=_cparams(2),
        name="ssd_scan",
    )(*args, *args)


def _repack_w_in(w):
    cut = lambda names: [w[:, _IN_START[n]:_IN_START[n] + _IN_SIZE[n]] for n in names]
    zeros = lambda n: jnp.zeros((w.shape[0], n), w.dtype)
    n_small = sum(_IN_SIZE[n] for n in _S_ORDER)
    cols = cut(_P_ORDER) + [zeros(D_INP - LANES - _off)] + cut(_S_ORDER) + [zeros(LANES - n_small)]
    return jnp.concatenate(cols, axis=1).astype(BF16)


def kernel(x, c, ctx, c_ctx, norm_w, ada_w, ada_b, w_in, gdn_conv_w, gdn_A_log, gdn_dt_bias, gdn_norm_w, na_q_norm, na_k_norm, na_rpb, mla_qa_norm, mla_w_uq, mla_kva_norm, mla_w_ukv, mla_q_norm, mla_k_norm, ssm_conv_w, ssm_conv_b, ssm_A_log, ssm_dt_bias, ssm_D, ssm_norm_w, w_out):
    B = x.shape[0]
    xs = jnp.concatenate([x, ctx], axis=1)
    c8 = jnp.zeros((8, D_MODEL), F32).at[:B].set(c).at[B].set(c_ctx)
    mods = _ada_all(c8, ada_w, ada_b)
    cos_np, sin_np = _rope_tables()
    cos, sin = jnp.asarray(cos_np), jnp.asarray(sin_np)
    na_bias = _na_bias(na_rpb)
    for l in range(DEPTH):
        shift, scale, gate = jnp.split(mods[l, :B], 3, axis=-1)
        shift_c, scale_c, gate_c = jnp.split(mods[l, B], 3, axis=-1)
        bc = lambda v: jnp.broadcast_to(v[None], (B, D_MODEL))
        mod4 = jnp.stack([shift, scale, bc(shift_c), bc(scale_c)], axis=1)
        gate2 = jnp.stack([gate, bc(gate_c)], axis=1)
        p, ps = _inproj(xs, norm_w[l], mod4, _repack_w_in(w_in[l]))

        gq, gk, gv, gg = _gdn_prep(p, ps, gdn_conv_w[l], gdn_A_log[l], gdn_dt_bias[l])
        o_f, o_b = _gdn_scan(gq, gk, gv, gg)

        qn, kn, vn = _na_prep(p, na_q_norm[l], na_k_norm[l])
        ob = _na_attend(qn, kn, vn, p, na_bias[l])

        mq, mk, mv = _mla_prep(p, cos, sin, mla_qa_norm[l], mla_w_uq[l], mla_kva_norm[l], mla_w_ukv[l],
                               mla_q_norm[l], mla_k_norm[l])
        oc = _mla_attend(mq, mk, mv, p)

        sx, sb, sc, sbt, scf, scr = _ssd_prep(p, ps, ssm_conv_w[l], ssm_conv_b[l], ssm_A_log[l], ssm_dt_bias[l])
        y_f, y_b = _ssd_scan(sx, sb, sc, sbt, scf, scr)
        xs = _outproj((o_f, o_b, gdn_norm_w[l]), ob, oc, (y_f, y_b, sx, ssm_D[l], ssm_norm_w[l]), p,
                      w_out[l].astype(BF16), xs, gate2, last=(l == DEPTH - 1))
    return xs
```

```python
import functools

import jax
import jax.numpy as jnp
import numpy as np
from jax import lax
from jax.experimental import pallas as pl
from jax.experimental.pallas import tpu as pltpu

F32 = jnp.float32
BF16 = jnp.bfloat16

D_MODEL = 2048
BATCH = 4
SEQ = 4096
DEPTH = 4
GRID_W = 64
GRID_H = SEQ // GRID_W
CTX_LEN = 256
S_ALL = SEQ + CTX_LEN
EPS = 1e-6
NEG_INF = -1e30
LOG2E = 1.4426950408889634

D_BRANCH = 512
D_MIX = 4 * D_BRANCH
SHORT_CONV = 3

GDN_HEADS = 4
GDN_DK = 128
GDN_DV = 128
GDN_W = GDN_HEADS * GDN_DV
GDN_CHUNK = 64

NA_HEADS = 4
NA_DH = 128
NA_W = NA_HEADS * NA_DH
NA_WIN_R = 8
NA_WIN_C = 16

MLA_HEADS = 4
MLA_Q_RANK = 384
MLA_KV_RANK = 256
MLA_NOPE = 128
MLA_ROPE = 64
MLA_QK = MLA_NOPE + MLA_ROPE
MLA_V = 128
MLA_W = MLA_HEADS * MLA_V
ROPE_THETA = 10000.0

SSM_HEADDIM = 64
SSM_HEADS = D_BRANCH // SSM_HEADDIM
SSM_W = SSM_HEADS * SSM_HEADDIM
SSM_GROUPS = 2
SSM_STATE = 128
SSM_CONV_DIM = SSM_W + 2 * SSM_GROUPS * SSM_STATE

IN_SIZES = (3 * GDN_W, GDN_W, 2 * GDN_HEADS, 2 * GDN_HEADS,
            3 * NA_W, NA_W,
            MLA_Q_RANK, MLA_KV_RANK, MLA_ROPE, MLA_W,
            SSM_W, SSM_CONV_DIM, 2 * SSM_HEADS)
D_IN = sum(IN_SIZES)
_IN_NAMES = ('g_qkv', 'g_z', 'g_beta', 'g_alpha', 'n_qkv', 'n_z',
             'm_q', 'm_kv', 'm_kr', 'm_z', 's_z', 's_xbc', 's_dt')
_IN_START = dict(zip(_IN_NAMES, np.cumsum((0,) + IN_SIZES[:-1]).tolist()))
_IN_SIZE = dict(zip(_IN_NAMES, IN_SIZES))

LANES = 128
_P_ORDER = ('g_qkv', 'g_z', 'n_qkv', 'n_z', 'm_z', 's_z', 's_xbc',
            'm_q', 'm_kv', 'm_kr', 'm_kr')
_S_ORDER = ('g_beta', 'g_alpha', 's_dt')
_P_START = {}
_off = 0
for _n in _P_ORDER:
    _P_START.setdefault(_n, _off)
    _off += _IN_SIZE[_n]
MXU_N = 256
D_INP = -(-_off // (2 * MXU_N)) * (2 * MXU_N)
P_MLA_BLK = MLA_Q_RANK + MLA_KV_RANK + 2 * MLA_ROPE
assert _P_START['m_q'] % P_MLA_BLK == 0 and D_INP % LANES == 0

VMEM_LIMIT = 52 * 1024 * 1024


def _silu(x):
    return x * jax.nn.sigmoid(x)


def _dot_nt(a, b):
    return lax.dot_general(a, b, (((1,), (1,)), ((), ())), preferred_element_type=F32)


def _dot_tn(a, b):
    return lax.dot_general(a, b, (((0,), (0,)), ((), ())), preferred_element_type=F32)


def _cparams(n_axes):
    return pltpu.CompilerParams(dimension_semantics=("arbitrary",) * n_axes,
                                vmem_limit_bytes=VMEM_LIMIT)


def _ada_kernel(c_ref, w_ref, b_ref, o_ref):
    a = _silu(c_ref[...]).astype(BF16)
    o_ref[0] = jnp.dot(a, w_ref[0].astype(BF16), preferred_element_type=F32) + b_ref[0]


def _ada_all(c8, ada_w, ada_b):
    tn = 1536
    L = ada_w.shape[0]
    n3 = ada_w.shape[2]
    return pl.pallas_call(
        _ada_kernel,
        out_shape=jax.ShapeDtypeStruct((L, 8, n3), F32),
        grid=(L, n3 // tn),
        in_specs=[pl.BlockSpec((8, D_MODEL), lambda l, j: (0, 0)),
                  pl.BlockSpec((1, D_MODEL, tn), lambda l, j: (l, 0, j)),
                  pl.BlockSpec((1, 1, tn), lambda l, j: (l, 0, j))],
        out_specs=pl.BlockSpec((1, 8, tn), lambda l, j: (l, 0, j)),
        compiler_params=_cparams(2),
        name="ada_mod",
    )(c8, ada_w, ada_b.reshape(L, 1, n3))


IN_TM = 1088
IN_TN = 1024
IN_RC = 16
IN_SECTIONS = 4


def _inproj_kernel(x_ref, nw_ref, mod_ref, w_ref, o_ref, os_ref, h_scr):
    i = pl.program_id(1)
    j = pl.program_id(2)

    @pl.when(j == 0)
    def _():
        m = mod_ref[0]
        nw = nw_ref[...]
        gain_l = nw * (1.0 + m[1:2])
        gain_c = nw * (1.0 + m[3:4])
        sec = IN_TM // IN_SECTIONS
        for c in range(IN_SECTIONS):
            for r0 in range(c * sec, (c + 1) * sec, IN_RC):
                x = x_ref[0, r0:r0 + IN_RC, :]
                ms = jnp.mean(x * x, axis=-1, keepdims=True)
                is_ctx = i * IN_TM + r0 >= SEQ
                gain = jnp.where(is_ctx, gain_c, gain_l)
                shift = jnp.where(is_ctx, m[2:3], m[0:1])
                h_scr[r0:r0 + IN_RC, :] = (x * lax.rsqrt(ms + EPS) * gain + shift).astype(BF16)
            rows = slice(c * sec, (c + 1) * sec)
            o_ref[0, rows, :] = jnp.dot(h_scr[rows, :], w_ref[...], preferred_element_type=F32).astype(BF16)

    @pl.when(j > 0)
    def _():
        y = jnp.dot(h_scr[...], w_ref[...], preferred_element_type=F32)
        o_ref[0] = y.astype(BF16)

        @pl.when(j == pl.num_programs(2) - 1)
        def _():
            os_ref[0] = y[:, IN_TN - LANES:]


def _inproj(xs, norm_w, mod4, w_main):
    B = xs.shape[0]
    return pl.pallas_call(
        _inproj_kernel,
        out_shape=(jax.ShapeDtypeStruct((B, S_ALL, D_INP), BF16),
                   jax.ShapeDtypeStruct((B, S_ALL, LANES), F32)),
        grid=(B, S_ALL // IN_TM, D_INP // IN_TN),
        in_specs=[pl.BlockSpec((1, IN_TM, D_MODEL), lambda b, i, j: (b, i, 0)),
                  pl.BlockSpec((1, D_MODEL), lambda b, i, j: (0, 0)),
                  pl.BlockSpec((1, 4, D_MODEL), lambda b, i, j: (b, 0, 0)),
                  pl.BlockSpec((D_MODEL, IN_TN), lambda b, i, j: (0, j))],
        out_specs=(pl.BlockSpec((1, IN_TM, IN_TN), lambda b, i, j: (b, i, j)),
                   pl.BlockSpec((1, IN_TM, LANES), lambda b, i, j: (b, i, 0))),
        scratch_shapes=[pltpu.VMEM((IN_TM, D_MODEL), BF16)],
        compiler_params=_cparams(3),
        name="inproj",
    )(xs, norm_w.reshape(1, D_MODEL), mod4, w_main)


OUT_TM = 544
OUT_TM_LAST = 512
OUT_SECTIONS = 2


def _gdn_gate(o_f, o_b, z, nw):
    o = o_f.astype(F32) + o_b.astype(F32)
    z = z.astype(F32)
    outs = []
    for h in range(GDN_HEADS):
        sl = slice(h * GDN_DV, (h + 1) * GDN_DV)
        oh = o[:, sl]
        y = oh * lax.rsqrt(jnp.mean(oh * oh, axis=-1, keepdims=True) + EPS) * nw
        outs.append((y * _silu(z[:, sl])).astype(BF16))
    return jnp.concatenate(outs, axis=-1)


def _ssd_gate(y_f, y_b, xs, z, d_skip, nw):
    y = y_f.astype(F32) + y_b.astype(F32) + d_skip * xs
    y = y * _silu(z.astype(F32))
    return (y * lax.rsqrt(jnp.mean(y * y, axis=-1, keepdims=True) + EPS) * nw).astype(BF16)


def _outproj_kernel(of_ref, ob_ref, gz_ref, gnw_ref, na_ref, mla_ref, yf_ref, yb_ref, sx_ref, sz_ref,
                    dsk_ref, snw_ref, w_ref, x_ref, g_ref, o_ref, *, tm):
    i = pl.program_id(1)
    g = g_ref[0]
    sec = tm // OUT_SECTIONS
    for c in range(OUT_SECTIONS):
        rows = slice(c * sec, (c + 1) * sec)
        branches = (_gdn_gate(of_ref[0, rows, :], ob_ref[0, rows, :], gz_ref[0, rows, :], gnw_ref[...]),
                    na_ref[0, rows, :], mla_ref[0, rows, :],
                    _ssd_gate(yf_ref[0, rows, :], yb_ref[0, rows, :], sx_ref[0, rows, :], sz_ref[0, rows, :],
                              dsk_ref[...], snw_ref[...]))
        y = None
        for n, a in enumerate(branches):
            t = jnp.dot(a, w_ref[n * D_BRANCH:(n + 1) * D_BRANCH, :], preferred_element_type=F32)
            y = t if y is None else y + t
        row = i * tm + c * sec + lax.broadcasted_iota(jnp.int32, (sec, 1), 0)
        gate = jnp.where(row >= SEQ, g[1:2], g[0:1])
        o_ref[0, rows, :] = x_ref[0, rows, :] + gate * y


def _outproj(gdn, na, mla, ssd, p, w_out_b, xs, gate2, last):
    B = xs.shape[0]
    o_f, o_b, g_nw = gdn
    y_f, y_b, s_x, d_skip, s_nw = ssd
    tm, rows = (OUT_TM_LAST, SEQ) if last else (OUT_TM, S_ALL)
    a_spec = pl.BlockSpec((1, tm, D_BRANCH), lambda b, i: (b, i, 0))
    z_spec = lambda name: pl.BlockSpec((1, tm, D_BRANCH), lambda b, i: (b, i, _P_START[name] // D_BRANCH))
    x_spec = pl.BlockSpec((1, tm, D_MODEL), lambda b, i: (b, i, 0))
    vec = lambda n: pl.BlockSpec((1, n), lambda b, i: (0, 0))
    return pl.pallas_call(
        functools.partial(_outproj_kernel, tm=tm),
        out_shape=jax.ShapeDtypeStruct((B, rows, D_MODEL), F32),
        grid=(B, rows // tm),
        in_specs=[a_spec, a_spec, z_spec('g_z'), vec(GDN_DV), a_spec, a_spec,
                  a_spec, a_spec, a_spec, z_spec('s_z'), vec(SSM_W), vec(SSM_W),
                  pl.BlockSpec((D_MIX, D_MODEL), lambda b, i: (0, 0)),
                  x_spec,
                  pl.BlockSpec((1, 2, D_MODEL), lambda b, i: (b, 0, 0))],
        out_specs=x_spec,
        compiler_params=_cparams(2),
        name="outproj",
    )(o_f, o_b, p, g_nw.reshape(1, GDN_DV), na, mla,
      y_f, y_b, s_x, p, jnp.repeat(d_skip, SSM_HEADDIM).reshape(1, SSM_W), s_nw.reshape(1, SSM_W),
      w_out_b, xs, gate2)


NA_RB = 4
NA_QB = NA_RB * GRID_W
NA_KR = 12
NA_KW = NA_KR * GRID_W
NA_NBLK = GRID_H // NA_RB
assert NA_QB == CTX_LEN


def _na_headnorm(x, w, extra):
    x = x.astype(F32)
    outs = []
    for h in range(NA_HEADS):
        xh = x[:, h * NA_DH:(h + 1) * NA_DH]
        ms = jnp.mean(xh * xh, axis=-1, keepdims=True)
        outs.append((xh * lax.rsqrt(ms + EPS) * w * extra).astype(BF16))
    return jnp.concatenate(outs, axis=-1)


def _na_fill_bias(toe_ref, bias_scr, cls):
    dr, ok, _ = _na_bias_index()
    dead = jnp.full((GRID_W, GRID_W), NEG_INF, F32)
    for h in range(NA_HEADS):
        for a in range(NA_RB):
            for b in range(0, NA_KR, 2):
                pair = [toe_ref[h, int(dr[cls, a, b + e])] if ok[cls, a, b + e] else dead for e in range(2)]
                bias_scr[h, a * GRID_W:(a + 1) * GRID_W, b * GRID_W:(b + 2) * GRID_W] = jnp.concatenate(pair, axis=1)


def _na_kernel(q_ref, kraw_ref, v_ref, z_ref, toe_ref, qn_ref, kn_ref, o_ref, bias_scr, k_ref):
    rb = pl.program_id(1)

    @pl.when(rb == 0)
    def _():
        def body(t, carry):
            r0 = pl.multiple_of(t * NA_QB, NA_QB)
            k_ref[pl.ds(r0, NA_QB), :] = _na_headnorm(kraw_ref[0, pl.ds(r0, NA_QB), :], kn_ref[...], 1.0)
            return carry

        lax.fori_loop(0, S_ALL // NA_QB, body, 0)

    q = _na_headnorm(q_ref[0], qn_ref[...], NA_DH ** -0.5 * LOG2E)
    z = z_ref[0].astype(F32)
    kc = k_ref[SEQ:S_ALL, :]
    vc = v_ref[0, SEQ:S_ALL, :]

    for cls, first_rb in enumerate((0, 1, NA_NBLK - 1)):
        @pl.when(rb == first_rb)
        def _(cls=cls):
            _na_fill_bias(toe_ref, bias_scr, cls)

    def finish(h, o, l):
        sl = slice(h * NA_DH, (h + 1) * NA_DH)
        o_ref[0, :, sl] = (o / l * _silu(z[:, sl])).astype(BF16)

    @pl.when(rb < NA_NBLK)
    def _latent():
        base = jnp.clip(rb * NA_RB - NA_RB, 0, GRID_H - NA_KR)
        start = pl.multiple_of(base * GRID_W, GRID_W)
        kw = k_ref[pl.ds(start, NA_KW), :]
        vw = v_ref[0, pl.ds(start, NA_KW), :]
        def head(h):
            sl = slice(h * NA_DH, (h + 1) * NA_DH)
            s_w = _dot_nt(q[:, sl], kw[:, sl])
            yield
            s_c = _dot_nt(q[:, sl], kc[:, sl])
            yield
            s_w = s_w + bias_scr[h]
            m = jnp.maximum(jnp.max(s_w, axis=-1, keepdims=True), jnp.max(s_c, axis=-1, keepdims=True))
            p_w = jnp.exp2(s_w - m)
            p_c = jnp.exp2(s_c - m)
            l = jnp.sum(p_w, axis=-1, keepdims=True) + jnp.sum(p_c, axis=-1, keepdims=True)
            o = jnp.dot(p_w.astype(BF16), vw[:, sl], preferred_element_type=F32)
            yield
            o = o + jnp.dot(p_c.astype(BF16), vc[:, sl], preferred_element_type=F32)
            yield
            finish(h, o, l)

        _round_robin(head(h) for h in range(NA_HEADS))

    @pl.when(rb == NA_NBLK)
    def _context():
        for h in range(NA_HEADS):
            sl = slice(h * NA_DH, (h + 1) * NA_DH)
            s_c = _dot_nt(q[:, sl], kc[:, sl])
            m = jnp.max(s_c, axis=-1, keepdims=True)
            p_c = jnp.exp2(s_c - m)
            l = jnp.sum(p_c, axis=-1, keepdims=True)
            o = jnp.dot(p_c.astype(BF16), vc[:, sl], preferred_element_type=F32)
            finish(h, o, l)


def _na_bias_index():
    dr = np.zeros((3, NA_RB, NA_KR), np.int64)
    ok = np.zeros((3, NA_RB, NA_KR), bool)
    for ci, rb in enumerate((0, 1, NA_NBLK - 1)):
        base = int(np.clip(rb * NA_RB - NA_RB, 0, GRID_H - NA_KR))
        qr = rb * NA_RB + np.arange(NA_RB)[:, None]
        kr = base + np.arange(NA_KR)[None, :]
        row0 = np.clip(qr - NA_WIN_R // 2, 0, GRID_H - NA_WIN_R)
        ok[ci] = (kr >= row0) & (kr < row0 + NA_WIN_R)
        dr[ci] = np.clip(kr - qr + NA_WIN_R - 1, 0, 2 * NA_WIN_R - 2)
    qc = np.arange(GRID_W)[:, None]
    kc = np.arange(GRID_W)[None, :]
    win0 = np.clip(qc - NA_WIN_C // 2, 0, GRID_W - NA_WIN_C)
    col_ok = (kc >= win0) & (kc < win0 + NA_WIN_C)
    return dr, ok, col_ok


def _na_bias(rpb):
    L, H = rpb.shape[:2]
    nd = 2 * NA_WIN_R - 1
    _, _, col_ok = _na_bias_index()
    left = GRID_W - NA_WIN_C
    f = jnp.pad(rpb * LOG2E, ((0, 0), (0, 0), (0, 0), (left, 2 * GRID_W - (2 * NA_WIN_C - 1) - left)))
    skew = jnp.broadcast_to(f[:, :, :, None, :], (L, H, nd, GRID_W, 2 * GRID_W))
    skew = skew.reshape(L, H, nd, -1)[..., :GRID_W * (2 * GRID_W - 1)].reshape(L, H, nd, GRID_W, 2 * GRID_W - 1)
    toe = skew[..., GRID_W - 1:]
    return jnp.where(col_ok, toe, NEG_INF)


def _na_attend(p, q_norm, k_norm, toe):
    B = p.shape[0]
    c0 = _P_START['n_qkv'] // NA_W
    zc = _P_START['n_z'] // NA_W
    blk = lambda c: pl.BlockSpec((1, NA_QB, NA_W), lambda b, r: (b, r, c))
    full = lambda c: pl.BlockSpec((1, S_ALL, NA_W), lambda b, r: (b, 0, c))
    wspec = pl.BlockSpec((1, NA_DH), lambda b, r: (0, 0))
    return pl.pallas_call(
        _na_kernel,
        out_shape=jax.ShapeDtypeStruct((B, S_ALL, NA_W), BF16),
        grid=(B, NA_NBLK + 1),
        in_specs=[blk(c0), full(c0 + 1), full(c0 + 2), blk(zc),
                  pl.BlockSpec(toe.shape, lambda b, r: (0, 0, 0, 0)), wspec, wspec],
        out_specs=blk(0),
        scratch_shapes=[pltpu.VMEM((NA_HEADS, NA_QB, NA_KW), F32), pltpu.VMEM((S_ALL, NA_W), BF16)],
        compiler_params=_cparams(2),
        name="na_attend",
    )(p, p, p, p, toe, q_norm.reshape(1, NA_DH), k_norm.reshape(1, NA_DH))


MP_TM = 544
MLA_HW = 2 * LANES
MLA_TQ = 1024
MLA_SUBQ = 512
MLA_TK = 1024


def _rope_tables():
    n_freq = MLA_ROPE // 4
    inv_freq = ROPE_THETA ** (-np.arange(n_freq, dtype=np.float64) / n_freq)
    t = np.arange(SEQ)
    ar = (t // GRID_W)[:, None] * inv_freq
    ac = (t % GRID_W)[:, None] * inv_freq
    cos = np.concatenate([np.cos(ar), np.cos(ar), np.cos(ac), np.cos(ac)], axis=1)
    sin = np.concatenate([-np.sin(ar), np.sin(ar), -np.sin(ac), np.sin(ac)], axis=1)
    cos = np.concatenate([cos, np.ones((CTX_LEN, MLA_ROPE))], axis=0)
    sin = np.concatenate([sin, np.zeros((CTX_LEN, MLA_ROPE))], axis=0)
    return (np.tile(cos, (1, MLA_HEADS)).astype(np.float32), np.tile(sin, (1, MLA_HEADS)).astype(np.float32))


def _rope_rotate(t, cos, sin):
    w = t.shape[1]
    lane = lax.broadcasted_iota(jnp.int32, (1, w), 1)
    first = (lane & 31) < 16
    up = pltpu.roll(t, w - 16, axis=1)
    dn = pltpu.roll(t, 16, axis=1)
    return t * cos + jnp.where(first, up, dn) * sin


def _mla_prep_kernel(p_ref, cos_ref, sin_ref, qan_ref, wuq_ref, kvan_ref, wukv_ref, qn_ref, kn_ref,
                     q_out, k_out, v_out):
    x = p_ref[0].astype(F32)
    cq = x[:, :MLA_Q_RANK]
    ckv = x[:, MLA_Q_RANK:MLA_Q_RANK + MLA_KV_RANK]
    kr2 = x[:, MLA_Q_RANK + MLA_KV_RANK:]

    def rms(t, w):
        return t * lax.rsqrt(jnp.mean(t * t, axis=-1, keepdims=True) + EPS) * w

    qf = jnp.dot(rms(cq, qan_ref[...]).astype(BF16), wuq_ref[...], preferred_element_type=F32)
    kvf = jnp.dot(rms(ckv, kvan_ref[...]).astype(BF16), wukv_ref[...], preferred_element_type=F32)
    cos = cos_ref[...]
    sin = sin_ref[...]
    qw = qn_ref[...]
    kw = kn_ref[...]
    n_all = MLA_HEADS * MLA_NOPE
    lane = lax.broadcasted_iota(jnp.int32, (1, LANES), 1)
    halves = (lane < MLA_ROPE, lane >= MLA_ROPE)

    q_rope = qf[:, n_all:]
    q_rope_sq = q_rope * q_rope
    q_rot = _rope_rotate(q_rope * qw[:, n_all:], cos, sin)
    kr_sq = jnp.sum(jnp.where(halves[0], kr2 * kr2, 0.0), axis=-1, keepdims=True)
    k_rot = _rope_rotate(kr2 * kw[:, n_all:], cos[:, :LANES], sin[:, :LANES])
    for h in range(MLA_HEADS):
        half = halves[h % 2]
        vsl = slice((h // 2) * LANES, (h // 2 + 1) * LANES)
        nsl = slice(h * MLA_NOPE, (h + 1) * MLA_NOPE)
        q_nope = qf[:, nsl]
        ss = (jnp.sum(q_nope * q_nope, axis=-1, keepdims=True)
              + jnp.sum(jnp.where(half, q_rope_sq[:, vsl], 0.0), axis=-1, keepdims=True))
        r = lax.rsqrt(ss * (1.0 / MLA_QK) + EPS) * (MLA_QK ** -0.5 * LOG2E)
        q_out[0, :, h * MLA_HW:h * MLA_HW + LANES] = (q_nope * qw[:, nsl] * r).astype(BF16)
        q_out[0, :, h * MLA_HW + LANES:(h + 1) * MLA_HW] = (jnp.where(half, q_rot[:, vsl], 0.0) * r).astype(BF16)
        k_nope = kvf[:, nsl]
        ss = jnp.sum(k_nope * k_nope, axis=-1, keepdims=True) + kr_sq
        r = lax.rsqrt(ss * (1.0 / MLA_QK) + EPS)
        k_out[0, :, h * MLA_HW:h * MLA_HW + LANES] = (k_nope * kw[:, nsl] * r).astype(BF16)
        k_out[0, :, h * MLA_HW + LANES:(h + 1) * MLA_HW] = (jnp.where(half, k_rot, 0.0) * r).astype(BF16)
    v_out[0] = kvf[:, n_all:].astype(BF16)


def _mla_prep(p, cos, sin, qa_norm, w_uq, kva_norm, w_ukv, q_norm, k_norm):
    B = p.shape[0]
    H = MLA_HEADS
    uq = w_uq.reshape(MLA_Q_RANK, H, MLA_QK)
    uq = jnp.concatenate([uq[:, :, :MLA_NOPE].reshape(MLA_Q_RANK, -1),
                          uq[:, :, MLA_NOPE:].reshape(MLA_Q_RANK, -1)], axis=1).astype(BF16)
    ukv = w_ukv.reshape(MLA_KV_RANK, H, MLA_NOPE + MLA_V)
    ukv = jnp.concatenate([ukv[:, :, :MLA_NOPE].reshape(MLA_KV_RANK, -1),
                           ukv[:, :, MLA_NOPE:].reshape(MLA_KV_RANK, -1)], axis=1).astype(BF16)
    qn = jnp.concatenate([jnp.tile(q_norm[:MLA_NOPE], H), jnp.tile(q_norm[MLA_NOPE:], H)]).reshape(1, -1)
    kn = jnp.concatenate([jnp.tile(k_norm[:MLA_NOPE], H), jnp.tile(k_norm[MLA_NOPE:], 2)]).reshape(1, -1)
    pc = _P_START['m_q'] // P_MLA_BLK
    const = lambda shape: pl.BlockSpec(shape, lambda b, i: (0, 0))
    rows = lambda w: pl.BlockSpec((MP_TM, w), lambda b, i: (i, 0))
    outs = lambda w: pl.BlockSpec((1, MP_TM, w), lambda b, i: (b, i, 0))
    return pl.pallas_call(
        _mla_prep_kernel,
        out_shape=(jax.ShapeDtypeStruct((B, S_ALL, H * MLA_HW), BF16),
                   jax.ShapeDtypeStruct((B, S_ALL, H * MLA_HW), BF16),
                   jax.ShapeDtypeStruct((B, S_ALL, MLA_W), BF16)),
        grid=(B, S_ALL // MP_TM),
        in_specs=[pl.BlockSpec((1, MP_TM, P_MLA_BLK), lambda b, i: (b, i, pc)),
                  rows(H * MLA_ROPE), rows(H * MLA_ROPE),
                  const((1, MLA_Q_RANK)), const(uq.shape), const((1, MLA_KV_RANK)), const(ukv.shape),
                  const(qn.shape), const(kn.shape)],
        out_specs=(outs(H * MLA_HW), outs(H * MLA_HW), outs(MLA_W)),
        compiler_params=_cparams(2),
        name="mla_prep",
    )(p, cos, sin, qa_norm.reshape(1, -1), uq, kva_norm.reshape(1, -1), ukv, qn, kn)


def _mla_attn_kernel(q_ref, k_ref, v_ref, z_ref, o_ref, *, ctx_start, n_lat_chunks):
    tq = q_ref.shape[1]
    sub = min(tq, MLA_SUBQ)
    bounds = [(ctx_start, ctx_start + CTX_LEN)] + [(i * MLA_TK, (i + 1) * MLA_TK) for i in range(n_lat_chunks)]

    def rows_chain(r0):
        q = q_ref[0, r0:r0 + sub, :]
        m = jnp.full((sub, 1), NEG_INF, F32)
        l = jnp.zeros((sub, 1), F32)
        acc = jnp.zeros((sub, MLA_V), F32)
        for lo, hi in bounds:
            s = _dot_nt(q, k_ref[0, lo:hi, :])
            yield
            m_new = jnp.maximum(m, jnp.max(s, axis=-1, keepdims=True))
            a = jnp.exp2(m - m_new)
            p = jnp.exp2(s - m_new)
            l = a * l + jnp.sum(p, axis=-1, keepdims=True)
            acc = a * acc + jnp.dot(p.astype(BF16), v_ref[0, lo:hi, :], preferred_element_type=F32)
            m = m_new
            yield
        z = z_ref[0, r0:r0 + sub, :].astype(F32)
        o_ref[0, r0:r0 + sub, :] = (acc / l * _silu(z)).astype(BF16)

    _round_robin(rows_chain(r0) for r0 in range(0, tq, sub))


def _mla_attn_ctx_kernel(q_ref, k_ref, v_ref, z_ref, lat_ref, o_ref, **kw):
    del lat_ref
    _mla_attn_kernel(q_ref, k_ref, v_ref, z_ref, o_ref, **kw)


def _mla_attend(q, k, v, p):
    B = p.shape[0]
    H = MLA_HEADS
    zc = _P_START['m_z'] // MLA_V
    ctx_blk = SEQ // CTX_LEN

    def call(tq, q_blk0, n_q, key_rows, key_blk, kern, name, into=None):
        specs = [pl.BlockSpec((1, tq, MLA_HW), lambda b, h, i: (b, q_blk0 + i, h)),
                 pl.BlockSpec((1, key_rows, MLA_HW), lambda b, h, i: (b, key_blk, h)),
                 pl.BlockSpec((1, key_rows, MLA_V), lambda b, h, i: (b, key_blk, h)),
                 pl.BlockSpec((1, tq, MLA_V), lambda b, h, i: (b, q_blk0 + i, zc + h))]
        args = (q, k, v, p)
        if into is not None:
            specs.append(pl.BlockSpec(memory_space=pl.ANY))
            args += (into,)
        return pl.pallas_call(
            kern,
            out_shape=jax.ShapeDtypeStruct((B, S_ALL, MLA_W), BF16),
            grid=(B, H, n_q),
            in_specs=specs,
            out_specs=pl.BlockSpec((1, tq, MLA_V), lambda b, h, i: (b, q_blk0 + i, h)),
            input_output_aliases={} if into is None else {len(args) - 1: 0},
            compiler_params=_cparams(3),
            name=name,
        )(*args)

    lat = call(MLA_TQ, 0, SEQ // MLA_TQ, S_ALL, 0,
               functools.partial(_mla_attn_kernel, ctx_start=SEQ, n_lat_chunks=SEQ // MLA_TK), "mla_attend")
    return call(CTX_LEN, ctx_blk, 1, CTX_LEN, ctx_blk,
                functools.partial(_mla_attn_ctx_kernel, ctx_start=0, n_lat_chunks=0), "mla_attend_ctx", into=lat)


SCAN_TILE = 256
SCAN_NT = S_ALL // SCAN_TILE
CTX_TILE = SEQ // SCAN_TILE
HALO = 16


def _split_dot(m_bf16, x):
    hi = x.astype(BF16)
    lo = (x - hi.astype(F32)).astype(BF16)
    return (jnp.dot(m_bf16, hi, preferred_element_type=F32)
            + jnp.dot(m_bf16, lo, preferred_element_type=F32))


def _softplus(t):
    return jnp.maximum(t, 0.0) + jnp.log1p(jnp.exp(-jnp.abs(t)))


def _conv3_silu(xb, prev_row, next_row, w, bias=None):
    n = xb.shape[0]
    i = lax.broadcasted_iota(jnp.int32, (n, 1), 0)
    j = lax.broadcasted_iota(jnp.int32, (1, n), 1)
    one = lambda m: jnp.where(m, 1.0, 0.0).astype(BF16)
    xp = jnp.dot(one(i == j + 1), xb, preferred_element_type=F32)
    xn = jnp.dot(one(i + 1 == j), xb, preferred_element_type=F32)
    y = xp * w[0:1] + xb.astype(F32) * w[1:2] + xn * w[2:3]
    r8 = lax.broadcasted_iota(jnp.int32, (8, 1), 0)
    top = y[0:8] + jnp.where(r8 == 0, prev_row * w[0:1], 0.0)
    bot = y[n - 8:] + jnp.where(r8 == 7, next_row * w[2:3], 0.0)
    y = jnp.concatenate([top, y[8:n - 8], bot], axis=0)
    if bias is not None:
        y = y + bias
    return _silu(y)


def _halo_rows(i, prev_ref, next_ref):
    pv = jnp.where((i == 0) | (i == CTX_TILE), 0.0, 1.0)
    nv = jnp.where((i == CTX_TILE - 1) | (i == SCAN_NT - 1), 0.0, 1.0)
    return prev_ref[0, HALO - 1:HALO, :].astype(F32) * pv, next_ref[0, 0:1, :].astype(F32) * nv


def _halo_specs(width, col_blk):
    rb = SCAN_TILE // HALO
    nblk = S_ALL // HALO
    return [pl.BlockSpec((1, SCAN_TILE, width), lambda b, i: (b, i, col_blk)),
            pl.BlockSpec((1, HALO, width), lambda b, i: (b, jnp.maximum(i * rb - 1, 0), col_blk)),
            pl.BlockSpec((1, HALO, width), lambda b, i: (b, jnp.minimum((i + 1) * rb, nblk - 1), col_blk))]


def _fwd_tile(t):
    return jnp.where(t == 0, CTX_TILE, t - 1)


def _bwd_tile(t):
    return jnp.where(t == 0, CTX_TILE, CTX_TILE - t)


def _chunk_masks(n):
    i = lax.broadcasted_iota(jnp.int32, (n, 1), 0)
    j = lax.broadcasted_iota(jnp.int32, (1, n), 1)
    same = (i // GDN_CHUNK) == (j // GDN_CHUNK)
    return i, j, same


def _gdn_prep_kernel(x_ref, prev_ref, next_ref, s_ref, cw_ref, rate_ref, dtb_ref,
                     q_out, k_out, v_out, g_out):
    i = pl.program_id(1)
    prev_row, next_row = _halo_rows(i, prev_ref, next_ref)
    y = _conv3_silu(x_ref[0], prev_row, next_row, cw_ref[...])
    for h in range(GDN_HEADS):
        sl = slice(h * GDN_DK, (h + 1) * GDN_DK)
        qh = y[:, sl]
        q_out[0, :, sl] = (qh * lax.rsqrt(jnp.sum(qh * qh, axis=-1, keepdims=True) + EPS)
                           * (GDN_DK ** -0.5)).astype(BF16)
        kh = y[:, GDN_W + h * GDN_DK:GDN_W + (h + 1) * GDN_DK]
        k_out[0, :, sl] = (kh * lax.rsqrt(jnp.sum(kh * kh, axis=-1, keepdims=True) + EPS)).astype(BF16)
    v_out[0] = y[:, 2 * GDN_W:].astype(BF16)

    s = s_ref[0]
    lane = lax.broadcasted_iota(jnp.int32, (1, LANES), 1)
    nh2 = 2 * GDN_HEADS
    beta = jax.nn.sigmoid(s)
    g = -rate_ref[...] * _softplus(s + dtb_ref[...])
    g = jnp.where((lane >= nh2) & (lane < 2 * nh2), g, 0.0)
    ii, jj, same = _chunk_masks(SCAN_TILE)
    one = lambda m: jnp.where(m, 1.0, 0.0).astype(BF16)
    fwd_lane = lane < nh2 + GDN_HEADS
    gam = jnp.where(fwd_lane, _split_dot(one(same & (jj <= ii)), g), _split_dot(one(same & (jj >= ii)), g))
    rem = jnp.where(fwd_lane, _split_dot(one(same & (jj > ii)), g), _split_dot(one(same & (jj < ii)), g))
    cf = jnp.where(lane < nh2, beta, jnp.where(lane < 2 * nh2, gam, pltpu.roll(rem, nh2, axis=1)))
    tr = cf.T
    for h in range(GDN_HEADS):
        for r, src in enumerate((h, GDN_HEADS + h, nh2 + h, nh2 + GDN_HEADS + h,
                                 2 * nh2 + h, 2 * nh2 + GDN_HEADS + h)):
            g_out[0, h, r:r + 1, :] = tr[src:src + 1, :]
        g_out[0, h, 6:8, :] = jnp.zeros((2, SCAN_TILE), F32)


def _gdn_prep(p, ps, conv_w, A_log, dt_bias):
    B = p.shape[0]
    W3 = 3 * GDN_W
    nh2 = 2 * GDN_HEADS
    rate = jnp.zeros((1, LANES), F32).at[0, nh2:2 * nh2].set(jnp.exp(A_log).reshape(-1))
    dtb = jnp.zeros((1, LANES), F32).at[0, nh2:2 * nh2].set(dt_bias.reshape(-1))
    shp = jax.ShapeDtypeStruct((B, S_ALL, GDN_W), BF16)
    ospec = pl.BlockSpec((1, SCAN_TILE, GDN_W), lambda b, i: (b, i, 0))
    const = lambda shape: pl.BlockSpec(shape, lambda b, i: (0, 0))
    return pl.pallas_call(
        _gdn_prep_kernel,
        out_shape=(shp, shp, shp, jax.ShapeDtypeStruct((B, GDN_HEADS, 8, S_ALL), F32)),
        grid=(B, SCAN_NT),
        in_specs=_halo_specs(W3, _P_START['g_qkv'] // W3)
        + [pl.BlockSpec((1, SCAN_TILE, LANES), lambda b, i: (b, i, 0)),
           const((SHORT_CONV, W3)), const((1, LANES)), const((1, LANES))],
        out_specs=(ospec, ospec, ospec,
                   pl.BlockSpec((1, GDN_HEADS, 8, SCAN_TILE), lambda b, i: (b, 0, 0, i))),
        compiler_params=_cparams(2),
        name="gdn_prep",
    )(p, p, p, ps, conv_w, rate, dtb)


def _gdn_dir(q, k, v, gr, s_ref, o_ref, d):
    n = SCAN_TILE
    cf = jnp.concatenate([gr, jnp.zeros((LANES - 8, n), F32)], axis=0).T
    beta, gam_c = cf[:, d:d + 1], cf[:, 2 + d:3 + d]
    ecf = jnp.exp(cf)
    e_gam, e_rem = ecf[:, 2 + d:3 + d], ecf[:, 4 + d:5 + d]
    gam_r = gr[2 + d:3 + d, :]
    ii, jj, same = _chunk_masks(n)
    incl = same & ((jj <= ii) if d == 0 else (jj >= ii))
    strict = same & ((jj < ii) if d == 0 else (jj > ii))
    kk = _dot_nt(k, k)
    yield
    qk = _dot_nt(q, k)
    yield
    dec = jnp.exp(jnp.where(incl, gam_c - gam_r, NEG_INF))
    a = jnp.where(strict, beta * kk * dec, 0.0)
    qkd = (qk * dec).astype(BF16)
    kf = k.astype(F32)
    x = jnp.concatenate([v.astype(F32) * beta, kf * (beta * e_gam)], axis=1)
    pb = a.astype(BF16)
    x = x - jnp.dot(pb, x.astype(BF16), preferred_element_type=F32)
    yield
    for _ in range(5):
        pb = jnp.dot(pb, pb, preferred_element_type=F32).astype(BF16)
        yield
        x = x + jnp.dot(pb, x.astype(BF16), preferred_element_type=F32)
        yield
    u, w =x[:, :GDN_DV], x[:, GDN_DV:].astype(BF16)
    qd = (q.astype(F32) * e_gam).astype(BF16)
    kd = (kf * e_rem).astype(BF16)
    s = s_ref[...]
    nchunk = n // GDN_CHUNK
    v_new = [None] * nchunk
    qs = [None] * nchunk
    for c in (range(nchunk) if d == 0 else reversed(range(nchunk))):
        rows = slice(c * GDN_CHUNK, (c + 1) * GDN_CHUNK)
        r1 = jnp.dot(jnp.concatenate([w[rows], qd[rows]], axis=0), s.astype(BF16), preferred_element_type=F32)
        yield
        vn = u[rows] - r1[:GDN_CHUNK]
        qs[c] = r1[GDN_CHUNK:]
        v_new[c] = vn
        last = (c + 1) * GDN_CHUNK - 1 if d == 0 else c * GDN_CHUNK
        s = s * ecf[last:last + 1, 2 + d:3 + d] + _dot_tn(kd[rows], vn.astype(BF16))
        yield
    s_ref[...] = s
    vn_all = jnp.concatenate(v_new, axis=0).astype(BF16)
    o_ref[0] = (jnp.concatenate(qs, axis=0) + jnp.dot(qkd, vn_all, preferred_element_type=F32)).astype(BF16)


def _round_robin(gens):
    gens = list(gens)
    while gens:
        alive = []
        for g in gens:
            try:
                next(g)
                alive.append(g)
            except StopIteration:
                pass
        gens = alive


def _gdn_scan_kernel(qf, kf, vf, gf, qb, kb, vb, gb, of_ref, ob_ref, sf_ref, sb_ref):
    @pl.when(pl.program_id(1) == 0)
    def _():
        sf_ref[...] = jnp.zeros_like(sf_ref)
        sb_ref[...] = jnp.zeros_like(sb_ref)

    chains = []
    for h in range(GDN_HEADS):
        sl = slice(h * GDN_DK, (h + 1) * GDN_DK)
        chains.append(_gdn_dir(qf[0, :, sl], kf[0, :, sl], vf[0, :, sl], gf[0, h],
                               sf_ref.at[h], of_ref.at[:, :, sl], 0))
        chains.append(_gdn_dir(qb[0, :, sl], kb[0, :, sl], vb[0, :, sl], gb[0, h],
                               sb_ref.at[h], ob_ref.at[:, :, sl], 1))
    _round_robin(chains)


def _gdn_scan(q, k, v, g):
    B = q.shape[0]
    tok = lambda order: pl.BlockSpec((1, SCAN_TILE, GDN_W), lambda b, t: (b, order(t), 0))
    gsp = lambda order: pl.BlockSpec((1, GDN_HEADS, 8, SCAN_TILE), lambda b, t: (b, 0, 0, order(t)))
    shp = jax.ShapeDtypeStruct((B, S_ALL, GDN_W), BF16)
    f, r = _fwd_tile, _bwd_tile
    state = pltpu.VMEM((GDN_HEADS, GDN_DK, GDN_DV), F32)
    return pl.pallas_call(
        _gdn_scan_kernel,
        out_shape=(shp, shp),
        grid=(B, SCAN_NT),
        in_specs=[tok(f), tok(f), tok(f), gsp(f), tok(r), tok(r), tok(r), gsp(r)],
        out_specs=(tok(f), tok(r)),
        scratch_shapes=[state, state],
        compiler_params=_cparams(2),
        name="gdn_scan",
    )(q, k, v, g, q, k, v, g)


SSD_BC = SSM_GROUPS * SSM_STATE
SSD_HD = 2 * SSM_HEADS
SSD_GW = (SSM_HEADS // SSM_GROUPS) * SSM_HEADDIM


def _ssd_prep_kernel(x_ref, prev_ref, next_ref, s_ref, cw_ref, cb_ref, a_ref, dtb_ref,
                     xs_out, b_out, c_out, bt_out, cf_out, cr_out):
    i = pl.program_id(1)
    prev_row, next_row = _halo_rows(i, prev_ref, next_ref)
    y = _conv3_silu(x_ref[0], prev_row, next_row, cw_ref[...], cb_ref[...])
    xs_out[0] = y[:, :SSM_W]
    b_out[0] = y[:, SSM_W:SSM_W + SSD_BC].astype(BF16)
    c_out[0] = y[:, SSM_W + SSD_BC:].astype(BF16)
    bt_out[0] = y[:, SSM_W:SSM_W + SSD_BC].T.astype(BF16)

    s = s_ref[0]
    lane = lax.broadcasted_iota(jnp.int32, (1, LANES), 1)
    dt = _softplus(s + dtb_ref[...])
    on = (lane >= SSD_HD) & (lane < 2 * SSD_HD)
    a = jnp.where(on, dt * a_ref[...], 0.0)
    n = SCAN_TILE
    ii = lax.broadcasted_iota(jnp.int32, (n, 1), 0)
    jj = lax.broadcasted_iota(jnp.int32, (1, n), 1)
    one = lambda m: jnp.where(m, 1.0, 0.0).astype(BF16)
    fwd_lane = lane < SSD_HD + SSM_HEADS
    cum = jnp.where(fwd_lane, _split_dot(one(jj <= ii), a), _split_dot(one(jj >= ii), a))
    rem = jnp.where(fwd_lane, _split_dot(one(jj > ii), a), _split_dot(one(jj < ii), a))
    cf = jnp.where(lane < SSD_HD, pltpu.roll(dt, LANES - SSD_HD, axis=1),
                   jnp.where(lane < 2 * SSD_HD, cum, pltpu.roll(rem, SSD_HD, axis=1)))
    cf_out[0] = cf
    cr_out[0] = cf.T[SSD_HD:2 * SSD_HD, :]


def _ssd_prep(p, ps, conv_w, conv_b, A_log, dt_bias):
    B = p.shape[0]
    W = SSM_CONV_DIM
    a_vec = jnp.zeros((1, LANES), F32).at[0, SSD_HD:2 * SSD_HD].set(-jnp.exp(A_log).reshape(-1))
    dtb = jnp.zeros((1, LANES), F32).at[0, SSD_HD:2 * SSD_HD].set(dt_bias.reshape(-1))
    const = lambda shape: pl.BlockSpec(shape, lambda b, i: (0, 0))
    tok = lambda w: pl.BlockSpec((1, SCAN_TILE, w), lambda b, i: (b, i, 0))
    return pl.pallas_call(
        _ssd_prep_kernel,
        out_shape=(jax.ShapeDtypeStruct((B, S_ALL, SSM_W), F32),
                   jax.ShapeDtypeStruct((B, S_ALL, SSD_BC), BF16),
                   jax.ShapeDtypeStruct((B, S_ALL, SSD_BC), BF16),
                   jax.ShapeDtypeStruct((B, SSD_BC, S_ALL), BF16),
                   jax.ShapeDtypeStruct((B, S_ALL, LANES), F32),
                   jax.ShapeDtypeStruct((B, SSD_HD, S_ALL), F32)),
        grid=(B, SCAN_NT),
        in_specs=_halo_specs(W, _P_START['s_xbc'] // W)
        + [pl.BlockSpec((1, SCAN_TILE, LANES), lambda b, i: (b, i, 0)),
           const((SHORT_CONV, W)), const((1, W)), const((1, LANES)), const((1, LANES))],
        out_specs=(tok(SSM_W), tok(SSD_BC), tok(SSD_BC),
                   pl.BlockSpec((1, SSD_BC, SCAN_TILE), lambda b, i: (b, 0, i)),
                   tok(LANES),
                   pl.BlockSpec((1, SSD_HD, SCAN_TILE), lambda b, i: (b, 0, i))),
        compiler_params=_cparams(2),
        name="ssd_prep",
    )(p, p, p, ps, conv_w, conv_b.reshape(1, W), a_vec, dtb)


def _ssd_dir(x_ref, b_ref, c_ref, bt_ref, cf_ref, cr_ref, h_ref, y_ref, d, g):
    n = SCAN_TILE
    hpg = SSM_HEADS // SSM_GROUPS
    hd0 = d * SSM_HEADS + g * hpg
    gs = slice(g * SSM_STATE, (g + 1) * SSM_STATE)
    xl = slice(g * SSD_GW, (g + 1) * SSD_GW)
    cf = cf_ref[0]
    cr = cr_ref[0]
    cm = c_ref[0, :, gs]
    log_lanes = lax.broadcasted_iota(jnp.int32, (1, LANES), 1) >= SSD_HD
    ecf = jnp.exp(jnp.where(log_lanes, cf, 0.0))
    col = lambda base, h: cf[:, base + hd0 + h:base + hd0 + h + 1]
    last = n - 1 if d == 0 else 0
    src = jnp.where(log_lanes, ecf, cf)
    r = lax.broadcasted_iota(jnp.int32, (LANES, 1), 0)
    c = lax.broadcasted_iota(jnp.int32, (1, 3 * SSD_GW), 1)
    want = (c // SSD_GW) * SSD_HD + hd0 + (c % SSD_GW) // SSM_HEADDIM
    sel = jnp.where(r == want, 1.0, 0.0).astype(BF16)
    hi = src.astype(BF16)
    lo_part = (src - hi.astype(F32)).astype(BF16)
    spread = (jnp.dot(hi, sel, preferred_element_type=F32) + jnp.dot(lo_part, sel, preferred_element_type=F32))
    yield
    dt_x, ecum_x, erem_x = spread[:, :SSD_GW], spread[:, SSD_GW:2 * SSD_GW], spread[:, 2 * SSD_GW:]
    xdt = x_ref[0, :, xl] * dt_x
    xdt_b = xdt.astype(BF16)
    xdec = (xdt * erem_x).astype(BF16)
    cb = _dot_nt(cm, b_ref[0, :, gs])
    yield
    h_prev = h_ref[g]
    y_off = jnp.dot(cm, h_prev.astype(BF16), preferred_element_type=F32)
    yield
    y_off = y_off * ecum_x
    h_ref[g] = h_prev * ecum_x[last:last + 1, :] + jnp.dot(bt_ref[0, gs, :], xdec, preferred_element_type=F32)
    yield
    ii = lax.broadcasted_iota(jnp.int32, (n, 1), 0)
    jj = lax.broadcasted_iota(jnp.int32, (1, n), 1)
    causal = (jj <= ii) if d == 0 else (jj >= ii)
    lo = lax.broadcasted_iota(jnp.int32, (1, LANES), 1) < SSM_HEADDIM
    pair_out = []
    for j in range(hpg // 2):
        ys = []
        for e in range(2):
            h = 2 * j + e
            seg = col(SSD_HD, h) - cr[hd0 + h:hd0 + h + 1, :]
            sc = (cb * jnp.exp(jnp.where(causal, seg, NEG_INF))).astype(BF16)
            ys.append(jnp.dot(sc, xdt_b[:, j * LANES:(j + 1) * LANES], preferred_element_type=F32))
            yield
        pair_out.append(jnp.where(lo, ys[0], ys[1]))
    y_ref[0, :, xl] = (jnp.concatenate(pair_out, axis=1) + y_off).astype(BF16)


def _ssd_scan_kernel(xf, bf, cf_, btf, colf, rowf, xb, bb, cb_, btb, colb, rowb, yf_ref, yb_ref, hf_ref, hb_ref):
    @pl.when(pl.program_id(1) == 0)
    def _():
        hf_ref[...] = jnp.zeros_like(hf_ref)
        hb_ref[...] = jnp.zeros_like(hb_ref)

    chains = []
    for g in range(SSM_GROUPS):
        chains.append(_ssd_dir(xf, bf, cf_, btf, colf, rowf, hf_ref, yf_ref, 0, g))
        chains.append(_ssd_dir(xb, bb, cb_, btb, colb, rowb, hb_ref, yb_ref, 1, g))
    _round_robin(chains)


def _ssd_scan(xs, bm, cm, bt, cf, cr):
    B = xs.shape[0]

    def specs(order):
        tok = lambda w: pl.BlockSpec((1, SCAN_TILE, w), lambda b, t: (b, order(t), 0))
        return [tok(SSM_W), tok(SSD_BC), tok(SSD_BC),
                pl.BlockSpec((1, SSD_BC, SCAN_TILE), lambda b, t: (b, 0, order(t))),
                tok(LANES),
                pl.BlockSpec((1, SSD_HD, SCAN_TILE), lambda b, t: (b, 0, order(t)))]

    shp = jax.ShapeDtypeStruct((B, S_ALL, SSM_W), BF16)
    out = lambda order: pl.BlockSpec((1, SCAN_TILE, SSM_W), lambda b, t: (b, order(t), 0))
    hshape = pltpu.VMEM((SSM_GROUPS, SSM_STATE, SSD_GW), F32)
    args = (xs, bm, cm, bt, cf, cr)
    return pl.pallas_call(
        _ssd_scan_kernel,
        out_shape=(shp, shp),
        grid=(B, SCAN_NT),
        in_specs=specs(_fwd_tile) + specs(_bwd_tile),
        out_specs=(out(_fwd_tile), out(_bwd_tile)),
        scratch_shapes=[hshape, hshape],
        compiler_params=_cparams(2),
        name="ssd_scan",
    )(*args, *args)


def _repack_w_in(w):
    cut = lambda names: [w[:, _IN_START[n]:_IN_START[n] + _IN_SIZE[n]] for n in names]
    zeros = lambda n: jnp.zeros((w.shape[0], n), w.dtype)
    n_small = sum(_IN_SIZE[n] for n in _S_ORDER)
    cols = cut(_P_ORDER) + [zeros(D_INP - LANES - _off)] + cut(_S_ORDER) + [zeros(LANES - n_small)]
    return jnp.concatenate(cols, axis=1).astype(BF16)


def kernel(x, c, ctx, c_ctx, norm_w, ada_w, ada_b, w_in, gdn_conv_w, gdn_A_log, gdn_dt_bias, gdn_norm_w, na_q_norm, na_k_norm, na_rpb, mla_qa_norm, mla_w_uq, mla_kva_norm, mla_w_ukv, mla_q_norm, mla_k_norm, ssm_conv_w, ssm_conv_b, ssm_A_log, ssm_dt_bias, ssm_D, ssm_norm_w, w_out):
    B = x.shape[0]
    xs = jnp.concatenate([x, ctx], axis=1)
    c8 = jnp.zeros((8, D_MODEL), F32).at[:B].set(c).at[B].set(c_ctx)
    mods = _ada_all(c8, ada_w, ada_b)
    cos_np, sin_np = _rope_tables()
    cos, sin = jnp.asarray(cos_np), jnp.asarray(sin_np)
    na_bias = _na_bias(na_rpb)
    for l in range(DEPTH):
        shift, scale, gate = jnp.split(mods[l, :B], 3, axis=-1)
        shift_c, scale_c, gate_c = jnp.split(mods[l, B], 3, axis=-1)
        bc = lambda v: jnp.broadcast_to(v[None], (B, D_MODEL))
        mod4 = jnp.stack([shift, scale, bc(shift_c), bc(scale_c)], axis=1)
        gate2 = jnp.stack([gate, bc(gate_c)], axis=1)
        p, ps = _inproj(xs, norm_w[l], mod4, _repack_w_in(w_in[l]))

        gq, gk, gv, gg = _gdn_prep(p, ps, gdn_conv_w[l], gdn_A_log[l], gdn_dt_bias[l])
        o_f, o_b = _gdn_scan(gq, gk, gv, gg)

        ob = _na_attend(p, na_q_norm[l], na_k_norm[l], na_bias[l])

        mq, mk, mv = _mla_prep(p, cos, sin, mla_qa_norm[l], mla_w_uq[l], mla_kva_norm[l], mla_w_ukv[l],
                               mla_q_norm[l], mla_k_norm[l])
        oc = _mla_attend(mq, mk, mv, p)

        sx, sb, sc, sbt, scf, scr = _ssd_prep(p, ps, ssm_conv_w[l], ssm_conv_b[l], ssm_A_log[l], ssm_dt_bias[l])
        y_f, y_b = _ssd_scan(sx, sb, sc, sbt, scf, scr)
        xs = _outproj((o_f, o_b, gdn_norm_w[l]), ob, oc, (y_f, y_b, sx, ssm_D[l], ssm_norm_w[l]), p,
                      w_out[l].astype(BF16), xs, gate2, last=(l == DEPTH - 1))
    return xs
```

```python
import functools

import jax
import jax.numpy as jnp
import numpy as np
from jax import lax
from jax.experimental import pallas as pl
from jax.experimental.pallas import tpu as pltpu

F32 = jnp.float32
BF16 = jnp.bfloat16

D_MODEL = 2048
BATCH = 4
SEQ = 4096
DEPTH = 4
GRID_W = 64
GRID_H = SEQ // GRID_W
CTX_LEN = 256
S_ALL = SEQ + CTX_LEN
EPS = 1e-6
NEG_INF = -1e30
LOG2E = 1.4426950408889634

D_BRANCH = 512
D_MIX = 4 * D_BRANCH
SHORT_CONV = 3

GDN_HEADS = 4
GDN_DK = 128
GDN_DV = 128
GDN_W = GDN_HEADS * GDN_DV
GDN_CHUNK = 64

NA_HEADS = 4
NA_DH = 128
NA_W = NA_HEADS * NA_DH
NA_WIN_R = 8
NA_WIN_C = 16

MLA_HEADS = 4
MLA_Q_RANK = 384
MLA_KV_RANK = 256
MLA_NOPE = 128
MLA_ROPE = 64
MLA_QK = MLA_NOPE + MLA_ROPE
MLA_V = 128
MLA_W = MLA_HEADS * MLA_V
ROPE_THETA = 10000.0

SSM_HEADDIM = 64
SSM_HEADS = D_BRANCH // SSM_HEADDIM
SSM_W = SSM_HEADS * SSM_HEADDIM
SSM_GROUPS = 2
SSM_STATE = 128
SSM_CONV_DIM = SSM_W + 2 * SSM_GROUPS * SSM_STATE

IN_SIZES = (3 * GDN_W, GDN_W, 2 * GDN_HEADS, 2 * GDN_HEADS,
            3 * NA_W, NA_W,
            MLA_Q_RANK, MLA_KV_RANK, MLA_ROPE, MLA_W,
            SSM_W, SSM_CONV_DIM, 2 * SSM_HEADS)
D_IN = sum(IN_SIZES)
_IN_NAMES = ('g_qkv', 'g_z', 'g_beta', 'g_alpha', 'n_qkv', 'n_z',
             'm_q', 'm_kv', 'm_kr', 'm_z', 's_z', 's_xbc', 's_dt')
_IN_START = dict(zip(_IN_NAMES, np.cumsum((0,) + IN_SIZES[:-1]).tolist()))
_IN_SIZE = dict(zip(_IN_NAMES, IN_SIZES))

LANES = 128
_P_ORDER = ('g_qkv', 'g_z', 'n_qkv', 'n_z', 'm_z', 's_z', 's_xbc',
            'm_q', 'm_kv', 'm_kr', 'm_kr')
_S_ORDER = ('g_beta', 'g_alpha', 's_dt')
_P_START = {}
_off = 0
for _n in _P_ORDER:
    _P_START.setdefault(_n, _off)
    _off += _IN_SIZE[_n]
MXU_N = 256
D_INP = -(-_off // (2 * MXU_N)) * (2 * MXU_N)
P_MLA_BLK = MLA_Q_RANK + MLA_KV_RANK + 2 * MLA_ROPE
assert _P_START['m_q'] % P_MLA_BLK == 0 and D_INP % LANES == 0

VMEM_LIMIT = 52 * 1024 * 1024


def _silu(x):
    h = 0.5 * x
    return h + h * jnp.tanh(h)


def _dot_nt(a, b):
    return lax.dot_general(a, b, (((1,), (1,)), ((), ())), preferred_element_type=F32)


def _dot_tn(a, b):
    return lax.dot_general(a, b, (((0,), (0,)), ((), ())), preferred_element_type=F32)


def _cparams(n_axes):
    return pltpu.CompilerParams(dimension_semantics=("arbitrary",) * n_axes,
                                vmem_limit_bytes=VMEM_LIMIT)


def _ada_kernel(c_ref, w_ref, b_ref, o_ref):
    a = _silu(c_ref[...]).astype(BF16)
    o_ref[0] = jnp.dot(a, w_ref[0].astype(BF16), preferred_element_type=F32) + b_ref[0]


def _ada_all(c8, ada_w, ada_b):
    tn = 1536
    L = ada_w.shape[0]
    n3 = ada_w.shape[2]
    return pl.pallas_call(
        _ada_kernel,
        out_shape=jax.ShapeDtypeStruct((L, 8, n3), F32),
        grid=(L, n3 // tn),
        in_specs=[pl.BlockSpec((8, D_MODEL), lambda l, j: (0, 0)),
                  pl.BlockSpec((1, D_MODEL, tn), lambda l, j: (l, 0, j)),
                  pl.BlockSpec((1, 1, tn), lambda l, j: (l, 0, j))],
        out_specs=pl.BlockSpec((1, 8, tn), lambda l, j: (l, 0, j)),
        compiler_params=_cparams(2),
        name="ada_mod",
    )(c8, ada_w, ada_b.reshape(L, 1, n3))


IN_TM = 1088
IN_TN = 1024
IN_RC = 16
IN_SECTIONS = 4


def _inproj_kernel(x_ref, nw_ref, mod_ref, w_ref, o_ref, os_ref, h_scr):
    i = pl.program_id(1)
    j = pl.program_id(2)

    @pl.when(j == 0)
    def _():
        m = mod_ref[0]
        nw = nw_ref[...]
        gain_l = nw * (1.0 + m[1:2])
        gain_c = nw * (1.0 + m[3:4])
        sec = IN_TM // IN_SECTIONS
        for c in range(IN_SECTIONS):
            for r0 in range(c * sec, (c + 1) * sec, IN_RC):
                x = x_ref[0, r0:r0 + IN_RC, :]
                ms = jnp.mean(x * x, axis=-1, keepdims=True)
                is_ctx = i * IN_TM + r0 >= SEQ
                gain = jnp.where(is_ctx, gain_c, gain_l)
                shift = jnp.where(is_ctx, m[2:3], m[0:1])
                h_scr[r0:r0 + IN_RC, :] = (x * lax.rsqrt(ms + EPS) * gain + shift).astype(BF16)
            rows = slice(c * sec, (c + 1) * sec)
            o_ref[0, rows, :] = jnp.dot(h_scr[rows, :], w_ref[...], preferred_element_type=F32).astype(BF16)

    @pl.when(j > 0)
    def _():
        y = jnp.dot(h_scr[...], w_ref[...], preferred_element_type=F32)
        o_ref[0] = y.astype(BF16)

        @pl.when(j == pl.num_programs(2) - 1)
        def _():
            os_ref[0] = y[:, IN_TN - LANES:]


def _inproj(xs, norm_w, mod4, w_main):
    B = xs.shape[0]
    return pl.pallas_call(
        _inproj_kernel,
        out_shape=(jax.ShapeDtypeStruct((B, S_ALL, D_INP), BF16),
                   jax.ShapeDtypeStruct((B, S_ALL, LANES), F32)),
        grid=(B, S_ALL // IN_TM, D_INP // IN_TN),
        in_specs=[pl.BlockSpec((1, IN_TM, D_MODEL), lambda b, i, j: (b, i, 0)),
                  pl.BlockSpec((1, D_MODEL), lambda b, i, j: (0, 0)),
                  pl.BlockSpec((1, 4, D_MODEL), lambda b, i, j: (b, 0, 0)),
                  pl.BlockSpec((D_MODEL, IN_TN), lambda b, i, j: (0, j))],
        out_specs=(pl.BlockSpec((1, IN_TM, IN_TN), lambda b, i, j: (b, i, j)),
                   pl.BlockSpec((1, IN_TM, LANES), lambda b, i, j: (b, i, 0))),
        scratch_shapes=[pltpu.VMEM((IN_TM, D_MODEL), BF16)],
        compiler_params=_cparams(3),
        name="inproj",
    )(xs, norm_w.reshape(1, D_MODEL), mod4, w_main)


OUT_TM = 544
OUT_TM_LAST = 512
OUT_SECTIONS = 2


def _gdn_gate(o_f, o_b, z, nw):
    o = o_f.astype(F32) + o_b.astype(F32)
    z = z.astype(F32)
    outs = []
    for h in range(GDN_HEADS):
        sl = slice(h * GDN_DV, (h + 1) * GDN_DV)
        oh = o[:, sl]
        y = oh * lax.rsqrt(jnp.mean(oh * oh, axis=-1, keepdims=True) + EPS) * nw
        outs.append((y * _silu(z[:, sl])).astype(BF16))
    return jnp.concatenate(outs, axis=-1)


def _ssd_gate(y_f, y_b, xs, z, d_skip, nw):
    y = y_f.astype(F32) + y_b.astype(F32) + d_skip * xs
    y = y * _silu(z.astype(F32))
    return (y * lax.rsqrt(jnp.mean(y * y, axis=-1, keepdims=True) + EPS) * nw).astype(BF16)


def _outproj_kernel(of_ref, ob_ref, gz_ref, gnw_ref, na_ref, mla_ref, yf_ref, yb_ref, sx_ref, sz_ref,
                    dsk_ref, snw_ref, w_ref, x_ref, g_ref, o_ref, *, tm):
    i = pl.program_id(1)
    g = g_ref[0]
    sec = tm // OUT_SECTIONS
    for c in range(OUT_SECTIONS):
        rows = slice(c * sec, (c + 1) * sec)
        branches = (_gdn_gate(of_ref[0, rows, :], ob_ref[0, rows, :], gz_ref[0, rows, :], gnw_ref[...]),
                    na_ref[0, rows, :], mla_ref[0, rows, :],
                    _ssd_gate(yf_ref[0, rows, :], yb_ref[0, rows, :], sx_ref[0, rows, :], sz_ref[0, rows, :],
                              dsk_ref[...], snw_ref[...]))
        y = None
        for n, a in enumerate(branches):
            t = jnp.dot(a, w_ref[n * D_BRANCH:(n + 1) * D_BRANCH, :], preferred_element_type=F32)
            y = t if y is None else y + t
        row = i * tm + c * sec + lax.broadcasted_iota(jnp.int32, (sec, 1), 0)
        gate = jnp.where(row >= SEQ, g[1:2], g[0:1])
        o_ref[0, rows, :] = x_ref[0, rows, :] + gate * y


def _outproj(gdn, na, mla, ssd, p, w_out_b, xs, gate2, last):
    B = xs.shape[0]
    o_f, o_b, g_nw = gdn
    y_f, y_b, s_x, d_skip, s_nw = ssd
    tm, rows = (OUT_TM_LAST, SEQ) if last else (OUT_TM, S_ALL)
    a_spec = pl.BlockSpec((1, tm, D_BRANCH), lambda b, i: (b, i, 0))
    z_spec = lambda name: pl.BlockSpec((1, tm, D_BRANCH), lambda b, i: (b, i, _P_START[name] // D_BRANCH))
    x_spec = pl.BlockSpec((1, tm, D_MODEL), lambda b, i: (b, i, 0))
    vec = lambda n: pl.BlockSpec((1, n), lambda b, i: (0, 0))
    return pl.pallas_call(
        functools.partial(_outproj_kernel, tm=tm),
        out_shape=jax.ShapeDtypeStruct((B, rows, D_MODEL), F32),
        grid=(B, rows // tm),
        in_specs=[a_spec, a_spec, z_spec('g_z'), vec(GDN_DV), a_spec, a_spec,
                  a_spec, a_spec, a_spec, z_spec('s_z'), vec(SSM_W), vec(SSM_W),
                  pl.BlockSpec((D_MIX, D_MODEL), lambda b, i: (0, 0)),
                  x_spec,
                  pl.BlockSpec((1, 2, D_MODEL), lambda b, i: (b, 0, 0))],
        out_specs=x_spec,
        compiler_params=_cparams(2),
        name="outproj",
    )(o_f, o_b, p, g_nw.reshape(1, GDN_DV), na, mla,
      y_f, y_b, s_x, p, jnp.repeat(d_skip, SSM_HEADDIM).reshape(1, SSM_W), s_nw.reshape(1, SSM_W),
      w_out_b, xs, gate2)


NA_RB = 4
NA_QB = NA_RB * GRID_W
NA_KR = 12
NA_KW = NA_KR * GRID_W
NA_NBLK = GRID_H // NA_RB
assert NA_QB == CTX_LEN


def _na_headnorm(x, w, extra):
    x = x.astype(F32)
    outs = []
    for h in range(NA_HEADS):
        xh = x[:, h * NA_DH:(h + 1) * NA_DH]
        ms = jnp.mean(xh * xh, axis=-1, keepdims=True)
        outs.append((xh * lax.rsqrt(ms + EPS) * w * extra).astype(BF16))
    return jnp.concatenate(outs, axis=-1)


def _na_fill_bias(toe_ref, bias_scr, cls):
    dr, ok, _ = _na_bias_index()
    dead = jnp.full((GRID_W, GRID_W), NEG_INF, F32)
    for h in range(NA_HEADS):
        for a in range(NA_RB):
            for b in range(0, NA_KR, 2):
                pair = [toe_ref[h, int(dr[cls, a, b + e])] if ok[cls, a, b + e] else dead for e in range(2)]
                bias_scr[h, a * GRID_W:(a + 1) * GRID_W, b * GRID_W:(b + 2) * GRID_W] = jnp.concatenate(pair, axis=1)


def _na_kernel(q_ref, kraw_ref, v_ref, z_ref, toe_ref, qn_ref, kn_ref, o_ref, bias_scr, k_ref):
    rb = pl.program_id(1)

    @pl.when(rb == 0)
    def _():
        def body(t, carry):
            r0 = pl.multiple_of(t * NA_QB, NA_QB)
            k_ref[pl.ds(r0, NA_QB), :] = _na_headnorm(kraw_ref[0, pl.ds(r0, NA_QB), :], kn_ref[...], 1.0)
            return carry

        lax.fori_loop(0, S_ALL // NA_QB, body, 0)

    q = _na_headnorm(q_ref[0], qn_ref[...], NA_DH ** -0.5 * LOG2E)
    z = z_ref[0].astype(F32)
    kc = k_ref[SEQ:S_ALL, :]
    vc = v_ref[0, SEQ:S_ALL, :]

    for cls, first_rb in enumerate((0, 1, NA_NBLK - 1)):
        @pl.when(rb == first_rb)
        def _(cls=cls):
            _na_fill_bias(toe_ref, bias_scr, cls)

    def finish(h, o, l):
        sl = slice(h * NA_DH, (h + 1) * NA_DH)
        o_ref[0, :, sl] = (o / l * _silu(z[:, sl])).astype(BF16)

    @pl.when(rb < NA_NBLK)
    def _latent():
        base = jnp.clip(rb * NA_RB - NA_RB, 0, GRID_H - NA_KR)
        start = pl.multiple_of(base * GRID_W, GRID_W)
        kw = k_ref[pl.ds(start, NA_KW), :]
        vw = v_ref[0, pl.ds(start, NA_KW), :]
        def head(h):
            sl = slice(h * NA_DH, (h + 1) * NA_DH)
            s_w = _dot_nt(q[:, sl], kw[:, sl])
            yield
            s_c = _dot_nt(q[:, sl], kc[:, sl])
            yield
            s_w = s_w + bias_scr[h]
            m = jnp.maximum(jnp.max(s_w, axis=-1, keepdims=True), jnp.max(s_c, axis=-1, keepdims=True))
            p_w = jnp.exp2(s_w - m)
            p_c = jnp.exp2(s_c - m)
            l = jnp.sum(p_w, axis=-1, keepdims=True) + jnp.sum(p_c, axis=-1, keepdims=True)
            o = jnp.dot(p_w.astype(BF16), vw[:, sl], preferred_element_type=F32)
            yield
            o = o + jnp.dot(p_c.astype(BF16), vc[:, sl], preferred_element_type=F32)
            yield
            finish(h, o, l)

        _round_robin(head(h) for h in range(NA_HEADS))

    @pl.when(rb == NA_NBLK)
    def _context():
        for h in range(NA_HEADS):
            sl = slice(h * NA_DH, (h + 1) * NA_DH)
            s_c = _dot_nt(q[:, sl], kc[:, sl])
            m = jnp.max(s_c, axis=-1, keepdims=True)
            p_c = jnp.exp2(s_c - m)
            l = jnp.sum(p_c, axis=-1, keepdims=True)
            o = jnp.dot(p_c.astype(BF16), vc[:, sl], preferred_element_type=F32)
            finish(h, o, l)


def _na_bias_index():
    dr = np.zeros((3, NA_RB, NA_KR), np.int64)
    ok = np.zeros((3, NA_RB, NA_KR), bool)
    for ci, rb in enumerate((0, 1, NA_NBLK - 1)):
        base = int(np.clip(rb * NA_RB - NA_RB, 0, GRID_H - NA_KR))
        qr = rb * NA_RB + np.arange(NA_RB)[:, None]
        kr = base + np.arange(NA_KR)[None, :]
        row0 = np.clip(qr - NA_WIN_R // 2, 0, GRID_H - NA_WIN_R)
        ok[ci] = (kr >= row0) & (kr < row0 + NA_WIN_R)
        dr[ci] = np.clip(kr - qr + NA_WIN_R - 1, 0, 2 * NA_WIN_R - 2)
    qc = np.arange(GRID_W)[:, None]
    kc = np.arange(GRID_W)[None, :]
    win0 = np.clip(qc - NA_WIN_C // 2, 0, GRID_W - NA_WIN_C)
    col_ok = (kc >= win0) & (kc < win0 + NA_WIN_C)
    return dr, ok, col_ok


def _na_bias(rpb):
    L, H = rpb.shape[:2]
    nd = 2 * NA_WIN_R - 1
    _, _, col_ok = _na_bias_index()
    left = GRID_W - NA_WIN_C
    f = jnp.pad(rpb * LOG2E, ((0, 0), (0, 0), (0, 0), (left, 2 * GRID_W - (2 * NA_WIN_C - 1) - left)))
    skew = jnp.broadcast_to(f[:, :, :, None, :], (L, H, nd, GRID_W, 2 * GRID_W))
    skew = skew.reshape(L, H, nd, -1)[..., :GRID_W * (2 * GRID_W - 1)].reshape(L, H, nd, GRID_W, 2 * GRID_W - 1)
    toe = skew[..., GRID_W - 1:]
    return jnp.where(col_ok, toe, NEG_INF)


def _na_attend(p, q_norm, k_norm, toe):
    B = p.shape[0]
    c0 = _P_START['n_qkv'] // NA_W
    zc = _P_START['n_z'] // NA_W
    blk = lambda c: pl.BlockSpec((1, NA_QB, NA_W), lambda b, r: (b, r, c))
    full = lambda c: pl.BlockSpec((1, S_ALL, NA_W), lambda b, r: (b, 0, c))
    wspec = pl.BlockSpec((1, NA_DH), lambda b, r: (0, 0))
    return pl.pallas_call(
        _na_kernel,
        out_shape=jax.ShapeDtypeStruct((B, S_ALL, NA_W), BF16),
        grid=(B, NA_NBLK + 1),
        in_specs=[blk(c0), full(c0 + 1), full(c0 + 2), blk(zc),
                  pl.BlockSpec(toe.shape, lambda b, r: (0, 0, 0, 0)), wspec, wspec],
        out_specs=blk(0),
        scratch_shapes=[pltpu.VMEM((NA_HEADS, NA_QB, NA_KW), F32), pltpu.VMEM((S_ALL, NA_W), BF16)],
        compiler_params=_cparams(2),
        name="na_attend",
    )(p, p, p, p, toe, q_norm.reshape(1, NA_DH), k_norm.reshape(1, NA_DH))


MP_TM = 544
MLA_HW = 2 * LANES
MLA_TQ = 1024
MLA_SUBQ = 512
MLA_TK = 1024


def _rope_tables():
    n_freq = MLA_ROPE // 4
    inv_freq = ROPE_THETA ** (-np.arange(n_freq, dtype=np.float64) / n_freq)
    t = np.arange(SEQ)
    ar = (t // GRID_W)[:, None] * inv_freq
    ac = (t % GRID_W)[:, None] * inv_freq
    cos = np.concatenate([np.cos(ar), np.cos(ar), np.cos(ac), np.cos(ac)], axis=1)
    sin = np.concatenate([-np.sin(ar), np.sin(ar), -np.sin(ac), np.sin(ac)], axis=1)
    cos = np.concatenate([cos, np.ones((CTX_LEN, MLA_ROPE))], axis=0)
    sin = np.concatenate([sin, np.zeros((CTX_LEN, MLA_ROPE))], axis=0)
    return (np.tile(cos, (1, MLA_HEADS)).astype(np.float32), np.tile(sin, (1, MLA_HEADS)).astype(np.float32))


def _rope_rotate(t, cos, sin):
    w = t.shape[1]
    lane = lax.broadcasted_iota(jnp.int32, (1, w), 1)
    first = (lane & 31) < 16
    up = pltpu.roll(t, w - 16, axis=1)
    dn = pltpu.roll(t, 16, axis=1)
    return t * cos + jnp.where(first, up, dn) * sin


def _mla_prep_kernel(p_ref, cos_ref, sin_ref, qan_ref, wuq_ref, kvan_ref, wukv_ref, qn_ref, kn_ref,
                     q_out, k_out, v_out):
    x = p_ref[0].astype(F32)
    cq = x[:, :MLA_Q_RANK]
    ckv = x[:, MLA_Q_RANK:MLA_Q_RANK + MLA_KV_RANK]
    kr2 = x[:, MLA_Q_RANK + MLA_KV_RANK:]

    def rms(t, w):
        return t * lax.rsqrt(jnp.mean(t * t, axis=-1, keepdims=True) + EPS) * w

    qf = jnp.dot(rms(cq, qan_ref[...]).astype(BF16), wuq_ref[...], preferred_element_type=F32)
    kvf = jnp.dot(rms(ckv, kvan_ref[...]).astype(BF16), wukv_ref[...], preferred_element_type=F32)
    cos = cos_ref[...]
    sin = sin_ref[...]
    qw = qn_ref[...]
    kw = kn_ref[...]
    n_all = MLA_HEADS * MLA_NOPE
    lane = lax.broadcasted_iota(jnp.int32, (1, LANES), 1)
    halves = (lane < MLA_ROPE, lane >= MLA_ROPE)

    q_rope = qf[:, n_all:]
    q_rope_sq = q_rope * q_rope
    q_rot = _rope_rotate(q_rope * qw[:, n_all:], cos, sin)
    kr_sq = jnp.sum(jnp.where(halves[0], kr2 * kr2, 0.0), axis=-1, keepdims=True)
    k_rot = _rope_rotate(kr2 * kw[:, n_all:], cos[:, :LANES], sin[:, :LANES])
    for h in range(MLA_HEADS):
        half = halves[h % 2]
        vsl = slice((h // 2) * LANES, (h // 2 + 1) * LANES)
        nsl = slice(h * MLA_NOPE, (h + 1) * MLA_NOPE)
        q_nope = qf[:, nsl]
        ss = (jnp.sum(q_nope * q_nope, axis=-1, keepdims=True)
              + jnp.sum(jnp.where(half, q_rope_sq[:, vsl], 0.0), axis=-1, keepdims=True))
        r = lax.rsqrt(ss * (1.0 / MLA_QK) + EPS) * (MLA_QK ** -0.5 * LOG2E)
        q_out[0, :, h * MLA_HW:h * MLA_HW + LANES] = (q_nope * qw[:, nsl] * r).astype(BF16)
        q_out[0, :, h * MLA_HW + LANES:(h + 1) * MLA_HW] = (jnp.where(half, q_rot[:, vsl], 0.0) * r).astype(BF16)
        k_nope = kvf[:, nsl]
        ss = jnp.sum(k_nope * k_nope, axis=-1, keepdims=True) + kr_sq
        r = lax.rsqrt(ss * (1.0 / MLA_QK) + EPS)
        k_out[0, :, h * MLA_HW:h * MLA_HW + LANES] = (k_nope * kw[:, nsl] * r).astype(BF16)
        k_out[0, :, h * MLA_HW + LANES:(h + 1) * MLA_HW] = (jnp.where(half, k_rot, 0.0) * r).astype(BF16)
    v_out[0] = kvf[:, n_all:].astype(BF16)


def _mla_prep(p, cos, sin, qa_norm, w_uq, kva_norm, w_ukv, q_norm, k_norm):
    B = p.shape[0]
    H = MLA_HEADS
    uq = w_uq.reshape(MLA_Q_RANK, H, MLA_QK)
    uq = jnp.concatenate([uq[:, :, :MLA_NOPE].reshape(MLA_Q_RANK, -1),
                          uq[:, :, MLA_NOPE:].reshape(MLA_Q_RANK, -1)], axis=1).astype(BF16)
    ukv = w_ukv.reshape(MLA_KV_RANK, H, MLA_NOPE + MLA_V)
    ukv = jnp.concatenate([ukv[:, :, :MLA_NOPE].reshape(MLA_KV_RANK, -1),
                           ukv[:, :, MLA_NOPE:].reshape(MLA_KV_RANK, -1)], axis=1).astype(BF16)
    qn = jnp.concatenate([jnp.tile(q_norm[:MLA_NOPE], H), jnp.tile(q_norm[MLA_NOPE:], H)]).reshape(1, -1)
    kn = jnp.concatenate([jnp.tile(k_norm[:MLA_NOPE], H), jnp.tile(k_norm[MLA_NOPE:], 2)]).reshape(1, -1)
    pc = _P_START['m_q'] // P_MLA_BLK
    const = lambda shape: pl.BlockSpec(shape, lambda b, i: (0, 0))
    rows = lambda w: pl.BlockSpec((MP_TM, w), lambda b, i: (i, 0))
    outs = lambda w: pl.BlockSpec((1, MP_TM, w), lambda b, i: (b, i, 0))
    return pl.pallas_call(
        _mla_prep_kernel,
        out_shape=(jax.ShapeDtypeStruct((B, S_ALL, H * MLA_HW), BF16),
                   jax.ShapeDtypeStruct((B, S_ALL, H * MLA_HW), BF16),
                   jax.ShapeDtypeStruct((B, S_ALL, MLA_W), BF16)),
        grid=(B, S_ALL // MP_TM),
        in_specs=[pl.BlockSpec((1, MP_TM, P_MLA_BLK), lambda b, i: (b, i, pc)),
                  rows(H * MLA_ROPE), rows(H * MLA_ROPE),
                  const((1, MLA_Q_RANK)), const(uq.shape), const((1, MLA_KV_RANK)), const(ukv.shape),
                  const(qn.shape), const(kn.shape)],
        out_specs=(outs(H * MLA_HW), outs(H * MLA_HW), outs(MLA_W)),
        compiler_params=_cparams(2),
        name="mla_prep",
    )(p, cos, sin, qa_norm.reshape(1, -1), uq, kva_norm.reshape(1, -1), ukv, qn, kn)


def _mla_attn_kernel(q_ref, k_ref, v_ref, z_ref, o_ref, *, ctx_start, n_lat_chunks):
    tq = q_ref.shape[1]
    sub = min(tq, MLA_SUBQ)
    bounds = [(ctx_start, ctx_start + CTX_LEN)] + [(i * MLA_TK, (i + 1) * MLA_TK) for i in range(n_lat_chunks)]

    def rows_chain(r0):
        q = q_ref[0, r0:r0 + sub, :]
        m = jnp.full((sub, 1), NEG_INF, F32)
        l = jnp.zeros((sub, 1), F32)
        acc = jnp.zeros((sub, MLA_V), F32)
        for lo, hi in bounds:
            s = _dot_nt(q, k_ref[0, lo:hi, :])
            yield
            m_new = jnp.maximum(m, jnp.max(s, axis=-1, keepdims=True))
            a = jnp.exp2(m - m_new)
            p = jnp.exp2(s - m_new)
            l = a * l + jnp.sum(p, axis=-1, keepdims=True)
            acc = a * acc + jnp.dot(p.astype(BF16), v_ref[0, lo:hi, :], preferred_element_type=F32)
            m = m_new
            yield
        z = z_ref[0, r0:r0 + sub, :].astype(F32)
        o_ref[0, r0:r0 + sub, :] = (acc / l * _silu(z)).astype(BF16)

    _round_robin(rows_chain(r0) for r0 in range(0, tq, sub))


def _mla_attn_ctx_kernel(q_ref, k_ref, v_ref, z_ref, lat_ref, o_ref, **kw):
    del lat_ref
    _mla_attn_kernel(q_ref, k_ref, v_ref, z_ref, o_ref, **kw)


def _mla_attend(q, k, v, p):
    B = p.shape[0]
    H = MLA_HEADS
    zc = _P_START['m_z'] // MLA_V
    ctx_blk = SEQ // CTX_LEN

    def call(tq, q_blk0, n_q, key_rows, key_blk, kern, name, into=None):
        specs = [pl.BlockSpec((1, tq, MLA_HW), lambda b, h, i: (b, q_blk0 + i, h)),
                 pl.BlockSpec((1, key_rows, MLA_HW), lambda b, h, i: (b, key_blk, h)),
                 pl.BlockSpec((1, key_rows, MLA_V), lambda b, h, i: (b, key_blk, h)),
                 pl.BlockSpec((1, tq, MLA_V), lambda b, h, i: (b, q_blk0 + i, zc + h))]
        args = (q, k, v, p)
        if into is not None:
            specs.append(pl.BlockSpec(memory_space=pl.ANY))
            args += (into,)
        return pl.pallas_call(
            kern,
            out_shape=jax.ShapeDtypeStruct((B, S_ALL, MLA_W), BF16),
            grid=(B, H, n_q),
            in_specs=specs,
            out_specs=pl.BlockSpec((1, tq, MLA_V), lambda b, h, i: (b, q_blk0 + i, h)),
            input_output_aliases={} if into is None else {len(args) - 1: 0},
            compiler_params=_cparams(3),
            name=name,
        )(*args)

    lat = call(MLA_TQ, 0, SEQ // MLA_TQ, S_ALL, 0,
               functools.partial(_mla_attn_kernel, ctx_start=SEQ, n_lat_chunks=SEQ // MLA_TK), "mla_attend")
    return call(CTX_LEN, ctx_blk, 1, CTX_LEN, ctx_blk,
                functools.partial(_mla_attn_ctx_kernel, ctx_start=0, n_lat_chunks=0), "mla_attend_ctx", into=lat)


SCAN_TILE = 256
SCAN_NT = S_ALL // SCAN_TILE
CTX_TILE = SEQ // SCAN_TILE
HALO = 16


def _split_dot(m_bf16, x):
    hi = x.astype(BF16)
    lo = (x - hi.astype(F32)).astype(BF16)
    return (jnp.dot(m_bf16, hi, preferred_element_type=F32)
            + jnp.dot(m_bf16, lo, preferred_element_type=F32))


def _softplus(t):
    return jnp.maximum(t, 0.0) + jnp.log1p(jnp.exp(-jnp.abs(t)))


def _conv3_silu(xb, prev_row, next_row, w, bias=None):
    n = xb.shape[0]
    i = lax.broadcasted_iota(jnp.int32, (n, 1), 0)
    j = lax.broadcasted_iota(jnp.int32, (1, n), 1)
    one = lambda m: jnp.where(m, 1.0, 0.0).astype(BF16)
    xp = jnp.dot(one(i == j + 1), xb, preferred_element_type=F32)
    xn = jnp.dot(one(i + 1 == j), xb, preferred_element_type=F32)
    y = xp * w[0:1] + xb.astype(F32) * w[1:2] + xn * w[2:3]
    r8 = lax.broadcasted_iota(jnp.int32, (8, 1), 0)
    top = y[0:8] + jnp.where(r8 == 0, prev_row * w[0:1], 0.0)
    bot = y[n - 8:] + jnp.where(r8 == 7, next_row * w[2:3], 0.0)
    y = jnp.concatenate([top, y[8:n - 8], bot], axis=0)
    if bias is not None:
        y = y + bias
    return _silu(y)


def _halo_rows(i, prev_ref, next_ref):
    pv = jnp.where((i == 0) | (i == CTX_TILE), 0.0, 1.0)
    nv = jnp.where((i == CTX_TILE - 1) | (i == SCAN_NT - 1), 0.0, 1.0)
    return prev_ref[0, HALO - 1:HALO, :].astype(F32) * pv, next_ref[0, 0:1, :].astype(F32) * nv


def _halo_specs(width, col_blk):
    rb = SCAN_TILE // HALO
    nblk = S_ALL // HALO
    return [pl.BlockSpec((1, SCAN_TILE, width), lambda b, i: (b, i, col_blk)),
            pl.BlockSpec((1, HALO, width), lambda b, i: (b, jnp.maximum(i * rb - 1, 0), col_blk)),
            pl.BlockSpec((1, HALO, width), lambda b, i: (b, jnp.minimum((i + 1) * rb, nblk - 1), col_blk))]


def _fwd_tile(t):
    return jnp.where(t == 0, CTX_TILE, t - 1)


def _bwd_tile(t):
    return jnp.where(t == 0, CTX_TILE, CTX_TILE - t)


def _chunk_masks(n):
    i = lax.broadcasted_iota(jnp.int32, (n, 1), 0)
    j = lax.broadcasted_iota(jnp.int32, (1, n), 1)
    same = (i // GDN_CHUNK) == (j // GDN_CHUNK)
    return i, j, same


def _gdn_prep_kernel(x_ref, prev_ref, next_ref, s_ref, cw_ref, rate_ref, dtb_ref,
                     q_out, k_out, v_out, g_out):
    i = pl.program_id(1)
    prev_row, next_row = _halo_rows(i, prev_ref, next_ref)
    y = _conv3_silu(x_ref[0], prev_row, next_row, cw_ref[...])
    for h in range(GDN_HEADS):
        sl = slice(h * GDN_DK, (h + 1) * GDN_DK)
        qh = y[:, sl]
        q_out[0, :, sl] = (qh * lax.rsqrt(jnp.sum(qh * qh, axis=-1, keepdims=True) + EPS)
                           * (GDN_DK ** -0.5)).astype(BF16)
        kh = y[:, GDN_W + h * GDN_DK:GDN_W + (h + 1) * GDN_DK]
        k_out[0, :, sl] = (kh * lax.rsqrt(jnp.sum(kh * kh, axis=-1, keepdims=True) + EPS)).astype(BF16)
    v_out[0] = y[:, 2 * GDN_W:].astype(BF16)

    s = s_ref[0]
    lane = lax.broadcasted_iota(jnp.int32, (1, LANES), 1)
    nh2 = 2 * GDN_HEADS
    beta = jax.nn.sigmoid(s)
    g = -rate_ref[...] * _softplus(s + dtb_ref[...])
    g = jnp.where((lane >= nh2) & (lane < 2 * nh2), g, 0.0)
    ii, jj, same = _chunk_masks(SCAN_TILE)
    one = lambda m: jnp.where(m, 1.0, 0.0).astype(BF16)
    fwd_lane = lane < nh2 + GDN_HEADS
    gam = jnp.where(fwd_lane, _split_dot(one(same & (jj <= ii)), g), _split_dot(one(same & (jj >= ii)), g))
    rem = jnp.where(fwd_lane, _split_dot(one(same & (jj > ii)), g), _split_dot(one(same & (jj < ii)), g))
    cf = jnp.where(lane < nh2, beta, jnp.where(lane < 2 * nh2, gam, pltpu.roll(rem, nh2, axis=1)))
    tr = cf.T
    for h in range(GDN_HEADS):
        for r, src in enumerate((h, GDN_HEADS + h, nh2 + h, nh2 + GDN_HEADS + h,
                                 2 * nh2 + h, 2 * nh2 + GDN_HEADS + h)):
            g_out[0, h, r:r + 1, :] = tr[src:src + 1, :]
        g_out[0, h, 6:8, :] = jnp.zeros((2, SCAN_TILE), F32)


def _gdn_prep(p, ps, conv_w, A_log, dt_bias):
    B = p.shape[0]
    W3 = 3 * GDN_W
    nh2 = 2 * GDN_HEADS
    rate = jnp.zeros((1, LANES), F32).at[0, nh2:2 * nh2].set(jnp.exp(A_log).reshape(-1))
    dtb = jnp.zeros((1, LANES), F32).at[0, nh2:2 * nh2].set(dt_bias.reshape(-1))
    shp = jax.ShapeDtypeStruct((B, S_ALL, GDN_W), BF16)
    ospec = pl.BlockSpec((1, SCAN_TILE, GDN_W), lambda b, i: (b, i, 0))
    const = lambda shape: pl.BlockSpec(shape, lambda b, i: (0, 0))
    return pl.pallas_call(
        _gdn_prep_kernel,
        out_shape=(shp, shp, shp, jax.ShapeDtypeStruct((B, GDN_HEADS, 8, S_ALL), F32)),
        grid=(B, SCAN_NT),
        in_specs=_halo_specs(W3, _P_START['g_qkv'] // W3)
        + [pl.BlockSpec((1, SCAN_TILE, LANES), lambda b, i: (b, i, 0)),
           const((SHORT_CONV, W3)), const((1, LANES)), const((1, LANES))],
        out_specs=(ospec, ospec, ospec,
                   pl.BlockSpec((1, GDN_HEADS, 8, SCAN_TILE), lambda b, i: (b, 0, 0, i))),
        compiler_params=_cparams(2),
        name="gdn_prep",
    )(p, p, p, ps, conv_w, rate, dtb)


def _gdn_dir(q, k, v, gr, s_ref, o_ref, d):
    n = SCAN_TILE
    cf = jnp.concatenate([gr, jnp.zeros((LANES - 8, n), F32)], axis=0).T
    beta, gam_c = cf[:, d:d + 1], cf[:, 2 + d:3 + d]
    ecf = jnp.exp(cf)
    e_gam, e_rem = ecf[:, 2 + d:3 + d], ecf[:, 4 + d:5 + d]
    gam_r = gr[2 + d:3 + d, :]
    ii, jj, same = _chunk_masks(n)
    incl = same & ((jj <= ii) if d == 0 else (jj >= ii))
    strict = same & ((jj < ii) if d == 0 else (jj > ii))
    kk = _dot_nt(k, k)
    yield
    qk = _dot_nt(q, k)
    yield
    dec = jnp.exp(jnp.where(incl, gam_c - gam_r, NEG_INF))
    a = jnp.where(strict, beta * kk * dec, 0.0)
    qkd = (qk * dec).astype(BF16)
    kf = k.astype(F32)
    x = jnp.concatenate([v.astype(F32) * beta, kf * (beta * e_gam)], axis=1)
    pb = a.astype(BF16)
    x = x - jnp.dot(pb, x.astype(BF16), preferred_element_type=F32)
    yield
    for _ in range(5):
        pb = jnp.dot(pb, pb, preferred_element_type=F32).astype(BF16)
        yield
        x = x + jnp.dot(pb, x.astype(BF16), preferred_element_type=F32)
        yield
    u, w =x[:, :GDN_DV], x[:, GDN_DV:].astype(BF16)
    qd = (q.astype(F32) * e_gam).astype(BF16)
    kd = (kf * e_rem).astype(BF16)
    s = s_ref[...]
    nchunk = n // GDN_CHUNK
    v_new = [None] * nchunk
    qs = [None] * nchunk
    for c in (range(nchunk) if d == 0 else reversed(range(nchunk))):
        rows = slice(c * GDN_CHUNK, (c + 1) * GDN_CHUNK)
        r1 = jnp.dot(jnp.concatenate([w[rows], qd[rows]], axis=0), s.astype(BF16), preferred_element_type=F32)
        yield
        vn = u[rows] - r1[:GDN_CHUNK]
        qs[c] = r1[GDN_CHUNK:]
        v_new[c] = vn
        last = (c + 1) * GDN_CHUNK - 1 if d == 0 else c * GDN_CHUNK
        s = s * ecf[last:last + 1, 2 + d:3 + d] + _dot_tn(kd[rows], vn.astype(BF16))
        yield
    s_ref[...] = s
    vn_all = jnp.concatenate(v_new, axis=0).astype(BF16)
    o_ref[0] = (jnp.concatenate(qs, axis=0) + jnp.dot(qkd, vn_all, preferred_element_type=F32)).astype(BF16)


def _round_robin(gens):
    gens = list(gens)
    while gens:
        alive = []
        for g in gens:
            try:
                next(g)
                alive.append(g)
            except StopIteration:
                pass
        gens = alive


def _gdn_scan_kernel(qf, kf, vf, gf, qb, kb, vb, gb, of_ref, ob_ref, sf_ref, sb_ref):
    @pl.when(pl.program_id(1) == 0)
    def _():
        sf_ref[...] = jnp.zeros_like(sf_ref)
        sb_ref[...] = jnp.zeros_like(sb_ref)

    chains = []
    for h in range(GDN_HEADS):
        sl = slice(h * GDN_DK, (h + 1) * GDN_DK)
        chains.append(_gdn_dir(qf[0, :, sl], kf[0, :, sl], vf[0, :, sl], gf[0, h],
                               sf_ref.at[h], of_ref.at[:, :, sl], 0))
        chains.append(_gdn_dir(qb[0, :, sl], kb[0, :, sl], vb[0, :, sl], gb[0, h],
                               sb_ref.at[h], ob_ref.at[:, :, sl], 1))
    _round_robin(chains)


def _gdn_scan(q, k, v, g):
    B = q.shape[0]
    tok = lambda order: pl.BlockSpec((1, SCAN_TILE, GDN_W), lambda b, t: (b, order(t), 0))
    gsp = lambda order: pl.BlockSpec((1, GDN_HEADS, 8, SCAN_TILE), lambda b, t: (b, 0, 0, order(t)))
    shp = jax.ShapeDtypeStruct((B, S_ALL, GDN_W), BF16)
    f, r = _fwd_tile, _bwd_tile
    state = pltpu.VMEM((GDN_HEADS, GDN_DK, GDN_DV), F32)
    return pl.pallas_call(
        _gdn_scan_kernel,
        out_shape=(shp, shp),
        grid=(B, SCAN_NT),
        in_specs=[tok(f), tok(f), tok(f), gsp(f), tok(r), tok(r), tok(r), gsp(r)],
        out_specs=(tok(f), tok(r)),
        scratch_shapes=[state, state],
        compiler_params=_cparams(2),
        name="gdn_scan",
    )(q, k, v, g, q, k, v, g)


SSD_BC = SSM_GROUPS * SSM_STATE
SSD_HD = 2 * SSM_HEADS
SSD_GW = (SSM_HEADS // SSM_GROUPS) * SSM_HEADDIM


def _ssd_prep_kernel(x_ref, prev_ref, next_ref, s_ref, cw_ref, cb_ref, a_ref, dtb_ref,
                     xs_out, b_out, c_out, bt_out, cf_out, cr_out):
    i = pl.program_id(1)
    prev_row, next_row = _halo_rows(i, prev_ref, next_ref)
    y = _conv3_silu(x_ref[0], prev_row, next_row, cw_ref[...], cb_ref[...])
    xs_out[0] = y[:, :SSM_W]
    b_out[0] = y[:, SSM_W:SSM_W + SSD_BC].astype(BF16)
    c_out[0] = y[:, SSM_W + SSD_BC:].astype(BF16)
    bt_out[0] = y[:, SSM_W:SSM_W + SSD_BC].T.astype(BF16)

    s = s_ref[0]
    lane = lax.broadcasted_iota(jnp.int32, (1, LANES), 1)
    dt = _softplus(s + dtb_ref[...])
    on = (lane >= SSD_HD) & (lane < 2 * SSD_HD)
    a = jnp.where(on, dt * a_ref[...], 0.0)
    n = SCAN_TILE
    ii = lax.broadcasted_iota(jnp.int32, (n, 1), 0)
    jj = lax.broadcasted_iota(jnp.int32, (1, n), 1)
    one = lambda m: jnp.where(m, 1.0, 0.0).astype(BF16)
    fwd_lane = lane < SSD_HD + SSM_HEADS
    cum = jnp.where(fwd_lane, _split_dot(one(jj <= ii), a), _split_dot(one(jj >= ii), a))
    rem = jnp.where(fwd_lane, _split_dot(one(jj > ii), a), _split_dot(one(jj < ii), a))
    cf = jnp.where(lane < SSD_HD, pltpu.roll(dt, LANES - SSD_HD, axis=1),
                   jnp.where(lane < 2 * SSD_HD, cum, pltpu.roll(rem, SSD_HD, axis=1)))
    cf_out[0] = cf
    cr_out[0] = cf.T[SSD_HD:2 * SSD_HD, :]


def _ssd_prep(p, ps, conv_w, conv_b, A_log, dt_bias):
    B = p.shape[0]
    W = SSM_CONV_DIM
    a_vec = jnp.zeros((1, LANES), F32).at[0, SSD_HD:2 * SSD_HD].set(-jnp.exp(A_log).reshape(-1))
    dtb = jnp.zeros((1, LANES), F32).at[0, SSD_HD:2 * SSD_HD].set(dt_bias.reshape(-1))
    const = lambda shape: pl.BlockSpec(shape, lambda b, i: (0, 0))
    tok = lambda w: pl.BlockSpec((1, SCAN_TILE, w), lambda b, i: (b, i, 0))
    return pl.pallas_call(
        _ssd_prep_kernel,
        out_shape=(jax.ShapeDtypeStruct((B, S_ALL, SSM_W), F32),
                   jax.ShapeDtypeStruct((B, S_ALL, SSD_BC), BF16),
                   jax.ShapeDtypeStruct((B, S_ALL, SSD_BC), BF16),
                   jax.ShapeDtypeStruct((B, SSD_BC, S_ALL), BF16),
                   jax.ShapeDtypeStruct((B, S_ALL, LANES), F32),
                   jax.ShapeDtypeStruct((B, SSD_HD, S_ALL), F32)),
        grid=(B, SCAN_NT),
        in_specs=_halo_specs(W, _P_START['s_xbc'] // W)
        + [pl.BlockSpec((1, SCAN_TILE, LANES), lambda b, i: (b, i, 0)),
           const((SHORT_CONV, W)), const((1, W)), const((1, LANES)), const((1, LANES))],
        out_specs=(tok(SSM_W), tok(SSD_BC), tok(SSD_BC),
                   pl.BlockSpec((1, SSD_BC, SCAN_TILE), lambda b, i: (b, 0, i)),
                   tok(LANES),
                   pl.BlockSpec((1, SSD_HD, SCAN_TILE), lambda b, i: (b, 0, i))),
        compiler_params=_cparams(2),
        name="ssd_prep",
    )(p, p, p, ps, conv_w, conv_b.reshape(1, W), a_vec, dtb)


def _ssd_dir(x_ref, b_ref, c_ref, bt_ref, cf_ref, cr_ref, h_ref, y_ref, d, g):
    n = SCAN_TILE
    hpg = SSM_HEADS // SSM_GROUPS
    hd0 = d * SSM_HEADS + g * hpg
    gs = slice(g * SSM_STATE, (g + 1) * SSM_STATE)
    xl = slice(g * SSD_GW, (g + 1) * SSD_GW)
    cf = cf_ref[0]
    cr = cr_ref[0]
    cm = c_ref[0, :, gs]
    log_lanes = lax.broadcasted_iota(jnp.int32, (1, LANES), 1) >= SSD_HD
    ecf = jnp.exp(jnp.where(log_lanes, cf, 0.0))
    col = lambda base, h: cf[:, base + hd0 + h:base + hd0 + h + 1]
    last = n - 1 if d == 0 else 0
    src = jnp.where(log_lanes, ecf, cf)
    r = lax.broadcasted_iota(jnp.int32, (LANES, 1), 0)
    c = lax.broadcasted_iota(jnp.int32, (1, 3 * SSD_GW), 1)
    want = (c // SSD_GW) * SSD_HD + hd0 + (c % SSD_GW) // SSM_HEADDIM
    sel = jnp.where(r == want, 1.0, 0.0).astype(BF16)
    hi = src.astype(BF16)
    lo_part = (src - hi.astype(F32)).astype(BF16)
    spread = (jnp.dot(hi, sel, preferred_element_type=F32) + jnp.dot(lo_part, sel, preferred_element_type=F32))
    yield
    dt_x, ecum_x, erem_x = spread[:, :SSD_GW], spread[:, SSD_GW:2 * SSD_GW], spread[:, 2 * SSD_GW:]
    xdt = x_ref[0, :, xl] * dt_x
    xdt_b = xdt.astype(BF16)
    xdec = (xdt * erem_x).astype(BF16)
    cb = _dot_nt(cm, b_ref[0, :, gs])
    yield
    h_prev = h_ref[g]
    y_off = jnp.dot(cm, h_prev.astype(BF16), preferred_element_type=F32)
    yield
    y_off = y_off * ecum_x
    h_ref[g] = h_prev * ecum_x[last:last + 1, :] + jnp.dot(bt_ref[0, gs, :], xdec, preferred_element_type=F32)
    yield
    ii = lax.broadcasted_iota(jnp.int32, (n, 1), 0)
    jj = lax.broadcasted_iota(jnp.int32, (1, n), 1)
    causal = (jj <= ii) if d == 0 else (jj >= ii)
    lo = lax.broadcasted_iota(jnp.int32, (1, LANES), 1) < SSM_HEADDIM
    pair_out = []
    for j in range(hpg // 2):
        ys = []
        for e in range(2):
            h = 2 * j + e
            seg = col(SSD_HD, h) - cr[hd0 + h:hd0 + h + 1, :]
            sc = (cb * jnp.exp(jnp.where(causal, seg, NEG_INF))).astype(BF16)
            ys.append(jnp.dot(sc, xdt_b[:, j * LANES:(j + 1) * LANES], preferred_element_type=F32))
            yield
        pair_out.append(jnp.where(lo, ys[0], ys[1]))
    y_ref[0, :, xl] = (jnp.concatenate(pair_out, axis=1) + y_off).astype(BF16)


def _ssd_scan_kernel(xf, bf, cf_, btf, colf, rowf, xb, bb, cb_, btb, colb, rowb, yf_ref, yb_ref, hf_ref, hb_ref):
    @pl.when(pl.program_id(1) == 0)
    def _():
        hf_ref[...] = jnp.zeros_like(hf_ref)
        hb_ref[...] = jnp.zeros_like(hb_ref)

    chains = []
    for g in range(SSM_GROUPS):
        chains.append(_ssd_dir(xf, bf, cf_, btf, colf, rowf, hf_ref, yf_ref, 0, g))
        chains.append(_ssd_dir(xb, bb, cb_, btb, colb, rowb, hb_ref, yb_ref, 1, g))
    _round_robin(chains)


def _ssd_scan(xs, bm, cm, bt, cf, cr):
    B = xs.shape[0]

    def specs(order):
        tok = lambda w: pl.BlockSpec((1, SCAN_TILE, w), lambda b, t: (b, order(t), 0))
        return [tok(SSM_W), tok(SSD_BC), tok(SSD_BC),
                pl.BlockSpec((1, SSD_BC, SCAN_TILE), lambda b, t: (b, 0, order(t))),
                tok(LANES),
                pl.BlockSpec((1, SSD_HD, SCAN_TILE), lambda b, t: (b, 0, order(t)))]

    shp = jax.ShapeDtypeStruct((B, S_ALL, SSM_W), BF16)
    out = lambda order: pl.BlockSpec((1, SCAN_TILE, SSM_W), lambda b, t: (b, order(t), 0))
    hshape = pltpu.VMEM((SSM_GROUPS, SSM_STATE, SSD_GW), F32)
    args = (xs, bm, cm, bt, cf, cr)
    return pl.pallas_call(
        _ssd_scan_kernel,
        out_shape=(shp, shp),
        grid=(B, SCAN_NT),
        in_specs=specs(_fwd_tile) + specs(_bwd_tile),
        out_specs=(out(_fwd_tile), out(_bwd_tile)),
        scratch_shapes=[hshape, hshape],
        compiler_params=_cparams(2),
        name="ssd_scan",
    )(*args, *args)


def _join_kernel(x_ref, c_ref, o_ref):
    i = pl.program_id(1)

    @pl.when(i < CTX_TILE)
    def _():
        o_ref[...] = x_ref[...]

    @pl.when(i == CTX_TILE)
    def _():
        o_ref[...] = c_ref[...]


def _join(x, ctx):
    B = x.shape[0]
    blk = lambda index: pl.BlockSpec((1, SCAN_TILE, D_MODEL), index)
    return pl.pallas_call(
        _join_kernel,
        out_shape=jax.ShapeDtypeStruct((B, S_ALL, D_MODEL), x.dtype),
        grid=(B, SCAN_NT),
        in_specs=[blk(lambda b, i: (b, jnp.minimum(i, CTX_TILE - 1), 0)), blk(lambda b, i: (b, 0, 0))],
        out_specs=blk(lambda b, i: (b, i, 0)),
        compiler_params=_cparams(2),
        name="join_tokens",
    )(x, ctx)


def _repack_w_in(w):
    cut = lambda names: [w[:, _IN_START[n]:_IN_START[n] + _IN_SIZE[n]] for n in names]
    zeros = lambda n: jnp.zeros((w.shape[0], n), w.dtype)
    n_small = sum(_IN_SIZE[n] for n in _S_ORDER)
    cols = cut(_P_ORDER) + [zeros(D_INP - LANES - _off)] + cut(_S_ORDER) + [zeros(LANES - n_small)]
    return jnp.concatenate(cols, axis=1).astype(BF16)


def kernel(x, c, ctx, c_ctx, norm_w, ada_w, ada_b, w_in, gdn_conv_w, gdn_A_log, gdn_dt_bias, gdn_norm_w, na_q_norm, na_k_norm, na_rpb, mla_qa_norm, mla_w_uq, mla_kva_norm, mla_w_ukv, mla_q_norm, mla_k_norm, ssm_conv_w, ssm_conv_b, ssm_A_log, ssm_dt_bias, ssm_D, ssm_norm_w, w_out):
    B = x.shape[0]
    xs = _join(x, ctx)
    c8 = jnp.zeros((8, D_MODEL), F32).at[:B].set(c).at[B].set(c_ctx)
    mods = _ada_all(c8, ada_w, ada_b)
    cos_np, sin_np = _rope_tables()
    cos, sin = jnp.asarray(cos_np), jnp.asarray(sin_np)
    na_bias = _na_bias(na_rpb)
    for l in range(DEPTH):
        shift, scale, gate = jnp.split(mods[l, :B], 3, axis=-1)
        shift_c, scale_c, gate_c = jnp.split(mods[l, B], 3, axis=-1)
        bc = lambda v: jnp.broadcast_to(v[None], (B, D_MODEL))
        mod4 = jnp.stack([shift, scale, bc(shift_c), bc(scale_c)], axis=1)
        gate2 = jnp.stack([gate, bc(gate_c)], axis=1)
        p, ps = _inproj(xs, norm_w[l], mod4, _repack_w_in(w_in[l]))

        gq, gk, gv, gg = _gdn_prep(p, ps, gdn_conv_w[l], gdn_A_log[l], gdn_dt_bias[l])
        o_f, o_b = _gdn_scan(gq, gk, gv, gg)

        ob = _na_attend(p, na_q_norm[l], na_k_norm[l], na_bias[l])

        mq, mk, mv = _mla_prep(p, cos, sin, mla_qa_norm[l], mla_w_uq[l], mla_kva_norm[l], mla_w_ukv[l],
                               mla_q_norm[l], mla_k_norm[l])
        oc = _mla_attend(mq, mk, mv, p)

        sx, sb, sc, sbt, scf, scr = _ssd_prep(p, ps, ssm_conv_w[l], ssm_conv_b[l], ssm_A_log[l], ssm_dt_bias[l])
        y_f, y_b = _ssd_scan(sx, sb, sc, sbt, scf, scr)
        xs = _outproj((o_f, o_b, gdn_norm_w[l]), ob, oc, (y_f, y_b, sx, ssm_D[l], ssm_norm_w[l]), p,
                      w_out[l].astype(BF16), xs, gate2, last=(l == DEPTH - 1))
    return xs
```

```python
import functools

import jax
import jax.numpy as jnp
import numpy as np
from jax import lax
from jax.experimental import pallas as pl
from jax.experimental.pallas import tpu as pltpu

F32 = jnp.float32
BF16 = jnp.bfloat16

D_MODEL = 2048
BATCH = 4
SEQ = 4096
DEPTH = 4
GRID_W = 64
GRID_H = SEQ // GRID_W
CTX_LEN = 256
S_ALL = SEQ + CTX_LEN
EPS = 1e-6
NEG_INF = -1e30
LOG2E = 1.4426950408889634

D_BRANCH = 512
D_MIX = 4 * D_BRANCH
SHORT_CONV = 3

GDN_HEADS = 4
GDN_DK = 128
GDN_DV = 128
GDN_W = GDN_HEADS * GDN_DV
GDN_CHUNK = 64

NA_HEADS = 4
NA_DH = 128
NA_W = NA_HEADS * NA_DH
NA_WIN_R = 8
NA_WIN_C = 16

MLA_HEADS = 4
MLA_Q_RANK = 384
MLA_KV_RANK = 256
MLA_NOPE = 128
MLA_ROPE = 64
MLA_QK = MLA_NOPE + MLA_ROPE
MLA_V = 128
MLA_W = MLA_HEADS * MLA_V
ROPE_THETA = 10000.0

SSM_HEADDIM = 64
SSM_HEADS = D_BRANCH // SSM_HEADDIM
SSM_W = SSM_HEADS * SSM_HEADDIM
SSM_GROUPS = 2
SSM_STATE = 128
SSM_CONV_DIM = SSM_W + 2 * SSM_GROUPS * SSM_STATE

IN_SIZES = (3 * GDN_W, GDN_W, 2 * GDN_HEADS, 2 * GDN_HEADS,
            3 * NA_W, NA_W,
            MLA_Q_RANK, MLA_KV_RANK, MLA_ROPE, MLA_W,
            SSM_W, SSM_CONV_DIM, 2 * SSM_HEADS)
D_IN = sum(IN_SIZES)
_IN_NAMES = ('g_qkv', 'g_z', 'g_beta', 'g_alpha', 'n_qkv', 'n_z',
             'm_q', 'm_kv', 'm_kr', 'm_z', 's_z', 's_xbc', 's_dt')
_IN_START = dict(zip(_IN_NAMES, np.cumsum((0,) + IN_SIZES[:-1]).tolist()))
_IN_SIZE = dict(zip(_IN_NAMES, IN_SIZES))

LANES = 128
_P_ORDER = ('g_qkv', 'g_z', 'n_qkv', 'n_z', 'm_z', 's_z', 's_xbc',
            'm_q', 'm_kv', 'm_kr', 'm_kr')
_S_ORDER = ('g_beta', 'g_alpha', 's_dt')
_P_START = {}
_off = 0
for _n in _P_ORDER:
    _P_START.setdefault(_n, _off)
    _off += _IN_SIZE[_n]
MXU_N = 256
D_INP = -(-_off // (2 * MXU_N)) * (2 * MXU_N)
P_MLA_BLK = MLA_Q_RANK + MLA_KV_RANK + 2 * MLA_ROPE
assert _P_START['m_q'] % P_MLA_BLK == 0 and D_INP % LANES == 0

VMEM_LIMIT = 52 * 1024 * 1024


def _silu(x):
    h = 0.5 * x
    return h + h * jnp.tanh(h)


def _dot_nt(a, b):
    return lax.dot_general(a, b, (((1,), (1,)), ((), ())), preferred_element_type=F32)


def _dot_tn(a, b):
    return lax.dot_general(a, b, (((0,), (0,)), ((), ())), preferred_element_type=F32)


def _cparams(n_axes):
    return pltpu.CompilerParams(dimension_semantics=("arbitrary",) * n_axes,
                                vmem_limit_bytes=VMEM_LIMIT)


ADA_TK = 512


def _ada_kernel(c_ref, w_ref, b_ref, o_ref):
    k = pl.program_id(1)
    a = _silu(c_ref[...]).astype(BF16)
    t = jnp.dot(a, w_ref[0].astype(BF16), preferred_element_type=F32)

    @pl.when(k == 0)
    def _():
        o_ref[0] = t + b_ref[0]

    @pl.when(k > 0)
    def _():
        o_ref[0] += t


def _ada_all(c8, ada_w, ada_b):
    L = ada_w.shape[0]
    n3 = ada_w.shape[2]
    return pl.pallas_call(
        _ada_kernel,
        out_shape=jax.ShapeDtypeStruct((L, 8, n3), F32),
        grid=(L, D_MODEL // ADA_TK),
        in_specs=[pl.BlockSpec((8, ADA_TK), lambda l, k: (0, k)),
                  pl.BlockSpec((1, ADA_TK, n3), lambda l, k: (l, k, 0)),
                  pl.BlockSpec((1, 1, n3), lambda l, k: (l, 0, 0))],
        out_specs=pl.BlockSpec((1, 8, n3), lambda l, k: (l, 0, 0)),
        compiler_params=_cparams(2),
        name="ada_mod",
    )(c8, ada_w, ada_b.reshape(L, 1, n3))


IN_TM = 1088
IN_TN = 1024
IN_RC = 16
IN_SECTIONS = 4


def _inproj_kernel(x_ref, nw_ref, mod_ref, w_ref, o_ref, os_ref, h_scr):
    i = pl.program_id(1)
    j = pl.program_id(2)

    @pl.when(j == 0)
    def _():
        m = mod_ref[0]
        nw = nw_ref[...]
        gain_l = nw * (1.0 + m[1:2])
        gain_c = nw * (1.0 + m[3:4])
        sec = IN_TM // IN_SECTIONS
        for c in range(IN_SECTIONS):
            for r0 in range(c * sec, (c + 1) * sec, IN_RC):
                x = x_ref[0, r0:r0 + IN_RC, :]
                ms = jnp.mean(x * x, axis=-1, keepdims=True)
                is_ctx = i * IN_TM + r0 >= SEQ
                gain = jnp.where(is_ctx, gain_c, gain_l)
                shift = jnp.where(is_ctx, m[2:3], m[0:1])
                h_scr[r0:r0 + IN_RC, :] = (x * lax.rsqrt(ms + EPS) * gain + shift).astype(BF16)
            rows = slice(c * sec, (c + 1) * sec)
            o_ref[0, rows, :] = jnp.dot(h_scr[rows, :], w_ref[...], preferred_element_type=F32).astype(BF16)

    @pl.when(j > 0)
    def _():
        y = jnp.dot(h_scr[...], w_ref[...], preferred_element_type=F32)
        o_ref[0] = y.astype(BF16)

        @pl.when(j == pl.num_programs(2) - 1)
        def _():
            os_ref[0] = y[:, IN_TN - LANES:]


def _inproj(xs, norm_w, mod4, w_main):
    B = xs.shape[0]
    return pl.pallas_call(
        _inproj_kernel,
        out_shape=(jax.ShapeDtypeStruct((B, S_ALL, D_INP), BF16),
                   jax.ShapeDtypeStruct((B, S_ALL, LANES), F32)),
        grid=(B, S_ALL // IN_TM, D_INP // IN_TN),
        in_specs=[pl.BlockSpec((1, IN_TM, D_MODEL), lambda b, i, j: (b, i, 0)),
                  pl.BlockSpec((1, D_MODEL), lambda b, i, j: (0, 0)),
                  pl.BlockSpec((1, 4, D_MODEL), lambda b, i, j: (b, 0, 0)),
                  pl.BlockSpec((D_MODEL, IN_TN), lambda b, i, j: (0, j))],
        out_specs=(pl.BlockSpec((1, IN_TM, IN_TN), lambda b, i, j: (b, i, j)),
                   pl.BlockSpec((1, IN_TM, LANES), lambda b, i, j: (b, i, 0))),
        scratch_shapes=[pltpu.VMEM((IN_TM, D_MODEL), BF16)],
        compiler_params=_cparams(3),
        name="inproj",
    )(xs, norm_w.reshape(1, D_MODEL), mod4, w_main)


OUT_TM = 544
OUT_TM_LAST = 512
OUT_SECTIONS = 2


def _gdn_gate(o_f, o_b, z, nw):
    o = o_f.astype(F32) + o_b.astype(F32)
    z = z.astype(F32)
    outs = []
    for h in range(GDN_HEADS):
        sl = slice(h * GDN_DV, (h + 1) * GDN_DV)
        oh = o[:, sl]
        y = oh * lax.rsqrt(jnp.mean(oh * oh, axis=-1, keepdims=True) + EPS) * nw
        outs.append((y * _silu(z[:, sl])).astype(BF16))
    return jnp.concatenate(outs, axis=-1)


def _ssd_gate(y_f, y_b, xs, z, d_skip, nw):
    y = y_f.astype(F32) + y_b.astype(F32) + d_skip * xs
    y = y * _silu(z.astype(F32))
    return (y * lax.rsqrt(jnp.mean(y * y, axis=-1, keepdims=True) + EPS) * nw).astype(BF16)


def _outproj_kernel(of_ref, ob_ref, gz_ref, gnw_ref, na_ref, mla_ref, yf_ref, yb_ref, sx_ref, sz_ref,
                    dsk_ref, snw_ref, w_ref, x_ref, g_ref, o_ref, *, tm):
    i = pl.program_id(1)
    g = g_ref[0]
    sec = tm // OUT_SECTIONS
    for c in range(OUT_SECTIONS):
        rows = slice(c * sec, (c + 1) * sec)
        branches = (_gdn_gate(of_ref[0, rows, :], ob_ref[0, rows, :], gz_ref[0, rows, :], gnw_ref[...]),
                    na_ref[0, rows, :], mla_ref[0, rows, :],
                    _ssd_gate(yf_ref[0, rows, :], yb_ref[0, rows, :], sx_ref[0, rows, :], sz_ref[0, rows, :],
                              dsk_ref[...], snw_ref[...]))
        y = None
        for n, a in enumerate(branches):
            t = jnp.dot(a, w_ref[n * D_BRANCH:(n + 1) * D_BRANCH, :], preferred_element_type=F32)
            y = t if y is None else y + t
        row = i * tm + c * sec + lax.broadcasted_iota(jnp.int32, (sec, 1), 0)
        gate = jnp.where(row >= SEQ, g[1:2], g[0:1])
        o_ref[0, rows, :] = x_ref[0, rows, :] + gate * y


def _outproj(gdn, na, mla, ssd, p, w_out_b, xs, gate2, last):
    B = xs.shape[0]
    o_f, o_b, g_nw = gdn
    y_f, y_b, s_x, d_skip, s_nw = ssd
    tm, rows = (OUT_TM_LAST, SEQ) if last else (OUT_TM, S_ALL)
    a_spec = pl.BlockSpec((1, tm, D_BRANCH), lambda b, i: (b, i, 0))
    z_spec = lambda name: pl.BlockSpec((1, tm, D_BRANCH), lambda b, i: (b, i, _P_START[name] // D_BRANCH))
    x_spec = pl.BlockSpec((1, tm, D_MODEL), lambda b, i: (b, i, 0))
    vec = lambda n: pl.BlockSpec((1, n), lambda b, i: (0, 0))
    return pl.pallas_call(
        functools.partial(_outproj_kernel, tm=tm),
        out_shape=jax.ShapeDtypeStruct((B, rows, D_MODEL), F32),
        grid=(B, rows // tm),
        in_specs=[a_spec, a_spec, z_spec('g_z'), vec(GDN_DV), a_spec, a_spec,
                  a_spec, a_spec, a_spec, z_spec('s_z'), vec(SSM_W), vec(SSM_W),
                  pl.BlockSpec((D_MIX, D_MODEL), lambda b, i: (0, 0)),
                  x_spec,
                  pl.BlockSpec((1, 2, D_MODEL), lambda b, i: (b, 0, 0))],
        out_specs=x_spec,
        compiler_params=_cparams(2),
        name="outproj",
    )(o_f, o_b, p, g_nw.reshape(1, GDN_DV), na, mla,
      y_f, y_b, s_x, p, jnp.repeat(d_skip, SSM_HEADDIM).reshape(1, SSM_W), s_nw.reshape(1, SSM_W),
      w_out_b, xs, gate2)


NA_RB = 4
NA_QB = NA_RB * GRID_W
NA_KR = 12
NA_KW = NA_KR * GRID_W
NA_NBLK = GRID_H // NA_RB
assert NA_QB == CTX_LEN


def _na_headnorm(x, w, extra):
    x = x.astype(F32)
    outs = []
    for h in range(NA_HEADS):
        xh = x[:, h * NA_DH:(h + 1) * NA_DH]
        ms = jnp.mean(xh * xh, axis=-1, keepdims=True)
        outs.append((xh * lax.rsqrt(ms + EPS) * w * extra).astype(BF16))
    return jnp.concatenate(outs, axis=-1)


def _na_fill_bias(toe_ref, bias_scr, cls):
    dr, ok, _ = _na_bias_index()
    dead = jnp.full((GRID_W, GRID_W), NEG_INF, F32)
    for h in range(NA_HEADS):
        for a in range(NA_RB):
            for b in range(0, NA_KR, 2):
                pair = [toe_ref[h, int(dr[cls, a, b + e])] if ok[cls, a, b + e] else dead for e in range(2)]
                bias_scr[h, a * GRID_W:(a + 1) * GRID_W, b * GRID_W:(b + 2) * GRID_W] = jnp.concatenate(pair, axis=1)


def _na_kernel(q_ref, kraw_ref, v_ref, z_ref, toe_ref, qn_ref, kn_ref, o_ref, bias_scr, k_ref):
    rb = pl.program_id(1)

    @pl.when(rb == 0)
    def _():
        def body(t, carry):
            r0 = pl.multiple_of(t * NA_QB, NA_QB)
            k_ref[pl.ds(r0, NA_QB), :] = _na_headnorm(kraw_ref[0, pl.ds(r0, NA_QB), :], kn_ref[...], 1.0)
            return carry

        lax.fori_loop(0, S_ALL // NA_QB, body, 0)

    q = _na_headnorm(q_ref[0], qn_ref[...], NA_DH ** -0.5 * LOG2E)
    z = z_ref[0].astype(F32)
    kc = k_ref[SEQ:S_ALL, :]
    vc = v_ref[0, SEQ:S_ALL, :]

    for cls, first_rb in enumerate((0, 1, NA_NBLK - 1)):
        @pl.when(rb == first_rb)
        def _(cls=cls):
            _na_fill_bias(toe_ref, bias_scr, cls)

    def finish(h, o, l):
        sl = slice(h * NA_DH, (h + 1) * NA_DH)
        o_ref[0, :, sl] = (o / l * _silu(z[:, sl])).astype(BF16)

    @pl.when(rb < NA_NBLK)
    def _latent():
        base = jnp.clip(rb * NA_RB - NA_RB, 0, GRID_H - NA_KR)
        start = pl.multiple_of(base * GRID_W, GRID_W)
        kw = k_ref[pl.ds(start, NA_KW), :]
        vw = v_ref[0, pl.ds(start, NA_KW), :]
        def head(h):
            sl = slice(h * NA_DH, (h + 1) * NA_DH)
            s_w = _dot_nt(q[:, sl], kw[:, sl])
            yield
            s_c = _dot_nt(q[:, sl], kc[:, sl])
            yield
            s_w = s_w + bias_scr[h]
            m = jnp.maximum(jnp.max(s_w, axis=-1, keepdims=True), jnp.max(s_c, axis=-1, keepdims=True))
            p_w = jnp.exp2(s_w - m)
            p_c = jnp.exp2(s_c - m)
            l = jnp.sum(p_w, axis=-1, keepdims=True) + jnp.sum(p_c, axis=-1, keepdims=True)
            o = jnp.dot(p_w.astype(BF16), vw[:, sl], preferred_element_type=F32)
            yield
            o = o + jnp.dot(p_c.astype(BF16), vc[:, sl], preferred_element_type=F32)
            yield
            finish(h, o, l)

        _round_robin(head(h) for h in range(NA_HEADS))

    @pl.when(rb == NA_NBLK)
    def _context():
        for h in range(NA_HEADS):
            sl = slice(h * NA_DH, (h + 1) * NA_DH)
            s_c = _dot_nt(q[:, sl], kc[:, sl])
            m = jnp.max(s_c, axis=-1, keepdims=True)
            p_c = jnp.exp2(s_c - m)
            l = jnp.sum(p_c, axis=-1, keepdims=True)
            o = jnp.dot(p_c.astype(BF16), vc[:, sl], preferred_element_type=F32)
            finish(h, o, l)


def _na_bias_index():
    dr = np.zeros((3, NA_RB, NA_KR), np.int64)
    ok = np.zeros((3, NA_RB, NA_KR), bool)
    for ci, rb in enumerate((0, 1, NA_NBLK - 1)):
        base = int(np.clip(rb * NA_RB - NA_RB, 0, GRID_H - NA_KR))
        qr = rb * NA_RB + np.arange(NA_RB)[:, None]
        kr = base + np.arange(NA_KR)[None, :]
        row0 = np.clip(qr - NA_WIN_R // 2, 0, GRID_H - NA_WIN_R)
        ok[ci] = (kr >= row0) & (kr < row0 + NA_WIN_R)
        dr[ci] = np.clip(kr - qr + NA_WIN_R - 1, 0, 2 * NA_WIN_R - 2)
    qc = np.arange(GRID_W)[:, None]
    kc = np.arange(GRID_W)[None, :]
    win0 = np.clip(qc - NA_WIN_C // 2, 0, GRID_W - NA_WIN_C)
    col_ok = (kc >= win0) & (kc < win0 + NA_WIN_C)
    return dr, ok, col_ok


def _na_bias(rpb):
    L, H = rpb.shape[:2]
    nd = 2 * NA_WIN_R - 1
    _, _, col_ok = _na_bias_index()
    left = GRID_W - NA_WIN_C
    f = jnp.pad(rpb * LOG2E, ((0, 0), (0, 0), (0, 0), (left, 2 * GRID_W - (2 * NA_WIN_C - 1) - left)))
    skew = jnp.broadcast_to(f[:, :, :, None, :], (L, H, nd, GRID_W, 2 * GRID_W))
    skew = skew.reshape(L, H, nd, -1)[..., :GRID_W * (2 * GRID_W - 1)].reshape(L, H, nd, GRID_W, 2 * GRID_W - 1)
    toe = skew[..., GRID_W - 1:]
    return jnp.where(col_ok, toe, NEG_INF)


def _na_attend(p, q_norm, k_norm, toe):
    B = p.shape[0]
    c0 = _P_START['n_qkv'] // NA_W
    zc = _P_START['n_z'] // NA_W
    blk = lambda c: pl.BlockSpec((1, NA_QB, NA_W), lambda b, r: (b, r, c))
    full = lambda c: pl.BlockSpec((1, S_ALL, NA_W), lambda b, r: (b, 0, c))
    wspec = pl.BlockSpec((1, NA_DH), lambda b, r: (0, 0))
    return pl.pallas_call(
        _na_kernel,
        out_shape=jax.ShapeDtypeStruct((B, S_ALL, NA_W), BF16),
        grid=(B, NA_NBLK + 1),
        in_specs=[blk(c0), full(c0 + 1), full(c0 + 2), blk(zc),
                  pl.BlockSpec(toe.shape, lambda b, r: (0, 0, 0, 0)), wspec, wspec],
        out_specs=blk(0),
        scratch_shapes=[pltpu.VMEM((NA_HEADS, NA_QB, NA_KW), F32), pltpu.VMEM((S_ALL, NA_W), BF16)],
        compiler_params=_cparams(2),
        name="na_attend",
    )(p, p, p, p, toe, q_norm.reshape(1, NA_DH), k_norm.reshape(1, NA_DH))


MP_TM = 544
MLA_HW = 2 * LANES
MLA_TQ = 1024
MLA_SUBQ = 512
MLA_TK = 1024


def _rope_tables():
    n_freq = MLA_ROPE // 4
    inv_freq = ROPE_THETA ** (-np.arange(n_freq, dtype=np.float64) / n_freq)
    t = np.arange(SEQ)
    ar = (t // GRID_W)[:, None] * inv_freq
    ac = (t % GRID_W)[:, None] * inv_freq
    cos = np.concatenate([np.cos(ar), np.cos(ar), np.cos(ac), np.cos(ac)], axis=1)
    sin = np.concatenate([-np.sin(ar), np.sin(ar), -np.sin(ac), np.sin(ac)], axis=1)
    cos = np.concatenate([cos, np.ones((CTX_LEN, MLA_ROPE))], axis=0)
    sin = np.concatenate([sin, np.zeros((CTX_LEN, MLA_ROPE))], axis=0)
    return (np.tile(cos, (1, MLA_HEADS)).astype(np.float32), np.tile(sin, (1, MLA_HEADS)).astype(np.float32))


def _rope_rotate(t, cos, sin):
    w = t.shape[1]
    lane = lax.broadcasted_iota(jnp.int32, (1, w), 1)
    first = (lane & 31) < 16
    up = pltpu.roll(t, w - 16, axis=1)
    dn = pltpu.roll(t, 16, axis=1)
    return t * cos + jnp.where(first, up, dn) * sin


def _mla_prep_kernel(p_ref, cos_ref, sin_ref, qan_ref, wuq_ref, kvan_ref, wukv_ref, qn_ref, kn_ref,
                     q_out, k_out, v_out):
    x = p_ref[0].astype(F32)
    cq = x[:, :MLA_Q_RANK]
    ckv = x[:, MLA_Q_RANK:MLA_Q_RANK + MLA_KV_RANK]
    kr2 = x[:, MLA_Q_RANK + MLA_KV_RANK:]

    def rms(t, w):
        return t * lax.rsqrt(jnp.mean(t * t, axis=-1, keepdims=True) + EPS) * w

    qf = jnp.dot(rms(cq, qan_ref[...]).astype(BF16), wuq_ref[...], preferred_element_type=F32)
    kvf = jnp.dot(rms(ckv, kvan_ref[...]).astype(BF16), wukv_ref[...], preferred_element_type=F32)
    cos = cos_ref[...]
    sin = sin_ref[...]
    qw = qn_ref[...]
    kw = kn_ref[...]
    n_all = MLA_HEADS * MLA_NOPE
    lane = lax.broadcasted_iota(jnp.int32, (1, LANES), 1)
    halves = (lane < MLA_ROPE, lane >= MLA_ROPE)

    q_rope = qf[:, n_all:]
    q_rope_sq = q_rope * q_rope
    q_rot = _rope_rotate(q_rope * qw[:, n_all:], cos, sin)
    kr_sq = jnp.sum(jnp.where(halves[0], kr2 * kr2, 0.0), axis=-1, keepdims=True)
    k_rot = _rope_rotate(kr2 * kw[:, n_all:], cos[:, :LANES], sin[:, :LANES])
    for h in range(MLA_HEADS):
        half = halves[h % 2]
        vsl = slice((h // 2) * LANES, (h // 2 + 1) * LANES)
        nsl = slice(h * MLA_NOPE, (h + 1) * MLA_NOPE)
        q_nope = qf[:, nsl]
        ss = (jnp.sum(q_nope * q_nope, axis=-1, keepdims=True)
              + jnp.sum(jnp.where(half, q_rope_sq[:, vsl], 0.0), axis=-1, keepdims=True))
        r = lax.rsqrt(ss * (1.0 / MLA_QK) + EPS) * (MLA_QK ** -0.5 * LOG2E)
        q_out[0, :, h * MLA_HW:h * MLA_HW + LANES] = (q_nope * qw[:, nsl] * r).astype(BF16)
        q_out[0, :, h * MLA_HW + LANES:(h + 1) * MLA_HW] = (jnp.where(half, q_rot[:, vsl], 0.0) * r).astype(BF16)
        k_nope = kvf[:, nsl]
        ss = jnp.sum(k_nope * k_nope, axis=-1, keepdims=True) + kr_sq
        r = lax.rsqrt(ss * (1.0 / MLA_QK) + EPS)
        k_out[0, :, h * MLA_HW:h * MLA_HW + LANES] = (k_nope * kw[:, nsl] * r).astype(BF16)
        k_out[0, :, h * MLA_HW + LANES:(h + 1) * MLA_HW] = (jnp.where(half, k_rot, 0.0) * r).astype(BF16)
    v_out[0] = kvf[:, n_all:].astype(BF16)


def _mla_prep(p, cos, sin, qa_norm, w_uq, kva_norm, w_ukv, q_norm, k_norm):
    B = p.shape[0]
    H = MLA_HEADS
    uq = w_uq.reshape(MLA_Q_RANK, H, MLA_QK)
    uq = jnp.concatenate([uq[:, :, :MLA_NOPE].reshape(MLA_Q_RANK, -1),
                          uq[:, :, MLA_NOPE:].reshape(MLA_Q_RANK, -1)], axis=1).astype(BF16)
    ukv = w_ukv.reshape(MLA_KV_RANK, H, MLA_NOPE + MLA_V)
    ukv = jnp.concatenate([ukv[:, :, :MLA_NOPE].reshape(MLA_KV_RANK, -1),
                           ukv[:, :, MLA_NOPE:].reshape(MLA_KV_RANK, -1)], axis=1).astype(BF16)
    qn = jnp.concatenate([jnp.tile(q_norm[:MLA_NOPE], H), jnp.tile(q_norm[MLA_NOPE:], H)]).reshape(1, -1)
    kn = jnp.concatenate([jnp.tile(k_norm[:MLA_NOPE], H), jnp.tile(k_norm[MLA_NOPE:], 2)]).reshape(1, -1)
    pc = _P_START['m_q'] // P_MLA_BLK
    const = lambda shape: pl.BlockSpec(shape, lambda b, i: (0, 0))
    rows = lambda w: pl.BlockSpec((MP_TM, w), lambda b, i: (i, 0))
    outs = lambda w: pl.BlockSpec((1, MP_TM, w), lambda b, i: (b, i, 0))
    return pl.pallas_call(
        _mla_prep_kernel,
        out_shape=(jax.ShapeDtypeStruct((B, S_ALL, H * MLA_HW), BF16),
                   jax.ShapeDtypeStruct((B, S_ALL, H * MLA_HW), BF16),
                   jax.ShapeDtypeStruct((B, S_ALL, MLA_W), BF16)),
        grid=(B, S_ALL // MP_TM),
        in_specs=[pl.BlockSpec((1, MP_TM, P_MLA_BLK), lambda b, i: (b, i, pc)),
                  rows(H * MLA_ROPE), rows(H * MLA_ROPE),
                  const((1, MLA_Q_RANK)), const(uq.shape), const((1, MLA_KV_RANK)), const(ukv.shape),
                  const(qn.shape), const(kn.shape)],
        out_specs=(outs(H * MLA_HW), outs(H * MLA_HW), outs(MLA_W)),
        compiler_params=_cparams(2),
        name="mla_prep",
    )(p, cos, sin, qa_norm.reshape(1, -1), uq, kva_norm.reshape(1, -1), ukv, qn, kn)


def _mla_attn_kernel(q_ref, k_ref, v_ref, z_ref, o_ref, *, ctx_start, n_lat_chunks):
    tq = q_ref.shape[1]
    sub = min(tq, MLA_SUBQ)
    bounds = [(ctx_start, ctx_start + CTX_LEN)] + [(i * MLA_TK, (i + 1) * MLA_TK) for i in range(n_lat_chunks)]

    def rows_chain(r0):
        q = q_ref[0, r0:r0 + sub, :]
        m = jnp.full((sub, 1), NEG_INF, F32)
        l = jnp.zeros((sub, 1), F32)
        acc = jnp.zeros((sub, MLA_V), F32)
        for lo, hi in bounds:
            s = _dot_nt(q, k_ref[0, lo:hi, :])
            yield
            m_new = jnp.maximum(m, jnp.max(s, axis=-1, keepdims=True))
            a = jnp.exp2(m - m_new)
            p = jnp.exp2(s - m_new)
            l = a * l + jnp.sum(p, axis=-1, keepdims=True)
            acc = a * acc + jnp.dot(p.astype(BF16), v_ref[0, lo:hi, :], preferred_element_type=F32)
            m = m_new
            yield
        z = z_ref[0, r0:r0 + sub, :].astype(F32)
        o_ref[0, r0:r0 + sub, :] = (acc / l * _silu(z)).astype(BF16)

    _round_robin(rows_chain(r0) for r0 in range(0, tq, sub))


def _mla_attn_ctx_kernel(q_ref, k_ref, v_ref, z_ref, lat_ref, o_ref, **kw):
    del lat_ref
    _mla_attn_kernel(q_ref, k_ref, v_ref, z_ref, o_ref, **kw)


def _mla_attend(q, k, v, p):
    B = p.shape[0]
    H = MLA_HEADS
    zc = _P_START['m_z'] // MLA_V
    ctx_blk = SEQ // CTX_LEN

    def call(tq, q_blk0, n_q, key_rows, key_blk, kern, name, into=None):
        specs = [pl.BlockSpec((1, tq, MLA_HW), lambda b, h, i: (b, q_blk0 + i, h)),
                 pl.BlockSpec((1, key_rows, MLA_HW), lambda b, h, i: (b, key_blk, h)),
                 pl.BlockSpec((1, key_rows, MLA_V), lambda b, h, i: (b, key_blk, h)),
                 pl.BlockSpec((1, tq, MLA_V), lambda b, h, i: (b, q_blk0 + i, zc + h))]
        args = (q, k, v, p)
        if into is not None:
            specs.append(pl.BlockSpec(memory_space=pl.ANY))
            args += (into,)
        return pl.pallas_call(
            kern,
            out_shape=jax.ShapeDtypeStruct((B, S_ALL, MLA_W), BF16),
            grid=(B, H, n_q),
            in_specs=specs,
            out_specs=pl.BlockSpec((1, tq, MLA_V), lambda b, h, i: (b, q_blk0 + i, h)),
            input_output_aliases={} if into is None else {len(args) - 1: 0},
            compiler_params=_cparams(3),
            name=name,
        )(*args)

    lat = call(MLA_TQ, 0, SEQ // MLA_TQ, S_ALL, 0,
               functools.partial(_mla_attn_kernel, ctx_start=SEQ, n_lat_chunks=SEQ // MLA_TK), "mla_attend")
    return call(CTX_LEN, ctx_blk, 1, CTX_LEN, ctx_blk,
                functools.partial(_mla_attn_ctx_kernel, ctx_start=0, n_lat_chunks=0), "mla_attend_ctx", into=lat)


SCAN_TILE = 256
SCAN_NT = S_ALL // SCAN_TILE
CTX_TILE = SEQ // SCAN_TILE
HALO = 16


def _split_dot(m_bf16, x):
    hi = x.astype(BF16)
    lo = (x - hi.astype(F32)).astype(BF16)
    return (jnp.dot(m_bf16, hi, preferred_element_type=F32)
            + jnp.dot(m_bf16, lo, preferred_element_type=F32))


def _softplus(t):
    return jnp.maximum(t, 0.0) + jnp.log1p(jnp.exp(-jnp.abs(t)))


def _conv3_silu(xb, prev_row, next_row, w, bias=None):
    n = xb.shape[0]
    i = lax.broadcasted_iota(jnp.int32, (n, 1), 0)
    j = lax.broadcasted_iota(jnp.int32, (1, n), 1)
    one = lambda m: jnp.where(m, 1.0, 0.0).astype(BF16)
    xp = jnp.dot(one(i == j + 1), xb, preferred_element_type=F32)
    xn = jnp.dot(one(i + 1 == j), xb, preferred_element_type=F32)
    y = xp * w[0:1] + xb.astype(F32) * w[1:2] + xn * w[2:3]
    r8 = lax.broadcasted_iota(jnp.int32, (8, 1), 0)
    top = y[0:8] + jnp.where(r8 == 0, prev_row * w[0:1], 0.0)
    bot = y[n - 8:] + jnp.where(r8 == 7, next_row * w[2:3], 0.0)
    y = jnp.concatenate([top, y[8:n - 8], bot], axis=0)
    if bias is not None:
        y = y + bias
    return _silu(y)


def _halo_rows(i, prev_ref, next_ref):
    pv = jnp.where((i == 0) | (i == CTX_TILE), 0.0, 1.0)
    nv = jnp.where((i == CTX_TILE - 1) | (i == SCAN_NT - 1), 0.0, 1.0)
    return prev_ref[0, HALO - 1:HALO, :].astype(F32) * pv, next_ref[0, 0:1, :].astype(F32) * nv


def _halo_specs(width, col_blk):
    rb = SCAN_TILE // HALO
    nblk = S_ALL // HALO
    return [pl.BlockSpec((1, SCAN_TILE, width), lambda b, i: (b, i, col_blk)),
            pl.BlockSpec((1, HALO, width), lambda b, i: (b, jnp.maximum(i * rb - 1, 0), col_blk)),
            pl.BlockSpec((1, HALO, width), lambda b, i: (b, jnp.minimum((i + 1) * rb, nblk - 1), col_blk))]


def _fwd_tile(t):
    return jnp.where(t == 0, CTX_TILE, t - 1)


def _bwd_tile(t):
    return jnp.where(t == 0, CTX_TILE, CTX_TILE - t)


def _chunk_masks(n):
    i = lax.broadcasted_iota(jnp.int32, (n, 1), 0)
    j = lax.broadcasted_iota(jnp.int32, (1, n), 1)
    same = (i // GDN_CHUNK) == (j // GDN_CHUNK)
    return i, j, same


def _gdn_prep_kernel(x_ref, prev_ref, next_ref, s_ref, cw_ref, rate_ref, dtb_ref,
                     q_out, k_out, v_out, g_out):
    i = pl.program_id(1)
    prev_row, next_row = _halo_rows(i, prev_ref, next_ref)
    y = _conv3_silu(x_ref[0], prev_row, next_row, cw_ref[...])
    for h in range(GDN_HEADS):
        sl = slice(h * GDN_DK, (h + 1) * GDN_DK)
        qh = y[:, sl]
        q_out[0, :, sl] = (qh * lax.rsqrt(jnp.sum(qh * qh, axis=-1, keepdims=True) + EPS)
                           * (GDN_DK ** -0.5)).astype(BF16)
        kh = y[:, GDN_W + h * GDN_DK:GDN_W + (h + 1) * GDN_DK]
        k_out[0, :, sl] = (kh * lax.rsqrt(jnp.sum(kh * kh, axis=-1, keepdims=True) + EPS)).astype(BF16)
    v_out[0] = y[:, 2 * GDN_W:].astype(BF16)

    s = s_ref[0]
    lane = lax.broadcasted_iota(jnp.int32, (1, LANES), 1)
    nh2 = 2 * GDN_HEADS
    beta = jax.nn.sigmoid(s)
    g = -rate_ref[...] * _softplus(s + dtb_ref[...])
    g = jnp.where((lane >= nh2) & (lane < 2 * nh2), g, 0.0)
    ii, jj, same = _chunk_masks(SCAN_TILE)
    one = lambda m: jnp.where(m, 1.0, 0.0).astype(BF16)
    fwd_lane = lane < nh2 + GDN_HEADS
    gam = jnp.where(fwd_lane, _split_dot(one(same & (jj <= ii)), g), _split_dot(one(same & (jj >= ii)), g))
    rem = jnp.where(fwd_lane, _split_dot(one(same & (jj > ii)), g), _split_dot(one(same & (jj < ii)), g))
    cf = jnp.where(lane < nh2, beta, jnp.where(lane < 2 * nh2, gam, pltpu.roll(rem, nh2, axis=1)))
    tr = cf.T
    for h in range(GDN_HEADS):
        for r, src in enumerate((h, GDN_HEADS + h, nh2 + h, nh2 + GDN_HEADS + h,
                                 2 * nh2 + h, 2 * nh2 + GDN_HEADS + h)):
            g_out[0, h, r:r + 1, :] = tr[src:src + 1, :]
        g_out[0, h, 6:8, :] = jnp.zeros((2, SCAN_TILE), F32)


def _gdn_prep(p, ps, conv_w, A_log, dt_bias):
    B = p.shape[0]
    W3 = 3 * GDN_W
    nh2 = 2 * GDN_HEADS
    rate = jnp.zeros((1, LANES), F32).at[0, nh2:2 * nh2].set(jnp.exp(A_log).reshape(-1))
    dtb = jnp.zeros((1, LANES), F32).at[0, nh2:2 * nh2].set(dt_bias.reshape(-1))
    shp = jax.ShapeDtypeStruct((B, S_ALL, GDN_W), BF16)
    ospec = pl.BlockSpec((1, SCAN_TILE, GDN_W), lambda b, i: (b, i, 0))
    const = lambda shape: pl.BlockSpec(shape, lambda b, i: (0, 0))
    return pl.pallas_call(
        _gdn_prep_kernel,
        out_shape=(shp, shp, shp, jax.ShapeDtypeStruct((B, GDN_HEADS, 8, S_ALL), F32)),
        grid=(B, SCAN_NT),
        in_specs=_halo_specs(W3, _P_START['g_qkv'] // W3)
        + [pl.BlockSpec((1, SCAN_TILE, LANES), lambda b, i: (b, i, 0)),
           const((SHORT_CONV, W3)), const((1, LANES)), const((1, LANES))],
        out_specs=(ospec, ospec, ospec,
                   pl.BlockSpec((1, GDN_HEADS, 8, SCAN_TILE), lambda b, i: (b, 0, 0, i))),
        compiler_params=_cparams(2),
        name="gdn_prep",
    )(p, p, p, ps, conv_w, rate, dtb)


def _gdn_dir(q, k, v, gr, s_ref, o_ref, d):
    n = SCAN_TILE
    cf = jnp.concatenate([gr, jnp.zeros((LANES - 8, n), F32)], axis=0).T
    beta, gam_c = cf[:, d:d + 1], cf[:, 2 + d:3 + d]
    ecf = jnp.exp(cf)
    e_gam, e_rem = ecf[:, 2 + d:3 + d], ecf[:, 4 + d:5 + d]
    gam_r = gr[2 + d:3 + d, :]
    ii, jj, same = _chunk_masks(n)
    incl = same & ((jj <= ii) if d == 0 else (jj >= ii))
    strict = same & ((jj < ii) if d == 0 else (jj > ii))
    kk = _dot_nt(k, k)
    yield
    qk = _dot_nt(q, k)
    yield
    dec = jnp.exp(jnp.where(incl, gam_c - gam_r, NEG_INF))
    a = jnp.where(strict, beta * kk * dec, 0.0)
    qkd = (qk * dec).astype(BF16)
    kf = k.astype(F32)
    x = jnp.concatenate([v.astype(F32) * beta, kf * (beta * e_gam)], axis=1)
    pb = a.astype(BF16)
    x = x - jnp.dot(pb, x.astype(BF16), preferred_element_type=F32)
    yield
    for _ in range(5):
        pb = jnp.dot(pb, pb, preferred_element_type=F32).astype(BF16)
        yield
        x = x + jnp.dot(pb, x.astype(BF16), preferred_element_type=F32)
        yield
    u, w =x[:, :GDN_DV], x[:, GDN_DV:].astype(BF16)
    qd = (q.astype(F32) * e_gam).astype(BF16)
    kd = (kf * e_rem).astype(BF16)
    s = s_ref[...]
    nchunk = n // GDN_CHUNK
    v_new = [None] * nchunk
    qs = [None] * nchunk
    for c in (range(nchunk) if d == 0 else reversed(range(nchunk))):
        rows = slice(c * GDN_CHUNK, (c + 1) * GDN_CHUNK)
        r1 = jnp.dot(jnp.concatenate([w[rows], qd[rows]], axis=0), s.astype(BF16), preferred_element_type=F32)
        yield
        vn = u[rows] - r1[:GDN_CHUNK]
        qs[c] = r1[GDN_CHUNK:]
        v_new[c] = vn
        last = (c + 1) * GDN_CHUNK - 1 if d == 0 else c * GDN_CHUNK
        s = s * ecf[last:last + 1, 2 + d:3 + d] + _dot_tn(kd[rows], vn.astype(BF16))
        yield
    s_ref[...] = s
    vn_all = jnp.concatenate(v_new, axis=0).astype(BF16)
    o_ref[0] = (jnp.concatenate(qs, axis=0) + jnp.dot(qkd, vn_all, preferred_element_type=F32)).astype(BF16)


def _round_robin(gens):
    gens = list(gens)
    while gens:
        alive = []
        for g in gens:
            try:
                next(g)
                alive.append(g)
            except StopIteration:
                pass
        gens = alive


def _gdn_scan_kernel(qf, kf, vf, gf, qb, kb, vb, gb, of_ref, ob_ref, sf_ref, sb_ref):
    @pl.when(pl.program_id(1) == 0)
    def _():
        sf_ref[...] = jnp.zeros_like(sf_ref)
        sb_ref[...] = jnp.zeros_like(sb_ref)

    chains = []
    for h in range(GDN_HEADS):
        sl = slice(h * GDN_DK, (h + 1) * GDN_DK)
        chains.append(_gdn_dir(qf[0, :, sl], kf[0, :, sl], vf[0, :, sl], gf[0, h],
                               sf_ref.at[h], of_ref.at[:, :, sl], 0))
        chains.append(_gdn_dir(qb[0, :, sl], kb[0, :, sl], vb[0, :, sl], gb[0, h],
                               sb_ref.at[h], ob_ref.at[:, :, sl], 1))
    _round_robin(chains)


def _gdn_scan(q, k, v, g):
    B = q.shape[0]
    tok = lambda order: pl.BlockSpec((1, SCAN_TILE, GDN_W), lambda b, t: (b, order(t), 0))
    gsp = lambda order: pl.BlockSpec((1, GDN_HEADS, 8, SCAN_TILE), lambda b, t: (b, 0, 0, order(t)))
    shp = jax.ShapeDtypeStruct((B, S_ALL, GDN_W), BF16)
    f, r = _fwd_tile, _bwd_tile
    state = pltpu.VMEM((GDN_HEADS, GDN_DK, GDN_DV), F32)
    return pl.pallas_call(
        _gdn_scan_kernel,
        out_shape=(shp, shp),
        grid=(B, SCAN_NT),
        in_specs=[tok(f), tok(f), tok(f), gsp(f), tok(r), tok(r), tok(r), gsp(r)],
        out_specs=(tok(f), tok(r)),
        scratch_shapes=[state, state],
        compiler_params=_cparams(2),
        name="gdn_scan",
    )(q, k, v, g, q, k, v, g)


SSD_BC = SSM_GROUPS * SSM_STATE
SSD_HD = 2 * SSM_HEADS
SSD_GW = (SSM_HEADS // SSM_GROUPS) * SSM_HEADDIM


def _ssd_prep_kernel(x_ref, prev_ref, next_ref, s_ref, cw_ref, cb_ref, a_ref, dtb_ref,
                     xs_out, b_out, c_out, bt_out, cf_out, cr_out):
    i = pl.program_id(1)
    prev_row, next_row = _halo_rows(i, prev_ref, next_ref)
    y = _conv3_silu(x_ref[0], prev_row, next_row, cw_ref[...], cb_ref[...])
    xs_out[0] = y[:, :SSM_W]
    b_out[0] = y[:, SSM_W:SSM_W + SSD_BC].astype(BF16)
    c_out[0] = y[:, SSM_W + SSD_BC:].astype(BF16)
    bt_out[0] = y[:, SSM_W:SSM_W + SSD_BC].T.astype(BF16)

    s = s_ref[0]
    lane = lax.broadcasted_iota(jnp.int32, (1, LANES), 1)
    dt = _softplus(s + dtb_ref[...])
    on = (lane >= SSD_HD) & (lane < 2 * SSD_HD)
    a = jnp.where(on, dt * a_ref[...], 0.0)
    n = SCAN_TILE
    ii = lax.broadcasted_iota(jnp.int32, (n, 1), 0)
    jj = lax.broadcasted_iota(jnp.int32, (1, n), 1)
    one = lambda m: jnp.where(m, 1.0, 0.0).astype(BF16)
    fwd_lane = lane < SSD_HD + SSM_HEADS
    cum = jnp.where(fwd_lane, _split_dot(one(jj <= ii), a), _split_dot(one(jj >= ii), a))
    rem = jnp.where(fwd_lane, _split_dot(one(jj > ii), a), _split_dot(one(jj < ii), a))
    cf = jnp.where(lane < SSD_HD, pltpu.roll(dt, LANES - SSD_HD, axis=1),
                   jnp.where(lane < 2 * SSD_HD, cum, pltpu.roll(rem, SSD_HD, axis=1)))
    cf_out[0] = cf
    cr_out[0] = cf.T[SSD_HD:2 * SSD_HD, :]


def _ssd_prep(p, ps, conv_w, conv_b, A_log, dt_bias):
    B = p.shape[0]
    W = SSM_CONV_DIM
    a_vec = jnp.zeros((1, LANES), F32).at[0, SSD_HD:2 * SSD_HD].set(-jnp.exp(A_log).reshape(-1))
    dtb = jnp.zeros((1, LANES), F32).at[0, SSD_HD:2 * SSD_HD].set(dt_bias.reshape(-1))
    const = lambda shape: pl.BlockSpec(shape, lambda b, i: (0, 0))
    tok = lambda w: pl.BlockSpec((1, SCAN_TILE, w), lambda b, i: (b, i, 0))
    return pl.pallas_call(
        _ssd_prep_kernel,
        out_shape=(jax.ShapeDtypeStruct((B, S_ALL, SSM_W), F32),
                   jax.ShapeDtypeStruct((B, S_ALL, SSD_BC), BF16),
                   jax.ShapeDtypeStruct((B, S_ALL, SSD_BC), BF16),
                   jax.ShapeDtypeStruct((B, SSD_BC, S_ALL), BF16),
                   jax.ShapeDtypeStruct((B, S_ALL, LANES), F32),
                   jax.ShapeDtypeStruct((B, SSD_HD, S_ALL), F32)),
        grid=(B, SCAN_NT),
        in_specs=_halo_specs(W, _P_START['s_xbc'] // W)
        + [pl.BlockSpec((1, SCAN_TILE, LANES), lambda b, i: (b, i, 0)),
           const((SHORT_CONV, W)), const((1, W)), const((1, LANES)), const((1, LANES))],
        out_specs=(tok(SSM_W), tok(SSD_BC), tok(SSD_BC),
                   pl.BlockSpec((1, SSD_BC, SCAN_TILE), lambda b, i: (b, 0, i)),
                   tok(LANES),
                   pl.BlockSpec((1, SSD_HD, SCAN_TILE), lambda b, i: (b, 0, i))),
        compiler_params=_cparams(2),
        name="ssd_prep",
    )(p, p, p, ps, conv_w, conv_b.reshape(1, W), a_vec, dtb)


def _ssd_dir(x_ref, b_ref, c_ref, bt_ref, cf_ref, cr_ref, h_ref, y_ref, d, g):
    n = SCAN_TILE
    hpg = SSM_HEADS // SSM_GROUPS
    hd0 = d * SSM_HEADS + g * hpg
    gs = slice(g * SSM_STATE, (g + 1) * SSM_STATE)
    xl = slice(g * SSD_GW, (g + 1) * SSD_GW)
    cf = cf_ref[0]
    cr = cr_ref[0]
    cm = c_ref[0, :, gs]
    log_lanes = lax.broadcasted_iota(jnp.int32, (1, LANES), 1) >= SSD_HD
    ecf = jnp.exp(jnp.where(log_lanes, cf, 0.0))
    col = lambda base, h: cf[:, base + hd0 + h:base + hd0 + h + 1]
    last = n - 1 if d == 0 else 0
    src = jnp.where(log_lanes, ecf, cf)
    r = lax.broadcasted_iota(jnp.int32, (LANES, 1), 0)
    c = lax.broadcasted_iota(jnp.int32, (1, 3 * SSD_GW), 1)
    want = (c // SSD_GW) * SSD_HD + hd0 + (c % SSD_GW) // SSM_HEADDIM
    sel = jnp.where(r == want, 1.0, 0.0).astype(BF16)
    hi = src.astype(BF16)
    lo_part = (src - hi.astype(F32)).astype(BF16)
    spread = (jnp.dot(hi, sel, preferred_element_type=F32) + jnp.dot(lo_part, sel, preferred_element_type=F32))
    yield
    dt_x, ecum_x, erem_x = spread[:, :SSD_GW], spread[:, SSD_GW:2 * SSD_GW], spread[:, 2 * SSD_GW:]
    xdt = x_ref[0, :, xl] * dt_x
    xdt_b = xdt.astype(BF16)
    xdec = (xdt * erem_x).astype(BF16)
    cb = _dot_nt(cm, b_ref[0, :, gs])
    yield
    h_prev = h_ref[g]
    y_off = jnp.dot(cm, h_prev.astype(BF16), preferred_element_type=F32)
    yield
    y_off = y_off * ecum_x
    h_ref[g] = h_prev * ecum_x[last:last + 1, :] + jnp.dot(bt_ref[0, gs, :], xdec, preferred_element_type=F32)
    yield
    ii = lax.broadcasted_iota(jnp.int32, (n, 1), 0)
    jj = lax.broadcasted_iota(jnp.int32, (1, n), 1)
    causal = (jj <= ii) if d == 0 else (jj >= ii)
    lo = lax.broadcasted_iota(jnp.int32, (1, LANES), 1) < SSM_HEADDIM
    pair_out = []
    for j in range(hpg // 2):
        ys = []
        for e in range(2):
            h = 2 * j + e
            seg = col(SSD_HD, h) - cr[hd0 + h:hd0 + h + 1, :]
            sc = (cb * jnp.exp(jnp.where(causal, seg, NEG_INF))).astype(BF16)
            ys.append(jnp.dot(sc, xdt_b[:, j * LANES:(j + 1) * LANES], preferred_element_type=F32))
            yield
        pair_out.append(jnp.where(lo, ys[0], ys[1]))
    y_ref[0, :, xl] = (jnp.concatenate(pair_out, axis=1) + y_off).astype(BF16)


def _ssd_scan_kernel(xf, bf, cf_, btf, colf, rowf, xb, bb, cb_, btb, colb, rowb, yf_ref, yb_ref, hf_ref, hb_ref):
    @pl.when(pl.program_id(1) == 0)
    def _():
        hf_ref[...] = jnp.zeros_like(hf_ref)
        hb_ref[...] = jnp.zeros_like(hb_ref)

    chains = []
    for g in range(SSM_GROUPS):
        chains.append(_ssd_dir(xf, bf, cf_, btf, colf, rowf, hf_ref, yf_ref, 0, g))
        chains.append(_ssd_dir(xb, bb, cb_, btb, colb, rowb, hb_ref, yb_ref, 1, g))
    _round_robin(chains)


def _ssd_scan(xs, bm, cm, bt, cf, cr):
    B = xs.shape[0]

    def specs(order):
        tok = lambda w: pl.BlockSpec((1, SCAN_TILE, w), lambda b, t: (b, order(t), 0))
        return [tok(SSM_W), tok(SSD_BC), tok(SSD_BC),
                pl.BlockSpec((1, SSD_BC, SCAN_TILE), lambda b, t: (b, 0, order(t))),
                tok(LANES),
                pl.BlockSpec((1, SSD_HD, SCAN_TILE), lambda b, t: (b, 0, order(t)))]

    shp = jax.ShapeDtypeStruct((B, S_ALL, SSM_W), BF16)
    out = lambda order: pl.BlockSpec((1, SCAN_TILE, SSM_W), lambda b, t: (b, order(t), 0))
    hshape = pltpu.VMEM((SSM_GROUPS, SSM_STATE, SSD_GW), F32)
    args = (xs, bm, cm, bt, cf, cr)
    return pl.pallas_call(
        _ssd_scan_kernel,
        out_shape=(shp, shp),
        grid=(B, SCAN_NT),
        in_specs=specs(_fwd_tile) + specs(_bwd_tile),
        out_specs=(out(_fwd_tile), out(_bwd_tile)),
        scratch_shapes=[hshape, hshape],
        compiler_params=_cparams(2),
        name="ssd_scan",
    )(*args, *args)


def _join_kernel(x_ref, c_ref, o_ref):
    i = pl.program_id(1)

    @pl.when(i < CTX_TILE)
    def _():
        o_ref[...] = x_ref[...]

    @pl.when(i == CTX_TILE)
    def _():
        o_ref[...] = c_ref[...]


def _join(x, ctx):
    B = x.shape[0]
    blk = lambda index: pl.BlockSpec((1, SCAN_TILE, D_MODEL), index)
    return pl.pallas_call(
        _join_kernel,
        out_shape=jax.ShapeDtypeStruct((B, S_ALL, D_MODEL), x.dtype),
        grid=(B, SCAN_NT),
        in_specs=[blk(lambda b, i: (b, jnp.minimum(i, CTX_TILE - 1), 0)), blk(lambda b, i: (b, 0, 0))],
        out_specs=blk(lambda b, i: (b, i, 0)),
        compiler_params=_cparams(2),
        name="join_tokens",
    )(x, ctx)


def _repack_w_in(w):
    cut = lambda names: [w[:, _IN_START[n]:_IN_START[n] + _IN_SIZE[n]].astype(BF16) for n in names]
    zeros = lambda n: jnp.zeros((w.shape[0], n), BF16)
    n_small = sum(_IN_SIZE[n] for n in _S_ORDER)
    cols = cut(_P_ORDER) + [zeros(D_INP - LANES - _off)] + cut(_S_ORDER) + [zeros(LANES - n_small)]
    return jnp.concatenate(cols, axis=1)


def kernel(x, c, ctx, c_ctx, norm_w, ada_w, ada_b, w_in, gdn_conv_w, gdn_A_log, gdn_dt_bias, gdn_norm_w, na_q_norm, na_k_norm, na_rpb, mla_qa_norm, mla_w_uq, mla_kva_norm, mla_w_ukv, mla_q_norm, mla_k_norm, ssm_conv_w, ssm_conv_b, ssm_A_log, ssm_dt_bias, ssm_D, ssm_norm_w, w_out):
    B = x.shape[0]
    xs = _join(x, ctx)
    c8 = jnp.zeros((8, D_MODEL), F32).at[:B].set(c).at[B].set(c_ctx)
    mods = _ada_all(c8, ada_w, ada_b)
    cos_np, sin_np = _rope_tables()
    cos, sin = jnp.asarray(cos_np), jnp.asarray(sin_np)
    na_bias = _na_bias(na_rpb)
    for l in range(DEPTH):
        shift, scale, gate = jnp.split(mods[l, :B], 3, axis=-1)
        shift_c, scale_c, gate_c = jnp.split(mods[l, B], 3, axis=-1)
        bc = lambda v: jnp.broadcast_to(v[None], (B, D_MODEL))
        mod4 = jnp.stack([shift, scale, bc(shift_c), bc(scale_c)], axis=1)
        gate2 = jnp.stack([gate, bc(gate_c)], axis=1)
        p, ps = _inproj(xs, norm_w[l], mod4, _repack_w_in(w_in[l]))

        gq, gk, gv, gg = _gdn_prep(p, ps, gdn_conv_w[l], gdn_A_log[l], gdn_dt_bias[l])
        o_f, o_b = _gdn_scan(gq, gk, gv, gg)

        ob = _na_attend(p, na_q_norm[l], na_k_norm[l], na_bias[l])

        mq, mk, mv = _mla_prep(p, cos, sin, mla_qa_norm[l], mla_w_uq[l], mla_kva_norm[l], mla_w_ukv[l],
                               mla_q_norm[l], mla_k_norm[l])
        oc = _mla_attend(mq, mk, mv, p)

        sx, sb, sc, sbt, scf, scr = _ssd_prep(p, ps, ssm_conv_w[l], ssm_conv_b[l], ssm_A_log[l], ssm_dt_bias[l])
        y_f, y_b = _ssd_scan(sx, sb, sc, sbt, scf, scr)
        xs = _outproj((o_f, o_b, gdn_norm_w[l]), ob, oc, (y_f, y_b, sx, ssm_D[l], ssm_norm_w[l]), p,
                      w_out[l].astype(BF16), xs, gate2, last=(l == DEPTH - 1))
    return xs
```

```python
import functools

import jax
import jax.numpy as jnp
import numpy as np
from jax import lax
from jax.experimental import pallas as pl
from jax.experimental.pallas import tpu as pltpu

F32 = jnp.float32
BF16 = jnp.bfloat16

D_MODEL = 2048
SEQ = 4096
DEPTH = 4
GRID_W = 64
GRID_H = SEQ // GRID_W
CTX_LEN = 256
S_ALL = SEQ + CTX_LEN
EPS = 1e-6
NEG_INF = -1e30
LOG2E = 1.4426950408889634

D_BRANCH = 512
D_MIX = 4 * D_BRANCH
SHORT_CONV = 3

GDN_HEADS = 4
GDN_DK = 128
GDN_DV = 128
GDN_W = GDN_HEADS * GDN_DV
GDN_CHUNK = 64

NA_HEADS = 4
NA_DH = 128
NA_W = NA_HEADS * NA_DH
NA_WIN_R = 8
NA_WIN_C = 16

MLA_HEADS = 4
MLA_Q_RANK = 384
MLA_KV_RANK = 256
MLA_NOPE = 128
MLA_ROPE = 64
MLA_QK = MLA_NOPE + MLA_ROPE
MLA_V = 128
MLA_W = MLA_HEADS * MLA_V
ROPE_THETA = 10000.0

SSM_HEADDIM = 64
SSM_HEADS = D_BRANCH // SSM_HEADDIM
SSM_W = SSM_HEADS * SSM_HEADDIM
SSM_GROUPS = 2
SSM_STATE = 128
SSM_CONV_DIM = SSM_W + 2 * SSM_GROUPS * SSM_STATE

IN_SIZES = (3 * GDN_W, GDN_W, 2 * GDN_HEADS, 2 * GDN_HEADS,
            3 * NA_W, NA_W,
            MLA_Q_RANK, MLA_KV_RANK, MLA_ROPE, MLA_W,
            SSM_W, SSM_CONV_DIM, 2 * SSM_HEADS)
D_IN = sum(IN_SIZES)
_IN_NAMES = ('g_qkv', 'g_z', 'g_beta', 'g_alpha', 'n_qkv', 'n_z',
             'm_q', 'm_kv', 'm_kr', 'm_z', 's_z', 's_xbc', 's_dt')
_IN_START = dict(zip(_IN_NAMES, np.cumsum((0,) + IN_SIZES[:-1]).tolist()))
_IN_SIZE = dict(zip(_IN_NAMES, IN_SIZES))

LANES = 128
_P_ORDER = ('g_qkv', 'g_z', 'n_qkv', 'n_z', 'm_z', 's_z', 's_xbc',
            'm_q', 'm_kv', 'm_kr', 'm_kr')
_S_ORDER = ('g_beta', 'g_alpha', 's_dt')
_P_START = {}
_off = 0
for _n in _P_ORDER:
    _P_START.setdefault(_n, _off)
    _off += _IN_SIZE[_n]
MXU_N = 256
D_INP = -(-_off // (2 * MXU_N)) * (2 * MXU_N)
P_MLA_BLK = MLA_Q_RANK + MLA_KV_RANK + 2 * MLA_ROPE
assert _P_START['m_q'] % P_MLA_BLK == 0 and D_INP % LANES == 0

V7X_VMEM_BYTES = 64 * 1024 * 1024
VMEM_LIMIT = V7X_VMEM_BYTES - 12 * 1024 * 1024


def _silu(x):
    h = 0.5 * x
    return h + h * jnp.tanh(h)


def _dot_nt(a, b):
    return lax.dot_general(a, b, (((1,), (1,)), ((), ())), preferred_element_type=F32)


def _dot_tn(a, b):
    return lax.dot_general(a, b, (((0,), (0,)), ((), ())), preferred_element_type=F32)


def _cparams(n_axes):
    return pltpu.CompilerParams(dimension_semantics=("arbitrary",) * n_axes,
                                vmem_limit_bytes=VMEM_LIMIT)


ADA_TN = 1536


def _ada_kernel(c_ref, w_ref, b_ref, o_ref):
    a = _silu(c_ref[...]).astype(BF16)
    o_ref[0] = jnp.dot(a, w_ref[0].astype(BF16), preferred_element_type=F32) + b_ref[0]


def _ada_all(c8, ada_w, ada_b):
    L = ada_w.shape[0]
    n3 = ada_w.shape[2]
    return pl.pallas_call(
        _ada_kernel,
        out_shape=jax.ShapeDtypeStruct((L, 8, n3), F32),
        grid=(L, n3 // ADA_TN),
        in_specs=[pl.BlockSpec((8, D_MODEL), lambda l, j: (0, 0)),
                  pl.BlockSpec((1, D_MODEL, ADA_TN), lambda l, j: (l, 0, j)),
                  pl.BlockSpec((1, 1, ADA_TN), lambda l, j: (l, 0, j))],
        out_specs=pl.BlockSpec((1, 8, ADA_TN), lambda l, j: (l, 0, j)),
        compiler_params=_cparams(2),
        name="ada_mod",
    )(c8, ada_w, ada_b.reshape(L, 1, n3))


IN_TM = 1088
IN_TN = 1024
IN_RC = 16
IN_SECTIONS = 4


def _inproj_kernel(x_ref, nw_ref, mod_ref, w_ref, o_ref, os_ref, h_scr):
    i = pl.program_id(1)
    j = pl.program_id(2)

    @pl.when(j == 0)
    def _():
        m = mod_ref[0]
        nw = nw_ref[...]
        gain_l = nw * (1.0 + m[1:2])
        gain_c = nw * (1.0 + m[3:4])
        sec = IN_TM // IN_SECTIONS
        for c in range(IN_SECTIONS):
            for r0 in range(c * sec, (c + 1) * sec, IN_RC):
                x = x_ref[0, r0:r0 + IN_RC, :]
                ms = jnp.mean(x * x, axis=-1, keepdims=True)
                is_ctx = i * IN_TM + r0 >= SEQ
                gain = jnp.where(is_ctx, gain_c, gain_l)
                shift = jnp.where(is_ctx, m[2:3], m[0:1])
                h_scr[r0:r0 + IN_RC, :] = (x * lax.rsqrt(ms + EPS) * gain + shift).astype(BF16)
            rows = slice(c * sec, (c + 1) * sec)
            o_ref[0, rows, :] = jnp.dot(h_scr[rows, :], w_ref[...], preferred_element_type=F32).astype(BF16)

    @pl.when(j > 0)
    def _():
        y = jnp.dot(h_scr[...], w_ref[...], preferred_element_type=F32)
        o_ref[0] = y.astype(BF16)

        @pl.when(j == pl.num_programs(2) - 1)
        def _():
            os_ref[0] = y[:, IN_TN - LANES:]


def _inproj(xs, norm_w, mod4, w_main):
    B = xs.shape[0]
    return pl.pallas_call(
        _inproj_kernel,
        out_shape=(jax.ShapeDtypeStruct((B, S_ALL, D_INP), BF16),
                   jax.ShapeDtypeStruct((B, S_ALL, LANES), F32)),
        grid=(B, S_ALL // IN_TM, D_INP // IN_TN),
        in_specs=[pl.BlockSpec((1, IN_TM, D_MODEL), lambda b, i, j: (b, i, 0)),
                  pl.BlockSpec((1, D_MODEL), lambda b, i, j: (0, 0)),
                  pl.BlockSpec((1, 4, D_MODEL), lambda b, i, j: (b, 0, 0)),
                  pl.BlockSpec((D_MODEL, IN_TN), lambda b, i, j: (0, j))],
        out_specs=(pl.BlockSpec((1, IN_TM, IN_TN), lambda b, i, j: (b, i, j)),
                   pl.BlockSpec((1, IN_TM, LANES), lambda b, i, j: (b, i, 0))),
        scratch_shapes=[pltpu.VMEM((IN_TM, D_MODEL), BF16)],
        compiler_params=_cparams(3),
        name="inproj",
    )(xs, norm_w.reshape(1, D_MODEL), mod4, w_main)


OUT_TM = 544
OUT_TM_LAST = 512
OUT_SECTIONS = 2


def _gdn_gate(o_f, o_b, z, nw):
    o = o_f.astype(F32) + o_b.astype(F32)
    z = z.astype(F32)
    outs = []
    for h in range(GDN_HEADS):
        sl = slice(h * GDN_DV, (h + 1) * GDN_DV)
        oh = o[:, sl]
        y = oh * lax.rsqrt(jnp.mean(oh * oh, axis=-1, keepdims=True) + EPS) * nw
        outs.append((y * _silu(z[:, sl])).astype(BF16))
    return jnp.concatenate(outs, axis=-1)


def _ssd_gate(y_f, y_b, xs, z, d_skip, nw):
    y = y_f.astype(F32) + y_b.astype(F32) + d_skip * xs
    y = y * _silu(z.astype(F32))
    return (y * lax.rsqrt(jnp.mean(y * y, axis=-1, keepdims=True) + EPS) * nw).astype(BF16)


def _outproj_kernel(of_ref, ob_ref, gz_ref, gnw_ref, na_ref, mla_ref, yf_ref, yb_ref, sx_ref, sz_ref,
                    dsk_ref, snw_ref, w_ref, x_ref, g_ref, o_ref, *, tm):
    i = pl.program_id(1)
    g = g_ref[0]
    sec = tm // OUT_SECTIONS
    for c in range(OUT_SECTIONS):
        rows = slice(c * sec, (c + 1) * sec)
        branches = (_gdn_gate(of_ref[0, rows, :], ob_ref[0, rows, :], gz_ref[0, rows, :], gnw_ref[...]),
                    na_ref[0, rows, :], mla_ref[0, rows, :],
                    _ssd_gate(yf_ref[0, rows, :], yb_ref[0, rows, :], sx_ref[0, rows, :], sz_ref[0, rows, :],
                              dsk_ref[...], snw_ref[...]))
        y = None
        for n, a in enumerate(branches):
            t = jnp.dot(a, w_ref[n * D_BRANCH:(n + 1) * D_BRANCH, :], preferred_element_type=F32)
            y = t if y is None else y + t
        row = i * tm + c * sec + lax.broadcasted_iota(jnp.int32, (sec, 1), 0)
        gate = jnp.where(row >= SEQ, g[1:2], g[0:1])
        o_ref[0, rows, :] = x_ref[0, rows, :] + gate * y


def _outproj(gdn, na, mla, ssd, p, w_out_b, xs, gate2, last):
    B = xs.shape[0]
    o_f, o_b, g_nw = gdn
    y_f, y_b, s_x, d_skip, s_nw = ssd
    tm, rows = (OUT_TM_LAST, SEQ) if last else (OUT_TM, S_ALL)
    a_spec = pl.BlockSpec((1, tm, D_BRANCH), lambda b, i: (b, i, 0))
    z_spec = lambda name: pl.BlockSpec((1, tm, D_BRANCH), lambda b, i: (b, i, _P_START[name] // D_BRANCH))
    x_spec = pl.BlockSpec((1, tm, D_MODEL), lambda b, i: (b, i, 0))
    vec = lambda n: pl.BlockSpec((1, n), lambda b, i: (0, 0))
    return pl.pallas_call(
        functools.partial(_outproj_kernel, tm=tm),
        out_shape=jax.ShapeDtypeStruct((B, rows, D_MODEL), F32),
        grid=(B, rows // tm),
        in_specs=[a_spec, a_spec, z_spec('g_z'), vec(GDN_DV), a_spec, a_spec,
                  a_spec, a_spec, a_spec, z_spec('s_z'), vec(SSM_W), vec(SSM_W),
                  pl.BlockSpec((D_MIX, D_MODEL), lambda b, i: (0, 0)),
                  x_spec,
                  pl.BlockSpec((1, 2, D_MODEL), lambda b, i: (b, 0, 0))],
        out_specs=x_spec,
        compiler_params=_cparams(2),
        name="outproj",
    )(o_f, o_b, p, g_nw.reshape(1, GDN_DV), na, mla,
      y_f, y_b, s_x, p, jnp.repeat(d_skip, SSM_HEADDIM).reshape(1, SSM_W), s_nw.reshape(1, SSM_W),
      w_out_b, xs, gate2)


NA_RB = 4
NA_QB = NA_RB * GRID_W
NA_KW = NA_WIN_R * GRID_W
NA_NBLK = GRID_H // NA_RB
assert NA_QB == CTX_LEN


def _na_headnorm(x, w, extra):
    x = x.astype(F32)
    outs = []
    for h in range(NA_HEADS):
        xh = x[:, h * NA_DH:(h + 1) * NA_DH]
        ms = jnp.mean(xh * xh, axis=-1, keepdims=True)
        outs.append((xh * lax.rsqrt(ms + EPS) * w * extra).astype(BF16))
    return jnp.concatenate(outs, axis=-1)


def _na_fill_bias(toe_ref, bias_scr):
    for c in range(NA_WIN_R):
        for h in range(NA_HEADS):
            for i in range(0, NA_WIN_R, 2):
                pair = [toe_ref[h, i + e + NA_WIN_R - 1 - c] for e in range(2)]
                bias_scr[c, h, :, i * GRID_W:(i + 2) * GRID_W] = jnp.concatenate(pair, axis=1)


def _na_kernel(q_ref, kraw_ref, v_ref, z_ref, toe_ref, qn_ref, kn_ref, o_ref, bias_scr, k_ref):
    rb = pl.program_id(1)

    @pl.when(rb == 0)
    def _():
        def body(t, carry):
            r0 = pl.multiple_of(t * NA_QB, NA_QB)
            k_ref[pl.ds(r0, NA_QB), :] = _na_headnorm(kraw_ref[0, pl.ds(r0, NA_QB), :], kn_ref[...], 1.0)
            return carry

        lax.fori_loop(0, S_ALL // NA_QB, body, 0)
        _na_fill_bias(toe_ref, bias_scr)

    q = _na_headnorm(q_ref[0], qn_ref[...], NA_DH ** -0.5 * LOG2E)
    z = z_ref[0].astype(F32)
    kc = k_ref[SEQ:S_ALL, :]
    vc = v_ref[0, SEQ:S_ALL, :]

    def chain(rows, h, kw, vw, bias):
        sl = slice(h * NA_DH, (h + 1) * NA_DH)
        qh = q[rows, sl]
        s_c = _dot_nt(qh, kc[:, sl])
        yield
        m = jnp.max(s_c, axis=-1, keepdims=True)
        if kw is not None:
            s_w = _dot_nt(qh, kw[:, sl])
            yield
            s_w = s_w + bias[h]
            m = jnp.maximum(m, jnp.max(s_w, axis=-1, keepdims=True))
            p_w = jnp.exp2(s_w - m)
        p_c = jnp.exp2(s_c - m)
        l = jnp.sum(p_c, axis=-1, keepdims=True)
        o = jnp.dot(p_c.astype(BF16), vc[:, sl], preferred_element_type=F32)
        yield
        if kw is not None:
            l = l + jnp.sum(p_w, axis=-1, keepdims=True)
            o = o + jnp.dot(p_w.astype(BF16), vw[:, sl], preferred_element_type=F32)
            yield
        o_ref[0, rows, sl] = (o / l * _silu(z[rows, sl])).astype(BF16)

    @pl.when(rb < NA_NBLK)
    def _latent():
        chains = []
        for a in range(NA_RB):
            r = rb * NA_RB + a
            row0 = jnp.clip(r - NA_WIN_R // 2, 0, GRID_H - NA_WIN_R)
            start = pl.multiple_of(row0 * GRID_W, GRID_W)
            kw = k_ref[pl.ds(start, NA_KW), :]
            vw = v_ref[0, pl.ds(start, NA_KW), :]
            bias = bias_scr.at[r - row0]
            rows = slice(a * GRID_W, (a + 1) * GRID_W)
            chains += [chain(rows, h, kw, vw, bias) for h in range(NA_HEADS)]
        _round_robin(chains)

    @pl.when(rb == NA_NBLK)
    def _context():
        _round_robin(chain(slice(0, NA_QB), h, None, None, None) for h in range(NA_HEADS))


def _na_col_ok():
    qc = np.arange(GRID_W)[:, None]
    kc = np.arange(GRID_W)[None, :]
    win0 = np.clip(qc - NA_WIN_C // 2, 0, GRID_W - NA_WIN_C)
    return (kc >= win0) & (kc < win0 + NA_WIN_C)


def _na_bias(rpb):
    L, H = rpb.shape[:2]
    nd = 2 * NA_WIN_R - 1
    col_ok = _na_col_ok()
    left = GRID_W - NA_WIN_C
    f = jnp.pad(rpb * LOG2E, ((0, 0), (0, 0), (0, 0), (left, 2 * GRID_W - (2 * NA_WIN_C - 1) - left)))
    skew = jnp.broadcast_to(f[:, :, :, None, :], (L, H, nd, GRID_W, 2 * GRID_W))
    skew = skew.reshape(L, H, nd, -1)[..., :GRID_W * (2 * GRID_W - 1)].reshape(L, H, nd, GRID_W, 2 * GRID_W - 1)
    toe = skew[..., GRID_W - 1:]
    return jnp.where(col_ok, toe, NEG_INF)


def _na_attend(p, q_norm, k_norm, toe):
    B = p.shape[0]
    c0 = _P_START['n_qkv'] // NA_W
    zc = _P_START['n_z'] // NA_W
    blk = lambda c: pl.BlockSpec((1, NA_QB, NA_W), lambda b, r: (b, r, c))
    full = lambda c: pl.BlockSpec((1, S_ALL, NA_W), lambda b, r: (b, 0, c))
    wspec = pl.BlockSpec((1, NA_DH), lambda b, r: (0, 0))
    return pl.pallas_call(
        _na_kernel,
        out_shape=jax.ShapeDtypeStruct((B, S_ALL, NA_W), BF16),
        grid=(B, NA_NBLK + 1),
        in_specs=[blk(c0), full(c0 + 1), full(c0 + 2), blk(zc),
                  pl.BlockSpec(toe.shape, lambda b, r: (0, 0, 0, 0)), wspec, wspec],
        out_specs=blk(0),
        scratch_shapes=[pltpu.VMEM((NA_WIN_R, NA_HEADS, GRID_W, NA_KW), F32), pltpu.VMEM((S_ALL, NA_W), BF16)],
        compiler_params=_cparams(2),
        name="na_attend",
    )(p, p, p, p, toe, q_norm.reshape(1, NA_DH), k_norm.reshape(1, NA_DH))


MP_TM = 544
MLA_HW = 2 * LANES
MLA_TQ = 1024
MLA_SUBQ = 512
MLA_TK = 1024


def _rope_tables():
    n_freq = MLA_ROPE // 4
    inv_freq = ROPE_THETA ** (-np.arange(n_freq, dtype=np.float64) / n_freq)
    t = np.arange(SEQ)
    ar = (t // GRID_W)[:, None] * inv_freq
    ac = (t % GRID_W)[:, None] * inv_freq
    cos = np.concatenate([np.cos(ar), np.cos(ar), np.cos(ac), np.cos(ac)], axis=1)
    sin = np.concatenate([-np.sin(ar), np.sin(ar), -np.sin(ac), np.sin(ac)], axis=1)
    cos = np.concatenate([cos, np.ones((CTX_LEN, MLA_ROPE))], axis=0)
    sin = np.concatenate([sin, np.zeros((CTX_LEN, MLA_ROPE))], axis=0)
    return (np.tile(cos, (1, MLA_HEADS)).astype(np.float32), np.tile(sin, (1, MLA_HEADS)).astype(np.float32))


def _rope_rotate(t, cos, sin):
    w = t.shape[1]
    lane = lax.broadcasted_iota(jnp.int32, (1, w), 1)
    first = (lane & 31) < 16
    up = pltpu.roll(t, w - 16, axis=1)
    dn = pltpu.roll(t, 16, axis=1)
    return t * cos + jnp.where(first, up, dn) * sin


def _mla_prep_kernel(p_ref, cos_ref, sin_ref, qan_ref, wuq_ref, kvan_ref, wukv_ref, qn_ref, kn_ref,
                     q_out, k_out, v_out):
    x = p_ref[0].astype(F32)
    cq = x[:, :MLA_Q_RANK]
    ckv = x[:, MLA_Q_RANK:MLA_Q_RANK + MLA_KV_RANK]
    kr2 = x[:, MLA_Q_RANK + MLA_KV_RANK:]

    def rms(t, w):
        return t * lax.rsqrt(jnp.mean(t * t, axis=-1, keepdims=True) + EPS) * w

    qf = jnp.dot(rms(cq, qan_ref[...]).astype(BF16), wuq_ref[...], preferred_element_type=F32)
    kvf = jnp.dot(rms(ckv, kvan_ref[...]).astype(BF16), wukv_ref[...], preferred_element_type=F32)
    cos = cos_ref[...]
    sin = sin_ref[...]
    qw = qn_ref[...]
    kw = kn_ref[...]
    n_all = MLA_HEADS * MLA_NOPE
    lane = lax.broadcasted_iota(jnp.int32, (1, LANES), 1)
    halves = (lane < MLA_ROPE, lane >= MLA_ROPE)

    q_rope = qf[:, n_all:]
    q_rope_sq = q_rope * q_rope
    q_rot = _rope_rotate(q_rope * qw[:, n_all:], cos, sin)
    kr_sq = jnp.sum(jnp.where(halves[0], kr2 * kr2, 0.0), axis=-1, keepdims=True)
    k_rot = _rope_rotate(kr2 * kw[:, n_all:], cos[:, :LANES], sin[:, :LANES])
    for h in range(MLA_HEADS):
        half = halves[h % 2]
        vsl = slice((h // 2) * LANES, (h // 2 + 1) * LANES)
        nsl = slice(h * MLA_NOPE, (h + 1) * MLA_NOPE)
        q_nope = qf[:, nsl]
        ss = (jnp.sum(q_nope * q_nope, axis=-1, keepdims=True)
              + jnp.sum(jnp.where(half, q_rope_sq[:, vsl], 0.0), axis=-1, keepdims=True))
        r = lax.rsqrt(ss * (1.0 / MLA_QK) + EPS) * (MLA_QK ** -0.5 * LOG2E)
        q_out[0, :, h * MLA_HW:h * MLA_HW + LANES] = (q_nope * qw[:, nsl] * r).astype(BF16)
        q_out[0, :, h * MLA_HW + LANES:(h + 1) * MLA_HW] = (jnp.where(half, q_rot[:, vsl], 0.0) * r).astype(BF16)
        k_nope = kvf[:, nsl]
        ss = jnp.sum(k_nope * k_nope, axis=-1, keepdims=True) + kr_sq
        r = lax.rsqrt(ss * (1.0 / MLA_QK) + EPS)
        k_out[0, :, h * MLA_HW:h * MLA_HW + LANES] = (k_nope * kw[:, nsl] * r).astype(BF16)
        k_out[0, :, h * MLA_HW + LANES:(h + 1) * MLA_HW] = (jnp.where(half, k_rot, 0.0) * r).astype(BF16)
    v_out[0] = kvf[:, n_all:].astype(BF16)


def _mla_prep(p, cos, sin, qa_norm, w_uq, kva_norm, w_ukv, q_norm, k_norm):
    B = p.shape[0]
    H = MLA_HEADS
    uq = w_uq.reshape(MLA_Q_RANK, H, MLA_QK)
    uq = jnp.concatenate([uq[:, :, :MLA_NOPE].reshape(MLA_Q_RANK, -1),
                          uq[:, :, MLA_NOPE:].reshape(MLA_Q_RANK, -1)], axis=1).astype(BF16)
    ukv = w_ukv.reshape(MLA_KV_RANK, H, MLA_NOPE + MLA_V)
    ukv = jnp.concatenate([ukv[:, :, :MLA_NOPE].reshape(MLA_KV_RANK, -1),
                           ukv[:, :, MLA_NOPE:].reshape(MLA_KV_RANK, -1)], axis=1).astype(BF16)
    qn = jnp.concatenate([jnp.tile(q_norm[:MLA_NOPE], H), jnp.tile(q_norm[MLA_NOPE:], H)]).reshape(1, -1)
    kn = jnp.concatenate([jnp.tile(k_norm[:MLA_NOPE], H), jnp.tile(k_norm[MLA_NOPE:], 2)]).reshape(1, -1)
    pc = _P_START['m_q'] // P_MLA_BLK
    const = lambda shape: pl.BlockSpec(shape, lambda b, i: (0, 0))
    rows = lambda w: pl.BlockSpec((MP_TM, w), lambda b, i: (i, 0))
    outs = lambda w: pl.BlockSpec((1, MP_TM, w), lambda b, i: (b, i, 0))
    return pl.pallas_call(
        _mla_prep_kernel,
        out_shape=(jax.ShapeDtypeStruct((B, S_ALL, H * MLA_HW), BF16),
                   jax.ShapeDtypeStruct((B, S_ALL, H * MLA_HW), BF16),
                   jax.ShapeDtypeStruct((B, S_ALL, MLA_W), BF16)),
        grid=(B, S_ALL // MP_TM),
        in_specs=[pl.BlockSpec((1, MP_TM, P_MLA_BLK), lambda b, i: (b, i, pc)),
                  rows(H * MLA_ROPE), rows(H * MLA_ROPE),
                  const((1, MLA_Q_RANK)), const(uq.shape), const((1, MLA_KV_RANK)), const(ukv.shape),
                  const(qn.shape), const(kn.shape)],
        out_specs=(outs(H * MLA_HW), outs(H * MLA_HW), outs(MLA_W)),
        compiler_params=_cparams(2),
        name="mla_prep",
    )(p, cos, sin, qa_norm.reshape(1, -1), uq, kva_norm.reshape(1, -1), ukv, qn, kn)


def _mla_attn_kernel(q_ref, k_ref, v_ref, z_ref, o_ref, *, ctx_start, n_lat_chunks):
    tq = q_ref.shape[1]
    sub = min(tq, MLA_SUBQ)
    bounds = [(ctx_start, ctx_start + CTX_LEN)] + [(i * MLA_TK, (i + 1) * MLA_TK) for i in range(n_lat_chunks)]

    def rows_chain(r0):
        q = q_ref[0, r0:r0 + sub, :]
        m = jnp.full((sub, 1), NEG_INF, F32)
        l = jnp.zeros((sub, 1), F32)
        acc = jnp.zeros((sub, MLA_V), F32)
        for lo, hi in bounds:
            s = _dot_nt(q, k_ref[0, lo:hi, :])
            yield
            m_new = jnp.maximum(m, jnp.max(s, axis=-1, keepdims=True))
            a = jnp.exp2(m - m_new)
            p = jnp.exp2(s - m_new)
            l = a * l + jnp.sum(p, axis=-1, keepdims=True)
            acc = a * acc + jnp.dot(p.astype(BF16), v_ref[0, lo:hi, :], preferred_element_type=F32)
            m = m_new
            yield
        z = z_ref[0, r0:r0 + sub, :].astype(F32)
        o_ref[0, r0:r0 + sub, :] = (acc / l * _silu(z)).astype(BF16)

    _round_robin(rows_chain(r0) for r0 in range(0, tq, sub))


def _mla_attn_ctx_kernel(q_ref, k_ref, v_ref, z_ref, lat_ref, o_ref, **kw):
    del lat_ref
    _mla_attn_kernel(q_ref, k_ref, v_ref, z_ref, o_ref, **kw)


def _mla_attend(q, k, v, p):
    B = p.shape[0]
    H = MLA_HEADS
    zc = _P_START['m_z'] // MLA_V
    ctx_blk = SEQ // CTX_LEN

    def call(tq, q_blk0, n_q, key_rows, key_blk, kern, name, into=None):
        specs = [pl.BlockSpec((1, tq, MLA_HW), lambda b, h, i: (b, q_blk0 + i, h)),
                 pl.BlockSpec((1, key_rows, MLA_HW), lambda b, h, i: (b, key_blk, h)),
                 pl.BlockSpec((1, key_rows, MLA_V), lambda b, h, i: (b, key_blk, h)),
                 pl.BlockSpec((1, tq, MLA_V), lambda b, h, i: (b, q_blk0 + i, zc + h))]
        args = (q, k, v, p)
        if into is not None:
            specs.append(pl.BlockSpec(memory_space=pl.ANY))
            args += (into,)
        return pl.pallas_call(
            kern,
            out_shape=jax.ShapeDtypeStruct((B, S_ALL, MLA_W), BF16),
            grid=(B, H, n_q),
            in_specs=specs,
            out_specs=pl.BlockSpec((1, tq, MLA_V), lambda b, h, i: (b, q_blk0 + i, h)),
            input_output_aliases={} if into is None else {len(args) - 1: 0},
            compiler_params=_cparams(3),
            name=name,
        )(*args)

    lat = call(MLA_TQ, 0, SEQ // MLA_TQ, S_ALL, 0,
               functools.partial(_mla_attn_kernel, ctx_start=SEQ, n_lat_chunks=SEQ // MLA_TK), "mla_attend")
    return call(CTX_LEN, ctx_blk, 1, CTX_LEN, ctx_blk,
                functools.partial(_mla_attn_ctx_kernel, ctx_start=0, n_lat_chunks=0), "mla_attend_ctx", into=lat)


SCAN_TILE = 256
SCAN_NT = S_ALL // SCAN_TILE
CTX_TILE = SEQ // SCAN_TILE
HALO = 16


def _split_dot(m_bf16, x):
    hi = x.astype(BF16)
    lo = (x - hi.astype(F32)).astype(BF16)
    return (jnp.dot(m_bf16, hi, preferred_element_type=F32)
            + jnp.dot(m_bf16, lo, preferred_element_type=F32))


def _softplus(t):
    return jnp.maximum(t, 0.0) + jnp.log1p(jnp.exp(-jnp.abs(t)))


def _conv3_silu(xb, prev_row, next_row, w, bias=None):
    n = xb.shape[0]
    i = lax.broadcasted_iota(jnp.int32, (n, 1), 0)
    j = lax.broadcasted_iota(jnp.int32, (1, n), 1)
    one = lambda m: jnp.where(m, 1.0, 0.0).astype(BF16)
    xp = jnp.dot(one(i == j + 1), xb, preferred_element_type=F32)
    xn = jnp.dot(one(i + 1 == j), xb, preferred_element_type=F32)
    y = xp * w[0:1] + xb.astype(F32) * w[1:2] + xn * w[2:3]
    r8 = lax.broadcasted_iota(jnp.int32, (8, 1), 0)
    top = y[0:8] + jnp.where(r8 == 0, prev_row * w[0:1], 0.0)
    bot = y[n - 8:] + jnp.where(r8 == 7, next_row * w[2:3], 0.0)
    y = jnp.concatenate([top, y[8:n - 8], bot], axis=0)
    if bias is not None:
        y = y + bias
    return _silu(y)


def _halo_rows(i, prev_ref, next_ref):
    pv = jnp.where((i == 0) | (i == CTX_TILE), 0.0, 1.0)
    nv = jnp.where((i == CTX_TILE - 1) | (i == SCAN_NT - 1), 0.0, 1.0)
    return prev_ref[0, HALO - 1:HALO, :].astype(F32) * pv, next_ref[0, 0:1, :].astype(F32) * nv


def _halo_specs(width, col_blk):
    rb = SCAN_TILE // HALO
    nblk = S_ALL // HALO
    return [pl.BlockSpec((1, SCAN_TILE, width), lambda b, i: (b, i, col_blk)),
            pl.BlockSpec((1, HALO, width), lambda b, i: (b, jnp.maximum(i * rb - 1, 0), col_blk)),
            pl.BlockSpec((1, HALO, width), lambda b, i: (b, jnp.minimum((i + 1) * rb, nblk - 1), col_blk))]


def _fwd_tile(t):
    return jnp.where(t == 0, CTX_TILE, t - 1)


def _bwd_tile(t):
    return jnp.where(t == 0, CTX_TILE, CTX_TILE - t)


def _chunk_masks(n):
    i = lax.broadcasted_iota(jnp.int32, (n, 1), 0)
    j = lax.broadcasted_iota(jnp.int32, (1, n), 1)
    same = (i // GDN_CHUNK) == (j // GDN_CHUNK)
    return i, j, same


def _gdn_prep_kernel(x_ref, prev_ref, next_ref, s_ref, cw_ref, rate_ref, dtb_ref,
                     q_out, k_out, v_out, g_out):
    i = pl.program_id(1)
    prev_row, next_row = _halo_rows(i, prev_ref, next_ref)
    y = _conv3_silu(x_ref[0], prev_row, next_row, cw_ref[...])
    for h in range(GDN_HEADS):
        sl = slice(h * GDN_DK, (h + 1) * GDN_DK)
        qh = y[:, sl]
        q_out[0, :, sl] = (qh * lax.rsqrt(jnp.sum(qh * qh, axis=-1, keepdims=True) + EPS)
                           * (GDN_DK ** -0.5)).astype(BF16)
        kh = y[:, GDN_W + h * GDN_DK:GDN_W + (h + 1) * GDN_DK]
        k_out[0, :, sl] = (kh * lax.rsqrt(jnp.sum(kh * kh, axis=-1, keepdims=True) + EPS)).astype(BF16)
    v_out[0] = y[:, 2 * GDN_W:].astype(BF16)

    s = s_ref[0]
    lane = lax.broadcasted_iota(jnp.int32, (1, LANES), 1)
    nh2 = 2 * GDN_HEADS
    beta = jax.nn.sigmoid(s)
    g = -rate_ref[...] * _softplus(s + dtb_ref[...])
    g = jnp.where((lane >= nh2) & (lane < 2 * nh2), g, 0.0)
    ii, jj, same = _chunk_masks(SCAN_TILE)
    one = lambda m: jnp.where(m, 1.0, 0.0).astype(BF16)
    fwd_lane = lane < nh2 + GDN_HEADS
    gam = jnp.where(fwd_lane, _split_dot(one(same & (jj <= ii)), g), _split_dot(one(same & (jj >= ii)), g))
    rem = jnp.where(fwd_lane, _split_dot(one(same & (jj > ii)), g), _split_dot(one(same & (jj < ii)), g))
    cf = jnp.where(lane < nh2, beta, jnp.where(lane < 2 * nh2, gam, pltpu.roll(rem, nh2, axis=1)))
    tr = cf.T
    for h in range(GDN_HEADS):
        for r, src in enumerate((h, GDN_HEADS + h, nh2 + h, nh2 + GDN_HEADS + h,
                                 2 * nh2 + h, 2 * nh2 + GDN_HEADS + h)):
            g_out[0, h, r:r + 1, :] = tr[src:src + 1, :]
        g_out[0, h, 6:8, :] = jnp.zeros((2, SCAN_TILE), F32)


def _gdn_prep(p, ps, conv_w, A_log, dt_bias):
    B = p.shape[0]
    W3 = 3 * GDN_W
    nh2 = 2 * GDN_HEADS
    rate = jnp.zeros((1, LANES), F32).at[0, nh2:2 * nh2].set(jnp.exp(A_log).reshape(-1))
    dtb = jnp.zeros((1, LANES), F32).at[0, nh2:2 * nh2].set(dt_bias.reshape(-1))
    shp = jax.ShapeDtypeStruct((B, S_ALL, GDN_W), BF16)
    ospec = pl.BlockSpec((1, SCAN_TILE, GDN_W), lambda b, i: (b, i, 0))
    const = lambda shape: pl.BlockSpec(shape, lambda b, i: (0, 0))
    return pl.pallas_call(
        _gdn_prep_kernel,
        out_shape=(shp, shp, shp, jax.ShapeDtypeStruct((B, GDN_HEADS, 8, S_ALL), F32)),
        grid=(B, SCAN_NT),
        in_specs=_halo_specs(W3, _P_START['g_qkv'] // W3)
        + [pl.BlockSpec((1, SCAN_TILE, LANES), lambda b, i: (b, i, 0)),
           const((SHORT_CONV, W3)), const((1, LANES)), const((1, LANES))],
        out_specs=(ospec, ospec, ospec,
                   pl.BlockSpec((1, GDN_HEADS, 8, SCAN_TILE), lambda b, i: (b, 0, 0, i))),
        compiler_params=_cparams(2),
        name="gdn_prep",
    )(p, p, p, ps, conv_w, rate, dtb)


def _gdn_dir(q, k, v, gr, s_ref, o_ref, d):
    n = SCAN_TILE
    cf = jnp.concatenate([gr, jnp.zeros((LANES - 8, n), F32)], axis=0).T
    beta, gam_c = cf[:, d:d + 1], cf[:, 2 + d:3 + d]
    ecf = jnp.exp(cf)
    e_gam, e_rem = ecf[:, 2 + d:3 + d], ecf[:, 4 + d:5 + d]
    gam_r = gr[2 + d:3 + d, :]
    ii, jj, same = _chunk_masks(n)
    incl = same & ((jj <= ii) if d == 0 else (jj >= ii))
    strict = same & ((jj < ii) if d == 0 else (jj > ii))
    kk = _dot_nt(k, k)
    yield
    qk = _dot_nt(q, k)
    yield
    dec = jnp.exp(jnp.where(incl, gam_c - gam_r, NEG_INF))
    a = jnp.where(strict, beta * kk * dec, 0.0)
    qkd = (qk * dec).astype(BF16)
    kf = k.astype(F32)
    x = jnp.concatenate([v.astype(F32) * beta, kf * (beta * e_gam)], axis=1)
    pb = a.astype(BF16)
    x = x - jnp.dot(pb, x.astype(BF16), preferred_element_type=F32)
    yield
    for _ in range(5):
        pb = jnp.dot(pb, pb, preferred_element_type=F32).astype(BF16)
        yield
        x = x + jnp.dot(pb, x.astype(BF16), preferred_element_type=F32)
        yield
    u, w =x[:, :GDN_DV], x[:, GDN_DV:].astype(BF16)
    qd = (q.astype(F32) * e_gam).astype(BF16)
    kd = (kf * e_rem).astype(BF16)
    s = s_ref[...]
    nchunk = n // GDN_CHUNK
    v_new = [None] * nchunk
    qs = [None] * nchunk
    for c in (range(nchunk) if d == 0 else reversed(range(nchunk))):
        rows = slice(c * GDN_CHUNK, (c + 1) * GDN_CHUNK)
        r1 = jnp.dot(jnp.concatenate([w[rows], qd[rows]], axis=0), s.astype(BF16), preferred_element_type=F32)
        yield
        vn = u[rows] - r1[:GDN_CHUNK]
        qs[c] = r1[GDN_CHUNK:]
        v_new[c] = vn
        last = (c + 1) * GDN_CHUNK - 1 if d == 0 else c * GDN_CHUNK
        s = s * ecf[last:last + 1, 2 + d:3 + d] + _dot_tn(kd[rows], vn.astype(BF16))
        yield
    s_ref[...] = s
    vn_all = jnp.concatenate(v_new, axis=0).astype(BF16)
    o_ref[0] = (jnp.concatenate(qs, axis=0) + jnp.dot(qkd, vn_all, preferred_element_type=F32)).astype(BF16)


def _round_robin(gens):
    gens = list(gens)
    while gens:
        alive = []
        for g in gens:
            try:
                next(g)
                alive.append(g)
            except StopIteration:
                pass
        gens = alive


def _gdn_scan_kernel(qf, kf, vf, gf, qb, kb, vb, gb, of_ref, ob_ref, sf_ref, sb_ref):
    @pl.when(pl.program_id(1) == 0)
    def _():
        sf_ref[...] = jnp.zeros_like(sf_ref)
        sb_ref[...] = jnp.zeros_like(sb_ref)

    chains = []
    for h in range(GDN_HEADS):
        sl = slice(h * GDN_DK, (h + 1) * GDN_DK)
        chains.append(_gdn_dir(qf[0, :, sl], kf[0, :, sl], vf[0, :, sl], gf[0, h],
                               sf_ref.at[h], of_ref.at[:, :, sl], 0))
        chains.append(_gdn_dir(qb[0, :, sl], kb[0, :, sl], vb[0, :, sl], gb[0, h],
                               sb_ref.at[h], ob_ref.at[:, :, sl], 1))
    _round_robin(chains)


def _gdn_scan(q, k, v, g):
    B = q.shape[0]
    tok = lambda order: pl.BlockSpec((1, SCAN_TILE, GDN_W), lambda b, t: (b, order(t), 0))
    gsp = lambda order: pl.BlockSpec((1, GDN_HEADS, 8, SCAN_TILE), lambda b, t: (b, 0, 0, order(t)))
    shp = jax.ShapeDtypeStruct((B, S_ALL, GDN_W), BF16)
    f, r = _fwd_tile, _bwd_tile
    state = pltpu.VMEM((GDN_HEADS, GDN_DK, GDN_DV), F32)
    return pl.pallas_call(
        _gdn_scan_kernel,
        out_shape=(shp, shp),
        grid=(B, SCAN_NT),
        in_specs=[tok(f), tok(f), tok(f), gsp(f), tok(r), tok(r), tok(r), gsp(r)],
        out_specs=(tok(f), tok(r)),
        scratch_shapes=[state, state],
        compiler_params=_cparams(2),
        name="gdn_scan",
    )(q, k, v, g, q, k, v, g)


SSD_BC = SSM_GROUPS * SSM_STATE
SSD_HD = 2 * SSM_HEADS
SSD_GW = (SSM_HEADS // SSM_GROUPS) * SSM_HEADDIM


def _ssd_prep_kernel(x_ref, prev_ref, next_ref, s_ref, cw_ref, cb_ref, a_ref, dtb_ref,
                     xs_out, b_out, c_out, bt_out, cf_out, cr_out):
    i = pl.program_id(1)
    prev_row, next_row = _halo_rows(i, prev_ref, next_ref)
    y = _conv3_silu(x_ref[0], prev_row, next_row, cw_ref[...], cb_ref[...])
    xs_out[0] = y[:, :SSM_W]
    b_out[0] = y[:, SSM_W:SSM_W + SSD_BC].astype(BF16)
    c_out[0] = y[:, SSM_W + SSD_BC:].astype(BF16)
    bt_out[0] = y[:, SSM_W:SSM_W + SSD_BC].T.astype(BF16)

    s = s_ref[0]
    lane = lax.broadcasted_iota(jnp.int32, (1, LANES), 1)
    dt = _softplus(s + dtb_ref[...])
    on = (lane >= SSD_HD) & (lane < 2 * SSD_HD)
    a = jnp.where(on, dt * a_ref[...], 0.0)
    n = SCAN_TILE
    ii = lax.broadcasted_iota(jnp.int32, (n, 1), 0)
    jj = lax.broadcasted_iota(jnp.int32, (1, n), 1)
    one = lambda m: jnp.where(m, 1.0, 0.0).astype(BF16)
    fwd_lane = lane < SSD_HD + SSM_HEADS
    cum = jnp.where(fwd_lane, _split_dot(one(jj <= ii), a), _split_dot(one(jj >= ii), a))
    rem = jnp.where(fwd_lane, _split_dot(one(jj > ii), a), _split_dot(one(jj < ii), a))
    cf = jnp.where(lane < SSD_HD, pltpu.roll(dt, LANES - SSD_HD, axis=1),
                   jnp.where(lane < 2 * SSD_HD, cum, pltpu.roll(rem, SSD_HD, axis=1)))
    cf_out[0] = cf
    cr_out[0] = cf.T[SSD_HD:2 * SSD_HD, :]


def _ssd_prep(p, ps, conv_w, conv_b, A_log, dt_bias):
    B = p.shape[0]
    W = SSM_CONV_DIM
    a_vec = jnp.zeros((1, LANES), F32).at[0, SSD_HD:2 * SSD_HD].set(-jnp.exp(A_log).reshape(-1))
    dtb = jnp.zeros((1, LANES), F32).at[0, SSD_HD:2 * SSD_HD].set(dt_bias.reshape(-1))
    const = lambda shape: pl.BlockSpec(shape, lambda b, i: (0, 0))
    tok = lambda w: pl.BlockSpec((1, SCAN_TILE, w), lambda b, i: (b, i, 0))
    return pl.pallas_call(
        _ssd_prep_kernel,
        out_shape=(jax.ShapeDtypeStruct((B, S_ALL, SSM_W), F32),
                   jax.ShapeDtypeStruct((B, S_ALL, SSD_BC), BF16),
                   jax.ShapeDtypeStruct((B, S_ALL, SSD_BC), BF16),
                   jax.ShapeDtypeStruct((B, SSD_BC, S_ALL), BF16),
                   jax.ShapeDtypeStruct((B, S_ALL, LANES), F32),
                   jax.ShapeDtypeStruct((B, SSD_HD, S_ALL), F32)),
        grid=(B, SCAN_NT),
        in_specs=_halo_specs(W, _P_START['s_xbc'] // W)
        + [pl.BlockSpec((1, SCAN_TILE, LANES), lambda b, i: (b, i, 0)),
           const((SHORT_CONV, W)), const((1, W)), const((1, LANES)), const((1, LANES))],
        out_specs=(tok(SSM_W), tok(SSD_BC), tok(SSD_BC),
                   pl.BlockSpec((1, SSD_BC, SCAN_TILE), lambda b, i: (b, 0, i)),
                   tok(LANES),
                   pl.BlockSpec((1, SSD_HD, SCAN_TILE), lambda b, i: (b, 0, i))),
        compiler_params=_cparams(2),
        name="ssd_prep",
    )(p, p, p, ps, conv_w, conv_b.reshape(1, W), a_vec, dtb)


def _ssd_dir(x_ref, b_ref, c_ref, bt_ref, cf_ref, cr_ref, h_ref, y_ref, d, g):
    n = SCAN_TILE
    hpg = SSM_HEADS // SSM_GROUPS
    hd0 = d * SSM_HEADS + g * hpg
    gs = slice(g * SSM_STATE, (g + 1) * SSM_STATE)
    xl = slice(g * SSD_GW, (g + 1) * SSD_GW)
    cf = cf_ref[0]
    cr = cr_ref[0]
    cm = c_ref[0, :, gs]
    log_lanes = lax.broadcasted_iota(jnp.int32, (1, LANES), 1) >= SSD_HD
    ecf = jnp.exp(jnp.where(log_lanes, cf, 0.0))
    col = lambda base, h: cf[:, base + hd0 + h:base + hd0 + h + 1]
    last = n - 1 if d == 0 else 0
    src = jnp.where(log_lanes, ecf, cf)
    r = lax.broadcasted_iota(jnp.int32, (LANES, 1), 0)
    c = lax.broadcasted_iota(jnp.int32, (1, 3 * SSD_GW), 1)
    want = (c // SSD_GW) * SSD_HD + hd0 + (c % SSD_GW) // SSM_HEADDIM
    sel = jnp.where(r == want, 1.0, 0.0).astype(BF16)
    hi = src.astype(BF16)
    lo_part = (src - hi.astype(F32)).astype(BF16)
    spread = (jnp.dot(hi, sel, preferred_element_type=F32) + jnp.dot(lo_part, sel, preferred_element_type=F32))
    yield
    dt_x, ecum_x, erem_x = spread[:, :SSD_GW], spread[:, SSD_GW:2 * SSD_GW], spread[:, 2 * SSD_GW:]
    xdt = x_ref[0, :, xl] * dt_x
    xdt_b = xdt.astype(BF16)
    xdec = (xdt * erem_x).astype(BF16)
    cb = _dot_nt(cm, b_ref[0, :, gs])
    yield
    h_prev = h_ref[g]
    y_off = jnp.dot(cm, h_prev.astype(BF16), preferred_element_type=F32)
    yield
    y_off = y_off * ecum_x
    h_ref[g] = h_prev * ecum_x[last:last + 1, :] + jnp.dot(bt_ref[0, gs, :], xdec, preferred_element_type=F32)
    yield
    ii = lax.broadcasted_iota(jnp.int32, (n, 1), 0)
    jj = lax.broadcasted_iota(jnp.int32, (1, n), 1)
    causal = (jj <= ii) if d == 0 else (jj >= ii)
    lo = lax.broadcasted_iota(jnp.int32, (1, LANES), 1) < SSM_HEADDIM
    pair_out = []
    for j in range(hpg // 2):
        ys = []
        for e in range(2):
            h = 2 * j + e
            seg = col(SSD_HD, h) - cr[hd0 + h:hd0 + h + 1, :]
            sc = (cb * jnp.exp(jnp.where(causal, seg, NEG_INF))).astype(BF16)
            ys.append(jnp.dot(sc, xdt_b[:, j * LANES:(j + 1) * LANES], preferred_element_type=F32))
            yield
        pair_out.append(jnp.where(lo, ys[0], ys[1]))
    y_ref[0, :, xl] = (jnp.concatenate(pair_out, axis=1) + y_off).astype(BF16)


def _ssd_scan_kernel(xf, bf, cf_, btf, colf, rowf, xb, bb, cb_, btb, colb, rowb, yf_ref, yb_ref, hf_ref, hb_ref):
    @pl.when(pl.program_id(1) == 0)
    def _():
        hf_ref[...] = jnp.zeros_like(hf_ref)
        hb_ref[...] = jnp.zeros_like(hb_ref)

    chains = []
    for g in range(SSM_GROUPS):
        chains.append(_ssd_dir(xf, bf, cf_, btf, colf, rowf, hf_ref, yf_ref, 0, g))
        chains.append(_ssd_dir(xb, bb, cb_, btb, colb, rowb, hb_ref, yb_ref, 1, g))
    _round_robin(chains)


def _ssd_scan(xs, bm, cm, bt, cf, cr):
    B = xs.shape[0]

    def specs(order):
        tok = lambda w: pl.BlockSpec((1, SCAN_TILE, w), lambda b, t: (b, order(t), 0))
        return [tok(SSM_W), tok(SSD_BC), tok(SSD_BC),
                pl.BlockSpec((1, SSD_BC, SCAN_TILE), lambda b, t: (b, 0, order(t))),
                tok(LANES),
                pl.BlockSpec((1, SSD_HD, SCAN_TILE), lambda b, t: (b, 0, order(t)))]

    shp = jax.ShapeDtypeStruct((B, S_ALL, SSM_W), BF16)
    out = lambda order: pl.BlockSpec((1, SCAN_TILE, SSM_W), lambda b, t: (b, order(t), 0))
    hshape = pltpu.VMEM((SSM_GROUPS, SSM_STATE, SSD_GW), F32)
    args = (xs, bm, cm, bt, cf, cr)
    return pl.pallas_call(
        _ssd_scan_kernel,
        out_shape=(shp, shp),
        grid=(B, SCAN_NT),
        in_specs=specs(_fwd_tile) + specs(_bwd_tile),
        out_specs=(out(_fwd_tile), out(_bwd_tile)),
        scratch_shapes=[hshape, hshape],
        compiler_params=_cparams(2),
        name="ssd_scan",
    )(*args, *args)


def _join_kernel(x_ref, c_ref, o_ref):
    i = pl.program_id(1)

    @pl.when(i < CTX_TILE)
    def _():
        o_ref[...] = x_ref[...]

    @pl.when(i == CTX_TILE)
    def _():
        o_ref[...] = c_ref[...]


def _join(x, ctx):
    B = x.shape[0]
    blk = lambda index: pl.BlockSpec((1, SCAN_TILE, D_MODEL), index)
    return pl.pallas_call(
        _join_kernel,
        out_shape=jax.ShapeDtypeStruct((B, S_ALL, D_MODEL), x.dtype),
        grid=(B, SCAN_NT),
        in_specs=[blk(lambda b, i: (b, jnp.minimum(i, CTX_TILE - 1), 0)), blk(lambda b, i: (b, 0, 0))],
        out_specs=blk(lambda b, i: (b, i, 0)),
        compiler_params=_cparams(2),
        name="join_tokens",
    )(x, ctx)


def _repack_w_in(w):
    cut = lambda names: [w[:, _IN_START[n]:_IN_START[n] + _IN_SIZE[n]].astype(BF16) for n in names]
    zeros = lambda n: jnp.zeros((w.shape[0], n), BF16)
    n_small = sum(_IN_SIZE[n] for n in _S_ORDER)
    cols = cut(_P_ORDER) + [zeros(D_INP - LANES - _off)] + cut(_S_ORDER) + [zeros(LANES - n_small)]
    return jnp.concatenate(cols, axis=1)


def kernel(x, c, ctx, c_ctx, norm_w, ada_w, ada_b, w_in, gdn_conv_w, gdn_A_log, gdn_dt_bias, gdn_norm_w, na_q_norm, na_k_norm, na_rpb, mla_qa_norm, mla_w_uq, mla_kva_norm, mla_w_ukv, mla_q_norm, mla_k_norm, ssm_conv_w, ssm_conv_b, ssm_A_log, ssm_dt_bias, ssm_D, ssm_norm_w, w_out):
    B = x.shape[0]
    xs = _join(x, ctx)
    c8 = jnp.zeros((8, D_MODEL), F32).at[:B].set(c).at[B].set(c_ctx)
    mods = _ada_all(c8, ada_w, ada_b)
    cos_np, sin_np = _rope_tables()
    cos, sin = jnp.asarray(cos_np), jnp.asarray(sin_np)
    na_bias = _na_bias(na_rpb)
    for l in range(DEPTH):
        shift, scale, gate = jnp.split(mods[l, :B], 3, axis=-1)
        shift_c, scale_c, gate_c = jnp.split(mods[l, B], 3, axis=-1)
        bc = lambda v: jnp.broadcast_to(v[None], (B, D_MODEL))
        mod4 = jnp.stack([shift, scale, bc(shift_c), bc(scale_c)], axis=1)
        gate2 = jnp.stack([gate, bc(gate_c)], axis=1)
        p, ps = _inproj(xs, norm_w[l], mod4, _repack_w_in(w_in[l]))

        gq, gk, gv, gg = _gdn_prep(p, ps, gdn_conv_w[l], gdn_A_log[l], gdn_dt_bias[l])
        o_f, o_b = _gdn_scan(gq, gk, gv, gg)

        ob = _na_attend(p, na_q_norm[l], na_k_norm[l], na_bias[l])

        mq, mk, mv = _mla_prep(p, cos, sin, mla_qa_norm[l], mla_w_uq[l], mla_kva_norm[l], mla_w_ukv[l],
                               mla_q_norm[l], mla_k_norm[l])
        oc = _mla_attend(mq, mk, mv, p)

        sx, sb, sc, sbt, scf, scr = _ssd_prep(p, ps, ssm_conv_w[l], ssm_conv_b[l], ssm_A_log[l], ssm_dt_bias[l])
        y_f, y_b = _ssd_scan(sx, sb, sc, sbt, scf, scr)
        xs = _outproj((o_f, o_b, gdn_norm_w[l]), ob, oc, (y_f, y_b, sx, ssm_D[l], ssm_norm_w[l]), p,
                      w_out[l].astype(BF16), xs, gate2, last=(l == DEPTH - 1))
    return xs
```

```python
import functools

import jax
import jax.numpy as jnp
import numpy as np
from jax import lax
from jax.experimental import pallas as pl
from jax.experimental.pallas import tpu as pltpu

F32 = jnp.float32
BF16 = jnp.bfloat16

D_MODEL = 2048
SEQ = 4096
DEPTH = 4
GRID_W = 64
GRID_H = SEQ // GRID_W
CTX_LEN = 256
S_ALL = SEQ + CTX_LEN
EPS = 1e-6
NEG_INF = -1e30
LOG2E = 1.4426950408889634

D_BRANCH = 512
D_MIX = 4 * D_BRANCH
SHORT_CONV = 3

GDN_HEADS = 4
GDN_DK = 128
GDN_DV = 128
GDN_W = GDN_HEADS * GDN_DV
GDN_CHUNK = 64

NA_HEADS = 4
NA_DH = 128
NA_W = NA_HEADS * NA_DH
NA_WIN_R = 8
NA_WIN_C = 16

MLA_HEADS = 4
MLA_Q_RANK = 384
MLA_KV_RANK = 256
MLA_NOPE = 128
MLA_ROPE = 64
MLA_QK = MLA_NOPE + MLA_ROPE
MLA_V = 128
MLA_W = MLA_HEADS * MLA_V
ROPE_THETA = 10000.0

SSM_HEADDIM = 64
SSM_HEADS = D_BRANCH // SSM_HEADDIM
SSM_W = SSM_HEADS * SSM_HEADDIM
SSM_GROUPS = 2
SSM_STATE = 128
SSM_CONV_DIM = SSM_W + 2 * SSM_GROUPS * SSM_STATE

IN_SIZES = (3 * GDN_W, GDN_W, 2 * GDN_HEADS, 2 * GDN_HEADS,
            3 * NA_W, NA_W,
            MLA_Q_RANK, MLA_KV_RANK, MLA_ROPE, MLA_W,
            SSM_W, SSM_CONV_DIM, 2 * SSM_HEADS)
D_IN = sum(IN_SIZES)
_IN_NAMES = ('g_qkv', 'g_z', 'g_beta', 'g_alpha', 'n_qkv', 'n_z',
             'm_q', 'm_kv', 'm_kr', 'm_z', 's_z', 's_xbc', 's_dt')
_IN_START = dict(zip(_IN_NAMES, np.cumsum((0,) + IN_SIZES[:-1]).tolist()))
_IN_SIZE = dict(zip(_IN_NAMES, IN_SIZES))

LANES = 128
_P_ORDER = ('g_qkv', 'g_z', 'n_qkv', 'n_z', 'm_z', 's_z', 's_xbc',
            'm_q', 'm_kv', 'm_kr', 'm_kr')
_S_ORDER = ('g_beta', 'g_alpha', 's_dt')
_P_START = {}
_off = 0
for _n in _P_ORDER:
    _P_START.setdefault(_n, _off)
    _off += _IN_SIZE[_n]
MXU_N = 256
D_INP = -(-_off // (2 * MXU_N)) * (2 * MXU_N)
P_MLA_BLK = MLA_Q_RANK + MLA_KV_RANK + 2 * MLA_ROPE
assert _P_START['m_q'] % P_MLA_BLK == 0 and D_INP % LANES == 0

V7X_VMEM_BYTES = 64 * 1024 * 1024
VMEM_LIMIT = V7X_VMEM_BYTES - 12 * 1024 * 1024


def _silu(x):
    h = 0.5 * x
    return h + h * jnp.tanh(h)


def _dot_nt(a, b):
    return lax.dot_general(a, b, (((1,), (1,)), ((), ())), preferred_element_type=F32)


def _dot_tn(a, b):
    return lax.dot_general(a, b, (((0,), (0,)), ((), ())), preferred_element_type=F32)


def _cparams(n_axes):
    return pltpu.CompilerParams(dimension_semantics=("arbitrary",) * n_axes,
                                vmem_limit_bytes=VMEM_LIMIT)


ADA_TN = 1536


def _ada_kernel(c_ref, w_ref, b_ref, o_ref):
    a = _silu(c_ref[...]).astype(BF16)
    o_ref[0] = jnp.dot(a, w_ref[0].astype(BF16), preferred_element_type=F32) + b_ref[0]


def _ada_all(c8, ada_w, ada_b):
    L = ada_w.shape[0]
    n3 = ada_w.shape[2]
    return pl.pallas_call(
        _ada_kernel,
        out_shape=jax.ShapeDtypeStruct((L, 8, n3), F32),
        grid=(L, n3 // ADA_TN),
        in_specs=[pl.BlockSpec((8, D_MODEL), lambda l, j: (0, 0)),
                  pl.BlockSpec((1, D_MODEL, ADA_TN), lambda l, j: (l, 0, j)),
                  pl.BlockSpec((1, 1, ADA_TN), lambda l, j: (l, 0, j))],
        out_specs=pl.BlockSpec((1, 8, ADA_TN), lambda l, j: (l, 0, j)),
        compiler_params=_cparams(2),
        name="ada_mod",
    )(c8, ada_w, ada_b.reshape(L, 1, n3))


IN_TM = 1088
IN_TN = 1024
IN_RC = 16
IN_SECTIONS = 4


def _inproj_kernel(x_ref, nw_ref, mod_ref, w_ref, o_ref, os_ref, h_scr):
    i = pl.program_id(1)
    j = pl.program_id(2)

    @pl.when(j == 0)
    def _():
        m = mod_ref[0]
        nw = nw_ref[...]
        gain_l = nw * (1.0 + m[1:2])
        gain_c = nw * (1.0 + m[3:4])
        sec = IN_TM // IN_SECTIONS
        for c in range(IN_SECTIONS):
            for r0 in range(c * sec, (c + 1) * sec, IN_RC):
                x = x_ref[0, r0:r0 + IN_RC, :]
                ms = jnp.mean(x * x, axis=-1, keepdims=True)
                is_ctx = i * IN_TM + r0 >= SEQ
                gain = jnp.where(is_ctx, gain_c, gain_l)
                shift = jnp.where(is_ctx, m[2:3], m[0:1])
                h_scr[r0:r0 + IN_RC, :] = (x * lax.rsqrt(ms + EPS) * gain + shift).astype(BF16)
            rows = slice(c * sec, (c + 1) * sec)
            o_ref[0, rows, :] = jnp.dot(h_scr[rows, :], w_ref[...], preferred_element_type=F32).astype(BF16)

    @pl.when(j > 0)
    def _():
        y = jnp.dot(h_scr[...], w_ref[...], preferred_element_type=F32)
        o_ref[0] = y.astype(BF16)

        @pl.when(j == pl.num_programs(2) - 1)
        def _():
            os_ref[0] = y[:, IN_TN - LANES:]


def _inproj(xs, norm_w, mod4, w_main):
    B = xs.shape[0]
    return pl.pallas_call(
        _inproj_kernel,
        out_shape=(jax.ShapeDtypeStruct((B, S_ALL, D_INP), BF16),
                   jax.ShapeDtypeStruct((B, S_ALL, LANES), F32)),
        grid=(B, S_ALL // IN_TM, D_INP // IN_TN),
        in_specs=[pl.BlockSpec((1, IN_TM, D_MODEL), lambda b, i, j: (b, i, 0)),
                  pl.BlockSpec((1, D_MODEL), lambda b, i, j: (0, 0)),
                  pl.BlockSpec((1, 4, D_MODEL), lambda b, i, j: (b, 0, 0)),
                  pl.BlockSpec((D_MODEL, IN_TN), lambda b, i, j: (0, j))],
        out_specs=(pl.BlockSpec((1, IN_TM, IN_TN), lambda b, i, j: (b, i, j)),
                   pl.BlockSpec((1, IN_TM, LANES), lambda b, i, j: (b, i, 0))),
        scratch_shapes=[pltpu.VMEM((IN_TM, D_MODEL), BF16)],
        compiler_params=_cparams(3),
        name="inproj",
    )(xs, norm_w.reshape(1, D_MODEL), mod4, w_main)


OUT_TM = 544
OUT_TM_LAST = 512
OUT_SECTIONS = 2


def _gdn_gate(o_f, o_b, z, nw):
    o = o_f.astype(F32) + o_b.astype(F32)
    z = z.astype(F32)
    outs = []
    for h in range(GDN_HEADS):
        sl = slice(h * GDN_DV, (h + 1) * GDN_DV)
        oh = o[:, sl]
        y = oh * lax.rsqrt(jnp.mean(oh * oh, axis=-1, keepdims=True) + EPS) * nw
        outs.append((y * _silu(z[:, sl])).astype(BF16))
    return jnp.concatenate(outs, axis=-1)


def _ssd_gate(y_f, y_b, xs, z, d_skip, nw):
    y = y_f.astype(F32) + y_b.astype(F32) + d_skip * xs
    y = y * _silu(z.astype(F32))
    return (y * lax.rsqrt(jnp.mean(y * y, axis=-1, keepdims=True) + EPS) * nw).astype(BF16)


def _outproj_kernel(of_ref, ob_ref, gz_ref, gnw_ref, na_ref, mla_ref, yf_ref, yb_ref, sx_ref, sz_ref,
                    dsk_ref, snw_ref, w_ref, x_ref, g_ref, o_ref, *, tm):
    i = pl.program_id(1)
    g = g_ref[0]
    sec = tm // OUT_SECTIONS
    for c in range(OUT_SECTIONS):
        rows = slice(c * sec, (c + 1) * sec)
        branches = (_gdn_gate(of_ref[0, rows, :], ob_ref[0, rows, :], gz_ref[0, rows, :], gnw_ref[...]),
                    na_ref[0, rows, :], mla_ref[0, rows, :],
                    _ssd_gate(yf_ref[0, rows, :], yb_ref[0, rows, :], sx_ref[0, rows, :], sz_ref[0, rows, :],
                              dsk_ref[...], snw_ref[...]))
        y = None
        for n, a in enumerate(branches):
            t = jnp.dot(a, w_ref[n * D_BRANCH:(n + 1) * D_BRANCH, :], preferred_element_type=F32)
            y = t if y is None else y + t
        row = i * tm + c * sec + lax.broadcasted_iota(jnp.int32, (sec, 1), 0)
        gate = jnp.where(row >= SEQ, g[1:2], g[0:1])
        o_ref[0, rows, :] = x_ref[0, rows, :] + gate * y


def _outproj(gdn, na, mla, ssd, p, w_out_b, xs, gate2, last):
    B = xs.shape[0]
    o_f, o_b, g_nw = gdn
    y_f, y_b, s_x, d_skip, s_nw = ssd
    tm, rows = (OUT_TM_LAST, SEQ) if last else (OUT_TM, S_ALL)
    a_spec = pl.BlockSpec((1, tm, D_BRANCH), lambda b, i: (b, i, 0))
    z_spec = lambda name: pl.BlockSpec((1, tm, D_BRANCH), lambda b, i: (b, i, _P_START[name] // D_BRANCH))
    x_spec = pl.BlockSpec((1, tm, D_MODEL), lambda b, i: (b, i, 0))
    vec = lambda n: pl.BlockSpec((1, n), lambda b, i: (0, 0))
    return pl.pallas_call(
        functools.partial(_outproj_kernel, tm=tm),
        out_shape=jax.ShapeDtypeStruct((B, rows, D_MODEL), F32),
        grid=(B, rows // tm),
        in_specs=[a_spec, a_spec, z_spec('g_z'), vec(GDN_DV), a_spec, a_spec,
                  a_spec, a_spec, a_spec, z_spec('s_z'), vec(SSM_W), vec(SSM_W),
                  pl.BlockSpec((D_MIX, D_MODEL), lambda b, i: (0, 0)),
                  x_spec,
                  pl.BlockSpec((1, 2, D_MODEL), lambda b, i: (b, 0, 0))],
        out_specs=x_spec,
        compiler_params=_cparams(2),
        name="outproj",
    )(o_f, o_b, p, g_nw.reshape(1, GDN_DV), na, mla,
      y_f, y_b, s_x, p, jnp.repeat(d_skip, SSM_HEADDIM).reshape(1, SSM_W), s_nw.reshape(1, SSM_W),
      w_out_b, xs, gate2)


NA_RB = 4
NA_QB = NA_RB * GRID_W
NA_KW = NA_WIN_R * GRID_W
NA_NBLK = GRID_H // NA_RB
assert NA_QB == CTX_LEN


def _na_headnorm(x, w, extra):
    x = x.astype(F32)
    outs = []
    for h in range(NA_HEADS):
        xh = x[:, h * NA_DH:(h + 1) * NA_DH]
        ms = jnp.mean(xh * xh, axis=-1, keepdims=True)
        outs.append((xh * lax.rsqrt(ms + EPS) * w * extra).astype(BF16))
    return jnp.concatenate(outs, axis=-1)


def _na_fill_bias(toe_ref, bias_scr):
    for c in range(NA_WIN_R):
        for h in range(NA_HEADS):
            for i in range(0, NA_WIN_R, 2):
                pair = [toe_ref[h, i + e + NA_WIN_R - 1 - c] for e in range(2)]
                bias_scr[c, h, :, i * GRID_W:(i + 2) * GRID_W] = jnp.concatenate(pair, axis=1)


def _na_kernel(q_ref, kraw_ref, v_ref, z_ref, toe_ref, qn_ref, kn_ref, o_ref, bias_scr, k_ref):
    rb = pl.program_id(1)

    @pl.when(rb == 0)
    def _():
        def body(t, carry):
            r0 = pl.multiple_of(t * NA_QB, NA_QB)
            k_ref[pl.ds(r0, NA_QB), :] = _na_headnorm(kraw_ref[0, pl.ds(r0, NA_QB), :], kn_ref[...], 1.0)
            return carry

        lax.fori_loop(0, S_ALL // NA_QB, body, 0)
        _na_fill_bias(toe_ref, bias_scr)

    q = _na_headnorm(q_ref[0], qn_ref[...], NA_DH ** -0.5 * LOG2E)
    z = z_ref[0].astype(F32)
    kc = k_ref[SEQ:S_ALL, :]
    vc = v_ref[0, SEQ:S_ALL, :]

    def chain(rows, h, kw, vw, bias):
        sl = slice(h * NA_DH, (h + 1) * NA_DH)
        qh = q[rows, sl]
        s_c = _dot_nt(qh, kc[:, sl])
        yield
        m = jnp.max(s_c, axis=-1, keepdims=True)
        if kw is not None:
            s_w = _dot_nt(qh, kw[:, sl])
            yield
            s_w = s_w + bias[h]
            m = jnp.maximum(m, jnp.max(s_w, axis=-1, keepdims=True))
            p_w = jnp.exp2(s_w - m)
        p_c = jnp.exp2(s_c - m)
        l = jnp.sum(p_c, axis=-1, keepdims=True)
        o = jnp.dot(p_c.astype(BF16), vc[:, sl], preferred_element_type=F32)
        yield
        if kw is not None:
            l = l + jnp.sum(p_w, axis=-1, keepdims=True)
            o = o + jnp.dot(p_w.astype(BF16), vw[:, sl], preferred_element_type=F32)
            yield
        o_ref[0, rows, sl] = (o / l * _silu(z[rows, sl])).astype(BF16)

    @pl.when(rb < NA_NBLK)
    def _latent():
        chains = []
        for a in range(NA_RB):
            r = rb * NA_RB + a
            row0 = jnp.clip(r - NA_WIN_R // 2, 0, GRID_H - NA_WIN_R)
            start = pl.multiple_of(row0 * GRID_W, GRID_W)
            kw = k_ref[pl.ds(start, NA_KW), :]
            vw = v_ref[0, pl.ds(start, NA_KW), :]
            bias = bias_scr.at[r - row0]
            rows = slice(a * GRID_W, (a + 1) * GRID_W)
            chains += [chain(rows, h, kw, vw, bias) for h in range(NA_HEADS)]
        _round_robin(chains)

    @pl.when(rb == NA_NBLK)
    def _context():
        _round_robin(chain(slice(0, NA_QB), h, None, None, None) for h in range(NA_HEADS))


def _na_col_ok():
    qc = np.arange(GRID_W)[:, None]
    kc = np.arange(GRID_W)[None, :]
    win0 = np.clip(qc - NA_WIN_C // 2, 0, GRID_W - NA_WIN_C)
    return (kc >= win0) & (kc < win0 + NA_WIN_C)


def _na_bias(rpb):
    L, H = rpb.shape[:2]
    nd = 2 * NA_WIN_R - 1
    col_ok = _na_col_ok()
    left = GRID_W - NA_WIN_C
    f = jnp.pad(rpb * LOG2E, ((0, 0), (0, 0), (0, 0), (left, 2 * GRID_W - (2 * NA_WIN_C - 1) - left)))
    skew = jnp.broadcast_to(f[:, :, :, None, :], (L, H, nd, GRID_W, 2 * GRID_W))
    skew = skew.reshape(L, H, nd, -1)[..., :GRID_W * (2 * GRID_W - 1)].reshape(L, H, nd, GRID_W, 2 * GRID_W - 1)
    toe = skew[..., GRID_W - 1:]
    return jnp.where(col_ok, toe, NEG_INF)


def _na_attend(p, q_norm, k_norm, toe):
    B = p.shape[0]
    c0 = _P_START['n_qkv'] // NA_W
    zc = _P_START['n_z'] // NA_W
    blk = lambda c: pl.BlockSpec((1, NA_QB, NA_W), lambda b, r: (b, r, c))
    full = lambda c: pl.BlockSpec((1, S_ALL, NA_W), lambda b, r: (b, 0, c))
    wspec = pl.BlockSpec((1, NA_DH), lambda b, r: (0, 0))
    return pl.pallas_call(
        _na_kernel,
        out_shape=jax.ShapeDtypeStruct((B, S_ALL, NA_W), BF16),
        grid=(B, NA_NBLK + 1),
        in_specs=[blk(c0), full(c0 + 1), full(c0 + 2), blk(zc),
                  pl.BlockSpec(toe.shape, lambda b, r: (0, 0, 0, 0)), wspec, wspec],
        out_specs=blk(0),
        scratch_shapes=[pltpu.VMEM((NA_WIN_R, NA_HEADS, GRID_W, NA_KW), F32), pltpu.VMEM((S_ALL, NA_W), BF16)],
        compiler_params=_cparams(2),
        name="na_attend",
    )(p, p, p, p, toe, q_norm.reshape(1, NA_DH), k_norm.reshape(1, NA_DH))


MP_TM = 544
MLA_HW = 2 * LANES
MLA_TQ = 1024
MLA_SUBQ = 512
MLA_TK = 1024


def _rope_tables():
    n_freq = MLA_ROPE // 4
    inv_freq = ROPE_THETA ** (-np.arange(n_freq, dtype=np.float64) / n_freq)
    t = np.arange(SEQ)
    ar = (t // GRID_W)[:, None] * inv_freq
    ac = (t % GRID_W)[:, None] * inv_freq
    cos = np.concatenate([np.cos(ar), np.cos(ar), np.cos(ac), np.cos(ac)], axis=1)
    sin = np.concatenate([-np.sin(ar), np.sin(ar), -np.sin(ac), np.sin(ac)], axis=1)
    cos = np.concatenate([cos, np.ones((CTX_LEN, MLA_ROPE))], axis=0)
    sin = np.concatenate([sin, np.zeros((CTX_LEN, MLA_ROPE))], axis=0)
    return (np.tile(cos, (1, MLA_HEADS)).astype(np.float32), np.tile(sin, (1, MLA_HEADS)).astype(np.float32))


def _rope_rotate(t, cos, sin):
    w = t.shape[1]
    lane = lax.broadcasted_iota(jnp.int32, (1, w), 1)
    first = (lane & 31) < 16
    up = pltpu.roll(t, w - 16, axis=1)
    dn = pltpu.roll(t, 16, axis=1)
    return t * cos + jnp.where(first, up, dn) * sin


def _mla_prep_kernel(p_ref, cos_ref, sin_ref, qan_ref, wuq_ref, kvan_ref, wukv_ref, qn_ref, kn_ref,
                     q_out, k_out, v_out):
    x = p_ref[0].astype(F32)
    cq = x[:, :MLA_Q_RANK]
    ckv = x[:, MLA_Q_RANK:MLA_Q_RANK + MLA_KV_RANK]
    kr2 = x[:, MLA_Q_RANK + MLA_KV_RANK:]

    def rms(t, w):
        return t * lax.rsqrt(jnp.mean(t * t, axis=-1, keepdims=True) + EPS) * w

    qf = jnp.dot(rms(cq, qan_ref[...]).astype(BF16), wuq_ref[...], preferred_element_type=F32)
    kvf = jnp.dot(rms(ckv, kvan_ref[...]).astype(BF16), wukv_ref[...], preferred_element_type=F32)
    cos = cos_ref[...]
    sin = sin_ref[...]
    qw = qn_ref[...]
    kw = kn_ref[...]
    n_all = MLA_HEADS * MLA_NOPE
    lane = lax.broadcasted_iota(jnp.int32, (1, LANES), 1)
    halves = (lane < MLA_ROPE, lane >= MLA_ROPE)

    q_rope = qf[:, n_all:]
    q_rope_sq = q_rope * q_rope
    q_rot = _rope_rotate(q_rope * qw[:, n_all:], cos, sin)
    kr_sq = jnp.sum(jnp.where(halves[0], kr2 * kr2, 0.0), axis=-1, keepdims=True)
    k_rot = _rope_rotate(kr2 * kw[:, n_all:], cos[:, :LANES], sin[:, :LANES])
    for h in range(MLA_HEADS):
        half = halves[h % 2]
        vsl = slice((h // 2) * LANES, (h // 2 + 1) * LANES)
        nsl = slice(h * MLA_NOPE, (h + 1) * MLA_NOPE)
        q_nope = qf[:, nsl]
        ss = (jnp.sum(q_nope * q_nope, axis=-1, keepdims=True)
              + jnp.sum(jnp.where(half, q_rope_sq[:, vsl], 0.0), axis=-1, keepdims=True))
        r = lax.rsqrt(ss * (1.0 / MLA_QK) + EPS) * (MLA_QK ** -0.5 * LOG2E)
        q_out[0, :, h * MLA_HW:h * MLA_HW + LANES] = (q_nope * qw[:, nsl] * r).astype(BF16)
        q_out[0, :, h * MLA_HW + LANES:(h + 1) * MLA_HW] = (jnp.where(half, q_rot[:, vsl], 0.0) * r).astype(BF16)
        k_nope = kvf[:, nsl]
        ss = jnp.sum(k_nope * k_nope, axis=-1, keepdims=True) + kr_sq
        r = lax.rsqrt(ss * (1.0 / MLA_QK) + EPS)
        k_out[0, :, h * MLA_HW:h * MLA_HW + LANES] = (k_nope * kw[:, nsl] * r).astype(BF16)
        k_out[0, :, h * MLA_HW + LANES:(h + 1) * MLA_HW] = (jnp.where(half, k_rot, 0.0) * r).astype(BF16)
    v_out[0] = kvf[:, n_all:].astype(BF16)


def _mla_prep(p, cos, sin, qa_norm, w_uq, kva_norm, w_ukv, q_norm, k_norm):
    B = p.shape[0]
    H = MLA_HEADS
    uq = w_uq.reshape(MLA_Q_RANK, H, MLA_QK)
    uq = jnp.concatenate([uq[:, :, :MLA_NOPE].reshape(MLA_Q_RANK, -1),
                          uq[:, :, MLA_NOPE:].reshape(MLA_Q_RANK, -1)], axis=1).astype(BF16)
    ukv = w_ukv.reshape(MLA_KV_RANK, H, MLA_NOPE + MLA_V)
    ukv = jnp.concatenate([ukv[:, :, :MLA_NOPE].reshape(MLA_KV_RANK, -1),
                           ukv[:, :, MLA_NOPE:].reshape(MLA_KV_RANK, -1)], axis=1).astype(BF16)
    qn = jnp.concatenate([jnp.tile(q_norm[:MLA_NOPE], H), jnp.tile(q_norm[MLA_NOPE:], H)]).reshape(1, -1)
    kn = jnp.concatenate([jnp.tile(k_norm[:MLA_NOPE], H), jnp.tile(k_norm[MLA_NOPE:], 2)]).reshape(1, -1)
    pc = _P_START['m_q'] // P_MLA_BLK
    const = lambda shape: pl.BlockSpec(shape, lambda b, i: (0, 0))
    rows = lambda w: pl.BlockSpec((MP_TM, w), lambda b, i: (i, 0))
    outs = lambda w: pl.BlockSpec((1, MP_TM, w), lambda b, i: (b, i, 0))
    return pl.pallas_call(
        _mla_prep_kernel,
        out_shape=(jax.ShapeDtypeStruct((B, S_ALL, H * MLA_HW), BF16),
                   jax.ShapeDtypeStruct((B, S_ALL, H * MLA_HW), BF16),
                   jax.ShapeDtypeStruct((B, S_ALL, MLA_W), BF16)),
        grid=(B, S_ALL // MP_TM),
        in_specs=[pl.BlockSpec((1, MP_TM, P_MLA_BLK), lambda b, i: (b, i, pc)),
                  rows(H * MLA_ROPE), rows(H * MLA_ROPE),
                  const((1, MLA_Q_RANK)), const(uq.shape), const((1, MLA_KV_RANK)), const(ukv.shape),
                  const(qn.shape), const(kn.shape)],
        out_specs=(outs(H * MLA_HW), outs(H * MLA_HW), outs(MLA_W)),
        compiler_params=_cparams(2),
        name="mla_prep",
    )(p, cos, sin, qa_norm.reshape(1, -1), uq, kva_norm.reshape(1, -1), ukv, qn, kn)


def _mla_attn_kernel(q_ref, k_ref, v_ref, z_ref, o_ref, *, ctx_start, n_lat_chunks):
    tq = q_ref.shape[1]
    sub = min(tq, MLA_SUBQ)
    bounds = [(ctx_start, ctx_start + CTX_LEN)] + [(i * MLA_TK, (i + 1) * MLA_TK) for i in range(n_lat_chunks)]

    def rows_chain(r0):
        q = q_ref[0, r0:r0 + sub, :]
        m = jnp.full((sub, 1), NEG_INF, F32)
        l = jnp.zeros((sub, 1), F32)
        acc = jnp.zeros((sub, MLA_V), F32)
        for lo, hi in bounds:
            s = _dot_nt(q, k_ref[0, lo:hi, :])
            yield
            m_new = jnp.maximum(m, jnp.max(s, axis=-1, keepdims=True))
            a = jnp.exp2(m - m_new)
            p = jnp.exp2(s - m_new)
            l = a * l + jnp.sum(p, axis=-1, keepdims=True)
            acc = a * acc + jnp.dot(p.astype(BF16), v_ref[0, lo:hi, :], preferred_element_type=F32)
            m = m_new
            yield
        z = z_ref[0, r0:r0 + sub, :].astype(F32)
        o_ref[0, r0:r0 + sub, :] = (acc / l * _silu(z)).astype(BF16)

    _round_robin(rows_chain(r0) for r0 in range(0, tq, sub))


def _mla_attn_ctx_kernel(q_ref, k_ref, v_ref, z_ref, lat_ref, o_ref, **kw):
    del lat_ref
    _mla_attn_kernel(q_ref, k_ref, v_ref, z_ref, o_ref, **kw)


def _mla_attend(q, k, v, p):
    B = p.shape[0]
    H = MLA_HEADS
    zc = _P_START['m_z'] // MLA_V
    ctx_blk = SEQ // CTX_LEN

    def call(tq, q_blk0, n_q, key_rows, key_blk, kern, name, into=None):
        specs = [pl.BlockSpec((1, tq, MLA_HW), lambda b, h, i: (b, q_blk0 + i, h)),
                 pl.BlockSpec((1, key_rows, MLA_HW), lambda b, h, i: (b, key_blk, h)),
                 pl.BlockSpec((1, key_rows, MLA_V), lambda b, h, i: (b, key_blk, h)),
                 pl.BlockSpec((1, tq, MLA_V), lambda b, h, i: (b, q_blk0 + i, zc + h))]
        args = (q, k, v, p)
        if into is not None:
            specs.append(pl.BlockSpec(memory_space=pl.ANY))
            args += (into,)
        return pl.pallas_call(
            kern,
            out_shape=jax.ShapeDtypeStruct((B, S_ALL, MLA_W), BF16),
            grid=(B, H, n_q),
            in_specs=specs,
            out_specs=pl.BlockSpec((1, tq, MLA_V), lambda b, h, i: (b, q_blk0 + i, h)),
            input_output_aliases={} if into is None else {len(args) - 1: 0},
            compiler_params=_cparams(3),
            name=name,
        )(*args)

    lat = call(MLA_TQ, 0, SEQ // MLA_TQ, S_ALL, 0,
               functools.partial(_mla_attn_kernel, ctx_start=SEQ, n_lat_chunks=SEQ // MLA_TK), "mla_attend")
    return call(CTX_LEN, ctx_blk, 1, CTX_LEN, ctx_blk,
                functools.partial(_mla_attn_ctx_kernel, ctx_start=0, n_lat_chunks=0), "mla_attend_ctx", into=lat)


SCAN_TILE = 256
SCAN_NT = S_ALL // SCAN_TILE
CTX_TILE = SEQ // SCAN_TILE
HALO = 16


def _split_dot(m_bf16, x):
    hi = x.astype(BF16)
    lo = (x - hi.astype(F32)).astype(BF16)
    return (jnp.dot(m_bf16, hi, preferred_element_type=F32)
            + jnp.dot(m_bf16, lo, preferred_element_type=F32))


def _softplus(t):
    return jnp.maximum(t, 0.0) + jnp.log1p(jnp.exp(-jnp.abs(t)))


def _conv3_silu(xb, prev_row, next_row, w, bias=None):
    n = xb.shape[0]
    i = lax.broadcasted_iota(jnp.int32, (n, 1), 0)
    j = lax.broadcasted_iota(jnp.int32, (1, n), 1)
    one = lambda m: jnp.where(m, 1.0, 0.0).astype(BF16)
    xp = jnp.dot(one(i == j + 1), xb, preferred_element_type=F32)
    xn = jnp.dot(one(i + 1 == j), xb, preferred_element_type=F32)
    y = xp * w[0:1] + xb.astype(F32) * w[1:2] + xn * w[2:3]
    r8 = lax.broadcasted_iota(jnp.int32, (8, 1), 0)
    top = y[0:8] + jnp.where(r8 == 0, prev_row * w[0:1], 0.0)
    bot = y[n - 8:] + jnp.where(r8 == 7, next_row * w[2:3], 0.0)
    y = jnp.concatenate([top, y[8:n - 8], bot], axis=0)
    if bias is not None:
        y = y + bias
    return _silu(y)


def _halo_rows(i, prev_ref, next_ref):
    pv = jnp.where((i == 0) | (i == CTX_TILE), 0.0, 1.0)
    nv = jnp.where((i == CTX_TILE - 1) | (i == SCAN_NT - 1), 0.0, 1.0)
    return prev_ref[0, HALO - 1:HALO, :].astype(F32) * pv, next_ref[0, 0:1, :].astype(F32) * nv


def _halo_specs(width, col_blk):
    rb = SCAN_TILE // HALO
    nblk = S_ALL // HALO
    return [pl.BlockSpec((1, SCAN_TILE, width), lambda b, i: (b, i, col_blk)),
            pl.BlockSpec((1, HALO, width), lambda b, i: (b, jnp.maximum(i * rb - 1, 0), col_blk)),
            pl.BlockSpec((1, HALO, width), lambda b, i: (b, jnp.minimum((i + 1) * rb, nblk - 1), col_blk))]


def _fwd_tile(t):
    return jnp.where(t == 0, CTX_TILE, t - 1)


def _bwd_tile(t):
    return jnp.where(t == 0, CTX_TILE, CTX_TILE - t)


def _chunk_masks(n):
    i = lax.broadcasted_iota(jnp.int32, (n, 1), 0)
    j = lax.broadcasted_iota(jnp.int32, (1, n), 1)
    same = (i // GDN_CHUNK) == (j // GDN_CHUNK)
    return i, j, same


def _gdn_prep_kernel(x_ref, prev_ref, next_ref, s_ref, cw_ref, rate_ref, dtb_ref,
                     q_out, k_out, v_out, g_out):
    i = pl.program_id(1)
    prev_row, next_row = _halo_rows(i, prev_ref, next_ref)
    y = _conv3_silu(x_ref[0], prev_row, next_row, cw_ref[...])
    for h in range(GDN_HEADS):
        sl = slice(h * GDN_DK, (h + 1) * GDN_DK)
        qh = y[:, sl]
        q_out[0, :, sl] = (qh * lax.rsqrt(jnp.sum(qh * qh, axis=-1, keepdims=True) + EPS)
                           * (GDN_DK ** -0.5)).astype(BF16)
        kh = y[:, GDN_W + h * GDN_DK:GDN_W + (h + 1) * GDN_DK]
        k_out[0, :, sl] = (kh * lax.rsqrt(jnp.sum(kh * kh, axis=-1, keepdims=True) + EPS)).astype(BF16)
    v_out[0] = y[:, 2 * GDN_W:].astype(BF16)

    s = s_ref[0]
    lane = lax.broadcasted_iota(jnp.int32, (1, LANES), 1)
    nh2 = 2 * GDN_HEADS
    beta = jax.nn.sigmoid(s)
    g = -rate_ref[...] * _softplus(s + dtb_ref[...])
    g = jnp.where((lane >= nh2) & (lane < 2 * nh2), g, 0.0)
    ii, jj, same = _chunk_masks(SCAN_TILE)
    one = lambda m: jnp.where(m, 1.0, 0.0).astype(BF16)
    fwd_lane = lane < nh2 + GDN_HEADS
    gam = jnp.where(fwd_lane, _split_dot(one(same & (jj <= ii)), g), _split_dot(one(same & (jj >= ii)), g))
    rem = jnp.where(fwd_lane, _split_dot(one(same & (jj > ii)), g), _split_dot(one(same & (jj < ii)), g))
    cf = jnp.where(lane < nh2, beta, jnp.where(lane < 2 * nh2, gam, pltpu.roll(rem, nh2, axis=1)))
    tr = cf.T
    for h in range(GDN_HEADS):
        for r, src in enumerate((h, GDN_HEADS + h, nh2 + h, nh2 + GDN_HEADS + h,
                                 2 * nh2 + h, 2 * nh2 + GDN_HEADS + h)):
            g_out[0, h, r:r + 1, :] = tr[src:src + 1, :]
        g_out[0, h, 6:8, :] = jnp.zeros((2, SCAN_TILE), F32)


def _gdn_prep(p, ps, conv_w, A_log, dt_bias):
    B = p.shape[0]
    W3 = 3 * GDN_W
    nh2 = 2 * GDN_HEADS
    rate = jnp.zeros((1, LANES), F32).at[0, nh2:2 * nh2].set(jnp.exp(A_log).reshape(-1))
    dtb = jnp.zeros((1, LANES), F32).at[0, nh2:2 * nh2].set(dt_bias.reshape(-1))
    shp = jax.ShapeDtypeStruct((B, S_ALL, GDN_W), BF16)
    ospec = pl.BlockSpec((1, SCAN_TILE, GDN_W), lambda b, i: (b, i, 0))
    const = lambda shape: pl.BlockSpec(shape, lambda b, i: (0, 0))
    return pl.pallas_call(
        _gdn_prep_kernel,
        out_shape=(shp, shp, shp, jax.ShapeDtypeStruct((B, GDN_HEADS, 8, S_ALL), F32)),
        grid=(B, SCAN_NT),
        in_specs=_halo_specs(W3, _P_START['g_qkv'] // W3)
        + [pl.BlockSpec((1, SCAN_TILE, LANES), lambda b, i: (b, i, 0)),
           const((SHORT_CONV, W3)), const((1, LANES)), const((1, LANES))],
        out_specs=(ospec, ospec, ospec,
                   pl.BlockSpec((1, GDN_HEADS, 8, SCAN_TILE), lambda b, i: (b, 0, 0, i))),
        compiler_params=_cparams(2),
        name="gdn_prep",
    )(p, p, p, ps, conv_w, rate, dtb)


def _gdn_dir(q, k, v, gr, s_ref, o_ref, d):
    n = SCAN_TILE
    cf = jnp.concatenate([gr, jnp.zeros((LANES - 8, n), F32)], axis=0).T
    beta, gam_c = cf[:, d:d + 1], cf[:, 2 + d:3 + d]
    ecf = jnp.exp(cf)
    e_gam, e_rem = ecf[:, 2 + d:3 + d], ecf[:, 4 + d:5 + d]
    gam_r = gr[2 + d:3 + d, :]
    ii, jj, same = _chunk_masks(n)
    incl = same & ((jj <= ii) if d == 0 else (jj >= ii))
    strict = same & ((jj < ii) if d == 0 else (jj > ii))
    kk = _dot_nt(k, k)
    yield
    qk = _dot_nt(q, k)
    yield
    dec = jnp.exp(jnp.where(incl, gam_c - gam_r, NEG_INF))
    a = jnp.where(strict, beta * kk * dec, 0.0)
    qkd = (qk * dec).astype(BF16)
    kf = k.astype(F32)
    x = jnp.concatenate([v.astype(F32) * beta, kf * (beta * e_gam)], axis=1)
    blk = lambda size: (ii // size) == (jj // size)
    inner = blk(2)
    t = jnp.where(ii == jj, 1.0, 0.0) - jnp.where(inner, a, 0.0)
    for size in (4, 8, 16, 32, GDN_CHUNK):
        outer = blk(size)
        e = jnp.where(outer & ~inner, a, 0.0)
        inner = outer
        et = jnp.dot(e.astype(BF16), t.astype(BF16), preferred_element_type=F32)
        yield
        t = t - jnp.dot(t.astype(BF16), et.astype(BF16), preferred_element_type=F32)
        yield
    x = jnp.dot(t.astype(BF16), x.astype(BF16), preferred_element_type=F32)
    yield
    u, w = x[:, :GDN_DV], x[:, GDN_DV:].astype(BF16)
    qd = (q.astype(F32) * e_gam).astype(BF16)
    kd = (kf * e_rem).astype(BF16)
    s = s_ref[...]
    nchunk = n // GDN_CHUNK
    v_new = [None] * nchunk
    qs = [None] * nchunk
    for c in (range(nchunk) if d == 0 else reversed(range(nchunk))):
        rows = slice(c * GDN_CHUNK, (c + 1) * GDN_CHUNK)
        r1 = jnp.dot(jnp.concatenate([w[rows], qd[rows]], axis=0), s.astype(BF16), preferred_element_type=F32)
        yield
        vn = u[rows] - r1[:GDN_CHUNK]
        qs[c] = r1[GDN_CHUNK:]
        v_new[c] = vn
        last = (c + 1) * GDN_CHUNK - 1 if d == 0 else c * GDN_CHUNK
        s = s * ecf[last:last + 1, 2 + d:3 + d] + _dot_tn(kd[rows], vn.astype(BF16))
        yield
    s_ref[...] = s
    vn_all = jnp.concatenate(v_new, axis=0).astype(BF16)
    o_ref[0] = (jnp.concatenate(qs, axis=0) + jnp.dot(qkd, vn_all, preferred_element_type=F32)).astype(BF16)


def _round_robin(gens):
    gens = list(gens)
    while gens:
        alive = []
        for g in gens:
            try:
                next(g)
                alive.append(g)
            except StopIteration:
                pass
        gens = alive


def _gdn_scan_kernel(qf, kf, vf, gf, qb, kb, vb, gb, of_ref, ob_ref, sf_ref, sb_ref):
    @pl.when(pl.program_id(1) == 0)
    def _():
        sf_ref[...] = jnp.zeros_like(sf_ref)
        sb_ref[...] = jnp.zeros_like(sb_ref)

    chains = []
    for h in range(GDN_HEADS):
        sl = slice(h * GDN_DK, (h + 1) * GDN_DK)
        chains.append(_gdn_dir(qf[0, :, sl], kf[0, :, sl], vf[0, :, sl], gf[0, h],
                               sf_ref.at[h], of_ref.at[:, :, sl], 0))
        chains.append(_gdn_dir(qb[0, :, sl], kb[0, :, sl], vb[0, :, sl], gb[0, h],
                               sb_ref.at[h], ob_ref.at[:, :, sl], 1))
    _round_robin(chains)


def _gdn_scan(q, k, v, g):
    B = q.shape[0]
    tok = lambda order: pl.BlockSpec((1, SCAN_TILE, GDN_W), lambda b, t: (b, order(t), 0))
    gsp = lambda order: pl.BlockSpec((1, GDN_HEADS, 8, SCAN_TILE), lambda b, t: (b, 0, 0, order(t)))
    shp = jax.ShapeDtypeStruct((B, S_ALL, GDN_W), BF16)
    f, r = _fwd_tile, _bwd_tile
    state = pltpu.VMEM((GDN_HEADS, GDN_DK, GDN_DV), F32)
    return pl.pallas_call(
        _gdn_scan_kernel,
        out_shape=(shp, shp),
        grid=(B, SCAN_NT),
        in_specs=[tok(f), tok(f), tok(f), gsp(f), tok(r), tok(r), tok(r), gsp(r)],
        out_specs=(tok(f), tok(r)),
        scratch_shapes=[state, state],
        compiler_params=_cparams(2),
        name="gdn_scan",
    )(q, k, v, g, q, k, v, g)


SSD_BC = SSM_GROUPS * SSM_STATE
SSD_HD = 2 * SSM_HEADS
SSD_GW = (SSM_HEADS // SSM_GROUPS) * SSM_HEADDIM


def _ssd_prep_kernel(x_ref, prev_ref, next_ref, s_ref, cw_ref, cb_ref, a_ref, dtb_ref,
                     xs_out, b_out, c_out, bt_out, cf_out, cr_out):
    i = pl.program_id(1)
    prev_row, next_row = _halo_rows(i, prev_ref, next_ref)
    y = _conv3_silu(x_ref[0], prev_row, next_row, cw_ref[...], cb_ref[...])
    xs_out[0] = y[:, :SSM_W]
    b_out[0] = y[:, SSM_W:SSM_W + SSD_BC].astype(BF16)
    c_out[0] = y[:, SSM_W + SSD_BC:].astype(BF16)
    bt_out[0] = y[:, SSM_W:SSM_W + SSD_BC].T.astype(BF16)

    s = s_ref[0]
    lane = lax.broadcasted_iota(jnp.int32, (1, LANES), 1)
    dt = _softplus(s + dtb_ref[...])
    on = (lane >= SSD_HD) & (lane < 2 * SSD_HD)
    a = jnp.where(on, dt * a_ref[...], 0.0)
    n = SCAN_TILE
    ii = lax.broadcasted_iota(jnp.int32, (n, 1), 0)
    jj = lax.broadcasted_iota(jnp.int32, (1, n), 1)
    one = lambda m: jnp.where(m, 1.0, 0.0).astype(BF16)
    fwd_lane = lane < SSD_HD + SSM_HEADS
    cum = jnp.where(fwd_lane, _split_dot(one(jj <= ii), a), _split_dot(one(jj >= ii), a))
    rem = jnp.where(fwd_lane, _split_dot(one(jj > ii), a), _split_dot(one(jj < ii), a))
    cf = jnp.where(lane < SSD_HD, pltpu.roll(dt, LANES - SSD_HD, axis=1),
                   jnp.where(lane < 2 * SSD_HD, cum, pltpu.roll(rem, SSD_HD, axis=1)))
    cf_out[0] = cf
    cr_out[0] = cf.T[SSD_HD:2 * SSD_HD, :]


def _ssd_prep(p, ps, conv_w, conv_b, A_log, dt_bias):
    B = p.shape[0]
    W = SSM_CONV_DIM
    a_vec = jnp.zeros((1, LANES), F32).at[0, SSD_HD:2 * SSD_HD].set(-jnp.exp(A_log).reshape(-1))
    dtb = jnp.zeros((1, LANES), F32).at[0, SSD_HD:2 * SSD_HD].set(dt_bias.reshape(-1))
    const = lambda shape: pl.BlockSpec(shape, lambda b, i: (0, 0))
    tok = lambda w: pl.BlockSpec((1, SCAN_TILE, w), lambda b, i: (b, i, 0))
    return pl.pallas_call(
        _ssd_prep_kernel,
        out_shape=(jax.ShapeDtypeStruct((B, S_ALL, SSM_W), F32),
                   jax.ShapeDtypeStruct((B, S_ALL, SSD_BC), BF16),
                   jax.ShapeDtypeStruct((B, S_ALL, SSD_BC), BF16),
                   jax.ShapeDtypeStruct((B, SSD_BC, S_ALL), BF16),
                   jax.ShapeDtypeStruct((B, S_ALL, LANES), F32),
                   jax.ShapeDtypeStruct((B, SSD_HD, S_ALL), F32)),
        grid=(B, SCAN_NT),
        in_specs=_halo_specs(W, _P_START['s_xbc'] // W)
        + [pl.BlockSpec((1, SCAN_TILE, LANES), lambda b, i: (b, i, 0)),
           const((SHORT_CONV, W)), const((1, W)), const((1, LANES)), const((1, LANES))],
        out_specs=(tok(SSM_W), tok(SSD_BC), tok(SSD_BC),
                   pl.BlockSpec((1, SSD_BC, SCAN_TILE), lambda b, i: (b, 0, i)),
                   tok(LANES),
                   pl.BlockSpec((1, SSD_HD, SCAN_TILE), lambda b, i: (b, 0, i))),
        compiler_params=_cparams(2),
        name="ssd_prep",
    )(p, p, p, ps, conv_w, conv_b.reshape(1, W), a_vec, dtb)


def _ssd_dir(x_ref, b_ref, c_ref, bt_ref, cf_ref, cr_ref, h_ref, y_ref, d, g):
    n = SCAN_TILE
    hpg = SSM_HEADS // SSM_GROUPS
    hd0 = d * SSM_HEADS + g * hpg
    gs = slice(g * SSM_STATE, (g + 1) * SSM_STATE)
    xl = slice(g * SSD_GW, (g + 1) * SSD_GW)
    cf = cf_ref[0]
    cr = cr_ref[0]
    cm = c_ref[0, :, gs]
    log_lanes = lax.broadcasted_iota(jnp.int32, (1, LANES), 1) >= SSD_HD
    ecf = jnp.exp(jnp.where(log_lanes, cf, 0.0))
    col = lambda base, h: cf[:, base + hd0 + h:base + hd0 + h + 1]
    last = n - 1 if d == 0 else 0
    src = jnp.where(log_lanes, ecf, cf)
    r = lax.broadcasted_iota(jnp.int32, (LANES, 1), 0)
    c = lax.broadcasted_iota(jnp.int32, (1, 3 * SSD_GW), 1)
    want = (c // SSD_GW) * SSD_HD + hd0 + (c % SSD_GW) // SSM_HEADDIM
    sel = jnp.where(r == want, 1.0, 0.0).astype(BF16)
    hi = src.astype(BF16)
    lo_part = (src - hi.astype(F32)).astype(BF16)
    spread = (jnp.dot(hi, sel, preferred_element_type=F32) + jnp.dot(lo_part, sel, preferred_element_type=F32))
    yield
    dt_x, ecum_x, erem_x = spread[:, :SSD_GW], spread[:, SSD_GW:2 * SSD_GW], spread[:, 2 * SSD_GW:]
    xdt = x_ref[0, :, xl] * dt_x
    xdt_b = xdt.astype(BF16)
    xdec = (xdt * erem_x).astype(BF16)
    cb = _dot_nt(cm, b_ref[0, :, gs])
    yield
    h_prev = h_ref[g]
    y_off = jnp.dot(cm, h_prev.astype(BF16), preferred_element_type=F32)
    yield
    y_off = y_off * ecum_x
    h_ref[g] = h_prev * ecum_x[last:last + 1, :] + jnp.dot(bt_ref[0, gs, :], xdec, preferred_element_type=F32)
    yield
    ii = lax.broadcasted_iota(jnp.int32, (n, 1), 0)
    jj = lax.broadcasted_iota(jnp.int32, (1, n), 1)
    causal = (jj <= ii) if d == 0 else (jj >= ii)
    lo = lax.broadcasted_iota(jnp.int32, (1, LANES), 1) < SSM_HEADDIM
    pair_out = []
    for j in range(hpg // 2):
        ys = []
        for e in range(2):
            h = 2 * j + e
            seg = col(SSD_HD, h) - cr[hd0 + h:hd0 + h + 1, :]
            sc = (cb * jnp.exp(jnp.where(causal, seg, NEG_INF))).astype(BF16)
            ys.append(jnp.dot(sc, xdt_b[:, j * LANES:(j + 1) * LANES], preferred_element_type=F32))
            yield
        pair_out.append(jnp.where(lo, ys[0], ys[1]))
    y_ref[0, :, xl] = (jnp.concatenate(pair_out, axis=1) + y_off).astype(BF16)


def _ssd_scan_kernel(xf, bf, cf_, btf, colf, rowf, xb, bb, cb_, btb, colb, rowb, yf_ref, yb_ref, hf_ref, hb_ref):
    @pl.when(pl.program_id(1) == 0)
    def _():
        hf_ref[...] = jnp.zeros_like(hf_ref)
        hb_ref[...] = jnp.zeros_like(hb_ref)

    chains = []
    for g in range(SSM_GROUPS):
        chains.append(_ssd_dir(xf, bf, cf_, btf, colf, rowf, hf_ref, yf_ref, 0, g))
        chains.append(_ssd_dir(xb, bb, cb_, btb, colb, rowb, hb_ref, yb_ref, 1, g))
    _round_robin(chains)


def _ssd_scan(xs, bm, cm, bt, cf, cr):
    B = xs.shape[0]

    def specs(order):
        tok = lambda w: pl.BlockSpec((1, SCAN_TILE, w), lambda b, t: (b, order(t), 0))
        return [tok(SSM_W), tok(SSD_BC), tok(SSD_BC),
                pl.BlockSpec((1, SSD_BC, SCAN_TILE), lambda b, t: (b, 0, order(t))),
                tok(LANES),
                pl.BlockSpec((1, SSD_HD, SCAN_TILE), lambda b, t: (b, 0, order(t)))]

    shp = jax.ShapeDtypeStruct((B, S_ALL, SSM_W), BF16)
    out = lambda order: pl.BlockSpec((1, SCAN_TILE, SSM_W), lambda b, t: (b, order(t), 0))
    hshape = pltpu.VMEM((SSM_GROUPS, SSM_STATE, SSD_GW), F32)
    args = (xs, bm, cm, bt, cf, cr)
    return pl.pallas_call(
        _ssd_scan_kernel,
        out_shape=(shp, shp),
        grid=(B, SCAN_NT),
        in_specs=specs(_fwd_tile) + specs(_bwd_tile),
        out_specs=(out(_fwd_tile), out(_bwd_tile)),
        scratch_shapes=[hshape, hshape],
        compiler_params=_cparams(2),
        name="ssd_scan",
    )(*args, *args)


def _join_kernel(x_ref, c_ref, o_ref):
    i = pl.program_id(1)

    @pl.when(i < CTX_TILE)
    def _():
        o_ref[...] = x_ref[...]

    @pl.when(i == CTX_TILE)
    def _():
        o_ref[...] = c_ref[...]


def _join(x, ctx):
    B = x.shape[0]
    blk = lambda index: pl.BlockSpec((1, SCAN_TILE, D_MODEL), index)
    return pl.pallas_call(
        _join_kernel,
        out_shape=jax.ShapeDtypeStruct((B, S_ALL, D_MODEL), x.dtype),
        grid=(B, SCAN_NT),
        in_specs=[blk(lambda b, i: (b, jnp.minimum(i, CTX_TILE - 1), 0)), blk(lambda b, i: (b, 0, 0))],
        out_specs=blk(lambda b, i: (b, i, 0)),
        compiler_params=_cparams(2),
        name="join_tokens",
    )(x, ctx)


def _repack_w_in(w):
    cut = lambda names: [w[:, _IN_START[n]:_IN_START[n] + _IN_SIZE[n]].astype(BF16) for n in names]
    zeros = lambda n: jnp.zeros((w.shape[0], n), BF16)
    n_small = sum(_IN_SIZE[n] for n in _S_ORDER)
    cols = cut(_P_ORDER) + [zeros(D_INP - LANES - _off)] + cut(_S_ORDER) + [zeros(LANES - n_small)]
    return jnp.concatenate(cols, axis=1)


def kernel(x, c, ctx, c_ctx, norm_w, ada_w, ada_b, w_in, gdn_conv_w, gdn_A_log, gdn_dt_bias, gdn_norm_w, na_q_norm, na_k_norm, na_rpb, mla_qa_norm, mla_w_uq, mla_kva_norm, mla_w_ukv, mla_q_norm, mla_k_norm, ssm_conv_w, ssm_conv_b, ssm_A_log, ssm_dt_bias, ssm_D, ssm_norm_w, w_out):
    B = x.shape[0]
    xs = _join(x, ctx)
    c8 = jnp.zeros((8, D_MODEL), F32).at[:B].set(c).at[B].set(c_ctx)
    mods = _ada_all(c8, ada_w, ada_b)
    cos_np, sin_np = _rope_tables()
    cos, sin = jnp.asarray(cos_np), jnp.asarray(sin_np)
    na_bias = _na_bias(na_rpb)
    for l in range(DEPTH):
        shift, scale, gate = jnp.split(mods[l, :B], 3, axis=-1)
        shift_c, scale_c, gate_c = jnp.split(mods[l, B], 3, axis=-1)
        bc = lambda v: jnp.broadcast_to(v[None], (B, D_MODEL))
        mod4 = jnp.stack([shift, scale, bc(shift_c), bc(scale_c)], axis=1)
        gate2 = jnp.stack([gate, bc(gate_c)], axis=1)
        p, ps = _inproj(xs, norm_w[l], mod4, _repack_w_in(w_in[l]))

        gq, gk, gv, gg = _gdn_prep(p, ps, gdn_conv_w[l], gdn_A_log[l], gdn_dt_bias[l])
        o_f, o_b = _gdn_scan(gq, gk, gv, gg)

        ob = _na_attend(p, na_q_norm[l], na_k_norm[l], na_bias[l])

        mq, mk, mv = _mla_prep(p, cos, sin, mla_qa_norm[l], mla_w_uq[l], mla_kva_norm[l], mla_w_ukv[l],
                               mla_q_norm[l], mla_k_norm[l])
        oc = _mla_attend(mq, mk, mv, p)

        sx, sb, sc, sbt, scf, scr = _ssd_prep(p, ps, ssm_conv_w[l], ssm_conv_b[l], ssm_A_log[l], ssm_dt_bias[l])
        y_f, y_b = _ssd_scan(sx, sb, sc, sbt, scf, scr)
        xs = _outproj((o_f, o_b, gdn_norm_w[l]), ob, oc, (y_f, y_b, sx, ssm_D[l], ssm_norm_w[l]), p,
                      w_out[l].astype(BF16), xs, gate2, last=(l == DEPTH - 1))
    return xs
```

```python
import functools

import jax
import jax.numpy as jnp
import numpy as np
from jax import lax
from jax.experimental import pallas as pl
from jax.experimental.pallas import tpu as pltpu

F32 = jnp.float32
BF16 = jnp.bfloat16

D_MODEL = 2048
SEQ = 4096
DEPTH = 4
GRID_W = 64
GRID_H = SEQ // GRID_W
CTX_LEN = 256
S_ALL = SEQ + CTX_LEN
EPS = 1e-6
NEG_INF = -1e30
LOG2E = 1.4426950408889634

D_BRANCH = 512
D_MIX = 4 * D_BRANCH
SHORT_CONV = 3

GDN_HEADS = 4
GDN_DK = 128
GDN_DV = 128
GDN_W = GDN_HEADS * GDN_DV
GDN_CHUNK = 64

NA_HEADS = 4
NA_DH = 128
NA_W = NA_HEADS * NA_DH
NA_WIN_R = 8
NA_WIN_C = 16

MLA_HEADS = 4
MLA_Q_RANK = 384
MLA_KV_RANK = 256
MLA_NOPE = 128
MLA_ROPE = 64
MLA_QK = MLA_NOPE + MLA_ROPE
MLA_V = 128
MLA_W = MLA_HEADS * MLA_V
ROPE_THETA = 10000.0

SSM_HEADDIM = 64
SSM_HEADS = D_BRANCH // SSM_HEADDIM
SSM_W = SSM_HEADS * SSM_HEADDIM
SSM_GROUPS = 2
SSM_STATE = 128
SSM_CONV_DIM = SSM_W + 2 * SSM_GROUPS * SSM_STATE

IN_SIZES = (3 * GDN_W, GDN_W, 2 * GDN_HEADS, 2 * GDN_HEADS,
            3 * NA_W, NA_W,
            MLA_Q_RANK, MLA_KV_RANK, MLA_ROPE, MLA_W,
            SSM_W, SSM_CONV_DIM, 2 * SSM_HEADS)
D_IN = sum(IN_SIZES)
_IN_NAMES = ('g_qkv', 'g_z', 'g_beta', 'g_alpha', 'n_qkv', 'n_z',
             'm_q', 'm_kv', 'm_kr', 'm_z', 's_z', 's_xbc', 's_dt')
_IN_START = dict(zip(_IN_NAMES, np.cumsum((0,) + IN_SIZES[:-1]).tolist()))
_IN_SIZE = dict(zip(_IN_NAMES, IN_SIZES))

LANES = 128
_P_ORDER = ('g_qkv', 'g_z', 'n_qkv', 'n_z', 'm_z', 's_z', 's_xbc',
            'm_q', 'm_kv', 'm_kr', 'm_kr')
_S_ORDER = ('g_beta', 'g_alpha', 's_dt')
_P_START = {}
_off = 0
for _n in _P_ORDER:
    _P_START.setdefault(_n, _off)
    _off += _IN_SIZE[_n]
MXU_N = 256
D_INP = -(-_off // (2 * MXU_N)) * (2 * MXU_N)
P_MLA_BLK = MLA_Q_RANK + MLA_KV_RANK + 2 * MLA_ROPE
assert _P_START['m_q'] % P_MLA_BLK == 0 and D_INP % LANES == 0

V7X_VMEM_BYTES = 64 * 1024 * 1024
VMEM_LIMIT = V7X_VMEM_BYTES - 12 * 1024 * 1024


def _silu(x):
    h = 0.5 * x
    return h + h * jnp.tanh(h)


def _dot_nt(a, b):
    return lax.dot_general(a, b, (((1,), (1,)), ((), ())), preferred_element_type=F32)


def _dot_tn(a, b):
    return lax.dot_general(a, b, (((0,), (0,)), ((), ())), preferred_element_type=F32)


def _cparams(n_axes):
    return pltpu.CompilerParams(dimension_semantics=("arbitrary",) * n_axes,
                                vmem_limit_bytes=VMEM_LIMIT)


ADA_TN = 1536


def _ada_kernel(c_ref, w_ref, b_ref, o_ref):
    a = _silu(c_ref[...]).astype(BF16)
    o_ref[0] = jnp.dot(a, w_ref[0].astype(BF16), preferred_element_type=F32) + b_ref[0]


def _ada_all(c8, ada_w, ada_b):
    L = ada_w.shape[0]
    n3 = ada_w.shape[2]
    return pl.pallas_call(
        _ada_kernel,
        out_shape=jax.ShapeDtypeStruct((L, 8, n3), F32),
        grid=(L, n3 // ADA_TN),
        in_specs=[pl.BlockSpec((8, D_MODEL), lambda l, j: (0, 0)),
                  pl.BlockSpec((1, D_MODEL, ADA_TN), lambda l, j: (l, 0, j)),
                  pl.BlockSpec((1, 1, ADA_TN), lambda l, j: (l, 0, j))],
        out_specs=pl.BlockSpec((1, 8, ADA_TN), lambda l, j: (l, 0, j)),
        compiler_params=_cparams(2),
        name="ada_mod",
    )(c8, ada_w, ada_b.reshape(L, 1, n3))


IN_TM = 1088
IN_TN = 1024
IN_RC = 16
IN_SECTIONS = 4


def _inproj_kernel(x_ref, nw_ref, mod_ref, w_ref, o_ref, os_ref, h_scr):
    i = pl.program_id(1)
    j = pl.program_id(2)

    @pl.when(j == 0)
    def _():
        m = mod_ref[0]
        nw = nw_ref[...]
        gain_l = nw * (1.0 + m[1:2])
        gain_c = nw * (1.0 + m[3:4])
        sec = IN_TM // IN_SECTIONS
        for c in range(IN_SECTIONS):
            for r0 in range(c * sec, (c + 1) * sec, IN_RC):
                x = x_ref[0, r0:r0 + IN_RC, :]
                ms = jnp.mean(x * x, axis=-1, keepdims=True)
                is_ctx = i * IN_TM + r0 >= SEQ
                gain = jnp.where(is_ctx, gain_c, gain_l)
                shift = jnp.where(is_ctx, m[2:3], m[0:1])
                h_scr[r0:r0 + IN_RC, :] = (x * lax.rsqrt(ms + EPS) * gain + shift).astype(BF16)
            rows = slice(c * sec, (c + 1) * sec)
            o_ref[0, rows, :] = jnp.dot(h_scr[rows, :], w_ref[...], preferred_element_type=F32).astype(BF16)

    @pl.when(j > 0)
    def _():
        y = jnp.dot(h_scr[...], w_ref[...], preferred_element_type=F32)
        o_ref[0] = y.astype(BF16)

        @pl.when(j == pl.num_programs(2) - 1)
        def _():
            os_ref[0] = y[:, IN_TN - LANES:]


def _inproj(xs, norm_w, mod4, w_main):
    B = xs.shape[0]
    return pl.pallas_call(
        _inproj_kernel,
        out_shape=(jax.ShapeDtypeStruct((B, S_ALL, D_INP), BF16),
                   jax.ShapeDtypeStruct((B, S_ALL, LANES), F32)),
        grid=(B, S_ALL // IN_TM, D_INP // IN_TN),
        in_specs=[pl.BlockSpec((1, IN_TM, D_MODEL), lambda b, i, j: (b, i, 0)),
                  pl.BlockSpec((1, D_MODEL), lambda b, i, j: (0, 0)),
                  pl.BlockSpec((1, 4, D_MODEL), lambda b, i, j: (b, 0, 0)),
                  pl.BlockSpec((D_MODEL, IN_TN), lambda b, i, j: (0, j))],
        out_specs=(pl.BlockSpec((1, IN_TM, IN_TN), lambda b, i, j: (b, i, j)),
                   pl.BlockSpec((1, IN_TM, LANES), lambda b, i, j: (b, i, 0))),
        scratch_shapes=[pltpu.VMEM((IN_TM, D_MODEL), BF16)],
        compiler_params=_cparams(3),
        name="inproj",
    )(xs, norm_w.reshape(1, D_MODEL), mod4, w_main)


OUT_TM = 544
OUT_TM_LAST = 512
OUT_SECTIONS = 2


def _gdn_gate(o_f, o_b, z, nw):
    o = o_f.astype(F32) + o_b.astype(F32)
    z = z.astype(F32)
    outs = []
    for h in range(GDN_HEADS):
        sl = slice(h * GDN_DV, (h + 1) * GDN_DV)
        oh = o[:, sl]
        y = oh * lax.rsqrt(jnp.mean(oh * oh, axis=-1, keepdims=True) + EPS) * nw
        outs.append((y * _silu(z[:, sl])).astype(BF16))
    return jnp.concatenate(outs, axis=-1)


def _ssd_gate(y_f, y_b, xs, z, d_skip, nw):
    y = y_f.astype(F32) + y_b.astype(F32) + d_skip * xs
    y = y * _silu(z.astype(F32))
    return (y * lax.rsqrt(jnp.mean(y * y, axis=-1, keepdims=True) + EPS) * nw).astype(BF16)


def _outproj_kernel(of_ref, ob_ref, gz_ref, gnw_ref, na_ref, mla_ref, yf_ref, yb_ref, sx_ref, sz_ref,
                    dsk_ref, snw_ref, w_ref, x_ref, g_ref, o_ref, *, tm):
    i = pl.program_id(1)
    g = g_ref[0]
    sec = tm // OUT_SECTIONS
    for c in range(OUT_SECTIONS):
        rows = slice(c * sec, (c + 1) * sec)
        branches = (_gdn_gate(of_ref[0, rows, :], ob_ref[0, rows, :], gz_ref[0, rows, :], gnw_ref[...]),
                    na_ref[0, rows, :], mla_ref[0, rows, :],
                    _ssd_gate(yf_ref[0, rows, :], yb_ref[0, rows, :], sx_ref[0, rows, :], sz_ref[0, rows, :],
                              dsk_ref[...], snw_ref[...]))
        y = None
        for n, a in enumerate(branches):
            t = jnp.dot(a, w_ref[n * D_BRANCH:(n + 1) * D_BRANCH, :], preferred_element_type=F32)
            y = t if y is None else y + t
        row = i * tm + c * sec + lax.broadcasted_iota(jnp.int32, (sec, 1), 0)
        gate = jnp.where(row >= SEQ, g[1:2], g[0:1])
        o_ref[0, rows, :] = x_ref[0, rows, :] + gate * y


def _outproj(gdn, na, mla, ssd, p, w_out_b, xs, gate2, last):
    B = xs.shape[0]
    o_f, o_b, g_nw = gdn
    y_f, y_b, s_x, d_skip, s_nw = ssd
    tm, rows = (OUT_TM_LAST, SEQ) if last else (OUT_TM, S_ALL)
    a_spec = pl.BlockSpec((1, tm, D_BRANCH), lambda b, i: (b, i, 0))
    z_spec = lambda name: pl.BlockSpec((1, tm, D_BRANCH), lambda b, i: (b, i, _P_START[name] // D_BRANCH))
    x_spec = pl.BlockSpec((1, tm, D_MODEL), lambda b, i: (b, i, 0))
    vec = lambda n: pl.BlockSpec((1, n), lambda b, i: (0, 0))
    return pl.pallas_call(
        functools.partial(_outproj_kernel, tm=tm),
        out_shape=jax.ShapeDtypeStruct((B, rows, D_MODEL), F32),
        grid=(B, rows // tm),
        in_specs=[a_spec, a_spec, z_spec('g_z'), vec(GDN_DV), a_spec, a_spec,
                  a_spec, a_spec, a_spec, z_spec('s_z'), vec(SSM_W), vec(SSM_W),
                  pl.BlockSpec((D_MIX, D_MODEL), lambda b, i: (0, 0)),
                  x_spec,
                  pl.BlockSpec((1, 2, D_MODEL), lambda b, i: (b, 0, 0))],
        out_specs=x_spec,
        compiler_params=_cparams(2),
        name="outproj",
    )(o_f, o_b, p, g_nw.reshape(1, GDN_DV), na, mla,
      y_f, y_b, s_x, p, jnp.repeat(d_skip, SSM_HEADDIM).reshape(1, SSM_W), s_nw.reshape(1, SSM_W),
      w_out_b, xs, gate2)


NA_RB = 4
NA_QB = NA_RB * GRID_W
NA_KW = NA_WIN_R * GRID_W
NA_NBLK = GRID_H // NA_RB
assert NA_QB == CTX_LEN


def _na_headnorm(x, w, extra):
    x = x.astype(F32)
    outs = []
    for h in range(NA_HEADS):
        xh = x[:, h * NA_DH:(h + 1) * NA_DH]
        ms = jnp.mean(xh * xh, axis=-1, keepdims=True)
        outs.append((xh * lax.rsqrt(ms + EPS) * w * extra).astype(BF16))
    return jnp.concatenate(outs, axis=-1)


def _na_fill_bias(toe_ref, bias_scr):
    for c in range(NA_WIN_R):
        for h in range(NA_HEADS):
            for i in range(0, NA_WIN_R, 2):
                pair = [toe_ref[h, i + e + NA_WIN_R - 1 - c] for e in range(2)]
                bias_scr[c, h, :, i * GRID_W:(i + 2) * GRID_W] = jnp.concatenate(pair, axis=1)


def _na_kernel(q_ref, kraw_ref, v_ref, z_ref, toe_ref, qn_ref, kn_ref, o_ref, bias_scr, k_ref):
    rb = pl.program_id(1)

    @pl.when(rb == 0)
    def _():
        def body(t, carry):
            r0 = pl.multiple_of(t * NA_QB, NA_QB)
            k_ref[pl.ds(r0, NA_QB), :] = _na_headnorm(kraw_ref[0, pl.ds(r0, NA_QB), :], kn_ref[...], 1.0)
            return carry

        lax.fori_loop(0, S_ALL // NA_QB, body, 0)
        _na_fill_bias(toe_ref, bias_scr)

    q = _na_headnorm(q_ref[0], qn_ref[...], NA_DH ** -0.5 * LOG2E)
    z = z_ref[0].astype(F32)
    kc = k_ref[SEQ:S_ALL, :]
    vc = v_ref[0, SEQ:S_ALL, :]

    def chain(rows, h, kw, vw, bias):
        sl = slice(h * NA_DH, (h + 1) * NA_DH)
        qh = q[rows, sl]
        s_c = _dot_nt(qh, kc[:, sl])
        yield
        m = jnp.max(s_c, axis=-1, keepdims=True)
        if kw is not None:
            s_w = _dot_nt(qh, kw[:, sl])
            yield
            s_w = s_w + bias[h]
            m = jnp.maximum(m, jnp.max(s_w, axis=-1, keepdims=True))
            p_w = jnp.exp2(s_w - m)
        p_c = jnp.exp2(s_c - m)
        l = jnp.sum(p_c, axis=-1, keepdims=True)
        o = jnp.dot(p_c.astype(BF16), vc[:, sl], preferred_element_type=F32)
        yield
        if kw is not None:
            l = l + jnp.sum(p_w, axis=-1, keepdims=True)
            o = o + jnp.dot(p_w.astype(BF16), vw[:, sl], preferred_element_type=F32)
            yield
        o_ref[0, rows, sl] = (o / l * _silu(z[rows, sl])).astype(BF16)

    @pl.when(rb < NA_NBLK)
    def _latent():
        chains = []
        for a in range(NA_RB):
            r = rb * NA_RB + a
            row0 = jnp.clip(r - NA_WIN_R // 2, 0, GRID_H - NA_WIN_R)
            start = pl.multiple_of(row0 * GRID_W, GRID_W)
            kw = k_ref[pl.ds(start, NA_KW), :]
            vw = v_ref[0, pl.ds(start, NA_KW), :]
            bias = bias_scr.at[r - row0]
            rows = slice(a * GRID_W, (a + 1) * GRID_W)
            chains += [chain(rows, h, kw, vw, bias) for h in range(NA_HEADS)]
        _round_robin(chains)

    @pl.when(rb == NA_NBLK)
    def _context():
        _round_robin(chain(slice(0, NA_QB), h, None, None, None) for h in range(NA_HEADS))


def _na_col_ok():
    qc = np.arange(GRID_W)[:, None]
    kc = np.arange(GRID_W)[None, :]
    win0 = np.clip(qc - NA_WIN_C // 2, 0, GRID_W - NA_WIN_C)
    return (kc >= win0) & (kc < win0 + NA_WIN_C)


def _na_bias(rpb):
    L, H = rpb.shape[:2]
    nd = 2 * NA_WIN_R - 1
    col_ok = _na_col_ok()
    left = GRID_W - NA_WIN_C
    f = jnp.pad(rpb * LOG2E, ((0, 0), (0, 0), (0, 0), (left, 2 * GRID_W - (2 * NA_WIN_C - 1) - left)))
    skew = jnp.broadcast_to(f[:, :, :, None, :], (L, H, nd, GRID_W, 2 * GRID_W))
    skew = skew.reshape(L, H, nd, -1)[..., :GRID_W * (2 * GRID_W - 1)].reshape(L, H, nd, GRID_W, 2 * GRID_W - 1)
    toe = skew[..., GRID_W - 1:]
    return jnp.where(col_ok, toe, NEG_INF)


def _na_attend(p, q_norm, k_norm, toe):
    B = p.shape[0]
    c0 = _P_START['n_qkv'] // NA_W
    zc = _P_START['n_z'] // NA_W
    blk = lambda c: pl.BlockSpec((1, NA_QB, NA_W), lambda b, r: (b, r, c))
    full = lambda c: pl.BlockSpec((1, S_ALL, NA_W), lambda b, r: (b, 0, c))
    wspec = pl.BlockSpec((1, NA_DH), lambda b, r: (0, 0))
    return pl.pallas_call(
        _na_kernel,
        out_shape=jax.ShapeDtypeStruct((B, S_ALL, NA_W), BF16),
        grid=(B, NA_NBLK + 1),
        in_specs=[blk(c0), full(c0 + 1), full(c0 + 2), blk(zc),
                  pl.BlockSpec(toe.shape, lambda b, r: (0, 0, 0, 0)), wspec, wspec],
        out_specs=blk(0),
        scratch_shapes=[pltpu.VMEM((NA_WIN_R, NA_HEADS, GRID_W, NA_KW), F32), pltpu.VMEM((S_ALL, NA_W), BF16)],
        compiler_params=_cparams(2),
        name="na_attend",
    )(p, p, p, p, toe, q_norm.reshape(1, NA_DH), k_norm.reshape(1, NA_DH))


MP_TM = 544
MLA_HW = 2 * LANES
MLA_TQ = 1024
MLA_SUBQ = 512
MLA_TK = 1024


def _rope_tables():
    n_freq = MLA_ROPE // 4
    inv_freq = ROPE_THETA ** (-np.arange(n_freq, dtype=np.float64) / n_freq)
    t = np.arange(SEQ)
    ar = (t // GRID_W)[:, None] * inv_freq
    ac = (t % GRID_W)[:, None] * inv_freq
    cos = np.concatenate([np.cos(ar), np.cos(ar), np.cos(ac), np.cos(ac)], axis=1)
    sin = np.concatenate([-np.sin(ar), np.sin(ar), -np.sin(ac), np.sin(ac)], axis=1)
    cos = np.concatenate([cos, np.ones((CTX_LEN, MLA_ROPE))], axis=0)
    sin = np.concatenate([sin, np.zeros((CTX_LEN, MLA_ROPE))], axis=0)
    return (np.tile(cos, (1, MLA_HEADS)).astype(np.float32), np.tile(sin, (1, MLA_HEADS)).astype(np.float32))


def _rope_rotate(t, cos, sin):
    w = t.shape[1]
    lane = lax.broadcasted_iota(jnp.int32, (1, w), 1)
    first = (lane & 31) < 16
    up = pltpu.roll(t, w - 16, axis=1)
    dn = pltpu.roll(t, 16, axis=1)
    return t * cos + jnp.where(first, up, dn) * sin


def _mla_prep_kernel(p_ref, cos_ref, sin_ref, qan_ref, wuq_ref, kvan_ref, wukv_ref, qn_ref, kn_ref,
                     q_out, k_out, v_out):
    x = p_ref[0].astype(F32)
    cq = x[:, :MLA_Q_RANK]
    ckv = x[:, MLA_Q_RANK:MLA_Q_RANK + MLA_KV_RANK]
    kr2 = x[:, MLA_Q_RANK + MLA_KV_RANK:]

    def rms(t, w):
        return t * lax.rsqrt(jnp.mean(t * t, axis=-1, keepdims=True) + EPS) * w

    qf = jnp.dot(rms(cq, qan_ref[...]).astype(BF16), wuq_ref[...], preferred_element_type=F32)
    kvf = jnp.dot(rms(ckv, kvan_ref[...]).astype(BF16), wukv_ref[...], preferred_element_type=F32)
    cos = cos_ref[...]
    sin = sin_ref[...]
    qw = qn_ref[...]
    kw = kn_ref[...]
    n_all = MLA_HEADS * MLA_NOPE
    lane = lax.broadcasted_iota(jnp.int32, (1, LANES), 1)
    halves = (lane < MLA_ROPE, lane >= MLA_ROPE)

    q_rope = qf[:, n_all:]
    q_rope_sq = q_rope * q_rope
    q_rot = _rope_rotate(q_rope * qw[:, n_all:], cos, sin)
    kr_sq = jnp.sum(jnp.where(halves[0], kr2 * kr2, 0.0), axis=-1, keepdims=True)
    k_rot = _rope_rotate(kr2 * kw[:, n_all:], cos[:, :LANES], sin[:, :LANES])
    for h in range(MLA_HEADS):
        half = halves[h % 2]
        vsl = slice((h // 2) * LANES, (h // 2 + 1) * LANES)
        nsl = slice(h * MLA_NOPE, (h + 1) * MLA_NOPE)
        q_nope = qf[:, nsl]
        ss = (jnp.sum(q_nope * q_nope, axis=-1, keepdims=True)
              + jnp.sum(jnp.where(half, q_rope_sq[:, vsl], 0.0), axis=-1, keepdims=True))
        r = lax.rsqrt(ss * (1.0 / MLA_QK) + EPS) * (MLA_QK ** -0.5 * LOG2E)
        q_out[0, :, h * MLA_HW:h * MLA_HW + LANES] = (q_nope * qw[:, nsl] * r).astype(BF16)
        q_out[0, :, h * MLA_HW + LANES:(h + 1) * MLA_HW] = (jnp.where(half, q_rot[:, vsl], 0.0) * r).astype(BF16)
        k_nope = kvf[:, nsl]
        ss = jnp.sum(k_nope * k_nope, axis=-1, keepdims=True) + kr_sq
        r = lax.rsqrt(ss * (1.0 / MLA_QK) + EPS)
        k_out[0, :, h * MLA_HW:h * MLA_HW + LANES] = (k_nope * kw[:, nsl] * r).astype(BF16)
        k_out[0, :, h * MLA_HW + LANES:(h + 1) * MLA_HW] = (jnp.where(half, k_rot, 0.0) * r).astype(BF16)
    v_out[0] = kvf[:, n_all:].astype(BF16)


def _mla_prep(p, cos, sin, qa_norm, w_uq, kva_norm, w_ukv, q_norm, k_norm):
    B = p.shape[0]
    H = MLA_HEADS
    uq = w_uq.reshape(MLA_Q_RANK, H, MLA_QK)
    uq = jnp.concatenate([uq[:, :, :MLA_NOPE].reshape(MLA_Q_RANK, -1),
                          uq[:, :, MLA_NOPE:].reshape(MLA_Q_RANK, -1)], axis=1).astype(BF16)
    ukv = w_ukv.reshape(MLA_KV_RANK, H, MLA_NOPE + MLA_V)
    ukv = jnp.concatenate([ukv[:, :, :MLA_NOPE].reshape(MLA_KV_RANK, -1),
                           ukv[:, :, MLA_NOPE:].reshape(MLA_KV_RANK, -1)], axis=1).astype(BF16)
    qn = jnp.concatenate([jnp.tile(q_norm[:MLA_NOPE], H), jnp.tile(q_norm[MLA_NOPE:], H)]).reshape(1, -1)
    kn = jnp.concatenate([jnp.tile(k_norm[:MLA_NOPE], H), jnp.tile(k_norm[MLA_NOPE:], 2)]).reshape(1, -1)
    pc = _P_START['m_q'] // P_MLA_BLK
    const = lambda shape: pl.BlockSpec(shape, lambda b, i: (0, 0))
    rows = lambda w: pl.BlockSpec((MP_TM, w), lambda b, i: (i, 0))
    outs = lambda w: pl.BlockSpec((1, MP_TM, w), lambda b, i: (b, i, 0))
    return pl.pallas_call(
        _mla_prep_kernel,
        out_shape=(jax.ShapeDtypeStruct((B, S_ALL, H * MLA_HW), BF16),
                   jax.ShapeDtypeStruct((B, S_ALL, H * MLA_HW), BF16),
                   jax.ShapeDtypeStruct((B, S_ALL, MLA_W), BF16)),
        grid=(B, S_ALL // MP_TM),
        in_specs=[pl.BlockSpec((1, MP_TM, P_MLA_BLK), lambda b, i: (b, i, pc)),
                  rows(H * MLA_ROPE), rows(H * MLA_ROPE),
                  const((1, MLA_Q_RANK)), const(uq.shape), const((1, MLA_KV_RANK)), const(ukv.shape),
                  const(qn.shape), const(kn.shape)],
        out_specs=(outs(H * MLA_HW), outs(H * MLA_HW), outs(MLA_W)),
        compiler_params=_cparams(2),
        name="mla_prep",
    )(p, cos, sin, qa_norm.reshape(1, -1), uq, kva_norm.reshape(1, -1), ukv, qn, kn)


def _mla_rows_chain(q, z, k_ref, v_ref, bounds, store):
    n = q.shape[0]
    m = jnp.full((n, 1), NEG_INF, F32)
    l = jnp.zeros((n, 1), F32)
    acc = jnp.zeros((n, MLA_V), F32)
    for lo, hi in bounds:
        s = _dot_nt(q, k_ref[0, lo:hi, :])
        yield
        m_new = jnp.maximum(m, jnp.max(s, axis=-1, keepdims=True))
        a = jnp.exp2(m - m_new)
        p = jnp.exp2(s - m_new)
        l = a * l + jnp.sum(p, axis=-1, keepdims=True)
        acc = a * acc + jnp.dot(p.astype(BF16), v_ref[0, lo:hi, :], preferred_element_type=F32)
        m = m_new
        yield
    store((acc / l * _silu(z.astype(F32))).astype(BF16))


def _mla_attn_kernel(q_ref, k_ref, v_ref, z_ref, qc_ref, zc_ref, o_ref):
    i = pl.program_id(2)
    ctx_keys = [(SEQ, S_ALL)]
    all_keys = ctx_keys + [(c * MLA_TK, (c + 1) * MLA_TK) for c in range(SEQ // MLA_TK)]

    @pl.when(i == 0)
    def _():
        def store(val):
            o_ref[0, SEQ:S_ALL, :] = val
        _round_robin([_mla_rows_chain(qc_ref[0], zc_ref[0], k_ref, v_ref, ctx_keys, store)])

    def latent_chain(r0):
        def store(val):
            o_ref[0, pl.ds(pl.multiple_of(i * MLA_TQ + r0, MLA_SUBQ), MLA_SUBQ), :] = val
        rows = slice(r0, r0 + MLA_SUBQ)
        return _mla_rows_chain(q_ref[0, rows, :], z_ref[0, rows, :], k_ref, v_ref, all_keys, store)

    _round_robin(latent_chain(r0) for r0 in range(0, MLA_TQ, MLA_SUBQ))


def _mla_attend(q, k, v, p):
    B = p.shape[0]
    H = MLA_HEADS
    zc = _P_START['m_z'] // MLA_V
    ctx_blk = SEQ // CTX_LEN
    return pl.pallas_call(
        _mla_attn_kernel,
        out_shape=jax.ShapeDtypeStruct((B, S_ALL, MLA_W), BF16),
        grid=(B, H, SEQ // MLA_TQ),
        in_specs=[pl.BlockSpec((1, MLA_TQ, MLA_HW), lambda b, h, i: (b, i, h)),
                  pl.BlockSpec((1, S_ALL, MLA_HW), lambda b, h, i: (b, 0, h)),
                  pl.BlockSpec((1, S_ALL, MLA_V), lambda b, h, i: (b, 0, h)),
                  pl.BlockSpec((1, MLA_TQ, MLA_V), lambda b, h, i: (b, i, zc + h)),
                  pl.BlockSpec((1, CTX_LEN, MLA_HW), lambda b, h, i: (b, ctx_blk, h)),
                  pl.BlockSpec((1, CTX_LEN, MLA_V), lambda b, h, i: (b, ctx_blk, zc + h))],
        out_specs=pl.BlockSpec((1, S_ALL, MLA_V), lambda b, h, i: (b, 0, h)),
        compiler_params=_cparams(3),
        name="mla_attend",
    )(q, k, v, p, q, p)


SCAN_TILE = 256
SCAN_NT = S_ALL // SCAN_TILE
CTX_TILE = SEQ // SCAN_TILE
HALO = 16


def _split_dot(m_bf16, x):
    hi = x.astype(BF16)
    lo = (x - hi.astype(F32)).astype(BF16)
    return (jnp.dot(m_bf16, hi, preferred_element_type=F32)
            + jnp.dot(m_bf16, lo, preferred_element_type=F32))


def _softplus(t):
    return jnp.maximum(t, 0.0) + jnp.log1p(jnp.exp(-jnp.abs(t)))


def _conv3_silu(xb, prev_row, next_row, w, bias=None):
    n = xb.shape[0]
    i = lax.broadcasted_iota(jnp.int32, (n, 1), 0)
    j = lax.broadcasted_iota(jnp.int32, (1, n), 1)
    one = lambda m: jnp.where(m, 1.0, 0.0).astype(BF16)
    xp = jnp.dot(one(i == j + 1), xb, preferred_element_type=F32)
    xn = jnp.dot(one(i + 1 == j), xb, preferred_element_type=F32)
    y = xp * w[0:1] + xb.astype(F32) * w[1:2] + xn * w[2:3]
    r8 = lax.broadcasted_iota(jnp.int32, (8, 1), 0)
    top = y[0:8] + jnp.where(r8 == 0, prev_row * w[0:1], 0.0)
    bot = y[n - 8:] + jnp.where(r8 == 7, next_row * w[2:3], 0.0)
    y = jnp.concatenate([top, y[8:n - 8], bot], axis=0)
    if bias is not None:
        y = y + bias
    return _silu(y)


def _halo_rows(i, prev_ref, next_ref):
    pv = jnp.where((i == 0) | (i == CTX_TILE), 0.0, 1.0)
    nv = jnp.where((i == CTX_TILE - 1) | (i == SCAN_NT - 1), 0.0, 1.0)
    return prev_ref[0, HALO - 1:HALO, :].astype(F32) * pv, next_ref[0, 0:1, :].astype(F32) * nv


def _halo_specs(width, col_blk):
    rb = SCAN_TILE // HALO
    nblk = S_ALL // HALO
    return [pl.BlockSpec((1, SCAN_TILE, width), lambda b, i: (b, i, col_blk)),
            pl.BlockSpec((1, HALO, width), lambda b, i: (b, jnp.maximum(i * rb - 1, 0), col_blk)),
            pl.BlockSpec((1, HALO, width), lambda b, i: (b, jnp.minimum((i + 1) * rb, nblk - 1), col_blk))]


def _fwd_tile(t):
    return jnp.where(t == 0, CTX_TILE, t - 1)


def _bwd_tile(t):
    return jnp.where(t == 0, CTX_TILE, CTX_TILE - t)


def _chunk_masks(n):
    i = lax.broadcasted_iota(jnp.int32, (n, 1), 0)
    j = lax.broadcasted_iota(jnp.int32, (1, n), 1)
    same = (i // GDN_CHUNK) == (j // GDN_CHUNK)
    return i, j, same


def _gdn_prep_kernel(x_ref, prev_ref, next_ref, s_ref, cw_ref, rate_ref, dtb_ref,
                     q_out, k_out, v_out, g_out):
    i = pl.program_id(1)
    prev_row, next_row = _halo_rows(i, prev_ref, next_ref)
    y = _conv3_silu(x_ref[0], prev_row, next_row, cw_ref[...])
    for h in range(GDN_HEADS):
        sl = slice(h * GDN_DK, (h + 1) * GDN_DK)
        qh = y[:, sl]
        q_out[0, :, sl] = (qh * lax.rsqrt(jnp.sum(qh * qh, axis=-1, keepdims=True) + EPS)
                           * (GDN_DK ** -0.5)).astype(BF16)
        kh = y[:, GDN_W + h * GDN_DK:GDN_W + (h + 1) * GDN_DK]
        k_out[0, :, sl] = (kh * lax.rsqrt(jnp.sum(kh * kh, axis=-1, keepdims=True) + EPS)).astype(BF16)
    v_out[0] = y[:, 2 * GDN_W:].astype(BF16)

    s = s_ref[0]
    lane = lax.broadcasted_iota(jnp.int32, (1, LANES), 1)
    nh2 = 2 * GDN_HEADS
    beta = jax.nn.sigmoid(s)
    g = -rate_ref[...] * _softplus(s + dtb_ref[...])
    g = jnp.where((lane >= nh2) & (lane < 2 * nh2), g, 0.0)
    ii, jj, same = _chunk_masks(SCAN_TILE)
    one = lambda m: jnp.where(m, 1.0, 0.0).astype(BF16)
    fwd_lane = lane < nh2 + GDN_HEADS
    gam = jnp.where(fwd_lane, _split_dot(one(same & (jj <= ii)), g), _split_dot(one(same & (jj >= ii)), g))
    rem = jnp.where(fwd_lane, _split_dot(one(same & (jj > ii)), g), _split_dot(one(same & (jj < ii)), g))
    cf = jnp.where(lane < nh2, beta, jnp.where(lane < 2 * nh2, gam, pltpu.roll(rem, nh2, axis=1)))
    tr = cf.T
    for h in range(GDN_HEADS):
        for r, src in enumerate((h, GDN_HEADS + h, nh2 + h, nh2 + GDN_HEADS + h,
                                 2 * nh2 + h, 2 * nh2 + GDN_HEADS + h)):
            g_out[0, h, r:r + 1, :] = tr[src:src + 1, :]
        g_out[0, h, 6:8, :] = jnp.zeros((2, SCAN_TILE), F32)


def _gdn_prep(p, ps, conv_w, A_log, dt_bias):
    B = p.shape[0]
    W3 = 3 * GDN_W
    nh2 = 2 * GDN_HEADS
    rate = jnp.zeros((1, LANES), F32).at[0, nh2:2 * nh2].set(jnp.exp(A_log).reshape(-1))
    dtb = jnp.zeros((1, LANES), F32).at[0, nh2:2 * nh2].set(dt_bias.reshape(-1))
    shp = jax.ShapeDtypeStruct((B, S_ALL, GDN_W), BF16)
    ospec = pl.BlockSpec((1, SCAN_TILE, GDN_W), lambda b, i: (b, i, 0))
    const = lambda shape: pl.BlockSpec(shape, lambda b, i: (0, 0))
    return pl.pallas_call(
        _gdn_prep_kernel,
        out_shape=(shp, shp, shp, jax.ShapeDtypeStruct((B, GDN_HEADS, 8, S_ALL), F32)),
        grid=(B, SCAN_NT),
        in_specs=_halo_specs(W3, _P_START['g_qkv'] // W3)
        + [pl.BlockSpec((1, SCAN_TILE, LANES), lambda b, i: (b, i, 0)),
           const((SHORT_CONV, W3)), const((1, LANES)), const((1, LANES))],
        out_specs=(ospec, ospec, ospec,
                   pl.BlockSpec((1, GDN_HEADS, 8, SCAN_TILE), lambda b, i: (b, 0, 0, i))),
        compiler_params=_cparams(2),
        name="gdn_prep",
    )(p, p, p, ps, conv_w, rate, dtb)


def _gdn_dir(q, k, v, gr, s_ref, o_ref, d):
    n = SCAN_TILE
    cf = jnp.concatenate([gr, jnp.zeros((LANES - 8, n), F32)], axis=0).T
    beta, gam_c = cf[:, d:d + 1], cf[:, 2 + d:3 + d]
    ecf = jnp.exp(cf)
    e_gam, e_rem = ecf[:, 2 + d:3 + d], ecf[:, 4 + d:5 + d]
    gam_r = gr[2 + d:3 + d, :]
    ii, jj, same = _chunk_masks(n)
    incl = same & ((jj <= ii) if d == 0 else (jj >= ii))
    strict = same & ((jj < ii) if d == 0 else (jj > ii))
    kk = _dot_nt(k, k)
    yield
    qk = _dot_nt(q, k)
    yield
    dec = jnp.exp(jnp.where(incl, gam_c - gam_r, NEG_INF))
    a = jnp.where(strict, beta * kk * dec, 0.0)
    qkd = (qk * dec).astype(BF16)
    kf = k.astype(F32)
    x = jnp.concatenate([v.astype(F32) * beta, kf * (beta * e_gam)], axis=1)
    blk = lambda size: (ii // size) == (jj // size)
    inner = blk(2)
    t = jnp.where(ii == jj, 1.0, 0.0) - jnp.where(inner, a, 0.0)
    for size in (4, 8, 16, 32, GDN_CHUNK):
        outer = blk(size)
        e = jnp.where(outer & ~inner, a, 0.0)
        inner = outer
        et = jnp.dot(e.astype(BF16), t.astype(BF16), preferred_element_type=F32)
        yield
        t = t - jnp.dot(t.astype(BF16), et.astype(BF16), preferred_element_type=F32)
        yield
    x = jnp.dot(t.astype(BF16), x.astype(BF16), preferred_element_type=F32)
    yield
    u, w = x[:, :GDN_DV], x[:, GDN_DV:].astype(BF16)
    qd = (q.astype(F32) * e_gam).astype(BF16)
    kd = (kf * e_rem).astype(BF16)
    s = s_ref[...]
    nchunk = n // GDN_CHUNK
    v_new = [None] * nchunk
    qs = [None] * nchunk
    for c in (range(nchunk) if d == 0 else reversed(range(nchunk))):
        rows = slice(c * GDN_CHUNK, (c + 1) * GDN_CHUNK)
        r1 = jnp.dot(jnp.concatenate([w[rows], qd[rows]], axis=0), s.astype(BF16), preferred_element_type=F32)
        yield
        vn = u[rows] - r1[:GDN_CHUNK]
        qs[c] = r1[GDN_CHUNK:]
        v_new[c] = vn
        last = (c + 1) * GDN_CHUNK - 1 if d == 0 else c * GDN_CHUNK
        s = s * ecf[last:last + 1, 2 + d:3 + d] + _dot_tn(kd[rows], vn.astype(BF16))
        yield
    s_ref[...] = s
    vn_all = jnp.concatenate(v_new, axis=0).astype(BF16)
    o_ref[0] = (jnp.concatenate(qs, axis=0) + jnp.dot(qkd, vn_all, preferred_element_type=F32)).astype(BF16)


def _round_robin(gens):
    gens = list(gens)
    while gens:
        alive = []
        for g in gens:
            try:
                next(g)
                alive.append(g)
            except StopIteration:
                pass
        gens = alive


def _gdn_scan_kernel(qf, kf, vf, gf, qb, kb, vb, gb, of_ref, ob_ref, sf_ref, sb_ref):
    @pl.when(pl.program_id(1) == 0)
    def _():
        sf_ref[...] = jnp.zeros_like(sf_ref)
        sb_ref[...] = jnp.zeros_like(sb_ref)

    chains = []
    for h in range(GDN_HEADS):
        sl = slice(h * GDN_DK, (h + 1) * GDN_DK)
        chains.append(_gdn_dir(qf[0, :, sl], kf[0, :, sl], vf[0, :, sl], gf[0, h],
                               sf_ref.at[h], of_ref.at[:, :, sl], 0))
        chains.append(_gdn_dir(qb[0, :, sl], kb[0, :, sl], vb[0, :, sl], gb[0, h],
                               sb_ref.at[h], ob_ref.at[:, :, sl], 1))
    _round_robin(chains)


def _gdn_scan(q, k, v, g):
    B = q.shape[0]
    tok = lambda order: pl.BlockSpec((1, SCAN_TILE, GDN_W), lambda b, t: (b, order(t), 0))
    gsp = lambda order: pl.BlockSpec((1, GDN_HEADS, 8, SCAN_TILE), lambda b, t: (b, 0, 0, order(t)))
    shp = jax.ShapeDtypeStruct((B, S_ALL, GDN_W), BF16)
    f, r = _fwd_tile, _bwd_tile
    state = pltpu.VMEM((GDN_HEADS, GDN_DK, GDN_DV), F32)
    return pl.pallas_call(
        _gdn_scan_kernel,
        out_shape=(shp, shp),
        grid=(B, SCAN_NT),
        in_specs=[tok(f), tok(f), tok(f), gsp(f), tok(r), tok(r), tok(r), gsp(r)],
        out_specs=(tok(f), tok(r)),
        scratch_shapes=[state, state],
        compiler_params=_cparams(2),
        name="gdn_scan",
    )(q, k, v, g, q, k, v, g)


SSD_BC = SSM_GROUPS * SSM_STATE
SSD_HD = 2 * SSM_HEADS
SSD_GW = (SSM_HEADS // SSM_GROUPS) * SSM_HEADDIM


def _ssd_prep_kernel(x_ref, prev_ref, next_ref, s_ref, cw_ref, cb_ref, a_ref, dtb_ref,
                     xs_out, b_out, c_out, bt_out, cf_out, cr_out):
    i = pl.program_id(1)
    prev_row, next_row = _halo_rows(i, prev_ref, next_ref)
    y = _conv3_silu(x_ref[0], prev_row, next_row, cw_ref[...], cb_ref[...])
    xs_out[0] = y[:, :SSM_W]
    b_out[0] = y[:, SSM_W:SSM_W + SSD_BC].astype(BF16)
    c_out[0] = y[:, SSM_W + SSD_BC:].astype(BF16)
    bt_out[0] = y[:, SSM_W:SSM_W + SSD_BC].T.astype(BF16)

    s = s_ref[0]
    lane = lax.broadcasted_iota(jnp.int32, (1, LANES), 1)
    dt = _softplus(s + dtb_ref[...])
    on = (lane >= SSD_HD) & (lane < 2 * SSD_HD)
    a = jnp.where(on, dt * a_ref[...], 0.0)
    n = SCAN_TILE
    ii = lax.broadcasted_iota(jnp.int32, (n, 1), 0)
    jj = lax.broadcasted_iota(jnp.int32, (1, n), 1)
    one = lambda m: jnp.where(m, 1.0, 0.0).astype(BF16)
    fwd_lane = lane < SSD_HD + SSM_HEADS
    cum = jnp.where(fwd_lane, _split_dot(one(jj <= ii), a), _split_dot(one(jj >= ii), a))
    rem = jnp.where(fwd_lane, _split_dot(one(jj > ii), a), _split_dot(one(jj < ii), a))
    cf = jnp.where(lane < SSD_HD, pltpu.roll(dt, LANES - SSD_HD, axis=1),
                   jnp.where(lane < 2 * SSD_HD, cum, pltpu.roll(rem, SSD_HD, axis=1)))
    cf_out[0] = cf
    cr_out[0] = cf.T[SSD_HD:2 * SSD_HD, :]


def _ssd_prep(p, ps, conv_w, conv_b, A_log, dt_bias):
    B = p.shape[0]
    W = SSM_CONV_DIM
    a_vec = jnp.zeros((1, LANES), F32).at[0, SSD_HD:2 * SSD_HD].set(-jnp.exp(A_log).reshape(-1))
    dtb = jnp.zeros((1, LANES), F32).at[0, SSD_HD:2 * SSD_HD].set(dt_bias.reshape(-1))
    const = lambda shape: pl.BlockSpec(shape, lambda b, i: (0, 0))
    tok = lambda w: pl.BlockSpec((1, SCAN_TILE, w), lambda b, i: (b, i, 0))
    return pl.pallas_call(
        _ssd_prep_kernel,
        out_shape=(jax.ShapeDtypeStruct((B, S_ALL, SSM_W), F32),
                   jax.ShapeDtypeStruct((B, S_ALL, SSD_BC), BF16),
                   jax.ShapeDtypeStruct((B, S_ALL, SSD_BC), BF16),
                   jax.ShapeDtypeStruct((B, SSD_BC, S_ALL), BF16),
                   jax.ShapeDtypeStruct((B, S_ALL, LANES), F32),
                   jax.ShapeDtypeStruct((B, SSD_HD, S_ALL), F32)),
        grid=(B, SCAN_NT),
        in_specs=_halo_specs(W, _P_START['s_xbc'] // W)
        + [pl.BlockSpec((1, SCAN_TILE, LANES), lambda b, i: (b, i, 0)),
           const((SHORT_CONV, W)), const((1, W)), const((1, LANES)), const((1, LANES))],
        out_specs=(tok(SSM_W), tok(SSD_BC), tok(SSD_BC),
                   pl.BlockSpec((1, SSD_BC, SCAN_TILE), lambda b, i: (b, 0, i)),
                   tok(LANES),
                   pl.BlockSpec((1, SSD_HD, SCAN_TILE), lambda b, i: (b, 0, i))),
        compiler_params=_cparams(2),
        name="ssd_prep",
    )(p, p, p, ps, conv_w, conv_b.reshape(1, W), a_vec, dtb)


def _ssd_dir(x_ref, b_ref, c_ref, bt_ref, cf_ref, cr_ref, h_ref, y_ref, d, g):
    n = SCAN_TILE
    hpg = SSM_HEADS // SSM_GROUPS
    hd0 = d * SSM_HEADS + g * hpg
    gs = slice(g * SSM_STATE, (g + 1) * SSM_STATE)
    xl = slice(g * SSD_GW, (g + 1) * SSD_GW)
    cf = cf_ref[0]
    cr = cr_ref[0]
    cm = c_ref[0, :, gs]
    log_lanes = lax.broadcasted_iota(jnp.int32, (1, LANES), 1) >= SSD_HD
    ecf = jnp.exp(jnp.where(log_lanes, cf, 0.0))
    col = lambda base, h: cf[:, base + hd0 + h:base + hd0 + h + 1]
    last = n - 1 if d == 0 else 0
    src = jnp.where(log_lanes, ecf, cf)
    r = lax.broadcasted_iota(jnp.int32, (LANES, 1), 0)
    c = lax.broadcasted_iota(jnp.int32, (1, 3 * SSD_GW), 1)
    want = (c // SSD_GW) * SSD_HD + hd0 + (c % SSD_GW) // SSM_HEADDIM
    sel = jnp.where(r == want, 1.0, 0.0).astype(BF16)
    hi = src.astype(BF16)
    lo_part = (src - hi.astype(F32)).astype(BF16)
    spread = (jnp.dot(hi, sel, preferred_element_type=F32) + jnp.dot(lo_part, sel, preferred_element_type=F32))
    yield
    dt_x, ecum_x, erem_x = spread[:, :SSD_GW], spread[:, SSD_GW:2 * SSD_GW], spread[:, 2 * SSD_GW:]
    xdt = x_ref[0, :, xl] * dt_x
    xdt_b = xdt.astype(BF16)
    xdec = (xdt * erem_x).astype(BF16)
    cb = _dot_nt(cm, b_ref[0, :, gs])
    yield
    h_prev = h_ref[g]
    y_off = jnp.dot(cm, h_prev.astype(BF16), preferred_element_type=F32)
    yield
    y_off = y_off * ecum_x
    h_ref[g] = h_prev * ecum_x[last:last + 1, :] + jnp.dot(bt_ref[0, gs, :], xdec, preferred_element_type=F32)
    yield
    ii = lax.broadcasted_iota(jnp.int32, (n, 1), 0)
    jj = lax.broadcasted_iota(jnp.int32, (1, n), 1)
    causal = (jj <= ii) if d == 0 else (jj >= ii)
    lo = lax.broadcasted_iota(jnp.int32, (1, LANES), 1) < SSM_HEADDIM
    pair_out = []
    for j in range(hpg // 2):
        ys = []
        for e in range(2):
            h = 2 * j + e
            seg = col(SSD_HD, h) - cr[hd0 + h:hd0 + h + 1, :]
            sc = (cb * jnp.exp(jnp.where(causal, seg, NEG_INF))).astype(BF16)
            ys.append(jnp.dot(sc, xdt_b[:, j * LANES:(j + 1) * LANES], preferred_element_type=F32))
            yield
        pair_out.append(jnp.where(lo, ys[0], ys[1]))
    y_ref[0, :, xl] = (jnp.concatenate(pair_out, axis=1) + y_off).astype(BF16)


def _ssd_scan_kernel(xf, bf, cf_, btf, colf, rowf, xb, bb, cb_, btb, colb, rowb, yf_ref, yb_ref, hf_ref, hb_ref):
    @pl.when(pl.program_id(1) == 0)
    def _():
        hf_ref[...] = jnp.zeros_like(hf_ref)
        hb_ref[...] = jnp.zeros_like(hb_ref)

    chains = []
    for g in range(SSM_GROUPS):
        chains.append(_ssd_dir(xf, bf, cf_, btf, colf, rowf, hf_ref, yf_ref, 0, g))
        chains.append(_ssd_dir(xb, bb, cb_, btb, colb, rowb, hb_ref, yb_ref, 1, g))
    _round_robin(chains)


def _ssd_scan(xs, bm, cm, bt, cf, cr):
    B = xs.shape[0]

    def specs(order):
        tok = lambda w: pl.BlockSpec((1, SCAN_TILE, w), lambda b, t: (b, order(t), 0))
        return [tok(SSM_W), tok(SSD_BC), tok(SSD_BC),
                pl.BlockSpec((1, SSD_BC, SCAN_TILE), lambda b, t: (b, 0, order(t))),
                tok(LANES),
                pl.BlockSpec((1, SSD_HD, SCAN_TILE), lambda b, t: (b, 0, order(t)))]

    shp = jax.ShapeDtypeStruct((B, S_ALL, SSM_W), BF16)
    out = lambda order: pl.BlockSpec((1, SCAN_TILE, SSM_W), lambda b, t: (b, order(t), 0))
    hshape = pltpu.VMEM((SSM_GROUPS, SSM_STATE, SSD_GW), F32)
    args = (xs, bm, cm, bt, cf, cr)
    return pl.pallas_call(
        _ssd_scan_kernel,
        out_shape=(shp, shp),
        grid=(B, SCAN_NT),
        in_specs=specs(_fwd_tile) + specs(_bwd_tile),
        out_specs=(out(_fwd_tile), out(_bwd_tile)),
        scratch_shapes=[hshape, hshape],
        compiler_params=_cparams(2),
        name="ssd_scan",
    )(*args, *args)


def _join_kernel(x_ref, c_ref, o_ref):
    i = pl.program_id(1)

    @pl.when(i < CTX_TILE)
    def _():
        o_ref[...] = x_ref[...]

    @pl.when(i == CTX_TILE)
    def _():
        o_ref[...] = c_ref[...]


def _join(x, ctx):
    B = x.shape[0]
    blk = lambda index: pl.BlockSpec((1, SCAN_TILE, D_MODEL), index)
    return pl.pallas_call(
        _join_kernel,
        out_shape=jax.ShapeDtypeStruct((B, S_ALL, D_MODEL), x.dtype),
        grid=(B, SCAN_NT),
        in_specs=[blk(lambda b, i: (b, jnp.minimum(i, CTX_TILE - 1), 0)), blk(lambda b, i: (b, 0, 0))],
        out_specs=blk(lambda b, i: (b, i, 0)),
        compiler_params=_cparams(2),
        name="join_tokens",
    )(x, ctx)


def _repack_w_in(w):
    cut = lambda names: [w[:, _IN_START[n]:_IN_START[n] + _IN_SIZE[n]].astype(BF16) for n in names]
    zeros = lambda n: jnp.zeros((w.shape[0], n), BF16)
    n_small = sum(_IN_SIZE[n] for n in _S_ORDER)
    cols = cut(_P_ORDER) + [zeros(D_INP - LANES - _off)] + cut(_S_ORDER) + [zeros(LANES - n_small)]
    return jnp.concatenate(cols, axis=1)


def kernel(x, c, ctx, c_ctx, norm_w, ada_w, ada_b, w_in, gdn_conv_w, gdn_A_log, gdn_dt_bias, gdn_norm_w, na_q_norm, na_k_norm, na_rpb, mla_qa_norm, mla_w_uq, mla_kva_norm, mla_w_ukv, mla_q_norm, mla_k_norm, ssm_conv_w, ssm_conv_b, ssm_A_log, ssm_dt_bias, ssm_D, ssm_norm_w, w_out):
    B = x.shape[0]
    xs = _join(x, ctx)
    c8 = jnp.zeros((8, D_MODEL), F32).at[:B].set(c).at[B].set(c_ctx)
    mods = _ada_all(c8, ada_w, ada_b)
    cos_np, sin_np = _rope_tables()
    cos, sin = jnp.asarray(cos_np), jnp.asarray(sin_np)
    na_bias = _na_bias(na_rpb)
    for l in range(DEPTH):
        shift, scale, gate = jnp.split(mods[l, :B], 3, axis=-1)
        shift_c, scale_c, gate_c = jnp.split(mods[l, B], 3, axis=-1)
        bc = lambda v: jnp.broadcast_to(v[None], (B, D_MODEL))
        mod4 = jnp.stack([shift, scale, bc(shift_c), bc(scale_c)], axis=1)
        gate2 = jnp.stack([gate, bc(gate_c)], axis=1)
        p, ps = _inproj(xs, norm_w[l], mod4, _repack_w_in(w_in[l]))

        gq, gk, gv, gg = _gdn_prep(p, ps, gdn_conv_w[l], gdn_A_log[l], gdn_dt_bias[l])
        o_f, o_b = _gdn_scan(gq, gk, gv, gg)

        ob = _na_attend(p, na_q_norm[l], na_k_norm[l], na_bias[l])

        mq, mk, mv = _mla_prep(p, cos, sin, mla_qa_norm[l], mla_w_uq[l], mla_kva_norm[l], mla_w_ukv[l],
                               mla_q_norm[l], mla_k_norm[l])
        oc = _mla_attend(mq, mk, mv, p)

        sx, sb, sc, sbt, scf, scr = _ssd_prep(p, ps, ssm_conv_w[l], ssm_conv_b[l], ssm_A_log[l], ssm_dt_bias[l])
        y_f, y_b = _ssd_scan(sx, sb, sc, sbt, scf, scr)
        xs = _outproj((o_f, o_b, gdn_norm_w[l]), ob, oc, (y_f, y_b, sx, ssm_D[l], ssm_norm_w[l]), p,
                      w_out[l].astype(BF16), xs, gate2, last=(l == DEPTH - 1))
    return xs
```

```python
import functools

import jax
import jax.numpy as jnp
import numpy as np
from jax import lax
from jax.experimental import pallas as pl
from jax.experimental.pallas import tpu as pltpu

F32 = jnp.float32
BF16 = jnp.bfloat16

D_MODEL = 2048
SEQ = 4096
DEPTH = 4
GRID_W = 64
GRID_H = SEQ // GRID_W
CTX_LEN = 256
S_ALL = SEQ + CTX_LEN
EPS = 1e-6
NEG_INF = -1e30
LOG2E = 1.4426950408889634

D_BRANCH = 512
D_MIX = 4 * D_BRANCH
SHORT_CONV = 3

GDN_HEADS = 4
GDN_DK = 128
GDN_DV = 128
GDN_W = GDN_HEADS * GDN_DV
GDN_CHUNK = 64

NA_HEADS = 4
NA_DH = 128
NA_W = NA_HEADS * NA_DH
NA_WIN_R = 8
NA_WIN_C = 16

MLA_HEADS = 4
MLA_Q_RANK = 384
MLA_KV_RANK = 256
MLA_NOPE = 128
MLA_ROPE = 64
MLA_QK = MLA_NOPE + MLA_ROPE
MLA_V = 128
MLA_W = MLA_HEADS * MLA_V
ROPE_THETA = 10000.0

SSM_HEADDIM = 64
SSM_HEADS = D_BRANCH // SSM_HEADDIM
SSM_W = SSM_HEADS * SSM_HEADDIM
SSM_GROUPS = 2
SSM_STATE = 128
SSM_CONV_DIM = SSM_W + 2 * SSM_GROUPS * SSM_STATE

IN_SIZES = (3 * GDN_W, GDN_W, 2 * GDN_HEADS, 2 * GDN_HEADS,
            3 * NA_W, NA_W,
            MLA_Q_RANK, MLA_KV_RANK, MLA_ROPE, MLA_W,
            SSM_W, SSM_CONV_DIM, 2 * SSM_HEADS)
D_IN = sum(IN_SIZES)
_IN_NAMES = ('g_qkv', 'g_z', 'g_beta', 'g_alpha', 'n_qkv', 'n_z',
             'm_q', 'm_kv', 'm_kr', 'm_z', 's_z', 's_xbc', 's_dt')
_IN_START = dict(zip(_IN_NAMES, np.cumsum((0,) + IN_SIZES[:-1]).tolist()))
_IN_SIZE = dict(zip(_IN_NAMES, IN_SIZES))

LANES = 128
_P_ORDER = ('g_qkv', 'g_z', 'n_qkv', 'n_z', 'm_z', 's_z', 's_xbc',
            'm_q', 'm_kv', 'm_kr', 'm_kr')
_S_ORDER = ('g_beta', 'g_alpha', 's_dt')
_P_START = {}
_off = 0
for _n in _P_ORDER:
    _P_START.setdefault(_n, _off)
    _off += _IN_SIZE[_n]
MXU_N = 256
D_INP = -(-_off // (2 * MXU_N)) * (2 * MXU_N)
P_MLA_BLK = MLA_Q_RANK + MLA_KV_RANK + 2 * MLA_ROPE
assert _P_START['m_q'] % P_MLA_BLK == 0 and D_INP % LANES == 0

V7X_VMEM_BYTES = 64 * 1024 * 1024
VMEM_LIMIT = V7X_VMEM_BYTES - 12 * 1024 * 1024


def _silu(x):
    h = 0.5 * x
    return h + h * jnp.tanh(h)


def _dot_nt(a, b):
    return lax.dot_general(a, b, (((1,), (1,)), ((), ())), preferred_element_type=F32)


def _dot_tn(a, b):
    return lax.dot_general(a, b, (((0,), (0,)), ((), ())), preferred_element_type=F32)


def _cparams(n_axes):
    return pltpu.CompilerParams(dimension_semantics=("arbitrary",) * n_axes,
                                vmem_limit_bytes=VMEM_LIMIT)


ADA_TN = 1536


def _ada_kernel(c_ref, w_ref, b_ref, o_ref):
    a = _silu(c_ref[...]).astype(BF16)
    o_ref[0] = jnp.dot(a, w_ref[0].astype(BF16), preferred_element_type=F32) + b_ref[0]


def _ada_all(c8, ada_w, ada_b):
    L = ada_w.shape[0]
    n3 = ada_w.shape[2]
    return pl.pallas_call(
        _ada_kernel,
        out_shape=jax.ShapeDtypeStruct((L, 8, n3), F32),
        grid=(L, n3 // ADA_TN),
        in_specs=[pl.BlockSpec((8, D_MODEL), lambda l, j: (0, 0)),
                  pl.BlockSpec((1, D_MODEL, ADA_TN), lambda l, j: (l, 0, j)),
                  pl.BlockSpec((1, 1, ADA_TN), lambda l, j: (l, 0, j))],
        out_specs=pl.BlockSpec((1, 8, ADA_TN), lambda l, j: (l, 0, j)),
        compiler_params=_cparams(2),
        name="ada_mod",
    )(c8, ada_w, ada_b.reshape(L, 1, n3))


IN_TM = 1088
IN_TN = 1024
IN_RC = 16
IN_SECTIONS = 4


def _inproj_kernel(x_ref, nw_ref, mod_ref, w_ref, o_ref, os_ref, h_scr):
    i = pl.program_id(1)
    j = pl.program_id(2)

    @pl.when(j == 0)
    def _():
        m = mod_ref[0]
        nw = nw_ref[...]
        gain_l = nw * (1.0 + m[1:2])
        gain_c = nw * (1.0 + m[3:4])
        sec = IN_TM // IN_SECTIONS
        for c in range(IN_SECTIONS):
            for r0 in range(c * sec, (c + 1) * sec, IN_RC):
                x = x_ref[0, r0:r0 + IN_RC, :]
                ms = jnp.mean(x * x, axis=-1, keepdims=True)
                is_ctx = i * IN_TM + r0 >= SEQ
                gain = jnp.where(is_ctx, gain_c, gain_l)
                shift = jnp.where(is_ctx, m[2:3], m[0:1])
                h_scr[r0:r0 + IN_RC, :] = (x * lax.rsqrt(ms + EPS) * gain + shift).astype(BF16)
            rows = slice(c * sec, (c + 1) * sec)
            o_ref[0, rows, :] = jnp.dot(h_scr[rows, :], w_ref[...], preferred_element_type=F32).astype(BF16)

    @pl.when(j > 0)
    def _():
        y = jnp.dot(h_scr[...], w_ref[...], preferred_element_type=F32)
        o_ref[0] = y.astype(BF16)

        @pl.when(j == pl.num_programs(2) - 1)
        def _():
            os_ref[0] = y[:, IN_TN - LANES:]


def _inproj(xs, norm_w, mod4, w_main):
    B = xs.shape[0]
    return pl.pallas_call(
        _inproj_kernel,
        out_shape=(jax.ShapeDtypeStruct((B, S_ALL, D_INP), BF16),
                   jax.ShapeDtypeStruct((B, S_ALL, LANES), F32)),
        grid=(B, S_ALL // IN_TM, D_INP // IN_TN),
        in_specs=[pl.BlockSpec((1, IN_TM, D_MODEL), lambda b, i, j: (b, i, 0)),
                  pl.BlockSpec((1, D_MODEL), lambda b, i, j: (0, 0)),
                  pl.BlockSpec((1, 4, D_MODEL), lambda b, i, j: (b, 0, 0)),
                  pl.BlockSpec((D_MODEL, IN_TN), lambda b, i, j: (0, j))],
        out_specs=(pl.BlockSpec((1, IN_TM, IN_TN), lambda b, i, j: (b, i, j)),
                   pl.BlockSpec((1, IN_TM, LANES), lambda b, i, j: (b, i, 0))),
        scratch_shapes=[pltpu.VMEM((IN_TM, D_MODEL), BF16)],
        compiler_params=_cparams(3),
        name="inproj",
    )(xs, norm_w.reshape(1, D_MODEL), mod4, w_main)


OUT_TM = 544
OUT_TM_LAST = 512
OUT_SECTIONS = 2


def _gdn_gate(o_f, o_b, z, nw):
    o = o_f.astype(F32) + o_b.astype(F32)
    z = z.astype(F32)
    outs = []
    for h in range(GDN_HEADS):
        sl = slice(h * GDN_DV, (h + 1) * GDN_DV)
        oh = o[:, sl]
        y = oh * lax.rsqrt(jnp.mean(oh * oh, axis=-1, keepdims=True) + EPS) * nw
        outs.append((y * _silu(z[:, sl])).astype(BF16))
    return jnp.concatenate(outs, axis=-1)


def _ssd_gate(y_f, y_b, xs, z, d_skip, nw):
    y = y_f.astype(F32) + y_b.astype(F32) + d_skip * xs
    y = y * _silu(z.astype(F32))
    return (y * lax.rsqrt(jnp.mean(y * y, axis=-1, keepdims=True) + EPS) * nw).astype(BF16)


def _outproj_kernel(of_ref, ob_ref, gz_ref, gnw_ref, na_ref, mla_ref, yf_ref, yb_ref, sx_ref, sz_ref,
                    dsk_ref, snw_ref, w_ref, x_ref, g_ref, o_ref, *, tm):
    i = pl.program_id(1)
    g = g_ref[0]
    sec = tm // OUT_SECTIONS
    for c in range(OUT_SECTIONS):
        rows = slice(c * sec, (c + 1) * sec)
        branches = (_gdn_gate(of_ref[0, rows, :], ob_ref[0, rows, :], gz_ref[0, rows, :], gnw_ref[...]),
                    na_ref[0, rows, :], mla_ref[0, rows, :],
                    _ssd_gate(yf_ref[0, rows, :], yb_ref[0, rows, :], sx_ref[0, rows, :], sz_ref[0, rows, :],
                              dsk_ref[...], snw_ref[...]))
        y = None
        for n, a in enumerate(branches):
            t = jnp.dot(a, w_ref[n * D_BRANCH:(n + 1) * D_BRANCH, :], preferred_element_type=F32)
            y = t if y is None else y + t
        row = i * tm + c * sec + lax.broadcasted_iota(jnp.int32, (sec, 1), 0)
        gate = jnp.where(row >= SEQ, g[1:2], g[0:1])
        o_ref[0, rows, :] = x_ref[0, rows, :] + gate * y


def _outproj(gdn, na, mla, ssd, p, w_out_b, xs, gate2, last):
    B = xs.shape[0]
    o_f, o_b, g_nw = gdn
    y_f, y_b, s_x, d_skip, s_nw = ssd
    tm, rows = (OUT_TM_LAST, SEQ) if last else (OUT_TM, S_ALL)
    a_spec = pl.BlockSpec((1, tm, D_BRANCH), lambda b, i: (b, i, 0))
    z_spec = lambda name: pl.BlockSpec((1, tm, D_BRANCH), lambda b, i: (b, i, _P_START[name] // D_BRANCH))
    x_spec = pl.BlockSpec((1, tm, D_MODEL), lambda b, i: (b, i, 0))
    vec = lambda n: pl.BlockSpec((1, n), lambda b, i: (0, 0))
    return pl.pallas_call(
        functools.partial(_outproj_kernel, tm=tm),
        out_shape=jax.ShapeDtypeStruct((B, rows, D_MODEL), F32),
        grid=(B, rows // tm),
        in_specs=[a_spec, a_spec, z_spec('g_z'), vec(GDN_DV), a_spec, a_spec,
                  a_spec, a_spec, a_spec, z_spec('s_z'), vec(SSM_W), vec(SSM_W),
                  pl.BlockSpec((D_MIX, D_MODEL), lambda b, i: (0, 0)),
                  x_spec,
                  pl.BlockSpec((1, 2, D_MODEL), lambda b, i: (b, 0, 0))],
        out_specs=x_spec,
        compiler_params=_cparams(2),
        name="outproj",
    )(o_f, o_b, p, g_nw.reshape(1, GDN_DV), na, mla,
      y_f, y_b, s_x, p, jnp.repeat(d_skip, SSM_HEADDIM).reshape(1, SSM_W), s_nw.reshape(1, SSM_W),
      w_out_b, xs, gate2)


NA_RB = 4
NA_QB = NA_RB * GRID_W
NA_KW = NA_WIN_R * GRID_W
NA_NBLK = GRID_H // NA_RB
assert NA_QB == CTX_LEN


def _na_headnorm(x, w, extra):
    x = x.astype(F32)
    outs = []
    for h in range(NA_HEADS):
        xh = x[:, h * NA_DH:(h + 1) * NA_DH]
        ms = jnp.mean(xh * xh, axis=-1, keepdims=True)
        outs.append((xh * lax.rsqrt(ms + EPS) * w * extra).astype(BF16))
    return jnp.concatenate(outs, axis=-1)


def _na_fill_bias(toe_ref, bias_scr):
    for c in range(NA_WIN_R):
        for h in range(NA_HEADS):
            for i in range(0, NA_WIN_R, 2):
                pair = [toe_ref[h, i + e + NA_WIN_R - 1 - c] for e in range(2)]
                bias_scr[c, h, :, i * GRID_W:(i + 2) * GRID_W] = jnp.concatenate(pair, axis=1)


def _na_kernel(q_ref, kraw_ref, v_ref, z_ref, toe_ref, qn_ref, kn_ref, o_ref, bias_scr, k_ref):
    rb = pl.program_id(1)

    @pl.when(rb == 0)
    def _():
        def body(t, carry):
            r0 = pl.multiple_of(t * NA_QB, NA_QB)
            k_ref[pl.ds(r0, NA_QB), :] = _na_headnorm(kraw_ref[0, pl.ds(r0, NA_QB), :], kn_ref[...], 1.0)
            return carry

        lax.fori_loop(0, S_ALL // NA_QB, body, 0)
        _na_fill_bias(toe_ref, bias_scr)

    q = _na_headnorm(q_ref[0], qn_ref[...], NA_DH ** -0.5 * LOG2E)
    z = z_ref[0].astype(F32)
    kc = k_ref[SEQ:S_ALL, :]
    vc = v_ref[0, SEQ:S_ALL, :]

    def chain(rows, h, kw, vw, bias):
        sl = slice(h * NA_DH, (h + 1) * NA_DH)
        qh = q[rows, sl]
        s_c = _dot_nt(qh, kc[:, sl])
        yield
        m = jnp.max(s_c, axis=-1, keepdims=True)
        if kw is not None:
            s_w = _dot_nt(qh, kw[:, sl])
            yield
            s_w = s_w + bias[h]
            m = jnp.maximum(m, jnp.max(s_w, axis=-1, keepdims=True))
            p_w = jnp.exp2(s_w - m)
        p_c = jnp.exp2(s_c - m)
        l = jnp.sum(p_c, axis=-1, keepdims=True)
        o = jnp.dot(p_c.astype(BF16), vc[:, sl], preferred_element_type=F32)
        yield
        if kw is not None:
            l = l + jnp.sum(p_w, axis=-1, keepdims=True)
            o = o + jnp.dot(p_w.astype(BF16), vw[:, sl], preferred_element_type=F32)
            yield
        o_ref[0, rows, sl] = (o / l * _silu(z[rows, sl])).astype(BF16)

    @pl.when(rb < NA_NBLK)
    def _latent():
        chains = []
        for a in range(NA_RB):
            r = rb * NA_RB + a
            row0 = jnp.clip(r - NA_WIN_R // 2, 0, GRID_H - NA_WIN_R)
            start = pl.multiple_of(row0 * GRID_W, GRID_W)
            kw = k_ref[pl.ds(start, NA_KW), :]
            vw = v_ref[0, pl.ds(start, NA_KW), :]
            bias = bias_scr.at[r - row0]
            rows = slice(a * GRID_W, (a + 1) * GRID_W)
            chains += [chain(rows, h, kw, vw, bias) for h in range(NA_HEADS)]
        _round_robin(chains)

    @pl.when(rb == NA_NBLK)
    def _context():
        _round_robin(chain(slice(0, NA_QB), h, None, None, None) for h in range(NA_HEADS))


def _na_col_ok():
    qc = np.arange(GRID_W)[:, None]
    kc = np.arange(GRID_W)[None, :]
    win0 = np.clip(qc - NA_WIN_C // 2, 0, GRID_W - NA_WIN_C)
    return (kc >= win0) & (kc < win0 + NA_WIN_C)


def _na_bias(rpb):
    L, H = rpb.shape[:2]
    nd = 2 * NA_WIN_R - 1
    col_ok = _na_col_ok()
    left = GRID_W - NA_WIN_C
    f = jnp.pad(rpb * LOG2E, ((0, 0), (0, 0), (0, 0), (left, 2 * GRID_W - (2 * NA_WIN_C - 1) - left)))
    skew = jnp.broadcast_to(f[:, :, :, None, :], (L, H, nd, GRID_W, 2 * GRID_W))
    skew = skew.reshape(L, H, nd, -1)[..., :GRID_W * (2 * GRID_W - 1)].reshape(L, H, nd, GRID_W, 2 * GRID_W - 1)
    toe = skew[..., GRID_W - 1:]
    return jnp.where(col_ok, toe, NEG_INF)


def _na_attend(p, q_norm, k_norm, toe):
    B = p.shape[0]
    c0 = _P_START['n_qkv'] // NA_W
    zc = _P_START['n_z'] // NA_W
    blk = lambda c: pl.BlockSpec((1, NA_QB, NA_W), lambda b, r: (b, r, c))
    full = lambda c: pl.BlockSpec((1, S_ALL, NA_W), lambda b, r: (b, 0, c))
    wspec = pl.BlockSpec((1, NA_DH), lambda b, r: (0, 0))
    return pl.pallas_call(
        _na_kernel,
        out_shape=jax.ShapeDtypeStruct((B, S_ALL, NA_W), BF16),
        grid=(B, NA_NBLK + 1),
        in_specs=[blk(c0), full(c0 + 1), full(c0 + 2), blk(zc),
                  pl.BlockSpec(toe.shape, lambda b, r: (0, 0, 0, 0)), wspec, wspec],
        out_specs=blk(0),
        scratch_shapes=[pltpu.VMEM((NA_WIN_R, NA_HEADS, GRID_W, NA_KW), F32), pltpu.VMEM((S_ALL, NA_W), BF16)],
        compiler_params=_cparams(2),
        name="na_attend",
    )(p, p, p, p, toe, q_norm.reshape(1, NA_DH), k_norm.reshape(1, NA_DH))


MP_TM = 544
MLA_HW = 2 * LANES
MLA_TQ = 2048
MLA_SUBQ = 512
MLA_TK = 1024


def _rope_tables():
    n_freq = MLA_ROPE // 4
    inv_freq = ROPE_THETA ** (-np.arange(n_freq, dtype=np.float64) / n_freq)
    t = np.arange(SEQ)
    ar = (t // GRID_W)[:, None] * inv_freq
    ac = (t % GRID_W)[:, None] * inv_freq
    cos = np.concatenate([np.cos(ar), np.cos(ar), np.cos(ac), np.cos(ac)], axis=1)
    sin = np.concatenate([-np.sin(ar), np.sin(ar), -np.sin(ac), np.sin(ac)], axis=1)
    cos = np.concatenate([cos, np.ones((CTX_LEN, MLA_ROPE))], axis=0)
    sin = np.concatenate([sin, np.zeros((CTX_LEN, MLA_ROPE))], axis=0)
    return (np.tile(cos, (1, MLA_HEADS)).astype(np.float32), np.tile(sin, (1, MLA_HEADS)).astype(np.float32))


def _rope_rotate(t, cos, sin):
    w = t.shape[1]
    lane = lax.broadcasted_iota(jnp.int32, (1, w), 1)
    first = (lane & 31) < 16
    up = pltpu.roll(t, w - 16, axis=1)
    dn = pltpu.roll(t, 16, axis=1)
    return t * cos + jnp.where(first, up, dn) * sin


def _mla_prep_kernel(p_ref, cos_ref, sin_ref, qan_ref, wuq_ref, kvan_ref, wukv_ref, qn_ref, kn_ref,
                     q_out, k_out, v_out):
    x = p_ref[0].astype(F32)
    cq = x[:, :MLA_Q_RANK]
    ckv = x[:, MLA_Q_RANK:MLA_Q_RANK + MLA_KV_RANK]
    kr2 = x[:, MLA_Q_RANK + MLA_KV_RANK:]

    def rms(t, w):
        return t * lax.rsqrt(jnp.mean(t * t, axis=-1, keepdims=True) + EPS) * w

    qf = jnp.dot(rms(cq, qan_ref[...]).astype(BF16), wuq_ref[...], preferred_element_type=F32)
    kvf = jnp.dot(rms(ckv, kvan_ref[...]).astype(BF16), wukv_ref[...], preferred_element_type=F32)
    cos = cos_ref[...]
    sin = sin_ref[...]
    qw = qn_ref[...]
    kw = kn_ref[...]
    n_all = MLA_HEADS * MLA_NOPE
    lane = lax.broadcasted_iota(jnp.int32, (1, LANES), 1)
    halves = (lane < MLA_ROPE, lane >= MLA_ROPE)

    q_rope = qf[:, n_all:]
    q_rope_sq = q_rope * q_rope
    q_rot = _rope_rotate(q_rope * qw[:, n_all:], cos, sin)
    kr_sq = jnp.sum(jnp.where(halves[0], kr2 * kr2, 0.0), axis=-1, keepdims=True)
    k_rot = _rope_rotate(kr2 * kw[:, n_all:], cos[:, :LANES], sin[:, :LANES])
    for h in range(MLA_HEADS):
        half = halves[h % 2]
        vsl = slice((h // 2) * LANES, (h // 2 + 1) * LANES)
        nsl = slice(h * MLA_NOPE, (h + 1) * MLA_NOPE)
        q_nope = qf[:, nsl]
        ss = (jnp.sum(q_nope * q_nope, axis=-1, keepdims=True)
              + jnp.sum(jnp.where(half, q_rope_sq[:, vsl], 0.0), axis=-1, keepdims=True))
        r = lax.rsqrt(ss * (1.0 / MLA_QK) + EPS) * (MLA_QK ** -0.5 * LOG2E)
        q_out[0, :, h * MLA_HW:h * MLA_HW + LANES] = (q_nope * qw[:, nsl] * r).astype(BF16)
        q_out[0, :, h * MLA_HW + LANES:(h + 1) * MLA_HW] = (jnp.where(half, q_rot[:, vsl], 0.0) * r).astype(BF16)
        k_nope = kvf[:, nsl]
        ss = jnp.sum(k_nope * k_nope, axis=-1, keepdims=True) + kr_sq
        r = lax.rsqrt(ss * (1.0 / MLA_QK) + EPS)
        k_out[0, :, h * MLA_HW:h * MLA_HW + LANES] = (k_nope * kw[:, nsl] * r).astype(BF16)
        k_out[0, :, h * MLA_HW + LANES:(h + 1) * MLA_HW] = (jnp.where(half, k_rot, 0.0) * r).astype(BF16)
    v_out[0] = kvf[:, n_all:].astype(BF16)


def _mla_prep(p, cos, sin, qa_norm, w_uq, kva_norm, w_ukv, q_norm, k_norm):
    B = p.shape[0]
    H = MLA_HEADS
    uq = w_uq.reshape(MLA_Q_RANK, H, MLA_QK)
    uq = jnp.concatenate([uq[:, :, :MLA_NOPE].reshape(MLA_Q_RANK, -1),
                          uq[:, :, MLA_NOPE:].reshape(MLA_Q_RANK, -1)], axis=1).astype(BF16)
    ukv = w_ukv.reshape(MLA_KV_RANK, H, MLA_NOPE + MLA_V)
    ukv = jnp.concatenate([ukv[:, :, :MLA_NOPE].reshape(MLA_KV_RANK, -1),
                           ukv[:, :, MLA_NOPE:].reshape(MLA_KV_RANK, -1)], axis=1).astype(BF16)
    qn = jnp.concatenate([jnp.tile(q_norm[:MLA_NOPE], H), jnp.tile(q_norm[MLA_NOPE:], H)]).reshape(1, -1)
    kn = jnp.concatenate([jnp.tile(k_norm[:MLA_NOPE], H), jnp.tile(k_norm[MLA_NOPE:], 2)]).reshape(1, -1)
    pc = _P_START['m_q'] // P_MLA_BLK
    const = lambda shape: pl.BlockSpec(shape, lambda b, i: (0, 0))
    rows = lambda w: pl.BlockSpec((MP_TM, w), lambda b, i: (i, 0))
    outs = lambda w: pl.BlockSpec((1, MP_TM, w), lambda b, i: (b, i, 0))
    return pl.pallas_call(
        _mla_prep_kernel,
        out_shape=(jax.ShapeDtypeStruct((B, S_ALL, H * MLA_HW), BF16),
                   jax.ShapeDtypeStruct((B, S_ALL, H * MLA_HW), BF16),
                   jax.ShapeDtypeStruct((B, S_ALL, MLA_W), BF16)),
        grid=(B, S_ALL // MP_TM),
        in_specs=[pl.BlockSpec((1, MP_TM, P_MLA_BLK), lambda b, i: (b, i, pc)),
                  rows(H * MLA_ROPE), rows(H * MLA_ROPE),
                  const((1, MLA_Q_RANK)), const(uq.shape), const((1, MLA_KV_RANK)), const(ukv.shape),
                  const(qn.shape), const(kn.shape)],
        out_specs=(outs(H * MLA_HW), outs(H * MLA_HW), outs(MLA_W)),
        compiler_params=_cparams(2),
        name="mla_prep",
    )(p, cos, sin, qa_norm.reshape(1, -1), uq, kva_norm.reshape(1, -1), ukv, qn, kn)


def _mla_rows_chain(q, z, k_ref, v_ref, bounds, store):
    n = q.shape[0]
    m = jnp.full((n, 1), NEG_INF, F32)
    l = jnp.zeros((n, 1), F32)
    acc = jnp.zeros((n, MLA_V), F32)
    for lo, hi in bounds:
        s = _dot_nt(q, k_ref[0, lo:hi, :])
        yield
        m_new = jnp.maximum(m, jnp.max(s, axis=-1, keepdims=True))
        a = jnp.exp2(m - m_new)
        p = jnp.exp2(s - m_new)
        l = a * l + jnp.sum(p, axis=-1, keepdims=True)
        acc = a * acc + jnp.dot(p.astype(BF16), v_ref[0, lo:hi, :], preferred_element_type=F32)
        m = m_new
        yield
    store((acc / l * _silu(z.astype(F32))).astype(BF16))


def _mla_attn_kernel(q_ref, k_ref, v_ref, z_ref, qc_ref, zc_ref, o_ref):
    i = pl.program_id(2)
    ctx_keys = [(SEQ, S_ALL)]
    all_keys = ctx_keys + [(c * MLA_TK, (c + 1) * MLA_TK) for c in range(SEQ // MLA_TK)]

    @pl.when(i == 0)
    def _():
        def store(val):
            o_ref[0, SEQ:S_ALL, :] = val
        _round_robin([_mla_rows_chain(qc_ref[0], zc_ref[0], k_ref, v_ref, ctx_keys, store)])

    def latent_chain(r0):
        def store(val):
            o_ref[0, pl.ds(pl.multiple_of(i * MLA_TQ + r0, MLA_SUBQ), MLA_SUBQ), :] = val
        rows = slice(r0, r0 + MLA_SUBQ)
        return _mla_rows_chain(q_ref[0, rows, :], z_ref[0, rows, :], k_ref, v_ref, all_keys, store)

    _round_robin(latent_chain(r0) for r0 in range(0, MLA_TQ, MLA_SUBQ))


def _mla_attend(q, k, v, p):
    B = p.shape[0]
    H = MLA_HEADS
    zc = _P_START['m_z'] // MLA_V
    ctx_blk = SEQ // CTX_LEN
    return pl.pallas_call(
        _mla_attn_kernel,
        out_shape=jax.ShapeDtypeStruct((B, S_ALL, MLA_W), BF16),
        grid=(B, H, SEQ // MLA_TQ),
        in_specs=[pl.BlockSpec((1, MLA_TQ, MLA_HW), lambda b, h, i: (b, i, h)),
                  pl.BlockSpec((1, S_ALL, MLA_HW), lambda b, h, i: (b, 0, h)),
                  pl.BlockSpec((1, S_ALL, MLA_V), lambda b, h, i: (b, 0, h)),
                  pl.BlockSpec((1, MLA_TQ, MLA_V), lambda b, h, i: (b, i, zc + h)),
                  pl.BlockSpec((1, CTX_LEN, MLA_HW), lambda b, h, i: (b, ctx_blk, h)),
                  pl.BlockSpec((1, CTX_LEN, MLA_V), lambda b, h, i: (b, ctx_blk, zc + h))],
        out_specs=pl.BlockSpec((1, S_ALL, MLA_V), lambda b, h, i: (b, 0, h)),
        compiler_params=_cparams(3),
        name="mla_attend",
    )(q, k, v, p, q, p)


SCAN_TILE = 256
SCAN_NT = S_ALL // SCAN_TILE
CTX_TILE = SEQ // SCAN_TILE
HALO = 16


def _split_dot(m_bf16, x):
    hi = x.astype(BF16)
    lo = (x - hi.astype(F32)).astype(BF16)
    return (jnp.dot(m_bf16, hi, preferred_element_type=F32)
            + jnp.dot(m_bf16, lo, preferred_element_type=F32))


def _softplus(t):
    return jnp.maximum(t, 0.0) + jnp.log1p(jnp.exp(-jnp.abs(t)))


def _conv3_silu(xb, prev_row, next_row, w, bias=None):
    n = xb.shape[0]
    i = lax.broadcasted_iota(jnp.int32, (n, 1), 0)
    j = lax.broadcasted_iota(jnp.int32, (1, n), 1)
    one = lambda m: jnp.where(m, 1.0, 0.0).astype(BF16)
    xp = jnp.dot(one(i == j + 1), xb, preferred_element_type=F32)
    xn = jnp.dot(one(i + 1 == j), xb, preferred_element_type=F32)
    y = xp * w[0:1] + xb.astype(F32) * w[1:2] + xn * w[2:3]
    r8 = lax.broadcasted_iota(jnp.int32, (8, 1), 0)
    top = y[0:8] + jnp.where(r8 == 0, prev_row * w[0:1], 0.0)
    bot = y[n - 8:] + jnp.where(r8 == 7, next_row * w[2:3], 0.0)
    y = jnp.concatenate([top, y[8:n - 8], bot], axis=0)
    if bias is not None:
        y = y + bias
    return _silu(y)


def _halo_rows(i, prev_ref, next_ref):
    pv = jnp.where((i == 0) | (i == CTX_TILE), 0.0, 1.0)
    nv = jnp.where((i == CTX_TILE - 1) | (i == SCAN_NT - 1), 0.0, 1.0)
    return prev_ref[0, HALO - 1:HALO, :].astype(F32) * pv, next_ref[0, 0:1, :].astype(F32) * nv


def _halo_specs(width, col_blk):
    rb = SCAN_TILE // HALO
    nblk = S_ALL // HALO
    return [pl.BlockSpec((1, SCAN_TILE, width), lambda b, i: (b, i, col_blk)),
            pl.BlockSpec((1, HALO, width), lambda b, i: (b, jnp.maximum(i * rb - 1, 0), col_blk)),
            pl.BlockSpec((1, HALO, width), lambda b, i: (b, jnp.minimum((i + 1) * rb, nblk - 1), col_blk))]


def _fwd_tile(t):
    return jnp.where(t == 0, CTX_TILE, t - 1)


def _bwd_tile(t):
    return jnp.where(t == 0, CTX_TILE, CTX_TILE - t)


def _chunk_masks(n):
    i = lax.broadcasted_iota(jnp.int32, (n, 1), 0)
    j = lax.broadcasted_iota(jnp.int32, (1, n), 1)
    same = (i // GDN_CHUNK) == (j // GDN_CHUNK)
    return i, j, same


def _gdn_prep_kernel(x_ref, prev_ref, next_ref, s_ref, cw_ref, rate_ref, dtb_ref,
                     q_out, k_out, v_out, g_out):
    i = pl.program_id(1)
    prev_row, next_row = _halo_rows(i, prev_ref, next_ref)
    y = _conv3_silu(x_ref[0], prev_row, next_row, cw_ref[...])
    for h in range(GDN_HEADS):
        sl = slice(h * GDN_DK, (h + 1) * GDN_DK)
        qh = y[:, sl]
        q_out[0, :, sl] = (qh * lax.rsqrt(jnp.sum(qh * qh, axis=-1, keepdims=True) + EPS)
                           * (GDN_DK ** -0.5)).astype(BF16)
        kh = y[:, GDN_W + h * GDN_DK:GDN_W + (h + 1) * GDN_DK]
        k_out[0, :, sl] = (kh * lax.rsqrt(jnp.sum(kh * kh, axis=-1, keepdims=True) + EPS)).astype(BF16)
    v_out[0] = y[:, 2 * GDN_W:].astype(BF16)

    s = s_ref[0]
    lane = lax.broadcasted_iota(jnp.int32, (1, LANES), 1)
    nh2 = 2 * GDN_HEADS
    beta = jax.nn.sigmoid(s)
    g = -rate_ref[...] * _softplus(s + dtb_ref[...])
    g = jnp.where((lane >= nh2) & (lane < 2 * nh2), g, 0.0)
    ii, jj, same = _chunk_masks(SCAN_TILE)
    one = lambda m: jnp.where(m, 1.0, 0.0).astype(BF16)
    fwd_lane = lane < nh2 + GDN_HEADS
    gam = jnp.where(fwd_lane, _split_dot(one(same & (jj <= ii)), g), _split_dot(one(same & (jj >= ii)), g))
    rem = jnp.where(fwd_lane, _split_dot(one(same & (jj > ii)), g), _split_dot(one(same & (jj < ii)), g))
    cf = jnp.where(lane < nh2, beta, jnp.where(lane < 2 * nh2, gam, pltpu.roll(rem, nh2, axis=1)))
    tr = cf.T
    for h in range(GDN_HEADS):
        for r, src in enumerate((h, GDN_HEADS + h, nh2 + h, nh2 + GDN_HEADS + h,
                                 2 * nh2 + h, 2 * nh2 + GDN_HEADS + h)):
            g_out[0, h, r:r + 1, :] = tr[src:src + 1, :]
        g_out[0, h, 6:8, :] = jnp.zeros((2, SCAN_TILE), F32)


def _gdn_prep(p, ps, conv_w, A_log, dt_bias):
    B = p.shape[0]
    W3 = 3 * GDN_W
    nh2 = 2 * GDN_HEADS
    rate = jnp.zeros((1, LANES), F32).at[0, nh2:2 * nh2].set(jnp.exp(A_log).reshape(-1))
    dtb = jnp.zeros((1, LANES), F32).at[0, nh2:2 * nh2].set(dt_bias.reshape(-1))
    shp = jax.ShapeDtypeStruct((B, S_ALL, GDN_W), BF16)
    ospec = pl.BlockSpec((1, SCAN_TILE, GDN_W), lambda b, i: (b, i, 0))
    const = lambda shape: pl.BlockSpec(shape, lambda b, i: (0, 0))
    return pl.pallas_call(
        _gdn_prep_kernel,
        out_shape=(shp, shp, shp, jax.ShapeDtypeStruct((B, GDN_HEADS, 8, S_ALL), F32)),
        grid=(B, SCAN_NT),
        in_specs=_halo_specs(W3, _P_START['g_qkv'] // W3)
        + [pl.BlockSpec((1, SCAN_TILE, LANES), lambda b, i: (b, i, 0)),
           const((SHORT_CONV, W3)), const((1, LANES)), const((1, LANES))],
        out_specs=(ospec, ospec, ospec,
                   pl.BlockSpec((1, GDN_HEADS, 8, SCAN_TILE), lambda b, i: (b, 0, 0, i))),
        compiler_params=_cparams(2),
        name="gdn_prep",
    )(p, p, p, ps, conv_w, rate, dtb)


def _gdn_dir(q, k, v, gr, s_ref, o_ref, d):
    n = SCAN_TILE
    cf = jnp.concatenate([gr, jnp.zeros((LANES - 8, n), F32)], axis=0).T
    beta, gam_c = cf[:, d:d + 1], cf[:, 2 + d:3 + d]
    ecf = jnp.exp(cf)
    e_gam, e_rem = ecf[:, 2 + d:3 + d], ecf[:, 4 + d:5 + d]
    gam_r = gr[2 + d:3 + d, :]
    ii, jj, same = _chunk_masks(n)
    incl = same & ((jj <= ii) if d == 0 else (jj >= ii))
    strict = same & ((jj < ii) if d == 0 else (jj > ii))
    kk = _dot_nt(k, k)
    yield
    qk = _dot_nt(q, k)
    yield
    dec = jnp.exp(jnp.where(incl, gam_c - gam_r, NEG_INF))
    a = jnp.where(strict, beta * kk * dec, 0.0)
    qkd = (qk * dec).astype(BF16)
    kf = k.astype(F32)
    x = jnp.concatenate([v.astype(F32) * beta, kf * (beta * e_gam)], axis=1)
    blk = lambda size: (ii // size) == (jj // size)
    inner = blk(2)
    t = jnp.where(ii == jj, 1.0, 0.0) - jnp.where(inner, a, 0.0)
    for size in (4, 8, 16, 32, GDN_CHUNK):
        outer = blk(size)
        e = jnp.where(outer & ~inner, a, 0.0)
        inner = outer
        et = jnp.dot(e.astype(BF16), t.astype(BF16), preferred_element_type=F32)
        yield
        t = t - jnp.dot(t.astype(BF16), et.astype(BF16), preferred_element_type=F32)
        yield
    x = jnp.dot(t.astype(BF16), x.astype(BF16), preferred_element_type=F32)
    yield
    u, w = x[:, :GDN_DV], x[:, GDN_DV:].astype(BF16)
    qd = (q.astype(F32) * e_gam).astype(BF16)
    kd = (kf * e_rem).astype(BF16)
    s = s_ref[...]
    nchunk = n // GDN_CHUNK
    v_new = [None] * nchunk
    qs = [None] * nchunk
    for c in (range(nchunk) if d == 0 else reversed(range(nchunk))):
        rows = slice(c * GDN_CHUNK, (c + 1) * GDN_CHUNK)
        r1 = jnp.dot(jnp.concatenate([w[rows], qd[rows]], axis=0), s.astype(BF16), preferred_element_type=F32)
        yield
        vn = u[rows] - r1[:GDN_CHUNK]
        qs[c] = r1[GDN_CHUNK:]
        v_new[c] = vn
        last = (c + 1) * GDN_CHUNK - 1 if d == 0 else c * GDN_CHUNK
        s = s * ecf[last:last + 1, 2 + d:3 + d] + _dot_tn(kd[rows], vn.astype(BF16))
        yield
    s_ref[...] = s
    vn_all = jnp.concatenate(v_new, axis=0).astype(BF16)
    o_ref[0] = (jnp.concatenate(qs, axis=0) + jnp.dot(qkd, vn_all, preferred_element_type=F32)).astype(BF16)


def _round_robin(gens):
    gens = list(gens)
    while gens:
        alive = []
        for g in gens:
            try:
                next(g)
                alive.append(g)
            except StopIteration:
                pass
        gens = alive


def _gdn_scan_kernel(qf, kf, vf, gf, qb, kb, vb, gb, of_ref, ob_ref, sf_ref, sb_ref):
    @pl.when(pl.program_id(1) == 0)
    def _():
        sf_ref[...] = jnp.zeros_like(sf_ref)
        sb_ref[...] = jnp.zeros_like(sb_ref)

    chains = []
    for h in range(GDN_HEADS):
        sl = slice(h * GDN_DK, (h + 1) * GDN_DK)
        chains.append(_gdn_dir(qf[0, :, sl], kf[0, :, sl], vf[0, :, sl], gf[0, h],
                               sf_ref.at[h], of_ref.at[:, :, sl], 0))
        chains.append(_gdn_dir(qb[0, :, sl], kb[0, :, sl], vb[0, :, sl], gb[0, h],
                               sb_ref.at[h], ob_ref.at[:, :, sl], 1))
    _round_robin(chains)


def _gdn_scan(q, k, v, g):
    B = q.shape[0]
    tok = lambda order: pl.BlockSpec((1, SCAN_TILE, GDN_W), lambda b, t: (b, order(t), 0))
    gsp = lambda order: pl.BlockSpec((1, GDN_HEADS, 8, SCAN_TILE), lambda b, t: (b, 0, 0, order(t)))
    shp = jax.ShapeDtypeStruct((B, S_ALL, GDN_W), BF16)
    f, r = _fwd_tile, _bwd_tile
    state = pltpu.VMEM((GDN_HEADS, GDN_DK, GDN_DV), F32)
    return pl.pallas_call(
        _gdn_scan_kernel,
        out_shape=(shp, shp),
        grid=(B, SCAN_NT),
        in_specs=[tok(f), tok(f), tok(f), gsp(f), tok(r), tok(r), tok(r), gsp(r)],
        out_specs=(tok(f), tok(r)),
        scratch_shapes=[state, state],
        compiler_params=_cparams(2),
        name="gdn_scan",
    )(q, k, v, g, q, k, v, g)


SSD_BC = SSM_GROUPS * SSM_STATE
SSD_HD = 2 * SSM_HEADS
SSD_GW = (SSM_HEADS // SSM_GROUPS) * SSM_HEADDIM


def _ssd_prep_kernel(x_ref, prev_ref, next_ref, s_ref, cw_ref, cb_ref, a_ref, dtb_ref,
                     xs_out, b_out, c_out, bt_out, cf_out, cr_out):
    i = pl.program_id(1)
    prev_row, next_row = _halo_rows(i, prev_ref, next_ref)
    y = _conv3_silu(x_ref[0], prev_row, next_row, cw_ref[...], cb_ref[...])
    xs_out[0] = y[:, :SSM_W]
    b_out[0] = y[:, SSM_W:SSM_W + SSD_BC].astype(BF16)
    c_out[0] = y[:, SSM_W + SSD_BC:].astype(BF16)
    bt_out[0] = y[:, SSM_W:SSM_W + SSD_BC].T.astype(BF16)

    s = s_ref[0]
    lane = lax.broadcasted_iota(jnp.int32, (1, LANES), 1)
    dt = _softplus(s + dtb_ref[...])
    on = (lane >= SSD_HD) & (lane < 2 * SSD_HD)
    a = jnp.where(on, dt * a_ref[...], 0.0)
    n = SCAN_TILE
    ii = lax.broadcasted_iota(jnp.int32, (n, 1), 0)
    jj = lax.broadcasted_iota(jnp.int32, (1, n), 1)
    one = lambda m: jnp.where(m, 1.0, 0.0).astype(BF16)
    fwd_lane = lane < SSD_HD + SSM_HEADS
    cum = jnp.where(fwd_lane, _split_dot(one(jj <= ii), a), _split_dot(one(jj >= ii), a))
    rem = jnp.where(fwd_lane, _split_dot(one(jj > ii), a), _split_dot(one(jj < ii), a))
    cf = jnp.where(lane < SSD_HD, pltpu.roll(dt, LANES - SSD_HD, axis=1),
                   jnp.where(lane < 2 * SSD_HD, cum, pltpu.roll(rem, SSD_HD, axis=1)))
    cf_out[0] = cf
    cr_out[0] = cf.T[SSD_HD:2 * SSD_HD, :]


def _ssd_prep(p, ps, conv_w, conv_b, A_log, dt_bias):
    B = p.shape[0]
    W = SSM_CONV_DIM
    a_vec = jnp.zeros((1, LANES), F32).at[0, SSD_HD:2 * SSD_HD].set(-jnp.exp(A_log).reshape(-1))
    dtb = jnp.zeros((1, LANES), F32).at[0, SSD_HD:2 * SSD_HD].set(dt_bias.reshape(-1))
    const = lambda shape: pl.BlockSpec(shape, lambda b, i: (0, 0))
    tok = lambda w: pl.BlockSpec((1, SCAN_TILE, w), lambda b, i: (b, i, 0))
    return pl.pallas_call(
        _ssd_prep_kernel,
        out_shape=(jax.ShapeDtypeStruct((B, S_ALL, SSM_W), F32),
                   jax.ShapeDtypeStruct((B, S_ALL, SSD_BC), BF16),
                   jax.ShapeDtypeStruct((B, S_ALL, SSD_BC), BF16),
                   jax.ShapeDtypeStruct((B, SSD_BC, S_ALL), BF16),
                   jax.ShapeDtypeStruct((B, S_ALL, LANES), F32),
                   jax.ShapeDtypeStruct((B, SSD_HD, S_ALL), F32)),
        grid=(B, SCAN_NT),
        in_specs=_halo_specs(W, _P_START['s_xbc'] // W)
        + [pl.BlockSpec((1, SCAN_TILE, LANES), lambda b, i: (b, i, 0)),
           const((SHORT_CONV, W)), const((1, W)), const((1, LANES)), const((1, LANES))],
        out_specs=(tok(SSM_W), tok(SSD_BC), tok(SSD_BC),
                   pl.BlockSpec((1, SSD_BC, SCAN_TILE), lambda b, i: (b, 0, i)),
                   tok(LANES),
                   pl.BlockSpec((1, SSD_HD, SCAN_TILE), lambda b, i: (b, 0, i))),
        compiler_params=_cparams(2),
        name="ssd_prep",
    )(p, p, p, ps, conv_w, conv_b.reshape(1, W), a_vec, dtb)


def _ssd_dir(x_ref, b_ref, c_ref, bt_ref, cf_ref, cr_ref, h_ref, y_ref, d, g):
    n = SCAN_TILE
    hpg = SSM_HEADS // SSM_GROUPS
    hd0 = d * SSM_HEADS + g * hpg
    gs = slice(g * SSM_STATE, (g + 1) * SSM_STATE)
    xl = slice(g * SSD_GW, (g + 1) * SSD_GW)
    cf = cf_ref[0]
    cr = cr_ref[0]
    cm = c_ref[0, :, gs]
    log_lanes = lax.broadcasted_iota(jnp.int32, (1, LANES), 1) >= SSD_HD
    ecf = jnp.exp(jnp.where(log_lanes, cf, 0.0))
    col = lambda base, h: cf[:, base + hd0 + h:base + hd0 + h + 1]
    last = n - 1 if d == 0 else 0
    src = jnp.where(log_lanes, ecf, cf)
    r = lax.broadcasted_iota(jnp.int32, (LANES, 1), 0)
    c = lax.broadcasted_iota(jnp.int32, (1, 3 * SSD_GW), 1)
    want = (c // SSD_GW) * SSD_HD + hd0 + (c % SSD_GW) // SSM_HEADDIM
    sel = jnp.where(r == want, 1.0, 0.0).astype(BF16)
    hi = src.astype(BF16)
    lo_part = (src - hi.astype(F32)).astype(BF16)
    spread = (jnp.dot(hi, sel, preferred_element_type=F32) + jnp.dot(lo_part, sel, preferred_element_type=F32))
    yield
    dt_x, ecum_x, erem_x = spread[:, :SSD_GW], spread[:, SSD_GW:2 * SSD_GW], spread[:, 2 * SSD_GW:]
    xdt = x_ref[0, :, xl] * dt_x
    xdt_b = xdt.astype(BF16)
    xdec = (xdt * erem_x).astype(BF16)
    cb = _dot_nt(cm, b_ref[0, :, gs])
    yield
    h_prev = h_ref[g]
    y_off = jnp.dot(cm, h_prev.astype(BF16), preferred_element_type=F32)
    yield
    y_off = y_off * ecum_x
    h_ref[g] = h_prev * ecum_x[last:last + 1, :] + jnp.dot(bt_ref[0, gs, :], xdec, preferred_element_type=F32)
    yield
    ii = lax.broadcasted_iota(jnp.int32, (n, 1), 0)
    jj = lax.broadcasted_iota(jnp.int32, (1, n), 1)
    causal = (jj <= ii) if d == 0 else (jj >= ii)
    lo = lax.broadcasted_iota(jnp.int32, (1, LANES), 1) < SSM_HEADDIM
    pair_out = []
    for j in range(hpg // 2):
        ys = []
        for e in range(2):
            h = 2 * j + e
            seg = col(SSD_HD, h) - cr[hd0 + h:hd0 + h + 1, :]
            sc = (cb * jnp.exp(jnp.where(causal, seg, NEG_INF))).astype(BF16)
            ys.append(jnp.dot(sc, xdt_b[:, j * LANES:(j + 1) * LANES], preferred_element_type=F32))
            yield
        pair_out.append(jnp.where(lo, ys[0], ys[1]))
    y_ref[0, :, xl] = (jnp.concatenate(pair_out, axis=1) + y_off).astype(BF16)


def _ssd_scan_kernel(xf, bf, cf_, btf, colf, rowf, xb, bb, cb_, btb, colb, rowb, yf_ref, yb_ref, hf_ref, hb_ref):
    @pl.when(pl.program_id(1) == 0)
    def _():
        hf_ref[...] = jnp.zeros_like(hf_ref)
        hb_ref[...] = jnp.zeros_like(hb_ref)

    chains = []
    for g in range(SSM_GROUPS):
        chains.append(_ssd_dir(xf, bf, cf_, btf, colf, rowf, hf_ref, yf_ref, 0, g))
        chains.append(_ssd_dir(xb, bb, cb_, btb, colb, rowb, hb_ref, yb_ref, 1, g))
    _round_robin(chains)


def _ssd_scan(xs, bm, cm, bt, cf, cr):
    B = xs.shape[0]

    def specs(order):
        tok = lambda w: pl.BlockSpec((1, SCAN_TILE, w), lambda b, t: (b, order(t), 0))
        return [tok(SSM_W), tok(SSD_BC), tok(SSD_BC),
                pl.BlockSpec((1, SSD_BC, SCAN_TILE), lambda b, t: (b, 0, order(t))),
                tok(LANES),
                pl.BlockSpec((1, SSD_HD, SCAN_TILE), lambda b, t: (b, 0, order(t)))]

    shp = jax.ShapeDtypeStruct((B, S_ALL, SSM_W), BF16)
    out = lambda order: pl.BlockSpec((1, SCAN_TILE, SSM_W), lambda b, t: (b, order(t), 0))
    hshape = pltpu.VMEM((SSM_GROUPS, SSM_STATE, SSD_GW), F32)
    args = (xs, bm, cm, bt, cf, cr)
    return pl.pallas_call(
        _ssd_scan_kernel,
        out_shape=(shp, shp),
        grid=(B, SCAN_NT),
        in_specs=specs(_fwd_tile) + specs(_bwd_tile),
        out_specs=(out(_fwd_tile), out(_bwd_tile)),
        scratch_shapes=[hshape, hshape],
        compiler_params=_cparams(2),
        name="ssd_scan",
    )(*args, *args)


def _join_kernel(x_ref, c_ref, o_ref):
    i = pl.program_id(1)

    @pl.when(i < CTX_TILE)
    def _():
        o_ref[...] = x_ref[...]

    @pl.when(i == CTX_TILE)
    def _():
        o_ref[...] = c_ref[...]


def _join(x, ctx):
    B = x.shape[0]
    blk = lambda index: pl.BlockSpec((1, SCAN_TILE, D_MODEL), index)
    return pl.pallas_call(
        _join_kernel,
        out_shape=jax.ShapeDtypeStruct((B, S_ALL, D_MODEL), x.dtype),
        grid=(B, SCAN_NT),
        in_specs=[blk(lambda b, i: (b, jnp.minimum(i, CTX_TILE - 1), 0)), blk(lambda b, i: (b, 0, 0))],
        out_specs=blk(lambda b, i: (b, i, 0)),
        compiler_params=_cparams(2),
        name="join_tokens",
    )(x, ctx)


def _repack_w_in(w):
    cut = lambda names: [w[:, _IN_START[n]:_IN_START[n] + _IN_SIZE[n]].astype(BF16) for n in names]
    zeros = lambda n: jnp.zeros((w.shape[0], n), BF16)
    n_small = sum(_IN_SIZE[n] for n in _S_ORDER)
    cols = cut(_P_ORDER) + [zeros(D_INP - LANES - _off)] + cut(_S_ORDER) + [zeros(LANES - n_small)]
    return jnp.concatenate(cols, axis=1)


def kernel(x, c, ctx, c_ctx, norm_w, ada_w, ada_b, w_in, gdn_conv_w, gdn_A_log, gdn_dt_bias, gdn_norm_w, na_q_norm, na_k_norm, na_rpb, mla_qa_norm, mla_w_uq, mla_kva_norm, mla_w_ukv, mla_q_norm, mla_k_norm, ssm_conv_w, ssm_conv_b, ssm_A_log, ssm_dt_bias, ssm_D, ssm_norm_w, w_out):
    B = x.shape[0]
    xs = _join(x, ctx)
    c8 = jnp.zeros((8, D_MODEL), F32).at[:B].set(c).at[B].set(c_ctx)
    mods = _ada_all(c8, ada_w, ada_b)
    cos_np, sin_np = _rope_tables()
    cos, sin = jnp.asarray(cos_np), jnp.asarray(sin_np)
    na_bias = _na_bias(na_rpb)
    for l in range(DEPTH):
        shift, scale, gate = jnp.split(mods[l, :B], 3, axis=-1)
        shift_c, scale_c, gate_c = jnp.split(mods[l, B], 3, axis=-1)
        bc = lambda v: jnp.broadcast_to(v[None], (B, D_MODEL))
        mod4 = jnp.stack([shift, scale, bc(shift_c), bc(scale_c)], axis=1)
        gate2 = jnp.stack([gate, bc(gate_c)], axis=1)
        p, ps = _inproj(xs, norm_w[l], mod4, _repack_w_in(w_in[l]))

        gq, gk, gv, gg = _gdn_prep(p, ps, gdn_conv_w[l], gdn_A_log[l], gdn_dt_bias[l])
        o_f, o_b = _gdn_scan(gq, gk, gv, gg)

        ob = _na_attend(p, na_q_norm[l], na_k_norm[l], na_bias[l])

        mq, mk, mv = _mla_prep(p, cos, sin, mla_qa_norm[l], mla_w_uq[l], mla_kva_norm[l], mla_w_ukv[l],
                               mla_q_norm[l], mla_k_norm[l])
        oc = _mla_attend(mq, mk, mv, p)

        sx, sb, sc, sbt, scf, scr = _ssd_prep(p, ps, ssm_conv_w[l], ssm_conv_b[l], ssm_A_log[l], ssm_dt_bias[l])
        y_f, y_b = _ssd_scan(sx, sb, sc, sbt, scf, scr)
        xs = _outproj((o_f, o_b, gdn_norm_w[l]), ob, oc, (y_f, y_b, sx, ssm_D[l], ssm_norm_w[l]), p,
                      w_out[l].astype(BF16), xs, gate2, last=(l == DEPTH - 1))
    return xs
```
